```python
import jax, jax.numpy as jnp
from jax import lax
import numpy as np


D_MODEL = 1024
BATCH = 8
SEQ = 4096
DEPTH = 2

CTX_LEN = 256
GRID_W = 64
N_BRANCH = 4
BR_W = D_MODEL // N_BRANCH
HEAD_DIM = 64
HEADS = BR_W // HEAD_DIM
N_PARTS = 12
P_IN = N_PARTS * BR_W
CHUNK = 64
GLA_LR = 16
GLA_NORMALIZER = 16.0
RWKV_DECAY_LR = 64
RWKV_A_LR = 64
RWKV_G_LR = 160
D_FF = 2816
ROPE_BASE = 10000.0
EPS = 1e-6

kernel_name = 'hybrid_gla_retnet_rwkv7_fnet_prefix_block'


def _rmsnorm(x, g):
    xf = x.astype(jnp.float32)
    y = xf * lax.rsqrt(jnp.mean(xf * xf, axis=-1, keepdims=True) + EPS)
    return y.astype(x.dtype) * g


def _head_norm(o, g, center):
    of = o.astype(jnp.float32)
    if center:
        of = of - jnp.mean(of, axis=-1, keepdims=True)
    of = of * lax.rsqrt(jnp.mean(of * of, axis=-1, keepdims=True) + EPS)
    B, T, H, d = o.shape
    return of.reshape(B, T, H * d) * g


def _heads(x):
    B, T, _ = x.shape
    return x.reshape(B, T, HEADS, HEAD_DIM).transpose(0, 2, 1, 3)


def _unheads(x):
    return x.transpose(0, 2, 1, 3)


def _to_chunks(x):
    B, H, T, d = x.shape
    return x.reshape(B, H, T // CHUNK, CHUNK, d).transpose(2, 0, 1, 3, 4)


def _from_chunks(x):
    N, B, H, C, d = x.shape
    return x.transpose(1, 2, 0, 3, 4).reshape(B, H, N * C, d)


def _neighbours(x):
    xp = jnp.pad(x, ((0, 0), (1, 1), (0, 0)))
    return xp[:, :-2], xp[:, 2:]


def _dwconv3(x, w):
    prev, nxt = _neighbours(x)
    return prev * w[0] + x * w[1] + nxt * w[2]


def _axial_rope(T):
    rows = T // GRID_W
    row = jnp.repeat(jnp.arange(rows, dtype=jnp.float32), GRID_W)
    col = jnp.tile(jnp.arange(GRID_W, dtype=jnp.float32), rows)
    n_freq = HEAD_DIM // 4
    inv = ROPE_BASE ** (-jnp.arange(n_freq, dtype=jnp.float32) / n_freq)
    ang = jnp.concatenate([row[:, None] * inv, col[:, None] * inv], axis=-1)
    return jnp.cos(ang), jnp.sin(ang)


def _rope(x, cos, sin):
    half = x.shape[-1] // 2
    x1, x2 = x[..., :half], x[..., half:]
    return jnp.concatenate([x1 * cos - x2 * sin, x1 * sin + x2 * cos], axis=-1)


def _gla_scan(q, k, v, log_a, s0, reverse):
    if s0 is None:
        s0 = jnp.zeros(q.shape[:2] + (q.shape[-1], v.shape[-1]), jnp.float32)
    if reverse:
        q, k, v, log_a = (jnp.flip(t, axis=2) for t in (q, k, v, log_a))
    tri = jnp.tril(jnp.ones((CHUNK, CHUNK), dtype=bool))[:, :, None]

    def step(s, inp):
        qc, kc, vc, ac = inp
        b = jnp.cumsum(ac, axis=2)
        diff = b[:, :, :, None, :] - b[:, :, None, :, :]
        decay = jnp.where(tri, jnp.exp(jnp.where(tri, diff, 0.0)), 0.0)
        scores = jnp.einsum('bhid,bhjd,bhijd->bhij', qc, kc, decay)
        o = (jnp.einsum('bhij,bhje->bhie', scores, vc)
             + jnp.einsum('bhid,bhde->bhie', qc * jnp.exp(b), s))
        b_end = b[:, :, -1:, :]
        s = (jnp.swapaxes(jnp.exp(b_end), 2, 3) * s
             + jnp.einsum('bhjd,bhje->bhde', kc * jnp.exp(b_end - b), vc))
        return s, o

    s, o = lax.scan(step, s0, tuple(_to_chunks(t) for t in (q, k, v, log_a)))
    o = _from_chunks(o)
    if reverse:
        o = jnp.flip(o, axis=2)
    return o, s


def _ret_log_decay(direction):
    expo = -5.0 - jnp.arange(HEADS, dtype=jnp.float32)
    if direction == 1:
        expo = expo[::-1]
    return jnp.log1p(-jnp.exp2(expo))


def _ret_scan(q, k, v, log_g, s0, reverse):
    if s0 is None:
        s0 = jnp.zeros(q.shape[:2] + (q.shape[-1], v.shape[-1]), jnp.float32)
    if reverse:
        q, k, v = (jnp.flip(t, axis=2) for t in (q, k, v))
    pos = jnp.arange(CHUNK, dtype=jnp.float32)
    rel = pos[:, None] - pos[None, :]
    tri = rel >= 0
    dmat = jnp.where(tri, jnp.exp(jnp.where(tri, rel, 0.0) * log_g[:, None, None]), 0.0)
    q_dec = jnp.exp((pos + 1.0) * log_g[:, None])[:, :, None]
    k_dec = jnp.exp((CHUNK - 1.0 - pos) * log_g[:, None])[:, :, None]
    c_dec = jnp.exp(CHUNK * log_g)[:, None, None]

    def step(s, inp):
        qc, kc, vc = inp
        scores = jnp.einsum('bhid,bhjd->bhij', qc, kc) * dmat
        o = (jnp.einsum('bhij,bhje->bhie', scores, vc)
             + jnp.einsum('bhid,bhde->bhie', qc * q_dec, s))
        s = c_dec * s + jnp.einsum('bhjd,bhje->bhde', kc * k_dec, vc)
        return s, o

    s, o = lax.scan(step, s0, tuple(_to_chunks(t) for t in (q, k, v)))
    o = _from_chunks(o)
    if reverse:
        o = jnp.flip(o, axis=2)
    return o, s


def _rwkv_scan(r, w, k, v, a, b, s0, reverse):
    if s0 is None:
        B, T, H, d = r.shape
        s0 = jnp.zeros((B, H, d, d), jnp.float32)
    xs = tuple(jnp.moveaxis(t, 1, 0) for t in (r, w, k, v, a, b))
    if reverse:
        xs = tuple(jnp.flip(t, axis=0) for t in xs)

    def step(s, inp):
        rt, wt, kt, vt, at, bt = inp
        sa = jnp.einsum('bhvk,bhk->bhv', s, at)
        s = (s * wt[:, :, None, :] + sa[..., None] * bt[:, :, None, :]
             + vt[..., None] * kt[:, :, None, :])
        return s, jnp.einsum('bhvk,bhk->bhv', s, rt)

    s, ys = lax.scan(step, s0, xs)
    if reverse:
        ys = jnp.flip(ys, axis=0)
    return jnp.moveaxis(ys, 0, 1), s


def _bidir(scan_fn, lat_dirs, ctx_dirs):
    out_l = 0.0
    out_c = 0.0
    for dr in range(2):
        oc, sc = scan_fn(*ctx_dirs[dr], None, dr == 1)
        ol, _ = scan_fn(*lat_dirs[dr], sc, dr == 1)
        out_l = out_l + ol
        out_c = out_c + oc
    return out_l, out_c


def _gla_feats(h, q, k, v, lp):
    f32 = jnp.float32
    q = _heads(q).astype(f32) * HEAD_DIM ** -0.5
    k = _heads(k).astype(f32)
    v = _heads(v).astype(f32)
    dirs = []
    for dr in range(2):
        z = (h @ lp['gla_wa1'][dr]) @ lp['gla_wa2'][dr] + lp['gla_ba'][dr]
        log_a = _heads(jax.nn.log_sigmoid(z.astype(f32)) / GLA_NORMALIZER)
        dirs.append((q, k, v, log_a))
    return dirs


def _ret_feats(q, k, v, rope):
    f32 = jnp.float32
    q = _heads(q).astype(f32)
    k = _heads(k).astype(f32) * HEAD_DIM ** -0.5
    v = _heads(v).astype(f32)
    if rope is not None:
        q = _rope(q, *rope)
        k = _rope(k, *rope)
    return [(q, k, v, _ret_log_decay(dr)) for dr in range(2)]


def _rwkv_feats(h, r, k, v, lp):
    f32 = jnp.float32
    B, T, _ = h.shape
    shp = (B, T, HEADS, HEAD_DIM)
    r, k, v = jnp.split(_dwconv3(jnp.concatenate([r, k, v], axis=-1), lp['rwkv_conv']), 3, axis=-1)
    prev, nxt = _neighbours(h)
    xx = 0.5 * (prev + nxt) - h
    xw = h + xx * lp['rwkv_mu'][0]
    xa = h + xx * lp['rwkv_mu'][1]
    xg = h + xx * lp['rwkv_mu'][2]
    g = jax.nn.sigmoid(xg @ lp['rwkv_g1']) @ lp['rwkv_g2']
    kk = (k * lp['rwkv_kk']).astype(f32).reshape(shp)
    kk = kk * lax.rsqrt(jnp.sum(kk * kk, axis=-1, keepdims=True) + EPS)
    r4 = r.astype(f32).reshape(shp)
    v4 = v.astype(f32).reshape(shp)
    rk = lp['rwkv_rk'].reshape(HEADS, HEAD_DIM)
    dirs = []
    bonus = 0.0
    for dr in range(2):
        w_raw = -jax.nn.softplus(-(lp['rwkv_w0'][dr] + jnp.tanh(xw @ lp['rwkv_w1'][dr]) @ lp['rwkv_w2'][dr])) - 0.5
        decay = jnp.exp(-jnp.exp(w_raw.astype(f32))).reshape(shp)
        a = jax.nn.sigmoid(lp['rwkv_a0'][dr] + (xa @ lp['rwkv_a1'][dr]) @ lp['rwkv_a2'][dr]).astype(f32)
        kd = (k.astype(f32) * (1.0 + (a - 1.0) * lp['rwkv_ka'])).reshape(shp)
        a4 = a.reshape(shp)
        dirs.append((r4, decay, kd, v4, -kk, kk * a4))
        bonus = bonus + jnp.sum(r4 * kd * rk, axis=-1, keepdims=True) * v4
    return dirs, bonus, g


def _fourier(f):
    B, T, _ = f.shape
    fg = f.astype(jnp.float32).reshape(B, T, HEADS, HEAD_DIM)
    return jnp.real(jnp.fft.fft2(fg, axes=(1, 3), norm='ortho')).reshape(B, T, BR_W)


def _branch_outputs(parts, o_gla, o_ret, y_rwkv, bonus, g_rwkv, lp):
    f32 = jnp.float32
    dt = parts[0].dtype
    B, T, _ = parts[0].shape
    gla = _head_norm(_unheads(o_gla), lp['gla_gn'], False) * jax.nn.silu(parts[3].astype(f32))
    ret = _head_norm(_unheads(o_ret), lp['ret_gn'], True) * jax.nn.silu(parts[7].astype(f32))
    rwkv = (_head_norm(y_rwkv, lp['rwkv_gn'], True) + bonus.reshape(B, T, BR_W)) * g_rwkv.astype(f32)
    fnet = _fourier(parts[11])
    return [t.astype(dt) for t in (gla, ret, rwkv, fnet)]


def _merge(h, outs, lp):
    z = 0.0
    for i in range(N_BRANCH):
        gate = jax.nn.sigmoid(h @ lp['w_gate'][i] + lp['b_gate'][i])
        z = z + gate * (outs[i] @ lp['w_br'][i])
    return z @ lp['w_out']


def _hybrid_mixer(h, hc, lp, rope, need_ctx):
    pl = jnp.split(h @ lp['w_in'], N_PARTS, axis=-1)
    pc = jnp.split(hc @ lp['w_in'], N_PARTS, axis=-1)
    o_gla, o_gla_c = _bidir(_gla_scan, _gla_feats(h, pl[0], pl[1], pl[2], lp),
                            _gla_feats(hc, pc[0], pc[1], pc[2], lp))
    o_ret, o_ret_c = _bidir(_ret_scan, _ret_feats(pl[4], pl[5], pl[6], rope),
                            _ret_feats(pc[4], pc[5], pc[6], None))
    rl_dirs, rl_bonus, rl_g = _rwkv_feats(h, pl[8], pl[9], pl[10], lp)
    rc_dirs, rc_bonus, rc_g = _rwkv_feats(hc, pc[8], pc[9], pc[10], lp)
    y_rwkv, y_rwkv_c = _bidir(_rwkv_scan, rl_dirs, rc_dirs)
    y = _merge(h, _branch_outputs(pl, o_gla, o_ret, y_rwkv, rl_bonus, rl_g, lp), lp)
    if not need_ctx:
        return y, None
    yc = _merge(hc, _branch_outputs(pc, o_gla_c, o_ret_c, y_rwkv_c, rc_bonus, rc_g, lp), lp)
    return y, yc


def _conv_ffn(h, lp):
    a, u = jnp.split(h @ lp['ffn_up'], 2, axis=-1)
    a = _dwconv3(a, lp['ffn_conv']) + lp['ffn_conv_b']
    return (jax.nn.silu(a) * u) @ lp['ffn_down']


def _modulation(cvec, w, b):
    return jnp.split(jax.nn.silu(cvec) @ w + b, 6, axis=-1)


def setup_inputs(seed: int = 0) -> dict:
    key = jax.random.key(seed)
    keys = jax.random.split(key, 40)
    L, D, F = DEPTH, D_MODEL, D_FF
    f32 = jnp.float32

    def nrm(i, shape, scale):
        return jax.random.normal(keys[i], shape, f32) * scale

    def gain(i, shape):
        return 1.0 + nrm(i, shape, 0.02)

    conv_base = jnp.array([0.25, 0.5, 0.25], f32)[None, :, None]
    return {
        'x': nrm(0, (BATCH, SEQ, D), 1.0),
        'c': nrm(1, (BATCH, D), 1.0),
        'ctx': nrm(2, (BATCH, CTX_LEN, D), 1.0),
        'c_ctx': nrm(3, (D,), 1.0),
        'w_ada': nrm(4, (L, D, 6 * D), 0.5 * D ** -0.5),
        'b_ada': nrm(5, (L, 6 * D), 0.01),
        'g_norm1': gain(6, (L, D)),
        'g_norm2': gain(7, (L, D)),
        'w_in': nrm(8, (L, D, P_IN), D ** -0.5),
        'gla_wa1': nrm(9, (L, 2, D, GLA_LR), D ** -0.5),
        'gla_wa2': nrm(10, (L, 2, GLA_LR, BR_W), GLA_LR ** -0.5),
        'gla_ba': nrm(11, (L, 2, BR_W), 0.5),
        'gla_gn': gain(12, (L, BR_W)),
        'ret_gn': gain(13, (L, BR_W)),
        'rwkv_conv': conv_base + nrm(14, (L, 3, 3 * BR_W), 0.05),
        'rwkv_mu': jax.random.uniform(keys[15], (L, 3, D), f32),
        'rwkv_w0': nrm(16, (L, 2, BR_W), 1.0),
        'rwkv_w1': nrm(17, (L, 2, D, RWKV_DECAY_LR), D ** -0.5),
        'rwkv_w2': nrm(18, (L, 2, RWKV_DECAY_LR, BR_W), 0.5 * RWKV_DECAY_LR ** -0.5),
        'rwkv_a0': nrm(19, (L, 2, BR_W), 0.5),
        'rwkv_a1': nrm(20, (L, 2, D, RWKV_A_LR), D ** -0.5),
        'rwkv_a2': nrm(21, (L, 2, RWKV_A_LR, BR_W), 0.5 * RWKV_A_LR ** -0.5),
        'rwkv_g1': nrm(22, (L, D, RWKV_G_LR), D ** -0.5),
        'rwkv_g2': nrm(23, (L, RWKV_G_LR, BR_W), RWKV_G_LR ** -0.5),
        'rwkv_kk': 0.85 + nrm(24, (L, BR_W), 0.05),
        'rwkv_ka': gain(25, (L, BR_W)),
        'rwkv_rk': nrm(26, (L, BR_W), 0.1),
        'rwkv_gn': gain(27, (L, BR_W)),
        'w_gate': nrm(28, (L, N_BRANCH, D, D), D ** -0.5),
        'b_gate': nrm(29, (L, N_BRANCH, D), 0.01),
        'w_br': nrm(30, (L, N_BRANCH, BR_W, D), BR_W ** -0.5),
        'w_out': nrm(31, (L, D, D), D ** -0.5),
        'ffn_up': nrm(32, (L, D, 2 * F), D ** -0.5),
        'ffn_conv': conv_base + nrm(33, (L, 3, F), 0.05),
        'ffn_conv_b': nrm(34, (L, F), 0.01),
        'ffn_down': nrm(35, (L, F, D), F ** -0.5),
        'g_final': gain(36, (D,)),
    }


def reference(x, c, ctx, c_ctx, w_ada, b_ada, g_norm1, g_norm2, w_in, gla_wa1, gla_wa2, gla_ba,
              gla_gn, ret_gn, rwkv_conv, rwkv_mu, rwkv_w0, rwkv_w1, rwkv_w2, rwkv_a0, rwkv_a1,
              rwkv_a2, rwkv_g1, rwkv_g2, rwkv_kk, rwkv_ka, rwkv_rk, rwkv_gn, w_gate, b_gate, w_br,
              w_out, ffn_up, ffn_conv, ffn_conv_b, ffn_down, g_final):
    rope = _axial_rope(x.shape[1])
    for l in range(DEPTH):
        last = l == DEPTH - 1
        lp = {
            'w_in': w_in[l], 'gla_wa1': gla_wa1[l], 'gla_wa2': gla_wa2[l], 'gla_ba': gla_ba[l],
            'gla_gn': gla_gn[l], 'ret_gn': ret_gn[l], 'rwkv_conv': rwkv_conv[l], 'rwkv_mu': rwkv_mu[l],
            'rwkv_w0': rwkv_w0[l], 'rwkv_w1': rwkv_w1[l], 'rwkv_w2': rwkv_w2[l],
            'rwkv_a0': rwkv_a0[l], 'rwkv_a1': rwkv_a1[l], 'rwkv_a2': rwkv_a2[l],
            'rwkv_g1': rwkv_g1[l], 'rwkv_g2': rwkv_g2[l], 'rwkv_kk': rwkv_kk[l],
            'rwkv_ka': rwkv_ka[l], 'rwkv_rk': rwkv_rk[l], 'rwkv_gn': rwkv_gn[l],
            'w_gate': w_gate[l], 'b_gate': b_gate[l], 'w_br': w_br[l], 'w_out': w_out[l],
            'ffn_up': ffn_up[l], 'ffn_conv': ffn_conv[l], 'ffn_conv_b': ffn_conv_b[l],
            'ffn_down': ffn_down[l],
        }
        m = [t[:, None, :] for t in _modulation(c, w_ada[l], b_ada[l])]
        mc = _modulation(c_ctx, w_ada[l], b_ada[l])
        h = _rmsnorm(x, g_norm1[l]) * (1.0 + m[1]) + m[0]
        hc = _rmsnorm(ctx, g_norm1[l]) * (1.0 + mc[1]) + mc[0]
        y, yc = _hybrid_mixer(h, hc, lp, rope, not last)
        x = x + m[2] * y
        x = x + m[5] * _conv_ffn(_rmsnorm(x, g_norm2[l]) * (1.0 + m[4]) + m[3], lp)
        if not last:
            ctx = ctx + mc[2] * yc
            ctx = ctx + mc[5] * _conv_ffn(_rmsnorm(ctx, g_norm2[l]) * (1.0 + mc[4]) + mc[3], lp)
    return _rmsnorm(x, g_final)
```

```python
import functools
import math

import numpy as np
import jax
import jax.numpy as jnp
from jax import lax
from jax.experimental import pallas as pl
from jax.experimental.pallas import tpu as pltpu

F32 = jnp.float32
BF16 = jnp.bfloat16

D_MODEL = 1024
BR_W = 256
HEAD_DIM = 64
HEADS = 4
N_PARTS = 12
P_IN = N_PARTS * BR_W
GRID_W = 64
GLA_LR = 16
GLA_NORMALIZER = 16.0
RWKV_LR = 64
RWKV_G_LR = 160
D_FF = 2816
ROPE_BASE = 10000.0
EPS = 1e-6

CHUNK = 64
RET_CHUNK = 128
HALO = 8
NEG_BIG = -1e30

(F_GQ, F_GK, F_GV, F_GG, F_RQ, F_RK, F_RV, F_RG, F_WR, F_WV, F_WKK, F_WG, F_BON,
 F_LAF, F_LAB, F_LWF, F_LWB, F_KDF, F_KDB, F_ASF, F_ASB) = range(21)
N_FEAT = 21

VMEM_LIMIT = 56 * 1024 * 1024


def _cparams(sem):
    return pltpu.CompilerParams(dimension_semantics=sem, vmem_limit_bytes=VMEM_LIMIT)


def _dot(a, b):
    return jnp.dot(a.astype(BF16), b.astype(BF16), preferred_element_type=F32)


def _dot_nt(a, b):
    return lax.dot_general(a.astype(BF16), b.astype(BF16), (((1,), (1,)), ((), ())),
                           preferred_element_type=F32)


def _dot_tn(a, b):
    return lax.dot_general(a.astype(BF16), b.astype(BF16), (((0,), (0,)), ((), ())),
                           preferred_element_type=F32)


def _split3(x):
    hi = x.astype(BF16)
    r1 = x - hi.astype(F32)
    mid = r1.astype(BF16)
    lo = (r1 - mid.astype(F32)).astype(BF16)
    return hi, mid, lo


def _dot_exact_lhs(a_bf16, x):
    hi, mid, lo = _split3(x)
    d = lambda t: jnp.dot(a_bf16, t, preferred_element_type=F32)
    return d(hi) + d(mid) + d(lo)


def _dot_exact_rhs(x, b_bf16):
    hi, mid, lo = _split3(x)
    d = lambda t: jnp.dot(t, b_bf16, preferred_element_type=F32)
    return d(hi) + d(mid) + d(lo)


def _sigmoid(x):
    return 1.0 / (1.0 + jnp.exp(-x))


def _silu(x):
    return x * _sigmoid(x)


def _softplus(x):
    return jnp.maximum(x, 0.0) + jnp.log(1.0 + jnp.exp(-jnp.abs(x)))


def _iota(shape, dim):
    return lax.broadcasted_iota(jnp.int32, shape, dim)


def _head_ones():
    r = _iota((BR_W, BR_W), 0) // HEAD_DIM
    c = _iota((BR_W, BR_W), 1) // HEAD_DIM
    return jnp.where(r == c, 1.0, 0.0).astype(BF16)


def _same_head(n_rows, rows_per_head):
    r = _iota((n_rows, BR_W), 0) // rows_per_head
    c = _iota((n_rows, BR_W), 1) // HEAD_DIM
    return r == c


def _stack_heads(x):
    c = x.shape[0]
    xs = jnp.concatenate([x] * HEADS, axis=0)
    return jnp.where(_same_head(HEADS * c, c), xs, 0.0)


def _unstack_sum(y, c):
    out = y[0:c]
    for h in range(1, HEADS):
        out = out + y[h * c:(h + 1) * c]
    return out


def _tri(c, rev, strict=False):
    r = _iota((c, c), 0)
    s = _iota((c, c), 1)
    if rev:
        return (r < s) if strict else (r <= s)
    return (r > s) if strict else (r >= s)


def _tri_blockdiag(c, rev, strict):
    n = HEADS * c
    r = _iota((n, n), 0)
    s = _iota((n, n), 1)
    same = (r // c) == (s // c)
    rr = r % c
    ss = s % c
    if rev:
        t = (rr < ss) if strict else (rr <= ss)
    else:
        t = (rr > ss) if strict else (rr >= ss)
    return jnp.logical_and(same, t)


def _halo_keep(n_ext, tb, i, nt):
    row = _iota((n_ext, 1), 0)
    first = jnp.where(i > 0, 1.0, 0.0)
    last = jnp.where(i < nt - 1, 1.0, 0.0)
    return jnp.where(row < HALO, first, jnp.where(row >= tb + HALO, last, 1.0))


def _rms_rows(x):
    return x * lax.rsqrt(jnp.mean(x * x, axis=-1, keepdims=True) + EPS)


def _mod_kernel(c_ref, w_ref, b_ref, o_ref):
    s = _silu(c_ref[...])
    o_ref[0] = _dot(s, w_ref[0]) + b_ref[0]


def _modulation(cmat, w_ada, b_ada):
    L = w_ada.shape[0]
    nblk = 1536
    return pl.pallas_call(
        _mod_kernel,
        grid=(L, 6 * D_MODEL // nblk),
        in_specs=[
            pl.BlockSpec((16, D_MODEL), lambda l, j: (0, 0)),
            pl.BlockSpec((1, D_MODEL, nblk), lambda l, j: (l, 0, j)),
            pl.BlockSpec((1, 1, nblk), lambda l, j: (l, 0, j)),
        ],
        out_specs=pl.BlockSpec((1, 16, nblk), lambda l, j: (l, 0, j)),
        out_shape=jax.ShapeDtypeStruct((L, 16, 6 * D_MODEL), F32),
        compiler_params=_cparams(("arbitrary", "arbitrary")),
        name="mod",
    )(cmat, w_ada, b_ada.reshape(L, 1, 6 * D_MODEL))


def _pre_kernel(use_rope, tb, nt,
                x_ref, xp_ref, xn_ref, mod_ref, g1_ref, win_ref, wh_ref, wxw_ref, wxa_ref,
                wxg_ref, w2_ref, vec_ref, conv_ref, mu_ref, dft_ref, cos_ref, sin_ref,
                f_ref, hb_ref, gf_ref, hs_ref, ps_ref):
    i = pl.program_id(1)
    n_ext = tb + 2 * HALO
    x_ext = jnp.concatenate([xp_ref[0], x_ref[0], xn_ref[0]], axis=0)
    mod = mod_ref[0]
    shift = mod[:, 0:D_MODEL]
    scale = mod[:, D_MODEL:2 * D_MODEL]
    h_ext = (_rms_rows(x_ext) * g1_ref[...]) * (1.0 + scale) + shift
    h_ext = h_ext * _halo_keep(n_ext, tb, i, nt)
    hs_ref[...] = h_ext
    hb_ext = h_ext.astype(BF16)
    ps_ref[...] = jnp.dot(hb_ext, win_ref[...], preferred_element_type=F32)

    h = hs_ref[HALO:HALO + tb, :]
    prev = hs_ref[HALO - 1:HALO - 1 + tb, :]
    nxt = hs_ref[HALO + 1:HALO + 1 + tb, :]
    hb = h.astype(BF16)
    hb_ref[0] = hb
    xx = 0.5 * (prev + nxt) - h
    mu = mu_ref[...]
    xw = h + xx * mu[0:1]
    xa = h + xx * mu[1:2]
    xg = h + xx * mu[2:3]

    vec = vec_ref[...]
    ba_f, ba_b, w0_f, w0_b = vec[0:1], vec[1:2], vec[2:3], vec[3:4]
    a0_f, a0_b, kkw, kaw, rkw = vec[4:5], vec[5:6], vec[6:7], vec[7:8], vec[8:9]

    def col(j):
        return ps_ref[HALO:HALO + tb, j * BR_W:(j + 1) * BR_W]

    def put(j, val):
        f_ref[0, :, j * BR_W:(j + 1) * BR_W] = val

    put(F_GQ, col(0) * HEAD_DIM ** -0.5)
    put(F_GK, col(1))
    put(F_GV, col(2))
    put(F_GG, col(3))
    zh = jnp.dot(hb, wh_ref[...], preferred_element_type=F32)
    for dr, (ba, dst) in enumerate(((ba_f, F_LAF), (ba_b, F_LAB))):
        z = _dot(zh, w2_ref[dr, 0:128, :]) + ba
        put(dst, -_softplus(-z) * (1.0 / GLA_NORMALIZER))

    rq = col(4)
    rk = col(5) * HEAD_DIM ** -0.5
    if use_rope:
        cosf = cos_ref[...]
        sins = sin_ref[...]
        low = (_iota((tb, BR_W), 1) % HEAD_DIM) < (HEAD_DIM // 2)

        def rope(t):
            partner = jnp.where(low, pltpu.roll(t, BR_W - HEAD_DIM // 2, 1),
                                pltpu.roll(t, HEAD_DIM // 2, 1))
            return t * cosf + partner * sins

        rq = rope(rq)
        rk = rope(rk)
    put(F_RQ, rq)
    put(F_RK, rk)
    put(F_RV, col(6))
    put(F_RG, col(7))

    conv = conv_ref[...]

    def dwconv(j, part):
        lo, hi = (8 + part) * BR_W, (9 + part) * BR_W
        cw = conv[:, part * BR_W:(part + 1) * BR_W]
        return (ps_ref[HALO - 1:HALO - 1 + tb, lo:hi] * cw[0:1]
                + ps_ref[HALO:HALO + tb, lo:hi] * cw[1:2]
                + ps_ref[HALO + 1:HALO + 1 + tb, lo:hi] * cw[2:3])

    r_c = dwconv(8, 0)
    k_c = dwconv(9, 1)
    v_c = dwconv(10, 2)
    ones_h = _head_ones()
    g = _dot(_sigmoid(jnp.dot(xg.astype(BF16), wxg_ref[...], preferred_element_type=F32)),
             w2_ref[6])
    kk = k_c * kkw
    kk = kk * lax.rsqrt(_dot_exact_rhs(kk * kk, ones_h) + EPS)
    tw = jnp.tanh(jnp.dot(xw.astype(BF16), wxw_ref[...], preferred_element_type=F32))
    ta = jnp.dot(xa.astype(BF16), wxa_ref[...], preferred_element_type=F32)
    bonus = jnp.zeros((tb, BR_W), F32)
    for dr, (w0, a0, d_lw, d_kd, d_as) in enumerate(
            ((w0_f, a0_f, F_LWF, F_KDF, F_ASF), (w0_b, a0_b, F_LWB, F_KDB, F_ASB))):
        w_raw = -_softplus(-(w0 + _dot(tw, w2_ref[2 + dr, 0:128, :]))) - 0.5
        put(d_lw, -jnp.exp(w_raw))
        a_sig = _sigmoid(a0 + _dot(ta, w2_ref[4 + dr, 0:128, :]))
        kd = k_c * (1.0 + (a_sig - 1.0) * kaw)
        put(d_kd, kd)
        put(d_as, a_sig)
        bonus = bonus + _dot_exact_rhs(r_c * kd * rkw, ones_h) * v_c
    put(F_WR, r_c)
    put(F_WV, v_c)
    put(F_WKK, kk)
    put(F_WG, g)
    put(F_BON, bonus)

    fb = col(11).astype(BF16)
    gf_ref[0, :, 0:BR_W] = jnp.dot(fb, dft_ref[0], preferred_element_type=F32).astype(BF16)
    gf_ref[0, :, BR_W:2 * BR_W] = jnp.dot(fb, dft_ref[1], preferred_element_type=F32).astype(BF16)


def _pre(x, mods, lw, use_rope, rope_tabs, dft_c):
    B, T, D = x.shape
    tb = min(256, T)
    nt = T // tb
    hb8 = tb // HALO
    n_h = T // HALO
    const2 = lambda b, i: (0, 0)
    const3 = lambda b, i: (0, 0, 0)
    kern = functools.partial(_pre_kernel, use_rope, tb, nt)
    return pl.pallas_call(
        kern,
        grid=(B, nt),
        in_specs=[
            pl.BlockSpec((1, tb, D), lambda b, i: (b, i, 0)),
            pl.BlockSpec((1, HALO, D), lambda b, i: (b, jnp.maximum(i * hb8 - 1, 0), 0)),
            pl.BlockSpec((1, HALO, D), lambda b, i: (b, jnp.minimum((i + 1) * hb8, n_h - 1), 0)),
            pl.BlockSpec((1, 1, 6 * D), lambda b, i: (b, 0, 0)),
            pl.BlockSpec((1, D), const2),
            pl.BlockSpec((D, P_IN), const2),
            pl.BlockSpec((D, 128), const2),
            pl.BlockSpec((D, 128), const2),
            pl.BlockSpec((D, 128), const2),
            pl.BlockSpec((D, 256), const2),
            pl.BlockSpec((7, 256, BR_W), const3),
            pl.BlockSpec((16, BR_W), const2),
            pl.BlockSpec((3, 3 * BR_W), const2),
            pl.BlockSpec((3, D), const2),
            pl.BlockSpec((2, BR_W, BR_W), const3),
            pl.BlockSpec((tb, BR_W), lambda b, i: (i, 0)),
            pl.BlockSpec((tb, BR_W), lambda b, i: (i, 0)),
        ],
        out_specs=[
            pl.BlockSpec((1, tb, N_FEAT * BR_W), lambda b, i: (b, i, 0)),
            pl.BlockSpec((1, tb, D), lambda b, i: (b, i, 0)),
            pl.BlockSpec((1, tb, 2 * BR_W), lambda b, i: (b, i, 0)),
        ],
        out_shape=[
            jax.ShapeDtypeStruct((B, T, N_FEAT * BR_W), F32),
            jax.ShapeDtypeStruct((B, T, D), BF16),
            jax.ShapeDtypeStruct((B, T, 2 * BR_W), BF16),
        ],
        scratch_shapes=[
            pltpu.VMEM((tb + 2 * HALO, D), F32),
            pltpu.VMEM((tb + 2 * HALO, P_IN), F32),
        ],
        compiler_params=_cparams(("parallel", "arbitrary")),
        name="pre",
    )(x, x, x, mods, lw["g1"], lw["w_in"], lw["w_h"], lw["w_xw"], lw["w_xa"], lw["w_xg"],
      lw["w2"], lw["vec"], lw["rwkv_conv"], lw["rwkv_mu"], dft_c, rope_tabs[0], rope_tabs[1])


def _cumsum_chunk(x, rev):
    c = x.shape[0]
    tri = jnp.where(_tri(c, rev), 1.0, 0.0).astype(BF16)
    return _dot_exact_lhs(tri, x)


def _gla_chunk(q, k, v, la, st, rev):
    c = q.shape[0]
    cum = _cumsum_chunk(la, rev)
    cum_end = cum[0:1] if rev else cum[c - 1:c]
    rowid = _iota((c, 1), 0)
    pieces = []
    spans = []
    for j in range(c):
        if rev:
            lo, hi = 0, (j // 8) * 8 + 8
        else:
            lo, hi = (j // 8) * 8, c
        valid = (rowid[lo:hi] <= j) if rev else (rowid[lo:hi] >= j)
        arg = jnp.where(valid, cum[lo:hi] - cum[j:j + 1], NEG_BIG)
        pieces.append(q[lo:hi] * (k[j:j + 1] * jnp.exp(arg)))
        spans.append((lo, hi))
    red = jnp.dot(jnp.concatenate(pieces, axis=0).astype(BF16), _head_ones(),
                  preferred_element_type=F32)
    groups = [None] * (c // 8)
    off = 0
    for j in range(c):
        lo, hi = spans[j]
        contrib = red[off:off + (hi - lo)] * v[j:j + 1]
        off += hi - lo
        for gi in range(lo // 8, hi // 8):
            piece = contrib[(gi * 8 - lo):(gi * 8 - lo + 8)]
            groups[gi] = piece if groups[gi] is None else groups[gi] + piece
    o = jnp.concatenate(groups, axis=0)
    o = o + _dot_nt(q * jnp.exp(cum), st)
    bd = (_iota((BR_W, BR_W), 0) // HEAD_DIM) == (_iota((BR_W, BR_W), 1) // HEAD_DIM)
    st_new = st * jnp.exp(cum_end) + jnp.where(bd, _dot_tn(v, k * jnp.exp(cum_end - cum)), 0.0)
    return o, st_new


def _ret_chunk(q, k, v, s, dmat, qdec, kdec, cdec):
    sc = _dot_nt(q, _stack_heads(k)) * dmat
    o = _dot(sc, _stack_heads(v)) + _dot(q * qdec, s)
    bd = (_iota((BR_W, BR_W), 0) // HEAD_DIM) == (_iota((BR_W, BR_W), 1) // HEAD_DIM)
    s_new = s * cdec + jnp.where(bd, _dot_tn(k * kdec, v), 0.0)
    return o, s_new


def _rwkv_chunk(r, v, kk, kd, asig, lw, s, rev):
    c = r.shape[0]
    n = HEADS * c
    cum = _cumsum_chunk(lw, rev)
    cum_end = cum[0:1] if rev else cum[c - 1:c]
    einv = jnp.exp(-cum)
    ehat = jnp.exp(cum_end - cum)
    bvec = kk * asig
    a_s = _stack_heads(-kk * jnp.exp(cum - lw))
    r_s = _stack_heads(r * jnp.exp(cum))
    b_s = _stack_heads(bvec * einv)
    k_s = _stack_heads(kd * einv)
    v_s = _stack_heads(v)
    bh_s = _stack_heads(bvec * ehat)
    kh_s = _stack_heads(kd * ehat)
    m = _dot_nt(jnp.concatenate([a_s, r_s], axis=0), jnp.concatenate([b_s, k_s], axis=0))
    strict = _tri_blockdiag(c, rev, True)
    incl = _tri_blockdiag(c, rev, False)
    l_ab = jnp.where(strict, m[0:n, 0:n], 0.0)
    m_ak = jnp.where(strict, m[0:n, n:2 * n], 0.0)
    n_rb = jnp.where(incl, m[n:2 * n, 0:n], 0.0)
    n_rk = jnp.where(incl, m[n:2 * n, n:2 * n], 0.0)
    eye = jnp.where(_iota((n, n), 0) == _iota((n, n), 1), 1.0, 0.0)
    t_inv = eye + l_ab
    p = l_ab
    steps = int(math.log2(c)) - 1
    for _ in range(steps):
        p = _dot(p, p)
        t_inv = t_inv + _dot(t_inv, p)
    w = _dot_nt(a_s, s) + _dot(m_ak, v_s)
    u = _dot(t_inv, w)
    y = _unstack_sum(_dot_nt(r_s, s) + _dot(n_rb, u) + _dot(n_rk, v_s), c)
    s_new = s * jnp.exp(cum_end) + _dot_tn(jnp.concatenate([u, v_s], axis=0),
                                           jnp.concatenate([bh_s, kh_s], axis=0))
    return y, s_new


def _scan_kernel(kind, n_in, n_tab, nc, *refs):
    fwd = refs[0:n_in]
    bwd = refs[n_in:2 * n_in]
    tabs = refs[2 * n_in:2 * n_in + n_tab]
    s0_ref = refs[2 * n_in + n_tab]
    of_ref, ob_ref, sf_ref, st_ref = refs[2 * n_in + n_tab + 1:]
    i = pl.program_id(1)

    @pl.when(i == 0)
    def _():
        st_ref[...] = s0_ref[:, 0]

    for dr, (ins, o_ref) in enumerate(((fwd, of_ref), (bwd, ob_ref))):
        args = [t[0] for t in ins]
        s = st_ref[dr]
        if kind == "gla":
            o, s_new = _gla_chunk(*args, s, dr == 1)
        elif kind == "ret":
            dmat, qdec, kdec, cdec = (t[dr] for t in tabs)
            o, s_new = _ret_chunk(*args, s, dmat, qdec, kdec, cdec)
        else:
            o, s_new = _rwkv_chunk(*args, s, dr == 1)
        o_ref[0] = o
        st_ref[dr] = s_new

    @pl.when(i == nc - 1)
    def _():
        sf_ref[:, 0] = st_ref[...]


def _scan(kind, feats, cols_f, cols_b, s0, tabs=()):
    B, T, _ = feats.shape
    c = RET_CHUNK if kind == "ret" else CHUNK
    c = min(c, T)
    nc = T // c
    n_in = len(cols_f)
    in_specs = []
    for j in cols_f:
        in_specs.append(pl.BlockSpec((1, c, BR_W), lambda b, i, j=j: (b, i, j)))
    for j in cols_b:
        in_specs.append(pl.BlockSpec((1, c, BR_W), lambda b, i, j=j: (b, nc - 1 - i, j)))
    for t in tabs:
        in_specs.append(pl.BlockSpec(t.shape, lambda b, i, nd=t.ndim: (0,) * nd))
    in_specs.append(pl.BlockSpec((2, 1, BR_W, BR_W), lambda b, i: (0, b, 0, 0)))
    kern = functools.partial(_scan_kernel, kind, n_in, len(tabs), nc)
    return pl.pallas_call(
        kern,
        grid=(B, nc),
        in_specs=in_specs,
        out_specs=[
            pl.BlockSpec((1, c, BR_W), lambda b, i: (b, i, 0)),
            pl.BlockSpec((1, c, BR_W), lambda b, i: (b, nc - 1 - i, 0)),
            pl.BlockSpec((2, 1, BR_W, BR_W), lambda b, i: (0, b, 0, 0)),
        ],
        out_shape=[
            jax.ShapeDtypeStruct((B, T, BR_W), F32),
            jax.ShapeDtypeStruct((B, T, BR_W), F32),
            jax.ShapeDtypeStruct((2, B, BR_W, BR_W), F32),
        ],
        scratch_shapes=[pltpu.VMEM((2, BR_W, BR_W), F32)],
        compiler_params=_cparams(("parallel", "arbitrary")),
        name=kind,
    )(*([feats] * (2 * n_in)), *tabs, s0)


def _ret_tables(c):
    pos = np.arange(c, dtype=np.float64)
    lane_head = np.arange(BR_W) // HEAD_DIM
    dmats, qd, kd, cd = [], [], [], []
    for dr in range(2):
        expo = -5.0 - np.arange(HEADS, dtype=np.float64)
        if dr == 1:
            expo = expo[::-1]
        log_g = np.log1p(-np.exp2(expo))
        if dr == 0:
            rel = pos[:, None] - pos[None, :]
            qpow = pos + 1.0
            kpow = c - 1.0 - pos
        else:
            rel = pos[None, :] - pos[:, None]
            qpow = c - pos
            kpow = pos
        tri = rel >= 0
        dm = np.where(tri[None], np.exp(np.where(tri, rel, 0.0)[None] * log_g[:, None, None]), 0.0)
        dmats.append(np.concatenate([dm[h] for h in range(HEADS)], axis=1))
        qd.append(np.exp(qpow[:, None] * log_g[lane_head][None, :]))
        kd.append(np.exp(kpow[:, None] * log_g[lane_head][None, :]))
        cd.append(np.exp(c * log_g[lane_head])[None, :])
    f = lambda xs: jnp.asarray(np.stack(xs), dtype=F32)
    return f(dmats), f(qd), f(kd), f(cd)


def _fnet_kernel(t_len, cs_ref, g_ref, o_ref):
    o_ref[0] = (jnp.dot(cs_ref[:, 0:t_len], g_ref[0, :, 0:BR_W], preferred_element_type=F32)
                + jnp.dot(cs_ref[:, t_len:2 * t_len], g_ref[0, :, BR_W:2 * BR_W],
                          preferred_element_type=F32))


def _fnet(gf, cs):
    B, T, _ = gf.shape
    tb = min(512, T)
    return pl.pallas_call(
        functools.partial(_fnet_kernel, T),
        grid=(T // tb, B),
        in_specs=[
            pl.BlockSpec((tb, 2 * T), lambda i, b: (i, 0)),
            pl.BlockSpec((1, T, 2 * BR_W), lambda i, b: (b, 0, 0)),
        ],
        out_specs=pl.BlockSpec((1, tb, BR_W), lambda i, b: (b, i, 0)),
        out_shape=jax.ShapeDtypeStruct((B, T, BR_W), F32),
        compiler_params=_cparams(("arbitrary", "arbitrary")),
        name="fnet",
    )(cs, gf)


def _dft_tables(T):
    t = jnp.arange(T, dtype=jnp.int32)
    ang = ((t[:, None] * t[None, :]) % T).astype(F32) * (2.0 * math.pi / T)
    cs = jnp.concatenate([jnp.cos(ang), -jnp.sin(ang)], axis=1).astype(BF16)
    ch = np.arange(HEAD_DIM)
    a64 = 2.0 * np.pi * ((ch[:, None] * ch[None, :]) % HEAD_DIM) / HEAD_DIM
    scale = (T * HEAD_DIM) ** -0.5
    eye = np.eye(HEADS)
    cbd = np.kron(eye, np.cos(a64)) * scale
    sbd = np.kron(eye, np.sin(a64)) * scale
    return cs, jnp.asarray(np.stack([cbd, sbd]), dtype=BF16)


def _head_norm(o, ones_h, center):
    if center:
        o = o - _dot_exact_rhs(o, ones_h) * (1.0 / HEAD_DIM)
    var = _dot_exact_rhs(o * o, ones_h) * (1.0 / HEAD_DIM)
    return o * lax.rsqrt(var + EPS)


def _merge_kernel(x_ref, hb_ref, mod_ref, gg_ref, rg_ref, wg_ref, bon_ref,
                  ogf_ref, ogb_ref, orf_ref, orb_ref, oyf_ref, oyb_ref, fn_ref,
                  gn_ref, wgate_ref, bgate_ref, wbr_ref, wout_ref, o_ref):
    ones_h = _head_ones()
    gn = gn_ref[...]
    gla = _head_norm(ogf_ref[0] + ogb_ref[0], ones_h, False) * gn[0:1] * _silu(gg_ref[0])
    ret = _head_norm(orf_ref[0] + orb_ref[0], ones_h, True) * gn[1:2] * _silu(rg_ref[0])
    rwkv = (_head_norm(oyf_ref[0] + oyb_ref[0], ones_h, True) * gn[2:3] + bon_ref[0]) * wg_ref[0]
    outs = (gla, ret, rwkv, fn_ref[0])
    hb = hb_ref[0]
    z = None
    for br in range(4):
        gate = _sigmoid(jnp.dot(hb, wgate_ref[br], preferred_element_type=F32) + bgate_ref[br])
        term = gate * _dot(outs[br], wbr_ref[br])
        z = term if z is None else z + term
    y = _dot(z, wout_ref[...])
    gate1 = mod_ref[0][:, 2 * D_MODEL:3 * D_MODEL]
    o_ref[0] = x_ref[0] + gate1 * y


def _merge(x, hb, mods, feats, scans, fnet_out, lw):
    B, T, D = x.shape
    tb = min(256, T)
    tok = lambda w: pl.BlockSpec((1, tb, w), lambda b, i: (b, i, 0))
    fcol = lambda j: pl.BlockSpec((1, tb, BR_W), lambda b, i, j=j: (b, i, j))
    const2 = lambda b, i: (0, 0)
    const3 = lambda b, i: (0, 0, 0)
    return pl.pallas_call(
        _merge_kernel,
        grid=(B, T // tb),
        in_specs=[
            tok(D), tok(D),
            pl.BlockSpec((1, 1, 6 * D), lambda b, i: (b, 0, 0)),
            fcol(F_GG), fcol(F_RG), fcol(F_WG), fcol(F_BON),
            tok(BR_W), tok(BR_W), tok(BR_W), tok(BR_W), tok(BR_W), tok(BR_W), tok(BR_W),
            pl.BlockSpec((3, BR_W), const2),
            pl.BlockSpec((4, D, D), const3),
            pl.BlockSpec((4, 1, D), const3),
            pl.BlockSpec((4, BR_W, D), const3),
            pl.BlockSpec((D, D), const2),
        ],
        out_specs=tok(D),
        out_shape=jax.ShapeDtypeStruct((B, T, D), F32),
        compiler_params=_cparams(("parallel", "arbitrary")),
        name="merge",
    )(x, hb, mods, feats, feats, feats, feats, *scans, fnet_out,
      lw["gn"], lw["w_gate"], lw["b_gate"], lw["w_br"], lw["w_out"])


def _ffn_kernel(final, tb, nt, nk,
                x_ref, xp_ref, xn_ref, mod_ref, g2_ref, upa_ref, upu_ref, cw_ref, cb_ref,
                down_ref, gfin_ref, o_ref, h2_ref, acc_ref):
    i = pl.program_id(1)
    kf = pl.program_id(2)
    n_ext = tb + 2 * HALO

    @pl.when(kf == 0)
    def _():
        x_ext = jnp.concatenate([xp_ref[0], x_ref[0], xn_ref[0]], axis=0)
        mod = mod_ref[0]
        shift = mod[:, 3 * D_MODEL:4 * D_MODEL]
        scale = mod[:, 4 * D_MODEL:5 * D_MODEL]
        h2 = (_rms_rows(x_ext) * g2_ref[...]) * (1.0 + scale) + shift
        h2_ref[...] = (h2 * _halo_keep(n_ext, tb, i, nt)).astype(BF16)
        acc_ref[...] = jnp.zeros_like(acc_ref)

    h2 = h2_ref[...]
    a_ext = jnp.dot(h2, upa_ref[...], preferred_element_type=F32)
    u = jnp.dot(h2, upu_ref[...], preferred_element_type=F32)[HALO:HALO + tb]
    cw = cw_ref[...]
    a = (a_ext[HALO - 1:HALO - 1 + tb] * cw[0:1] + a_ext[HALO:HALO + tb] * cw[1:2]
         + a_ext[HALO + 1:HALO + 1 + tb] * cw[2:3] + cb_ref[...])
    acc_ref[...] += _dot(_silu(a) * u, down_ref[...])

    @pl.when(kf == nk - 1)
    def _():
        gate2 = mod_ref[0][:, 5 * D_MODEL:6 * D_MODEL]
        res = x_ref[0] + gate2 * acc_ref[...]
        if final:
            res = _rms_rows(res) * gfin_ref[...]
        o_ref[0] = res


def _ffn(x, mods, lw, g_final, final):
    B, T, D = x.shape
    tb = min(512, T)
    nt = T // tb
    fb = D_FF // 2
    nk = D_FF // fb
    hb8 = tb // HALO
    n_h = T // HALO
    const2 = lambda b, i, k: (0, 0)
    kern = functools.partial(_ffn_kernel, final, tb, nt, nk)
    return pl.pallas_call(
        kern,
        grid=(B, nt, nk),
        in_specs=[
            pl.BlockSpec((1, tb, D), lambda b, i, k: (b, i, 0)),
            pl.BlockSpec((1, HALO, D), lambda b, i, k: (b, jnp.maximum(i * hb8 - 1, 0), 0)),
            pl.BlockSpec((1, HALO, D), lambda b, i, k: (b, jnp.minimum((i + 1) * hb8, n_h - 1), 0)),
            pl.BlockSpec((1, 1, 6 * D), lambda b, i, k: (b, 0, 0)),
            pl.BlockSpec((1, D), const2),
            pl.BlockSpec((D, fb), lambda b, i, k: (0, k)),
            pl.BlockSpec((D, fb), lambda b, i, k: (0, nk + k)),
            pl.BlockSpec((3, fb), lambda b, i, k: (0, k)),
            pl.BlockSpec((1, fb), lambda b, i, k: (0, k)),
            pl.BlockSpec((fb, D), lambda b, i, k: (k, 0)),
            pl.BlockSpec((1, D), const2),
        ],
        out_specs=pl.BlockSpec((1, tb, D), lambda b, i, k: (b, i, 0)),
        out_shape=jax.ShapeDtypeStruct((B, T, D), F32),
        scratch_shapes=[
            pltpu.VMEM((tb + 2 * HALO, D), BF16),
            pltpu.VMEM((tb, D), F32),
        ],
        compiler_params=_cparams(("parallel", "arbitrary", "arbitrary")),
        name="ffn",
    )(x, x, x, mods, lw["g2"], lw["ffn_up"], lw["ffn_up"], lw["ffn_conv"], lw["ffn_conv_b"],
      lw["ffn_down"], g_final)


def _rope_tables(T):
    rows = T // GRID_W
    row = jnp.repeat(jnp.arange(rows, dtype=F32), GRID_W)
    colp = jnp.tile(jnp.arange(GRID_W, dtype=F32), rows)
    n_freq = HEAD_DIM // 4
    inv = ROPE_BASE ** (-jnp.arange(n_freq, dtype=F32) / n_freq)
    ang = jnp.concatenate([row[:, None] * inv, colp[:, None] * inv], axis=-1)
    cos, sin = jnp.cos(ang), jnp.sin(ang)
    cos_h = jnp.concatenate([cos, cos], axis=-1)
    sin_h = jnp.concatenate([-sin, sin], axis=-1)
    return jnp.tile(cos_h, (1, HEADS)), jnp.tile(sin_h, (1, HEADS))


def _pad_rows(w, lo, n):
    return jnp.zeros((n, w.shape[1]), w.dtype).at[lo:lo + w.shape[0]].set(w)


def _layer_weights(l, p):
    D = D_MODEL
    bf = lambda t: t.astype(BF16)
    zcols = lambda w, n: jnp.concatenate([w, jnp.zeros((w.shape[0], n - w.shape[1]), w.dtype)], axis=1)
    w_h = zcols(jnp.concatenate([p["gla_wa1"][l, 0], p["gla_wa1"][l, 1]], axis=1), 128)
    w_xw = jnp.concatenate([p["rwkv_w1"][l, 0], p["rwkv_w1"][l, 1]], axis=1)
    w_xa = jnp.concatenate([p["rwkv_a1"][l, 0], p["rwkv_a1"][l, 1]], axis=1)
    w_xg = zcols(p["rwkv_g1"][l], 256)
    w2 = jnp.stack([
        _pad_rows(p["gla_wa2"][l, 0], 0, 256), _pad_rows(p["gla_wa2"][l, 1], GLA_LR, 256),
        _pad_rows(p["rwkv_w2"][l, 0], 0, 256), _pad_rows(p["rwkv_w2"][l, 1], RWKV_LR, 256),
        _pad_rows(p["rwkv_a2"][l, 0], 0, 256), _pad_rows(p["rwkv_a2"][l, 1], RWKV_LR, 256),
        _pad_rows(p["rwkv_g2"][l], 0, 256),
    ])
    vec = jnp.stack([p["gla_ba"][l, 0], p["gla_ba"][l, 1], p["rwkv_w0"][l, 0], p["rwkv_w0"][l, 1],
                     p["rwkv_a0"][l, 0], p["rwkv_a0"][l, 1], p["rwkv_kk"][l], p["rwkv_ka"][l],
                     p["rwkv_rk"][l]])
    vec = jnp.concatenate([vec, jnp.zeros((16 - vec.shape[0], BR_W), F32)], axis=0)
    return {
        "g1": p["g_norm1"][l].reshape(1, D), "g2": p["g_norm2"][l].reshape(1, D),
        "w_in": bf(p["w_in"][l]), "w_h": bf(w_h), "w_xw": bf(w_xw), "w_xa": bf(w_xa),
        "w_xg": bf(w_xg), "w2": bf(w2), "vec": vec,
        "rwkv_conv": p["rwkv_conv"][l], "rwkv_mu": p["rwkv_mu"][l],
        "gn": jnp.stack([p["gla_gn"][l], p["ret_gn"][l], p["rwkv_gn"][l]]),
        "w_gate": bf(p["w_gate"][l]), "b_gate": p["b_gate"][l].reshape(4, 1, D),
        "w_br": bf(p["w_br"][l]), "w_out": bf(p["w_out"][l]),
        "ffn_up": bf(p["ffn_up"][l]), "ffn_conv": p["ffn_conv"][l],
        "ffn_conv_b": p["ffn_conv_b"][l].reshape(1, D_FF), "ffn_down": bf(p["ffn_down"][l]),
    }


_GLA_F = (F_GQ, F_GK, F_GV, F_LAF)
_GLA_B = (F_GQ, F_GK, F_GV, F_LAB)
_RET = (F_RQ, F_RK, F_RV)
_RWKV_F = (F_WR, F_WV, F_WKK, F_KDF, F_ASF, F_LWF)
_RWKV_B = (F_WR, F_WV, F_WKK, F_KDB, F_ASB, F_LWB)


def _mixers(feats_c, feats_l, ret_tabs_c, ret_tabs_l, need_ctx):
    B = feats_l.shape[0]
    zero = jnp.zeros((2, B, BR_W, BR_W), F32)
    out_c, out_l = [], []
    for kind, cf, cb, tabs in (("gla", _GLA_F, _GLA_B, None), ("ret", _RET, _RET, True),
                               ("rwkv", _RWKV_F, _RWKV_B, None)):
        tc = ret_tabs_c if tabs else ()
        tl = ret_tabs_l if tabs else ()
        of_c, ob_c, s_c = _scan(kind, feats_c, cf, cb, zero, tc)
        of_l, ob_l, _ = _scan(kind, feats_l, cf, cb, s_c, tl)
        out_c += [of_c, ob_c]
        out_l += [of_l, ob_l]
    return out_l, (out_c if need_ctx else None)


def kernel(x, c, ctx, c_ctx, w_ada, b_ada, g_norm1, g_norm2, w_in, gla_wa1, gla_wa2, gla_ba, gla_gn, ret_gn, rwkv_conv, rwkv_mu, rwkv_w0, rwkv_w1, rwkv_w2, rwkv_a0, rwkv_a1, rwkv_a2, rwkv_g1, rwkv_g2, rwkv_kk, rwkv_ka, rwkv_rk, rwkv_gn, w_gate, b_gate, w_br, w_out, ffn_up, ffn_conv, ffn_conv_b, ffn_down, g_final):
    p = dict(g_norm1=g_norm1, g_norm2=g_norm2, w_in=w_in, gla_wa1=gla_wa1, gla_wa2=gla_wa2,
             gla_ba=gla_ba, gla_gn=gla_gn, ret_gn=ret_gn, rwkv_conv=rwkv_conv, rwkv_mu=rwkv_mu,
             rwkv_w0=rwkv_w0, rwkv_w1=rwkv_w1, rwkv_w2=rwkv_w2, rwkv_a0=rwkv_a0,
             rwkv_a1=rwkv_a1, rwkv_a2=rwkv_a2, rwkv_g1=rwkv_g1, rwkv_g2=rwkv_g2,
             rwkv_kk=rwkv_kk, rwkv_ka=rwkv_ka, rwkv_rk=rwkv_rk, rwkv_gn=rwkv_gn,
             w_gate=w_gate, b_gate=b_gate, w_br=w_br, w_out=w_out, ffn_up=ffn_up,
             ffn_conv=ffn_conv, ffn_conv_b=ffn_conv_b, ffn_down=ffn_down)
    B, T, D = x.shape
    Tc = ctx.shape[1]
    depth = w_ada.shape[0]

    cmat = jnp.concatenate([c, c_ctx[None, :], jnp.zeros((16 - B - 1, D), F32)], axis=0)
    mod_all = _modulation(cmat, w_ada, b_ada)

    rope_l = _rope_tables(T)
    rope_c = (jnp.zeros((Tc, BR_W), F32), jnp.zeros((Tc, BR_W), F32))
    cs_l, dft_l = _dft_tables(T)
    cs_c, dft_c = _dft_tables(Tc)
    ret_tabs_l = _ret_tables(min(RET_CHUNK, T))
    ret_tabs_c = _ret_tables(min(RET_CHUNK, Tc))
    g_fin = g_final.reshape(1, D)

    for l in range(depth):
        last = l == depth - 1
        lw = _layer_weights(l, p)
        mods_l = mod_all[l, 0:B].reshape(B, 1, 6 * D)
        mods_c = jnp.broadcast_to(mod_all[l, B:B + 1].reshape(1, 1, 6 * D), (B, 1, 6 * D))

        feats_l, hb_l, gf_l = _pre(x, mods_l, lw, True, rope_l, dft_l)
        feats_c, hb_c, gf_c = _pre(ctx, mods_c, lw, False, rope_c, dft_c)
        scans_l, scans_c = _mixers(feats_c, feats_l, ret_tabs_c, ret_tabs_l, not last)

        fn_l = _fnet(gf_l, cs_l)
        x = _merge(x, hb_l, mods_l, feats_l, scans_l, fn_l, lw)
        x = _ffn(x, mods_l, lw, g_fin, last)
        if not last:
            fn_c = _fnet(gf_c, cs_c)
            ctx = _merge(ctx, hb_c, mods_c, feats_c, scans_c, fn_c, lw)
            ctx = _ffn(ctx, mods_c, lw, g_fin, False)
    return x
```

```python
import functools
import math

import numpy as np
import jax
import jax.numpy as jnp
from jax import lax
from jax.experimental import pallas as pl
from jax.experimental.pallas import tpu as pltpu

F32 = jnp.float32
BF16 = jnp.bfloat16

D_MODEL = 1024
BR_W = 256
HEAD_DIM = 64
HEADS = 4
N_PARTS = 12
P_IN = N_PARTS * BR_W
GRID_W = 64
GLA_LR = 16
GLA_NORMALIZER = 16.0
RWKV_LR = 64
RWKV_G_LR = 160
D_FF = 2816
ROPE_BASE = 10000.0
EPS = 1e-6

CHUNK = 64
GLA_SUB = 16
RET_CHUNK = 128
HALO = 8
NEG_BIG = -1e30

(F_GQ, F_GK, F_GV, F_GG, F_RQ, F_RK, F_RV, F_RG, F_WR, F_WV, F_WKK, F_WG, F_BON,
 F_LAF, F_LAB, F_LWF, F_LWB, F_KDF, F_KDB, F_ASF, F_ASB) = range(21)
N_FEAT = 21

VMEM_LIMIT = 56 * 1024 * 1024


def _cparams(sem):
    return pltpu.CompilerParams(dimension_semantics=sem, vmem_limit_bytes=VMEM_LIMIT)


def _dot(a, b):
    return jnp.dot(a.astype(BF16), b.astype(BF16), preferred_element_type=F32)


def _dot_nt(a, b):
    return lax.dot_general(a.astype(BF16), b.astype(BF16), (((1,), (1,)), ((), ())),
                           preferred_element_type=F32)


def _dot_tn(a, b):
    return lax.dot_general(a.astype(BF16), b.astype(BF16), (((0,), (0,)), ((), ())),
                           preferred_element_type=F32)


def _split3(x):
    hi = x.astype(BF16)
    r1 = x - hi.astype(F32)
    mid = r1.astype(BF16)
    lo = (r1 - mid.astype(F32)).astype(BF16)
    return hi, mid, lo


def _dot_exact_lhs(a_bf16, x):
    hi, mid, lo = _split3(x)
    d = lambda t: jnp.dot(a_bf16, t, preferred_element_type=F32)
    return d(hi) + d(mid) + d(lo)


def _dot_exact_rhs(x, b_bf16):
    hi, mid, lo = _split3(x)
    d = lambda t: jnp.dot(t, b_bf16, preferred_element_type=F32)
    return d(hi) + d(mid) + d(lo)


def _sigmoid(x):
    return 1.0 / (1.0 + jnp.exp(-x))


def _silu(x):
    return x * _sigmoid(x)


def _softplus(x):
    return jnp.maximum(x, 0.0) + jnp.log(1.0 + jnp.exp(-jnp.abs(x)))


def _iota(shape, dim):
    return lax.broadcasted_iota(jnp.int32, shape, dim)


def _head_ones():
    r = _iota((BR_W, BR_W), 0) // HEAD_DIM
    c = _iota((BR_W, BR_W), 1) // HEAD_DIM
    return jnp.where(r == c, 1.0, 0.0).astype(BF16)


def _same_head(n_rows, rows_per_head):
    r = _iota((n_rows, BR_W), 0) // rows_per_head
    c = _iota((n_rows, BR_W), 1) // HEAD_DIM
    return r == c


def _stack_heads(x):
    c = x.shape[0]
    xs = jnp.concatenate([x] * HEADS, axis=0)
    return jnp.where(_same_head(HEADS * c, c), xs, 0.0)


def _unstack_sum(y, c):
    out = y[0:c]
    for h in range(1, HEADS):
        out = out + y[h * c:(h + 1) * c]
    return out


def _tri(c, rev, strict=False):
    r = _iota((c, c), 0)
    s = _iota((c, c), 1)
    if rev:
        return (r < s) if strict else (r <= s)
    return (r > s) if strict else (r >= s)


def _tri_blockdiag(c, rev, strict):
    n = HEADS * c
    r = _iota((n, n), 0)
    s = _iota((n, n), 1)
    same = (r // c) == (s // c)
    rr = r % c
    ss = s % c
    if rev:
        t = (rr < ss) if strict else (rr <= ss)
    else:
        t = (rr > ss) if strict else (rr >= ss)
    return jnp.logical_and(same, t)


def _halo_keep(n_ext, tb, i, nt):
    row = _iota((n_ext, 1), 0)
    first = jnp.where(i > 0, 1.0, 0.0)
    last = jnp.where(i < nt - 1, 1.0, 0.0)
    return jnp.where(row < HALO, first, jnp.where(row >= tb + HALO, last, 1.0))


def _rms_rows(x):
    return x * lax.rsqrt(jnp.mean(x * x, axis=-1, keepdims=True) + EPS)


def _mod_kernel(c_ref, w_ref, b_ref, o_ref):
    s = _silu(c_ref[...])
    o_ref[0] = _dot(s, w_ref[0]) + b_ref[0]


def _modulation(cmat, w_ada, b_ada):
    L = w_ada.shape[0]
    nblk = 1536
    return pl.pallas_call(
        _mod_kernel,
        grid=(L, 6 * D_MODEL // nblk),
        in_specs=[
            pl.BlockSpec((16, D_MODEL), lambda l, j: (0, 0)),
            pl.BlockSpec((1, D_MODEL, nblk), lambda l, j: (l, 0, j)),
            pl.BlockSpec((1, 1, nblk), lambda l, j: (l, 0, j)),
        ],
        out_specs=pl.BlockSpec((1, 16, nblk), lambda l, j: (l, 0, j)),
        out_shape=jax.ShapeDtypeStruct((L, 16, 6 * D_MODEL), F32),
        compiler_params=_cparams(("arbitrary", "arbitrary")),
        name="mod",
    )(cmat, w_ada, b_ada.reshape(L, 1, 6 * D_MODEL))


def _pre_kernel(use_rope, tb, nt,
                x_ref, xp_ref, xn_ref, mod_ref, g1_ref, win_ref, wh_ref, wxw_ref, wxa_ref,
                wxg_ref, w2_ref, vec_ref, conv_ref, mu_ref, dft_ref, cos_ref, sin_ref,
                f_ref, hb_ref, gf_ref, hs_ref, ps_ref):
    i = pl.program_id(1)
    n_ext = tb + 2 * HALO
    x_ext = jnp.concatenate([xp_ref[0], x_ref[0], xn_ref[0]], axis=0)
    mod = mod_ref[0]
    shift = mod[:, 0:D_MODEL]
    scale = mod[:, D_MODEL:2 * D_MODEL]
    h_ext = (_rms_rows(x_ext) * g1_ref[...]) * (1.0 + scale) + shift
    h_ext = h_ext * _halo_keep(n_ext, tb, i, nt)
    hs_ref[...] = h_ext
    hb_ext = h_ext.astype(BF16)
    ps_ref[...] = jnp.dot(hb_ext, win_ref[...], preferred_element_type=F32)

    h = hs_ref[HALO:HALO + tb, :]
    prev = hs_ref[HALO - 1:HALO - 1 + tb, :]
    nxt = hs_ref[HALO + 1:HALO + 1 + tb, :]
    hb = h.astype(BF16)
    hb_ref[0] = hb
    xx = 0.5 * (prev + nxt) - h
    mu = mu_ref[...]
    xw = h + xx * mu[0:1]
    xa = h + xx * mu[1:2]
    xg = h + xx * mu[2:3]

    vec = vec_ref[...]
    ba_f, ba_b, w0_f, w0_b = vec[0:1], vec[1:2], vec[2:3], vec[3:4]
    a0_f, a0_b, kkw, kaw, rkw = vec[4:5], vec[5:6], vec[6:7], vec[7:8], vec[8:9]

    def col(j):
        return ps_ref[HALO:HALO + tb, j * BR_W:(j + 1) * BR_W]

    def put(j, val):
        f_ref[0, :, j * BR_W:(j + 1) * BR_W] = val

    put(F_GQ, col(0) * HEAD_DIM ** -0.5)
    put(F_GK, col(1))
    put(F_GV, col(2))
    put(F_GG, col(3))
    zh = jnp.dot(hb, wh_ref[...], preferred_element_type=F32)
    for dr, (ba, dst) in enumerate(((ba_f, F_LAF), (ba_b, F_LAB))):
        z = _dot(zh, w2_ref[dr, 0:128, :]) + ba
        put(dst, -_softplus(-z) * (1.0 / GLA_NORMALIZER))

    rq = col(4)
    rk = col(5) * HEAD_DIM ** -0.5
    if use_rope:
        cosf = cos_ref[...]
        sins = sin_ref[...]
        low = (_iota((tb, BR_W), 1) % HEAD_DIM) < (HEAD_DIM // 2)

        def rope(t):
            partner = jnp.where(low, pltpu.roll(t, BR_W - HEAD_DIM // 2, 1),
                                pltpu.roll(t, HEAD_DIM // 2, 1))
            return t * cosf + partner * sins

        rq = rope(rq)
        rk = rope(rk)
    put(F_RQ, rq)
    put(F_RK, rk)
    put(F_RV, col(6))
    put(F_RG, col(7))

    conv = conv_ref[...]

    def dwconv(j, part):
        lo, hi = (8 + part) * BR_W, (9 + part) * BR_W
        cw = conv[:, part * BR_W:(part + 1) * BR_W]
        return (ps_ref[HALO - 1:HALO - 1 + tb, lo:hi] * cw[0:1]
                + ps_ref[HALO:HALO + tb, lo:hi] * cw[1:2]
                + ps_ref[HALO + 1:HALO + 1 + tb, lo:hi] * cw[2:3])

    r_c = dwconv(8, 0)
    k_c = dwconv(9, 1)
    v_c = dwconv(10, 2)
    ones_h = _head_ones()
    g = _dot(_sigmoid(jnp.dot(xg.astype(BF16), wxg_ref[...], preferred_element_type=F32)),
             w2_ref[6])
    kk = k_c * kkw
    kk = kk * lax.rsqrt(_dot_exact_rhs(kk * kk, ones_h) + EPS)
    tw = jnp.tanh(jnp.dot(xw.astype(BF16), wxw_ref[...], preferred_element_type=F32))
    ta = jnp.dot(xa.astype(BF16), wxa_ref[...], preferred_element_type=F32)
    bonus = jnp.zeros((tb, BR_W), F32)
    for dr, (w0, a0, d_lw, d_kd, d_as) in enumerate(
            ((w0_f, a0_f, F_LWF, F_KDF, F_ASF), (w0_b, a0_b, F_LWB, F_KDB, F_ASB))):
        w_raw = -_softplus(-(w0 + _dot(tw, w2_ref[2 + dr, 0:128, :]))) - 0.5
        put(d_lw, -jnp.exp(w_raw))
        a_sig = _sigmoid(a0 + _dot(ta, w2_ref[4 + dr, 0:128, :]))
        kd = k_c * (1.0 + (a_sig - 1.0) * kaw)
        put(d_kd, kd)
        put(d_as, a_sig)
        bonus = bonus + _dot_exact_rhs(r_c * kd * rkw, ones_h) * v_c
    put(F_WR, r_c)
    put(F_WV, v_c)
    put(F_WKK, kk)
    put(F_WG, g)
    put(F_BON, bonus)

    fb = col(11).astype(BF16)
    gf_ref[0, :, 0:BR_W] = jnp.dot(fb, dft_ref[0], preferred_element_type=F32).astype(BF16)
    gf_ref[0, :, BR_W:2 * BR_W] = jnp.dot(fb, dft_ref[1], preferred_element_type=F32).astype(BF16)


def _pre(x, mods, lw, use_rope, rope_tabs, dft_c):
    B, T, D = x.shape
    tb = min(256, T)
    nt = T // tb
    hb8 = tb // HALO
    n_h = T // HALO
    const2 = lambda b, i: (0, 0)
    const3 = lambda b, i: (0, 0, 0)
    kern = functools.partial(_pre_kernel, use_rope, tb, nt)
    return pl.pallas_call(
        kern,
        grid=(B, nt),
        in_specs=[
            pl.BlockSpec((1, tb, D), lambda b, i: (b, i, 0)),
            pl.BlockSpec((1, HALO, D), lambda b, i: (b, jnp.maximum(i * hb8 - 1, 0), 0)),
            pl.BlockSpec((1, HALO, D), lambda b, i: (b, jnp.minimum((i + 1) * hb8, n_h - 1), 0)),
            pl.BlockSpec((1, 1, 6 * D), lambda b, i: (b, 0, 0)),
            pl.BlockSpec((1, D), const2),
            pl.BlockSpec((D, P_IN), const2),
            pl.BlockSpec((D, 128), const2),
            pl.BlockSpec((D, 128), const2),
            pl.BlockSpec((D, 128), const2),
            pl.BlockSpec((D, 256), const2),
            pl.BlockSpec((7, 256, BR_W), const3),
            pl.BlockSpec((16, BR_W), const2),
            pl.BlockSpec((3, 3 * BR_W), const2),
            pl.BlockSpec((3, D), const2),
            pl.BlockSpec((2, BR_W, BR_W), const3),
            pl.BlockSpec((tb, BR_W), lambda b, i: (i, 0)),
            pl.BlockSpec((tb, BR_W), lambda b, i: (i, 0)),
        ],
        out_specs=[
            pl.BlockSpec((1, tb, N_FEAT * BR_W), lambda b, i: (b, i, 0)),
            pl.BlockSpec((1, tb, D), lambda b, i: (b, i, 0)),
            pl.BlockSpec((1, tb, 2 * BR_W), lambda b, i: (b, i, 0)),
        ],
        out_shape=[
            jax.ShapeDtypeStruct((B, T, N_FEAT * BR_W), F32),
            jax.ShapeDtypeStruct((B, T, D), BF16),
            jax.ShapeDtypeStruct((B, T, 2 * BR_W), BF16),
        ],
        scratch_shapes=[
            pltpu.VMEM((tb + 2 * HALO, D), F32),
            pltpu.VMEM((tb + 2 * HALO, P_IN), F32),
        ],
        compiler_params=_cparams(("parallel", "arbitrary")),
        name="pre",
    )(x, x, x, mods, lw["g1"], lw["w_in"], lw["w_h"], lw["w_xw"], lw["w_xa"], lw["w_xg"],
      lw["w2"], lw["vec"], lw["rwkv_conv"], lw["rwkv_mu"], dft_c, rope_tabs[0], rope_tabs[1])


def _cumsum_chunk(x, rev):
    c = x.shape[0]
    tri = jnp.where(_tri(c, rev), 1.0, 0.0).astype(BF16)
    return _dot_exact_lhs(tri, x)


def _gla_chunks(chains):
    c = chains[0][0].shape[0]
    nblk = c // GLA_SUB
    rowid = _iota((c, 1), 0)
    ones_h = _head_ones()
    same = _same_head(HEADS * GLA_SUB, GLA_SUB)
    bd = (_iota((BR_W, BR_W), 0) // HEAD_DIM) == (_iota((BR_W, BR_W), 1) // HEAD_DIM)
    cums = [_cumsum_chunk(ch[3], ch[5]) for ch in chains]

    offd, inter, st_new = [], [], []
    for (q, k, v, la, st, rev), cum in zip(chains, cums):
        cum_end = cum[0:1] if rev else cum[c - 1:c]
        parts = []
        for blk in range(nblk):
            base = blk * GLA_SUB
            if rev:
                k_lo, k_hi, ref = base + GLA_SUB, c, base + GLA_SUB
            else:
                k_lo, k_hi, ref = 0, base, base - 1
            if k_hi <= k_lo:
                parts.append(None)
                continue
            qt = q[base:base + GLA_SUB] * jnp.exp(cum[base:base + GLA_SUB] - cum[ref:ref + 1])
            kt = k[k_lo:k_hi] * jnp.exp(cum[ref:ref + 1] - cum[k_lo:k_hi])
            sc = _dot_nt(_stack_heads(qt), kt)
            ov = jnp.where(same, _dot(sc, v[k_lo:k_hi]), 0.0)
            parts.append(_unstack_sum(ov, GLA_SUB))
        offd.append(parts)
        inter.append(_dot_nt(q * jnp.exp(cum), st))
        st_new.append(st * jnp.exp(cum_end)
                      + jnp.where(bd, _dot_tn(v, k * jnp.exp(cum_end - cum)), 0.0))

    reds, spans = [], []
    for (q, k, v, la, st, rev), cum in zip(chains, cums):
        pieces, sp = [], []
        for j in range(c):
            base = (j // GLA_SUB) * GLA_SUB
            if rev:
                lo, hi = base, (j // 8) * 8 + 8
            else:
                lo, hi = (j // 8) * 8, base + GLA_SUB
            valid = (rowid[lo:hi] <= j) if rev else (rowid[lo:hi] >= j)
            arg = jnp.where(valid, cum[lo:hi] - cum[j:j + 1], NEG_BIG)
            pieces.append(q[lo:hi] * (k[j:j + 1] * jnp.exp(arg)))
            sp.append((lo, hi))
        reds.append(jnp.dot(jnp.concatenate(pieces, axis=0).astype(BF16), ones_h,
                            preferred_element_type=F32))
        spans.append(sp)

    outs = []
    for ci, (q, k, v, la, st, rev) in enumerate(chains):
        groups = [None] * (c // 8)
        off = 0
        for j in range(c):
            lo, hi = spans[ci][j]
            contrib = reds[ci][off:off + (hi - lo)] * v[j:j + 1]
            off += hi - lo
            for gi in range(lo // 8, hi // 8):
                piece = contrib[(gi * 8 - lo):(gi * 8 - lo + 8)]
                groups[gi] = piece if groups[gi] is None else groups[gi] + piece
        blocks = []
        for blk in range(nblk):
            o_blk = jnp.concatenate(groups[blk * GLA_SUB // 8:(blk + 1) * GLA_SUB // 8], axis=0)
            if offd[ci][blk] is not None:
                o_blk = o_blk + offd[ci][blk]
            blocks.append(o_blk)
        outs.append((jnp.concatenate(blocks, axis=0) + inter[ci], st_new[ci]))
    return outs


def _ret_chunk(q, k, v, s, dmat, qdec, kdec, cdec):
    sc = _dot_nt(q, _stack_heads(k)) * dmat
    o = _dot(sc, _stack_heads(v)) + _dot(q * qdec, s)
    bd = (_iota((BR_W, BR_W), 0) // HEAD_DIM) == (_iota((BR_W, BR_W), 1) // HEAD_DIM)
    s_new = s * cdec + jnp.where(bd, _dot_tn(k * kdec, v), 0.0)
    return o, s_new


def _rwkv_chunks(chains):
    c = chains[0][0].shape[0]
    n = HEADS * c
    bf = lambda t: t.astype(BF16)
    masks = {}
    for ch in chains:
        rev = ch[7]
        if rev not in masks:
            masks[rev] = (_tri_blockdiag(c, rev, True), _tri_blockdiag(c, rev, False))
    eye = jnp.where(_iota((n, n), 0) == _iota((n, n), 1), 1.0, 0.0)
    cums = [_cumsum_chunk(ch[5], ch[7]) for ch in chains]
    st = []
    for (r, v, kk, kd, asig, lw, s, rev), cum in zip(chains, cums):
        cum_end = cum[0:1] if rev else cum[c - 1:c]
        einv = jnp.exp(-cum)
        ehat = jnp.exp(cum_end - cum)
        bvec = kk * asig
        st.append(dict(
            a_s=bf(_stack_heads(-kk * jnp.exp(cum - lw))), r_s=bf(_stack_heads(r * jnp.exp(cum))),
            b_s=bf(_stack_heads(bvec * einv)), k_s=bf(_stack_heads(kd * einv)),
            v_s=bf(_stack_heads(v)), bh_s=bf(_stack_heads(bvec * ehat)),
            kh_s=bf(_stack_heads(kd * ehat)), g_end=jnp.exp(cum_end), s=s, sb=bf(s), rev=rev))
    for d in st:
        d["m"] = _dot_nt(jnp.concatenate([d["a_s"], d["r_s"]], axis=0),
                         jnp.concatenate([d["b_s"], d["k_s"]], axis=0))
    for d in st:
        strict, incl = masks[d["rev"]]
        m = d.pop("m")
        d["p"] = jnp.where(strict, m[0:n, 0:n], 0.0)
        d["t"] = eye + d["p"]
        d["m_ak"] = bf(jnp.where(strict, m[0:n, n:2 * n], 0.0))
        d["n_rb"] = bf(jnp.where(incl, m[n:2 * n, 0:n], 0.0))
        d["n_rk"] = bf(jnp.where(incl, m[n:2 * n, n:2 * n], 0.0))
    for d in st:
        d["w"] = _dot_nt(d["a_s"], d["sb"]) + _dot(d["m_ak"], d["v_s"])
        d["y0"] = _dot_nt(d["r_s"], d["sb"]) + _dot(d["n_rk"], d["v_s"])
    for _ in range(int(math.log2(c)) - 1):
        for d in st:
            pb = bf(d["p"])
            d["p"] = jnp.dot(pb, pb, preferred_element_type=F32)
        for d in st:
            d["t"] = d["t"] + _dot(d["t"], d["p"])
    for d in st:
        d["u"] = bf(_dot(d["t"], d["w"]))
    outs = []
    for d in st:
        y = _unstack_sum(d["y0"] + _dot(d["n_rb"], d["u"]), c)
        s_new = d["s"] * d["g_end"] + _dot_tn(jnp.concatenate([d["u"], d["v_s"]], axis=0),
                                              jnp.concatenate([d["bh_s"], d["kh_s"]], axis=0))
        outs.append((y, s_new))
    return outs


def _scan_kernel(kind, n_in, n_tab, nc, bb, *refs):
    fwd = refs[0:n_in]
    bwd = refs[n_in:2 * n_in]
    tabs = refs[2 * n_in:2 * n_in + n_tab]
    s0_ref = refs[2 * n_in + n_tab]
    of_ref, ob_ref, sf_ref, st_ref = refs[2 * n_in + n_tab + 1:]
    i = pl.program_id(1)

    @pl.when(i == 0)
    def _():
        st_ref[...] = s0_ref[...]

    chains = [(dr, b) for b in range(bb) for dr in range(2)]
    ins_of = lambda dr, b: [t[b] for t in (fwd, bwd)[dr]]
    if kind == "rwkv":
        res = _rwkv_chunks([(*ins_of(dr, b), st_ref[dr, b], dr == 1) for dr, b in chains])
    elif kind == "gla":
        res = _gla_chunks([(*ins_of(dr, b), st_ref[dr, b], dr == 1) for dr, b in chains])
    else:
        res = []
        for dr, b in chains:
            dmat, qdec, kdec, cdec = (t[dr] for t in tabs)
            res.append(_ret_chunk(*ins_of(dr, b), st_ref[dr, b], dmat, qdec, kdec, cdec))
    for (dr, b), (o, s_new) in zip(chains, res):
        (of_ref, ob_ref)[dr][b] = o
        st_ref[dr, b] = s_new

    @pl.when(i == nc - 1)
    def _():
        sf_ref[...] = st_ref[...]


_SCAN_ROWS = {"gla": 2, "ret": 2, "rwkv": 2}


def _scan(kind, feats, cols_f, cols_b, s0, tabs=()):
    B, T, _ = feats.shape
    c = RET_CHUNK if kind == "ret" else CHUNK
    c = min(c, T)
    nc = T // c
    bb = _SCAN_ROWS[kind]
    n_in = len(cols_f)
    in_specs = []
    for j in cols_f:
        in_specs.append(pl.BlockSpec((bb, c, BR_W), lambda b, i, j=j: (b, i, j)))
    for j in cols_b:
        in_specs.append(pl.BlockSpec((bb, c, BR_W), lambda b, i, j=j: (b, nc - 1 - i, j)))
    for t in tabs:
        in_specs.append(pl.BlockSpec(t.shape, lambda b, i, nd=t.ndim: (0,) * nd))
    in_specs.append(pl.BlockSpec((2, bb, BR_W, BR_W), lambda b, i: (0, b, 0, 0)))
    kern = functools.partial(_scan_kernel, kind, n_in, len(tabs), nc, bb)
    return pl.pallas_call(
        kern,
        grid=(B // bb, nc),
        in_specs=in_specs,
        out_specs=[
            pl.BlockSpec((bb, c, BR_W), lambda b, i: (b, i, 0)),
            pl.BlockSpec((bb, c, BR_W), lambda b, i: (b, nc - 1 - i, 0)),
            pl.BlockSpec((2, bb, BR_W, BR_W), lambda b, i: (0, b, 0, 0)),
        ],
        out_shape=[
            jax.ShapeDtypeStruct((B, T, BR_W), F32),
            jax.ShapeDtypeStruct((B, T, BR_W), F32),
            jax.ShapeDtypeStruct((2, B, BR_W, BR_W), F32),
        ],
        scratch_shapes=[pltpu.VMEM((2, bb, BR_W, BR_W), F32)],
        compiler_params=_cparams(("parallel", "arbitrary")),
        name=kind,
    )(*([feats] * (2 * n_in)), *tabs, s0)


def _ret_tables(c):
    pos = np.arange(c, dtype=np.float64)
    lane_head = np.arange(BR_W) // HEAD_DIM
    dmats, qd, kd, cd = [], [], [], []
    for dr in range(2):
        expo = -5.0 - np.arange(HEADS, dtype=np.float64)
        if dr == 1:
            expo = expo[::-1]
        log_g = np.log1p(-np.exp2(expo))
        if dr == 0:
            rel = pos[:, None] - pos[None, :]
            qpow = pos + 1.0
            kpow = c - 1.0 - pos
        else:
            rel = pos[None, :] - pos[:, None]
            qpow = c - pos
            kpow = pos
        tri = rel >= 0
        dm = np.where(tri[None], np.exp(np.where(tri, rel, 0.0)[None] * log_g[:, None, None]), 0.0)
        dmats.append(np.concatenate([dm[h] for h in range(HEADS)], axis=1))
        qd.append(np.exp(qpow[:, None] * log_g[lane_head][None, :]))
        kd.append(np.exp(kpow[:, None] * log_g[lane_head][None, :]))
        cd.append(np.exp(c * log_g[lane_head])[None, :])
    f = lambda xs: jnp.asarray(np.stack(xs), dtype=F32)
    return f(dmats), f(qd), f(kd), f(cd)


def _fnet_kernel(t_len, cs_ref, g_ref, o_ref):
    o_ref[0] = (jnp.dot(cs_ref[:, 0:t_len], g_ref[0, :, 0:BR_W], preferred_element_type=F32)
                + jnp.dot(cs_ref[:, t_len:2 * t_len], g_ref[0, :, BR_W:2 * BR_W],
                          preferred_element_type=F32))


def _fnet(gf, cs):
    B, T, _ = gf.shape
    tb = min(512, T)
    return pl.pallas_call(
        functools.partial(_fnet_kernel, T),
        grid=(T // tb, B),
        in_specs=[
            pl.BlockSpec((tb, 2 * T), lambda i, b: (i, 0)),
            pl.BlockSpec((1, T, 2 * BR_W), lambda i, b: (b, 0, 0)),
        ],
        out_specs=pl.BlockSpec((1, tb, BR_W), lambda i, b: (b, i, 0)),
        out_shape=jax.ShapeDtypeStruct((B, T, BR_W), F32),
        compiler_params=_cparams(("arbitrary", "arbitrary")),
        name="fnet",
    )(cs, gf)


def _dft_tables(T):
    t = jnp.arange(T, dtype=jnp.int32)
    ang = ((t[:, None] * t[None, :]) % T).astype(F32) * (2.0 * math.pi / T)
    cs = jnp.concatenate([jnp.cos(ang), -jnp.sin(ang)], axis=1).astype(BF16)
    ch = np.arange(HEAD_DIM)
    a64 = 2.0 * np.pi * ((ch[:, None] * ch[None, :]) % HEAD_DIM) / HEAD_DIM
    scale = (T * HEAD_DIM) ** -0.5
    eye = np.eye(HEADS)
    cbd = np.kron(eye, np.cos(a64)) * scale
    sbd = np.kron(eye, np.sin(a64)) * scale
    return cs, jnp.asarray(np.stack([cbd, sbd]), dtype=BF16)


def _head_norm(o, ones_h, center):
    if center:
        o = o - _dot_exact_rhs(o, ones_h) * (1.0 / HEAD_DIM)
    var = _dot_exact_rhs(o * o, ones_h) * (1.0 / HEAD_DIM)
    return o * lax.rsqrt(var + EPS)


def _merge_kernel(x_ref, hb_ref, mod_ref, gg_ref, rg_ref, wg_ref, bon_ref,
                  ogf_ref, ogb_ref, orf_ref, orb_ref, oyf_ref, oyb_ref, fn_ref,
                  gn_ref, wgate_ref, bgate_ref, wbr_ref, wout_ref, o_ref):
    ones_h = _head_ones()
    gn = gn_ref[...]
    gla = _head_norm(ogf_ref[0] + ogb_ref[0], ones_h, False) * gn[0:1] * _silu(gg_ref[0])
    ret = _head_norm(orf_ref[0] + orb_ref[0], ones_h, True) * gn[1:2] * _silu(rg_ref[0])
    rwkv = (_head_norm(oyf_ref[0] + oyb_ref[0], ones_h, True) * gn[2:3] + bon_ref[0]) * wg_ref[0]
    outs = (gla, ret, rwkv, fn_ref[0])
    hb = hb_ref[0]
    z = None
    for br in range(4):
        gate = _sigmoid(jnp.dot(hb, wgate_ref[br], preferred_element_type=F32) + bgate_ref[br])
        term = gate * _dot(outs[br], wbr_ref[br])
        z = term if z is None else z + term
    y = _dot(z, wout_ref[...])
    gate1 = mod_ref[0][:, 2 * D_MODEL:3 * D_MODEL]
    o_ref[0] = x_ref[0] + gate1 * y


def _merge(x, hb, mods, feats, scans, fnet_out, lw):
    B, T, D = x.shape
    tb = min(256, T)
    tok = lambda w: pl.BlockSpec((1, tb, w), lambda b, i: (b, i, 0))
    fcol = lambda j: pl.BlockSpec((1, tb, BR_W), lambda b, i, j=j: (b, i, j))
    const2 = lambda b, i: (0, 0)
    const3 = lambda b, i: (0, 0, 0)
    return pl.pallas_call(
        _merge_kernel,
        grid=(B, T // tb),
        in_specs=[
            tok(D), tok(D),
            pl.BlockSpec((1, 1, 6 * D), lambda b, i: (b, 0, 0)),
            fcol(F_GG), fcol(F_RG), fcol(F_WG), fcol(F_BON),
            tok(BR_W), tok(BR_W), tok(BR_W), tok(BR_W), tok(BR_W), tok(BR_W), tok(BR_W),
            pl.BlockSpec((3, BR_W), const2),
            pl.BlockSpec((4, D, D), const3),
            pl.BlockSpec((4, 1, D), const3),
            pl.BlockSpec((4, BR_W, D), const3),
            pl.BlockSpec((D, D), const2),
        ],
        out_specs=tok(D),
        out_shape=jax.ShapeDtypeStruct((B, T, D), F32),
        compiler_params=_cparams(("parallel", "arbitrary")),
        name="merge",
    )(x, hb, mods, feats, feats, feats, feats, *scans, fnet_out,
      lw["gn"], lw["w_gate"], lw["b_gate"], lw["w_br"], lw["w_out"])


def _ffn_kernel(final, tb, nt, nk,
                x_ref, xp_ref, xn_ref, mod_ref, g2_ref, upa_ref, upu_ref, cw_ref, cb_ref,
                down_ref, gfin_ref, o_ref, h2_ref, acc_ref):
    i = pl.program_id(1)
    kf = pl.program_id(2)
    n_ext = tb + 2 * HALO

    @pl.when(kf == 0)
    def _():
        x_ext = jnp.concatenate([xp_ref[0], x_ref[0], xn_ref[0]], axis=0)
        mod = mod_ref[0]
        shift = mod[:, 3 * D_MODEL:4 * D_MODEL]
        scale = mod[:, 4 * D_MODEL:5 * D_MODEL]
        h2 = (_rms_rows(x_ext) * g2_ref[...]) * (1.0 + scale) + shift
        h2_ref[...] = (h2 * _halo_keep(n_ext, tb, i, nt)).astype(BF16)
        acc_ref[...] = jnp.zeros_like(acc_ref)

    h2 = h2_ref[...]
    a_ext = jnp.dot(h2, upa_ref[...], preferred_element_type=F32)
    u = jnp.dot(h2, upu_ref[...], preferred_element_type=F32)[HALO:HALO + tb]
    cw = cw_ref[...]
    a = (a_ext[HALO - 1:HALO - 1 + tb] * cw[0:1] + a_ext[HALO:HALO + tb] * cw[1:2]
         + a_ext[HALO + 1:HALO + 1 + tb] * cw[2:3] + cb_ref[...])
    acc_ref[...] += _dot(_silu(a) * u, down_ref[...])

    @pl.when(kf == nk - 1)
    def _():
        gate2 = mod_ref[0][:, 5 * D_MODEL:6 * D_MODEL]
        res = x_ref[0] + gate2 * acc_ref[...]
        if final:
            res = _rms_rows(res) * gfin_ref[...]
        o_ref[0] = res


def _ffn(x, mods, lw, g_final, final):
    B, T, D = x.shape
    tb = min(512, T)
    nt = T // tb
    fb = D_FF // 2
    nk = D_FF // fb
    hb8 = tb // HALO
    n_h = T // HALO
    const2 = lambda b, i, k: (0, 0)
    kern = functools.partial(_ffn_kernel, final, tb, nt, nk)
    return pl.pallas_call(
        kern,
        grid=(B, nt, nk),
        in_specs=[
            pl.BlockSpec((1, tb, D), lambda b, i, k: (b, i, 0)),
            pl.BlockSpec((1, HALO, D), lambda b, i, k: (b, jnp.maximum(i * hb8 - 1, 0), 0)),
            pl.BlockSpec((1, HALO, D), lambda b, i, k: (b, jnp.minimum((i + 1) * hb8, n_h - 1), 0)),
            pl.BlockSpec((1, 1, 6 * D), lambda b, i, k: (b, 0, 0)),
            pl.BlockSpec((1, D), const2),
            pl.BlockSpec((D, fb), lambda b, i, k: (0, k)),
            pl.BlockSpec((D, fb), lambda b, i, k: (0, nk + k)),
            pl.BlockSpec((3, fb), lambda b, i, k: (0, k)),
            pl.BlockSpec((1, fb), lambda b, i, k: (0, k)),
            pl.BlockSpec((fb, D), lambda b, i, k: (k, 0)),
            pl.BlockSpec((1, D), const2),
        ],
        out_specs=pl.BlockSpec((1, tb, D), lambda b, i, k: (b, i, 0)),
        out_shape=jax.ShapeDtypeStruct((B, T, D), F32),
        scratch_shapes=[
            pltpu.VMEM((tb + 2 * HALO, D), BF16),
            pltpu.VMEM((tb, D), F32),
        ],
        compiler_params=_cparams(("parallel", "arbitrary", "arbitrary")),
        name="ffn",
    )(x, x, x, mods, lw["g2"], lw["ffn_up"], lw["ffn_up"], lw["ffn_conv"], lw["ffn_conv_b"],
      lw["ffn_down"], g_final)


def _rope_tables(T):
    rows = T // GRID_W
    row = jnp.repeat(jnp.arange(rows, dtype=F32), GRID_W)
    colp = jnp.tile(jnp.arange(GRID_W, dtype=F32), rows)
    n_freq = HEAD_DIM // 4
    inv = ROPE_BASE ** (-jnp.arange(n_freq, dtype=F32) / n_freq)
    ang = jnp.concatenate([row[:, None] * inv, colp[:, None] * inv], axis=-1)
    cos, sin = jnp.cos(ang), jnp.sin(ang)
    cos_h = jnp.concatenate([cos, cos], axis=-1)
    sin_h = jnp.concatenate([-sin, sin], axis=-1)
    return jnp.tile(cos_h, (1, HEADS)), jnp.tile(sin_h, (1, HEADS))


def _pad_rows(w, lo, n):
    return jnp.zeros((n, w.shape[1]), w.dtype).at[lo:lo + w.shape[0]].set(w)


def _layer_weights(l, p):
    D = D_MODEL
    bf = lambda t: t.astype(BF16)
    zcols = lambda w, n: jnp.concatenate([w, jnp.zeros((w.shape[0], n - w.shape[1]), w.dtype)], axis=1)
    w_h = zcols(jnp.concatenate([p["gla_wa1"][l, 0], p["gla_wa1"][l, 1]], axis=1), 128)
    w_xw = jnp.concatenate([p["rwkv_w1"][l, 0], p["rwkv_w1"][l, 1]], axis=1)
    w_xa = jnp.concatenate([p["rwkv_a1"][l, 0], p["rwkv_a1"][l, 1]], axis=1)
    w_xg = zcols(p["rwkv_g1"][l], 256)
    w2 = jnp.stack([
        _pad_rows(p["gla_wa2"][l, 0], 0, 256), _pad_rows(p["gla_wa2"][l, 1], GLA_LR, 256),
        _pad_rows(p["rwkv_w2"][l, 0], 0, 256), _pad_rows(p["rwkv_w2"][l, 1], RWKV_LR, 256),
        _pad_rows(p["rwkv_a2"][l, 0], 0, 256), _pad_rows(p["rwkv_a2"][l, 1], RWKV_LR, 256),
        _pad_rows(p["rwkv_g2"][l], 0, 256),
    ])
    vec = jnp.stack([p["gla_ba"][l, 0], p["gla_ba"][l, 1], p["rwkv_w0"][l, 0], p["rwkv_w0"][l, 1],
                     p["rwkv_a0"][l, 0], p["rwkv_a0"][l, 1], p["rwkv_kk"][l], p["rwkv_ka"][l],
                     p["rwkv_rk"][l]])
    vec = jnp.concatenate([vec, jnp.zeros((16 - vec.shape[0], BR_W), F32)], axis=0)
    return {
        "g1": p["g_norm1"][l].reshape(1, D), "g2": p["g_norm2"][l].reshape(1, D),
        "w_in": bf(p["w_in"][l]), "w_h": bf(w_h), "w_xw": bf(w_xw), "w_xa": bf(w_xa),
        "w_xg": bf(w_xg), "w2": bf(w2), "vec": vec,
        "rwkv_conv": p["rwkv_conv"][l], "rwkv_mu": p["rwkv_mu"][l],
        "gn": jnp.stack([p["gla_gn"][l], p["ret_gn"][l], p["rwkv_gn"][l]]),
        "w_gate": bf(p["w_gate"][l]), "b_gate": p["b_gate"][l].reshape(4, 1, D),
        "w_br": bf(p["w_br"][l]), "w_out": bf(p["w_out"][l]),
        "ffn_up": bf(p["ffn_up"][l]), "ffn_conv": p["ffn_conv"][l],
        "ffn_conv_b": p["ffn_conv_b"][l].reshape(1, D_FF), "ffn_down": bf(p["ffn_down"][l]),
    }


_GLA_F = (F_GQ, F_GK, F_GV, F_LAF)
_GLA_B = (F_GQ, F_GK, F_GV, F_LAB)
_RET = (F_RQ, F_RK, F_RV)
_RWKV_F = (F_WR, F_WV, F_WKK, F_KDF, F_ASF, F_LWF)
_RWKV_B = (F_WR, F_WV, F_WKK, F_KDB, F_ASB, F_LWB)


def _mixers(feats_c, feats_l, ret_tabs_c, ret_tabs_l, need_ctx):
    B = feats_l.shape[0]
    zero = jnp.zeros((2, B, BR_W, BR_W), F32)
    out_c, out_l = [], []
    for kind, cf, cb, tabs in (("gla", _GLA_F, _GLA_B, None), ("ret", _RET, _RET, True),
                               ("rwkv", _RWKV_F, _RWKV_B, None)):
        tc = ret_tabs_c if tabs else ()
        tl = ret_tabs_l if tabs else ()
        of_c, ob_c, s_c = _scan(kind, feats_c, cf, cb, zero, tc)
        of_l, ob_l, _ = _scan(kind, feats_l, cf, cb, s_c, tl)
        out_c += [of_c, ob_c]
        out_l += [of_l, ob_l]
    return out_l, (out_c if need_ctx else None)


def kernel(x, c, ctx, c_ctx, w_ada, b_ada, g_norm1, g_norm2, w_in, gla_wa1, gla_wa2, gla_ba, gla_gn, ret_gn, rwkv_conv, rwkv_mu, rwkv_w0, rwkv_w1, rwkv_w2, rwkv_a0, rwkv_a1, rwkv_a2, rwkv_g1, rwkv_g2, rwkv_kk, rwkv_ka, rwkv_rk, rwkv_gn, w_gate, b_gate, w_br, w_out, ffn_up, ffn_conv, ffn_conv_b, ffn_down, g_final):
    p = dict(g_norm1=g_norm1, g_norm2=g_norm2, w_in=w_in, gla_wa1=gla_wa1, gla_wa2=gla_wa2,
             gla_ba=gla_ba, gla_gn=gla_gn, ret_gn=ret_gn, rwkv_conv=rwkv_conv, rwkv_mu=rwkv_mu,
             rwkv_w0=rwkv_w0, rwkv_w1=rwkv_w1, rwkv_w2=rwkv_w2, rwkv_a0=rwkv_a0,
             rwkv_a1=rwkv_a1, rwkv_a2=rwkv_a2, rwkv_g1=rwkv_g1, rwkv_g2=rwkv_g2,
             rwkv_kk=rwkv_kk, rwkv_ka=rwkv_ka, rwkv_rk=rwkv_rk, rwkv_gn=rwkv_gn,
             w_gate=w_gate, b_gate=b_gate, w_br=w_br, w_out=w_out, ffn_up=ffn_up,
             ffn_conv=ffn_conv, ffn_conv_b=ffn_conv_b, ffn_down=ffn_down)
    B, T, D = x.shape
    Tc = ctx.shape[1]
    depth = w_ada.shape[0]

    cmat = jnp.concatenate([c, c_ctx[None, :], jnp.zeros((16 - B - 1, D), F32)], axis=0)
    mod_all = _modulation(cmat, w_ada, b_ada)

    rope_l = _rope_tables(T)
    rope_c = (jnp.zeros((Tc, BR_W), F32), jnp.zeros((Tc, BR_W), F32))
    cs_l, dft_l = _dft_tables(T)
    cs_c, dft_c = _dft_tables(Tc)
    ret_tabs_l = _ret_tables(min(RET_CHUNK, T))
    ret_tabs_c = _ret_tables(min(RET_CHUNK, Tc))
    g_fin = g_final.reshape(1, D)

    for l in range(depth):
        last = l == depth - 1
        lw = _layer_weights(l, p)
        mods_l = mod_all[l, 0:B].reshape(B, 1, 6 * D)
        mods_c = jnp.broadcast_to(mod_all[l, B:B + 1].reshape(1, 1, 6 * D), (B, 1, 6 * D))

        feats_l, hb_l, gf_l = _pre(x, mods_l, lw, True, rope_l, dft_l)
        feats_c, hb_c, gf_c = _pre(ctx, mods_c, lw, False, rope_c, dft_c)
        scans_l, scans_c = _mixers(feats_c, feats_l, ret_tabs_c, ret_tabs_l, not last)

        fn_l = _fnet(gf_l, cs_l)
        x = _merge(x, hb_l, mods_l, feats_l, scans_l, fn_l, lw)
        x = _ffn(x, mods_l, lw, g_fin, last)
        if not last:
            fn_c = _fnet(gf_c, cs_c)
            ctx = _merge(ctx, hb_c, mods_c, feats_c, scans_c, fn_c, lw)
            ctx = _ffn(ctx, mods_c, lw, g_fin, False)
    return x
```

```python
import functools
import math

import numpy as np
import jax
import jax.numpy as jnp
from jax import lax
from jax.experimental import pallas as pl
from jax.experimental.pallas import tpu as pltpu

F32 = jnp.float32
BF16 = jnp.bfloat16

D_MODEL = 1024
BR_W = 256
HEAD_DIM = 64
HEADS = 4
N_PARTS = 12
P_IN = N_PARTS * BR_W
GRID_W = 64
GLA_LR = 16
GLA_NORMALIZER = 16.0
RWKV_LR = 64
RWKV_G_LR = 160
D_FF = 2816
ROPE_BASE = 10000.0
EPS = 1e-6

CHUNK = 64
GLA_SUB = 16
RET_CHUNK = 128
HALO = 8
NEG_BIG = -1e30
LOG2E = 1.4426950408889634

(F_GQ, F_GK, F_GV, F_GG, F_RQ, F_RK, F_RV, F_RG, F_WR, F_WV, F_WKK, F_WG, F_BON,
 F_LAF, F_LAB, F_LWF, F_LWB, F_KDF, F_KDB, F_ASF, F_ASB) = range(21)
N_FEAT = 21

VMEM_LIMIT = 56 * 1024 * 1024


def _cparams(sem):
    return pltpu.CompilerParams(dimension_semantics=sem, vmem_limit_bytes=VMEM_LIMIT)


def _dot(a, b):
    return jnp.dot(a.astype(BF16), b.astype(BF16), preferred_element_type=F32)


def _dot_nt(a, b):
    return lax.dot_general(a.astype(BF16), b.astype(BF16), (((1,), (1,)), ((), ())),
                           preferred_element_type=F32)


def _dot_tn(a, b):
    return lax.dot_general(a.astype(BF16), b.astype(BF16), (((0,), (0,)), ((), ())),
                           preferred_element_type=F32)


def _split3(x):
    hi = x.astype(BF16)
    r1 = x - hi.astype(F32)
    mid = r1.astype(BF16)
    lo = (r1 - mid.astype(F32)).astype(BF16)
    return hi, mid, lo


def _dot_exact_lhs(a_bf16, x):
    hi, mid, lo = _split3(x)
    d = lambda t: jnp.dot(a_bf16, t, preferred_element_type=F32)
    return d(hi) + d(mid) + d(lo)


def _sigmoid(x):
    return 1.0 / (1.0 + jnp.exp(-x))


def _silu(x):
    return x * _sigmoid(x)


def _softplus(x):
    return jnp.maximum(x, 0.0) + jnp.log(1.0 + jnp.exp(-jnp.abs(x)))


def _iota(shape, dim):
    return lax.broadcasted_iota(jnp.int32, shape, dim)


def _head_ones():
    r = _iota((BR_W, BR_W), 0) // HEAD_DIM
    c = _iota((BR_W, BR_W), 1) // HEAD_DIM
    return jnp.where(r == c, 1.0, 0.0).astype(BF16)


def _same_head(n_rows, rows_per_head):
    r = _iota((n_rows, BR_W), 0) // rows_per_head
    c = _iota((n_rows, BR_W), 1) // HEAD_DIM
    return r == c


def _stack_heads(x):
    c = x.shape[0]
    xs = jnp.concatenate([x] * HEADS, axis=0)
    return jnp.where(_same_head(HEADS * c, c), xs, 0.0)


def _unstack_sum(y, c):
    out = y[0:c]
    for h in range(1, HEADS):
        out = out + y[h * c:(h + 1) * c]
    return out


def _tri(c, rev, strict=False):
    r = _iota((c, c), 0)
    s = _iota((c, c), 1)
    if rev:
        return (r < s) if strict else (r <= s)
    return (r > s) if strict else (r >= s)


def _tri_blockdiag(c, rev, strict):
    n = HEADS * c
    r = _iota((n, n), 0)
    s = _iota((n, n), 1)
    same = (r // c) == (s // c)
    rr = r % c
    ss = s % c
    if rev:
        t = (rr < ss) if strict else (rr <= ss)
    else:
        t = (rr > ss) if strict else (rr >= ss)
    return jnp.logical_and(same, t)


def _halo_keep(n_ext, tb, i, nt):
    row = _iota((n_ext, 1), 0)
    first = jnp.where(i > 0, 1.0, 0.0)
    last = jnp.where(i < nt - 1, 1.0, 0.0)
    return jnp.where(row < HALO, first, jnp.where(row >= tb + HALO, last, 1.0))


def _rms_rows(x):
    return x * lax.rsqrt(jnp.mean(x * x, axis=-1, keepdims=True) + EPS)


def _mod_kernel(c_ref, w_ref, b_ref, o_ref):
    s = _silu(c_ref[...])
    o_ref[0] = _dot(s, w_ref[0]) + b_ref[0]


def _modulation(cmat, w_ada, b_ada):
    L = w_ada.shape[0]
    nblk = 1536
    return pl.pallas_call(
        _mod_kernel,
        grid=(L, 6 * D_MODEL // nblk),
        in_specs=[
            pl.BlockSpec((16, D_MODEL), lambda l, j: (0, 0)),
            pl.BlockSpec((1, D_MODEL, nblk), lambda l, j: (l, 0, j)),
            pl.BlockSpec((1, 1, nblk), lambda l, j: (l, 0, j)),
        ],
        out_specs=pl.BlockSpec((1, 16, nblk), lambda l, j: (l, 0, j)),
        out_shape=jax.ShapeDtypeStruct((L, 16, 6 * D_MODEL), F32),
        compiler_params=_cparams(("arbitrary", "arbitrary")),
        name="mod",
    )(cmat, w_ada, b_ada.reshape(L, 1, 6 * D_MODEL))


def _pre_kernel(use_rope, tb, nt,
                x_ref, xp_ref, xn_ref, mod_ref, g1_ref, win_ref, wh_ref, wxw_ref, wxa_ref,
                wxg_ref, w2_ref, vec_ref, conv_ref, mu_ref, dft_ref, cos_ref, sin_ref,
                f_ref, hb_ref, gf_ref, hs_ref, ps_ref):
    i = pl.program_id(1)
    n_ext = tb + 2 * HALO
    x_ext = jnp.concatenate([xp_ref[0], x_ref[0], xn_ref[0]], axis=0)
    mod = mod_ref[0]
    shift = mod[:, 0:D_MODEL]
    scale = mod[:, D_MODEL:2 * D_MODEL]
    h_ext = (_rms_rows(x_ext) * g1_ref[...]) * (1.0 + scale) + shift
    h_ext = h_ext * _halo_keep(n_ext, tb, i, nt)
    hs_ref[...] = h_ext
    hb_ext = h_ext.astype(BF16)
    ps_ref[...] = jnp.dot(hb_ext, win_ref[...], preferred_element_type=F32)

    h = hs_ref[HALO:HALO + tb, :]
    prev = hs_ref[HALO - 1:HALO - 1 + tb, :]
    nxt = hs_ref[HALO + 1:HALO + 1 + tb, :]
    hb = h.astype(BF16)
    hb_ref[0] = hb
    xx = 0.5 * (prev + nxt) - h
    mu = mu_ref[...]
    xw = h + xx * mu[0:1]
    xa = h + xx * mu[1:2]
    xg = h + xx * mu[2:3]

    vec = vec_ref[...]
    ba_f, ba_b, w0_f, w0_b = vec[0:1], vec[1:2], vec[2:3], vec[3:4]
    a0_f, a0_b, kkw, kaw, rkw = vec[4:5], vec[5:6], vec[6:7], vec[7:8], vec[8:9]

    def col(j):
        return ps_ref[HALO:HALO + tb, j * BR_W:(j + 1) * BR_W]

    def put(j, val):
        f_ref[0, :, j * BR_W:(j + 1) * BR_W] = val

    put(F_GQ, col(0) * HEAD_DIM ** -0.5)
    put(F_GK, col(1))
    put(F_GV, col(2))
    put(F_GG, col(3))
    zh = jnp.dot(hb, wh_ref[...], preferred_element_type=F32)
    for dr, (ba, dst) in enumerate(((ba_f, F_LAF), (ba_b, F_LAB))):
        z = _dot(zh, w2_ref[dr, 0:128, :]) + ba
        put(dst, -_softplus(-z) * (1.0 / GLA_NORMALIZER))

    rq = col(4)
    rk = col(5) * HEAD_DIM ** -0.5
    if use_rope:
        cosf = cos_ref[...]
        sins = sin_ref[...]
        low = (_iota((tb, BR_W), 1) % HEAD_DIM) < (HEAD_DIM // 2)

        def rope(t):
            partner = jnp.where(low, pltpu.roll(t, BR_W - HEAD_DIM // 2, 1),
                                pltpu.roll(t, HEAD_DIM // 2, 1))
            return t * cosf + partner * sins

        rq = rope(rq)
        rk = rope(rk)
    put(F_RQ, rq)
    put(F_RK, rk)
    put(F_RV, col(6))
    put(F_RG, col(7))

    conv = conv_ref[...]

    def dwconv(j, part):
        lo, hi = (8 + part) * BR_W, (9 + part) * BR_W
        cw = conv[:, part * BR_W:(part + 1) * BR_W]
        return (ps_ref[HALO - 1:HALO - 1 + tb, lo:hi] * cw[0:1]
                + ps_ref[HALO:HALO + tb, lo:hi] * cw[1:2]
                + ps_ref[HALO + 1:HALO + 1 + tb, lo:hi] * cw[2:3])

    r_c = dwconv(8, 0)
    k_c = dwconv(9, 1)
    v_c = dwconv(10, 2)
    ones_h = _head_ones()
    g = _dot(_sigmoid(jnp.dot(xg.astype(BF16), wxg_ref[...], preferred_element_type=F32)),
             w2_ref[6])
    kk = k_c * kkw
    kk = kk * lax.rsqrt(_dot(kk * kk, ones_h) + EPS)
    tw = jnp.tanh(jnp.dot(xw.astype(BF16), wxw_ref[...], preferred_element_type=F32))
    ta = jnp.dot(xa.astype(BF16), wxa_ref[...], preferred_element_type=F32)
    bonus = jnp.zeros((tb, BR_W), F32)
    for dr, (w0, a0, d_lw, d_kd, d_as) in enumerate(
            ((w0_f, a0_f, F_LWF, F_KDF, F_ASF), (w0_b, a0_b, F_LWB, F_KDB, F_ASB))):
        w_raw = -_softplus(-(w0 + _dot(tw, w2_ref[2 + dr, 0:128, :]))) - 0.5
        put(d_lw, -jnp.exp(w_raw))
        a_sig = _sigmoid(a0 + _dot(ta, w2_ref[4 + dr, 0:128, :]))
        kd = k_c * (1.0 + (a_sig - 1.0) * kaw)
        put(d_kd, kd)
        put(d_as, a_sig)
        bonus = bonus + _dot(r_c * kd * rkw, ones_h) * v_c
    put(F_WR, r_c)
    put(F_WV, v_c)
    put(F_WKK, kk)
    put(F_WG, g)
    put(F_BON, bonus)

    fb = col(11).astype(BF16)
    gf_ref[0, :, 0:BR_W] = jnp.dot(fb, dft_ref[0], preferred_element_type=F32).astype(BF16)
    gf_ref[0, :, BR_W:2 * BR_W] = jnp.dot(fb, dft_ref[1], preferred_element_type=F32).astype(BF16)


def _pre(x, mods, lw, use_rope, rope_tabs, dft_c):
    B, T, D = x.shape
    tb = min(256, T)
    nt = T // tb
    hb8 = tb // HALO
    n_h = T // HALO
    const2 = lambda b, i: (0, 0)
    const3 = lambda b, i: (0, 0, 0)
    kern = functools.partial(_pre_kernel, use_rope, tb, nt)
    return pl.pallas_call(
        kern,
        grid=(B, nt),
        in_specs=[
            pl.BlockSpec((1, tb, D), lambda b, i: (b, i, 0)),
            pl.BlockSpec((1, HALO, D), lambda b, i: (b, jnp.maximum(i * hb8 - 1, 0), 0)),
            pl.BlockSpec((1, HALO, D), lambda b, i: (b, jnp.minimum((i + 1) * hb8, n_h - 1), 0)),
            pl.BlockSpec((1, 1, 6 * D), lambda b, i: (b, 0, 0)),
            pl.BlockSpec((1, D), const2),
            pl.BlockSpec((D, P_IN), const2),
            pl.BlockSpec((D, 128), const2),
            pl.BlockSpec((D, 128), const2),
            pl.BlockSpec((D, 128), const2),
            pl.BlockSpec((D, 256), const2),
            pl.BlockSpec((7, 256, BR_W), const3),
            pl.BlockSpec((16, BR_W), const2),
            pl.BlockSpec((3, 3 * BR_W), const2),
            pl.BlockSpec((3, D), const2),
            pl.BlockSpec((2, BR_W, BR_W), const3),
            pl.BlockSpec((tb, BR_W), lambda b, i: (i, 0)),
            pl.BlockSpec((tb, BR_W), lambda b, i: (i, 0)),
        ],
        out_specs=[
            pl.BlockSpec((1, tb, N_FEAT * BR_W), lambda b, i: (b, i, 0)),
            pl.BlockSpec((1, tb, D), lambda b, i: (b, i, 0)),
            pl.BlockSpec((1, tb, 2 * BR_W), lambda b, i: (b, i, 0)),
        ],
        out_shape=[
            jax.ShapeDtypeStruct((B, T, N_FEAT * BR_W), F32),
            jax.ShapeDtypeStruct((B, T, D), BF16),
            jax.ShapeDtypeStruct((B, T, 2 * BR_W), BF16),
        ],
        scratch_shapes=[
            pltpu.VMEM((tb + 2 * HALO, D), F32),
            pltpu.VMEM((tb + 2 * HALO, P_IN), F32),
        ],
        compiler_params=_cparams(("parallel", "arbitrary")),
        name="pre",
    )(x, x, x, mods, lw["g1"], lw["w_in"], lw["w_h"], lw["w_xw"], lw["w_xa"], lw["w_xg"],
      lw["w2"], lw["vec"], lw["rwkv_conv"], lw["rwkv_mu"], dft_c, rope_tabs[0], rope_tabs[1])


def _cumsum_chunk(x, rev):
    c = x.shape[0]
    tri = jnp.where(_tri(c, rev), 1.0, 0.0).astype(BF16)
    return _dot_exact_lhs(tri, x)


def _gla_chunks(chains):
    c = chains[0][0].shape[0]
    nblk = c // GLA_SUB
    rowid = _iota((c, 1), 0)
    ones_h = _head_ones()
    same = _same_head(HEADS * GLA_SUB, GLA_SUB)
    bd = (_iota((BR_W, BR_W), 0) // HEAD_DIM) == (_iota((BR_W, BR_W), 1) // HEAD_DIM)
    cums = [_cumsum_chunk(ch[3], ch[5]) for ch in chains]

    offd, inter, st_new = [], [], []
    for (q, k, v, la, st, rev), cum in zip(chains, cums):
        cum_end = cum[0:1] if rev else cum[c - 1:c]
        parts = []
        for blk in range(nblk):
            base = blk * GLA_SUB
            if rev:
                k_lo, k_hi, ref = base + GLA_SUB, c, base + GLA_SUB
            else:
                k_lo, k_hi, ref = 0, base, base - 1
            if k_hi <= k_lo:
                parts.append(None)
                continue
            qt = q[base:base + GLA_SUB] * jnp.exp(cum[base:base + GLA_SUB] - cum[ref:ref + 1])
            kt = k[k_lo:k_hi] * jnp.exp(cum[ref:ref + 1] - cum[k_lo:k_hi])
            sc = _dot_nt(_stack_heads(qt), kt)
            ov = jnp.where(same, _dot(sc, v[k_lo:k_hi]), 0.0)
            parts.append(_unstack_sum(ov, GLA_SUB))
        offd.append(parts)
        inter.append(_dot_nt(q * jnp.exp(cum), st))
        st_new.append(st * jnp.exp(cum_end)
                      + jnp.where(bd, _dot_tn(v, k * jnp.exp(cum_end - cum)), 0.0))

    reds, spans = [], []
    for (q, k, v, la, st, rev), cum in zip(chains, cums):
        cum2 = cum * LOG2E
        pieces, sp = [], []
        for j in range(c):
            base = (j // GLA_SUB) * GLA_SUB
            own = (j // 8) * 8
            others = range(base, own, 8) if rev else range(own + 8, base + GLA_SUB, 8)
            for g0 in (own, *others):
                arg = cum2[g0:g0 + 8] - cum2[j:j + 1]
                if g0 == own:
                    valid = (rowid[g0:g0 + 8] <= j) if rev else (rowid[g0:g0 + 8] >= j)
                    arg = jnp.where(valid, arg, NEG_BIG)
                pieces.append(q[g0:g0 + 8] * (k[j:j + 1] * jnp.exp2(arg)))
                sp.append((j, g0))
        reds.append(jnp.dot(jnp.concatenate(pieces, axis=0).astype(BF16), ones_h,
                            preferred_element_type=F32))
        spans.append(sp)

    outs = []
    for ci, (q, k, v, la, st, rev) in enumerate(chains):
        groups = [None] * (c // 8)
        for idx, (j, g0) in enumerate(spans[ci]):
            piece = reds[ci][idx * 8:idx * 8 + 8] * v[j:j + 1]
            gi = g0 // 8
            groups[gi] = piece if groups[gi] is None else groups[gi] + piece
        blocks = []
        for blk in range(nblk):
            o_blk = jnp.concatenate(groups[blk * GLA_SUB // 8:(blk + 1) * GLA_SUB // 8], axis=0)
            if offd[ci][blk] is not None:
                o_blk = o_blk + offd[ci][blk]
            blocks.append(o_blk)
        outs.append((jnp.concatenate(blocks, axis=0) + inter[ci], st_new[ci]))
    return outs


def _ret_chunk(q, k, v, s, dmat, qdec, kdec, cdec):
    sc = _dot_nt(q, _stack_heads(k)) * dmat
    o = _dot(sc, _stack_heads(v)) + _dot(q * qdec, s)
    bd = (_iota((BR_W, BR_W), 0) // HEAD_DIM) == (_iota((BR_W, BR_W), 1) // HEAD_DIM)
    s_new = s * cdec + jnp.where(bd, _dot_tn(k * kdec, v), 0.0)
    return o, s_new


def _rwkv_chunks(chains):
    c = chains[0][0].shape[0]
    n = HEADS * c
    bf = lambda t: t.astype(BF16)
    masks = {}
    for ch in chains:
        rev = ch[7]
        if rev not in masks:
            masks[rev] = (_tri_blockdiag(c, rev, True), _tri_blockdiag(c, rev, False))
    eye = jnp.where(_iota((n, n), 0) == _iota((n, n), 1), 1.0, 0.0)
    cums = [_cumsum_chunk(ch[5], ch[7]) for ch in chains]
    st = []
    for (r, v, kk, kd, asig, lw, s, rev), cum in zip(chains, cums):
        cum_end = cum[0:1] if rev else cum[c - 1:c]
        einv = jnp.exp(-cum)
        ehat = jnp.exp(cum_end - cum)
        bvec = kk * asig
        st.append(dict(
            a_s=bf(_stack_heads(-kk * jnp.exp(cum - lw))), r_s=bf(_stack_heads(r * jnp.exp(cum))),
            b_s=bf(_stack_heads(bvec * einv)), k_s=bf(_stack_heads(kd * einv)),
            v_s=bf(_stack_heads(v)), bh_s=bf(_stack_heads(bvec * ehat)),
            kh_s=bf(_stack_heads(kd * ehat)), g_end=jnp.exp(cum_end), s=s, sb=bf(s), rev=rev))
    for d in st:
        d["m"] = _dot_nt(jnp.concatenate([d["a_s"], d["r_s"]], axis=0),
                         jnp.concatenate([d["b_s"], d["k_s"]], axis=0))
    for d in st:
        strict, incl = masks[d["rev"]]
        m = d.pop("m")
        d["p"] = jnp.where(strict, m[0:n, 0:n], 0.0)
        d["t"] = eye + d["p"]
        d["m_ak"] = bf(jnp.where(strict, m[0:n, n:2 * n], 0.0))
        d["n_rb"] = bf(jnp.where(incl, m[n:2 * n, 0:n], 0.0))
        d["n_rk"] = bf(jnp.where(incl, m[n:2 * n, n:2 * n], 0.0))
    for d in st:
        d["w"] = _dot_nt(d["a_s"], d["sb"]) + _dot(d["m_ak"], d["v_s"])
        d["y0"] = _dot_nt(d["r_s"], d["sb"]) + _dot(d["n_rk"], d["v_s"])
    for _ in range(int(math.log2(c)) - 1):
        for d in st:
            pb = bf(d["p"])
            d["p"] = jnp.dot(pb, pb, preferred_element_type=F32)
        for d in st:
            d["t"] = d["t"] + _dot(d["t"], d["p"])
    for d in st:
        d["u"] = bf(_dot(d["t"], d["w"]))
    outs = []
    for d in st:
        y = _unstack_sum(d["y0"] + _dot(d["n_rb"], d["u"]), c)
        s_new = d["s"] * d["g_end"] + _dot_tn(jnp.concatenate([d["u"], d["v_s"]], axis=0),
                                              jnp.concatenate([d["bh_s"], d["kh_s"]], axis=0))
        outs.append((y, s_new))
    return outs


def _scan_kernel(kind, n_in, n_tab, nc, bb, *refs):
    fwd = refs[0:n_in]
    bwd = refs[n_in:2 * n_in]
    tabs = refs[2 * n_in:2 * n_in + n_tab]
    s0_ref = refs[2 * n_in + n_tab]
    of_ref, ob_ref, sf_ref, st_ref = refs[2 * n_in + n_tab + 1:]
    i = pl.program_id(1)

    @pl.when(i == 0)
    def _():
        st_ref[...] = s0_ref[...]

    chains = [(dr, b) for b in range(bb) for dr in range(2)]
    ins_of = lambda dr, b: [t[b] for t in (fwd, bwd)[dr]]
    if kind == "rwkv":
        res = _rwkv_chunks([(*ins_of(dr, b), st_ref[dr, b], dr == 1) for dr, b in chains])
    elif kind == "gla":
        res = _gla_chunks([(*ins_of(dr, b), st_ref[dr, b], dr == 1) for dr, b in chains])
    else:
        res = []
        for dr, b in chains:
            dmat, qdec, kdec, cdec = (t[dr] for t in tabs)
            res.append(_ret_chunk(*ins_of(dr, b), st_ref[dr, b], dmat, qdec, kdec, cdec))
    for (dr, b), (o, s_new) in zip(chains, res):
        (of_ref, ob_ref)[dr][b] = o
        st_ref[dr, b] = s_new

    @pl.when(i == nc - 1)
    def _():
        sf_ref[...] = st_ref[...]


_SCAN_ROWS = {"gla": 2, "ret": 2, "rwkv": 2}


def _scan(kind, feats, cols_f, cols_b, s0, tabs=()):
    B, T, _ = feats.shape
    c = RET_CHUNK if kind == "ret" else CHUNK
    c = min(c, T)
    nc = T // c
    bb = _SCAN_ROWS[kind]
    n_in = len(cols_f)
    in_specs = []
    for j in cols_f:
        in_specs.append(pl.BlockSpec((bb, c, BR_W), lambda b, i, j=j: (b, i, j)))
    for j in cols_b:
        in_specs.append(pl.BlockSpec((bb, c, BR_W), lambda b, i, j=j: (b, nc - 1 - i, j)))
    for t in tabs:
        in_specs.append(pl.BlockSpec(t.shape, lambda b, i, nd=t.ndim: (0,) * nd))
    in_specs.append(pl.BlockSpec((2, bb, BR_W, BR_W), lambda b, i: (0, b, 0, 0)))
    kern = functools.partial(_scan_kernel, kind, n_in, len(tabs), nc, bb)
    return pl.pallas_call(
        kern,
        grid=(B // bb, nc),
        in_specs=in_specs,
        out_specs=[
            pl.BlockSpec((bb, c, BR_W), lambda b, i: (b, i, 0)),
            pl.BlockSpec((bb, c, BR_W), lambda b, i: (b, nc - 1 - i, 0)),
            pl.BlockSpec((2, bb, BR_W, BR_W), lambda b, i: (0, b, 0, 0)),
        ],
        out_shape=[
            jax.ShapeDtypeStruct((B, T, BR_W), F32),
            jax.ShapeDtypeStruct((B, T, BR_W), F32),
            jax.ShapeDtypeStruct((2, B, BR_W, BR_W), F32),
        ],
        scratch_shapes=[pltpu.VMEM((2, bb, BR_W, BR_W), F32)],
        compiler_params=_cparams(("parallel", "arbitrary")),
        name=kind,
    )(*([feats] * (2 * n_in)), *tabs, s0)


def _ret_tables(c):
    pos = np.arange(c, dtype=np.float64)
    lane_head = np.arange(BR_W) // HEAD_DIM
    dmats, qd, kd, cd = [], [], [], []
    for dr in range(2):
        expo = -5.0 - np.arange(HEADS, dtype=np.float64)
        if dr == 1:
            expo = expo[::-1]
        log_g = np.log1p(-np.exp2(expo))
        if dr == 0:
            rel = pos[:, None] - pos[None, :]
            qpow = pos + 1.0
            kpow = c - 1.0 - pos
        else:
            rel = pos[None, :] - pos[:, None]
            qpow = c - pos
            kpow = pos
        tri = rel >= 0
        dm = np.where(tri[None], np.exp(np.where(tri, rel, 0.0)[None] * log_g[:, None, None]), 0.0)
        dmats.append(np.concatenate([dm[h] for h in range(HEADS)], axis=1))
        qd.append(np.exp(qpow[:, None] * log_g[lane_head][None, :]))
        kd.append(np.exp(kpow[:, None] * log_g[lane_head][None, :]))
        cd.append(np.exp(c * log_g[lane_head])[None, :])
    f = lambda xs: jnp.asarray(np.stack(xs), dtype=F32)
    return f(dmats), f(qd), f(kd), f(cd)


def _fnet_kernel(t_len, cs_ref, g_ref, o_ref):
    o_ref[0] = (jnp.dot(cs_ref[:, 0:t_len], g_ref[0, :, 0:BR_W], preferred_element_type=F32)
                + jnp.dot(cs_ref[:, t_len:2 * t_len], g_ref[0, :, BR_W:2 * BR_W],
                          preferred_element_type=F32))


def _fnet(gf, cs):
    B, T, _ = gf.shape
    tb = min(512, T)
    return pl.pallas_call(
        functools.partial(_fnet_kernel, T),
        grid=(T // tb, B),
        in_specs=[
            pl.BlockSpec((tb, 2 * T), lambda i, b: (i, 0)),
            pl.BlockSpec((1, T, 2 * BR_W), lambda i, b: (b, 0, 0)),
        ],
        out_specs=pl.BlockSpec((1, tb, BR_W), lambda i, b: (b, i, 0)),
        out_shape=jax.ShapeDtypeStruct((B, T, BR_W), F32),
        compiler_params=_cparams(("arbitrary", "arbitrary")),
        name="fnet",
    )(cs, gf)


def _dft_tables(T):
    t = jnp.arange(T, dtype=jnp.int32)
    ang = ((t[:, None] * t[None, :]) % T).astype(F32) * (2.0 * math.pi / T)
    cs = jnp.concatenate([jnp.cos(ang), -jnp.sin(ang)], axis=1).astype(BF16)
    ch = np.arange(HEAD_DIM)
    a64 = 2.0 * np.pi * ((ch[:, None] * ch[None, :]) % HEAD_DIM) / HEAD_DIM
    scale = (T * HEAD_DIM) ** -0.5
    eye = np.eye(HEADS)
    cbd = np.kron(eye, np.cos(a64)) * scale
    sbd = np.kron(eye, np.sin(a64)) * scale
    return cs, jnp.asarray(np.stack([cbd, sbd]), dtype=BF16)


def _head_norm(o, ones_h, center):
    if center:
        o = o - _dot(o, ones_h) * (1.0 / HEAD_DIM)
    var = _dot(o * o, ones_h) * (1.0 / HEAD_DIM)
    return o * lax.rsqrt(var + EPS)


def _merge_kernel(x_ref, hb_ref, mod_ref, gg_ref, rg_ref, wg_ref, bon_ref,
                  ogf_ref, ogb_ref, orf_ref, orb_ref, oyf_ref, oyb_ref, fn_ref,
                  gn_ref, wgate_ref, bgate_ref, wbr_ref, wout_ref, o_ref):
    ones_h = _head_ones()
    gn = gn_ref[...]
    gla = _head_norm(ogf_ref[0] + ogb_ref[0], ones_h, False) * gn[0:1] * _silu(gg_ref[0])
    ret = _head_norm(orf_ref[0] + orb_ref[0], ones_h, True) * gn[1:2] * _silu(rg_ref[0])
    rwkv = (_head_norm(oyf_ref[0] + oyb_ref[0], ones_h, True) * gn[2:3] + bon_ref[0]) * wg_ref[0]
    outs = (gla, ret, rwkv, fn_ref[0])
    hb = hb_ref[0]
    z = None
    for br in range(4):
        gate = _sigmoid(jnp.dot(hb, wgate_ref[br], preferred_element_type=F32) + bgate_ref[br])
        term = gate * _dot(outs[br], wbr_ref[br])
        z = term if z is None else z + term
    y = _dot(z, wout_ref[...])
    gate1 = mod_ref[0][:, 2 * D_MODEL:3 * D_MODEL]
    o_ref[0] = x_ref[0] + gate1 * y


def _merge(x, hb, mods, feats, scans, fnet_out, lw):
    B, T, D = x.shape
    tb = min(256, T)
    tok = lambda w: pl.BlockSpec((1, tb, w), lambda b, i: (b, i, 0))
    fcol = lambda j: pl.BlockSpec((1, tb, BR_W), lambda b, i, j=j: (b, i, j))
    const2 = lambda b, i: (0, 0)
    const3 = lambda b, i: (0, 0, 0)
    return pl.pallas_call(
        _merge_kernel,
        grid=(B, T // tb),
        in_specs=[
            tok(D), tok(D),
            pl.BlockSpec((1, 1, 6 * D), lambda b, i: (b, 0, 0)),
            fcol(F_GG), fcol(F_RG), fcol(F_WG), fcol(F_BON),
            tok(BR_W), tok(BR_W), tok(BR_W), tok(BR_W), tok(BR_W), tok(BR_W), tok(BR_W),
            pl.BlockSpec((3, BR_W), const2),
            pl.BlockSpec((4, D, D), const3),
            pl.BlockSpec((4, 1, D), const3),
            pl.BlockSpec((4, BR_W, D), const3),
            pl.BlockSpec((D, D), const2),
        ],
        out_specs=tok(D),
        out_shape=jax.ShapeDtypeStruct((B, T, D), F32),
        compiler_params=_cparams(("parallel", "arbitrary")),
        name="merge",
    )(x, hb, mods, feats, feats, feats, feats, *scans, fnet_out,
      lw["gn"], lw["w_gate"], lw["b_gate"], lw["w_br"], lw["w_out"])


FFN_COL_CHUNKS = (1024, 1024, 768)


def _ffn_kernel(final, tb, nt,
                x_ref, xp_ref, xn_ref, mod_ref, g2_ref, up_ref, cw_ref, cb_ref,
                down_ref, gfin_ref, o_ref, act_ref):
    i = pl.program_id(1)
    n_ext = tb + 2 * HALO
    x_ext = jnp.concatenate([xp_ref[0], x_ref[0], xn_ref[0]], axis=0)
    mod = mod_ref[0]
    shift = mod[:, 3 * D_MODEL:4 * D_MODEL]
    scale = mod[:, 4 * D_MODEL:5 * D_MODEL]
    h2 = (_rms_rows(x_ext) * g2_ref[...]) * (1.0 + scale) + shift
    h2 = h2 * _halo_keep(n_ext, tb, i, nt)
    h2_ext = h2.astype(BF16)
    h2_mid = h2[HALO:HALO + tb].astype(BF16)
    cw = cw_ref[...]
    cb = cb_ref[...]
    lo = 0
    for width in FFN_COL_CHUNKS:
        a_ext = jnp.dot(h2_ext, up_ref[:, lo:lo + width], preferred_element_type=F32)
        u = jnp.dot(h2_mid, up_ref[:, D_FF + lo:D_FF + lo + width], preferred_element_type=F32)
        a = (a_ext[HALO - 1:HALO - 1 + tb] * cw[0:1, lo:lo + width]
             + a_ext[HALO:HALO + tb] * cw[1:2, lo:lo + width]
             + a_ext[HALO + 1:HALO + 1 + tb] * cw[2:3, lo:lo + width] + cb[:, lo:lo + width])
        act_ref[:, lo:lo + width] = (_silu(a) * u).astype(BF16)
        lo += width
    y = jnp.dot(act_ref[...], down_ref[...], preferred_element_type=F32)
    gate2 = mod[:, 5 * D_MODEL:6 * D_MODEL]
    res = x_ref[0] + gate2 * y
    if final:
        res = _rms_rows(res) * gfin_ref[...]
    o_ref[0] = res


def _ffn(x, mods, lw, g_final, final):
    B, T, D = x.shape
    tb = min(512, T)
    nt = T // tb
    hb8 = tb // HALO
    n_h = T // HALO
    const2 = lambda b, i: (0, 0)
    resident = lambda shape: pl.BlockSpec(shape, const2, pipeline_mode=pl.Buffered(1))
    kern = functools.partial(_ffn_kernel, final, tb, nt)
    return pl.pallas_call(
        kern,
        grid=(B, nt),
        in_specs=[
            pl.BlockSpec((1, tb, D), lambda b, i: (b, i, 0)),
            pl.BlockSpec((1, HALO, D), lambda b, i: (b, jnp.maximum(i * hb8 - 1, 0), 0)),
            pl.BlockSpec((1, HALO, D), lambda b, i: (b, jnp.minimum((i + 1) * hb8, n_h - 1), 0)),
            pl.BlockSpec((1, 1, 6 * D), lambda b, i: (b, 0, 0)),
            pl.BlockSpec((1, D), const2),
            resident((D, 2 * D_FF)),
            pl.BlockSpec((3, D_FF), const2),
            pl.BlockSpec((1, D_FF), const2),
            resident((D_FF, D)),
            pl.BlockSpec((1, D), const2),
        ],
        out_specs=pl.BlockSpec((1, tb, D), lambda b, i: (b, i, 0)),
        out_shape=jax.ShapeDtypeStruct((B, T, D), F32),
        scratch_shapes=[pltpu.VMEM((tb, D_FF), BF16)],
        compiler_params=_cparams(("parallel", "arbitrary")),
        name="ffn",
    )(x, x, x, mods, lw["g2"], lw["ffn_up"], lw["ffn_conv"], lw["ffn_conv_b"],
      lw["ffn_down"], g_final)


def _rope_tables(T):
    rows = T // GRID_W
    row = jnp.repeat(jnp.arange(rows, dtype=F32), GRID_W)
    colp = jnp.tile(jnp.arange(GRID_W, dtype=F32), rows)
    n_freq = HEAD_DIM // 4
    inv = ROPE_BASE ** (-jnp.arange(n_freq, dtype=F32) / n_freq)
    ang = jnp.concatenate([row[:, None] * inv, colp[:, None] * inv], axis=-1)
    cos, sin = jnp.cos(ang), jnp.sin(ang)
    cos_h = jnp.concatenate([cos, cos], axis=-1)
    sin_h = jnp.concatenate([-sin, sin], axis=-1)
    return jnp.tile(cos_h, (1, HEADS)), jnp.tile(sin_h, (1, HEADS))


def _pad_rows(w, lo, n):
    return jnp.zeros((n, w.shape[1]), w.dtype).at[lo:lo + w.shape[0]].set(w)


def _layer_weights(l, p):
    D = D_MODEL
    bf = lambda t: t.astype(BF16)
    zcols = lambda w, n: jnp.concatenate([w, jnp.zeros((w.shape[0], n - w.shape[1]), w.dtype)], axis=1)
    w_h = zcols(jnp.concatenate([p["gla_wa1"][l, 0], p["gla_wa1"][l, 1]], axis=1), 128)
    w_xw = jnp.concatenate([p["rwkv_w1"][l, 0], p["rwkv_w1"][l, 1]], axis=1)
    w_xa = jnp.concatenate([p["rwkv_a1"][l, 0], p["rwkv_a1"][l, 1]], axis=1)
    w_xg = zcols(p["rwkv_g1"][l], 256)
    w2 = jnp.stack([
        _pad_rows(p["gla_wa2"][l, 0], 0, 256), _pad_rows(p["gla_wa2"][l, 1], GLA_LR, 256),
        _pad_rows(p["rwkv_w2"][l, 0], 0, 256), _pad_rows(p["rwkv_w2"][l, 1], RWKV_LR, 256),
        _pad_rows(p["rwkv_a2"][l, 0], 0, 256), _pad_rows(p["rwkv_a2"][l, 1], RWKV_LR, 256),
        _pad_rows(p["rwkv_g2"][l], 0, 256),
    ])
    vec = jnp.stack([p["gla_ba"][l, 0], p["gla_ba"][l, 1], p["rwkv_w0"][l, 0], p["rwkv_w0"][l, 1],
                     p["rwkv_a0"][l, 0], p["rwkv_a0"][l, 1], p["rwkv_kk"][l], p["rwkv_ka"][l],
                     p["rwkv_rk"][l]])
    vec = jnp.concatenate([vec, jnp.zeros((16 - vec.shape[0], BR_W), F32)], axis=0)
    return {
        "g1": p["g_norm1"][l].reshape(1, D), "g2": p["g_norm2"][l].reshape(1, D),
        "w_in": bf(p["w_in"][l]), "w_h": bf(w_h), "w_xw": bf(w_xw), "w_xa": bf(w_xa),
        "w_xg": bf(w_xg), "w2": bf(w2), "vec": vec,
        "rwkv_conv": p["rwkv_conv"][l], "rwkv_mu": p["rwkv_mu"][l],
        "gn": jnp.stack([p["gla_gn"][l], p["ret_gn"][l], p["rwkv_gn"][l]]),
        "w_gate": bf(p["w_gate"][l]), "b_gate": p["b_gate"][l].reshape(4, 1, D),
        "w_br": bf(p["w_br"][l]), "w_out": bf(p["w_out"][l]),
        "ffn_up": bf(p["ffn_up"][l]), "ffn_conv": p["ffn_conv"][l],
        "ffn_conv_b": p["ffn_conv_b"][l].reshape(1, D_FF), "ffn_down": bf(p["ffn_down"][l]),
    }


_GLA_F = (F_GQ, F_GK, F_GV, F_LAF)
_GLA_B = (F_GQ, F_GK, F_GV, F_LAB)
_RET = (F_RQ, F_RK, F_RV)
_RWKV_F = (F_WR, F_WV, F_WKK, F_KDF, F_ASF, F_LWF)
_RWKV_B = (F_WR, F_WV, F_WKK, F_KDB, F_ASB, F_LWB)


def _mixers(feats_c, feats_l, ret_tabs_c, ret_tabs_l, need_ctx):
    B = feats_l.shape[0]
    zero = jnp.zeros((2, B, BR_W, BR_W), F32)
    out_c, out_l = [], []
    for kind, cf, cb, tabs in (("gla", _GLA_F, _GLA_B, None), ("ret", _RET, _RET, True),
                               ("rwkv", _RWKV_F, _RWKV_B, None)):
        tc = ret_tabs_c if tabs else ()
        tl = ret_tabs_l if tabs else ()
        of_c, ob_c, s_c = _scan(kind, feats_c, cf, cb, zero, tc)
        of_l, ob_l, _ = _scan(kind, feats_l, cf, cb, s_c, tl)
        out_c += [of_c, ob_c]
        out_l += [of_l, ob_l]
    return out_l, (out_c if need_ctx else None)


def kernel(x, c, ctx, c_ctx, w_ada, b_ada, g_norm1, g_norm2, w_in, gla_wa1, gla_wa2, gla_ba, gla_gn, ret_gn, rwkv_conv, rwkv_mu, rwkv_w0, rwkv_w1, rwkv_w2, rwkv_a0, rwkv_a1, rwkv_a2, rwkv_g1, rwkv_g2, rwkv_kk, rwkv_ka, rwkv_rk, rwkv_gn, w_gate, b_gate, w_br, w_out, ffn_up, ffn_conv, ffn_conv_b, ffn_down, g_final):
    p = dict(g_norm1=g_norm1, g_norm2=g_norm2, w_in=w_in, gla_wa1=gla_wa1, gla_wa2=gla_wa2,
             gla_ba=gla_ba, gla_gn=gla_gn, ret_gn=ret_gn, rwkv_conv=rwkv_conv, rwkv_mu=rwkv_mu,
             rwkv_w0=rwkv_w0, rwkv_w1=rwkv_w1, rwkv_w2=rwkv_w2, rwkv_a0=rwkv_a0,
             rwkv_a1=rwkv_a1, rwkv_a2=rwkv_a2, rwkv_g1=rwkv_g1, rwkv_g2=rwkv_g2,
             rwkv_kk=rwkv_kk, rwkv_ka=rwkv_ka, rwkv_rk=rwkv_rk, rwkv_gn=rwkv_gn,
             w_gate=w_gate, b_gate=b_gate, w_br=w_br, w_out=w_out, ffn_up=ffn_up,
             ffn_conv=ffn_conv, ffn_conv_b=ffn_conv_b, ffn_down=ffn_down)
    B, T, D = x.shape
    Tc = ctx.shape[1]
    depth = w_ada.shape[0]

    cmat = jnp.concatenate([c, c_ctx[None, :], jnp.zeros((16 - B - 1, D), F32)], axis=0)
    mod_all = _modulation(cmat, w_ada, b_ada)

    rope_l = _rope_tables(T)
    rope_c = (jnp.zeros((Tc, BR_W), F32), jnp.zeros((Tc, BR_W), F32))
    cs_l, dft_l = _dft_tables(T)
    cs_c, dft_c = _dft_tables(Tc)
    ret_tabs_l = _ret_tables(min(RET_CHUNK, T))
    ret_tabs_c = _ret_tables(min(RET_CHUNK, Tc))
    g_fin = g_final.reshape(1, D)

    for l in range(depth):
        last = l == depth - 1
        lw = _layer_weights(l, p)
        mods_l = mod_all[l, 0:B].reshape(B, 1, 6 * D)
        mods_c = jnp.broadcast_to(mod_all[l, B:B + 1].reshape(1, 1, 6 * D), (B, 1, 6 * D))

        feats_l, hb_l, gf_l = _pre(x, mods_l, lw, True, rope_l, dft_l)
        feats_c, hb_c, gf_c = _pre(ctx, mods_c, lw, False, rope_c, dft_c)
        scans_l, scans_c = _mixers(feats_c, feats_l, ret_tabs_c, ret_tabs_l, not last)

        fn_l = _fnet(gf_l, cs_l)
        x = _merge(x, hb_l, mods_l, feats_l, scans_l, fn_l, lw)
        x = _ffn(x, mods_l, lw, g_fin, last)
        if not last:
            fn_c = _fnet(gf_c, cs_c)
            ctx = _merge(ctx, hb_c, mods_c, feats_c, scans_c, fn_c, lw)
            ctx = _ffn(ctx, mods_c, lw, g_fin, False)
    return x
```

```python
import functools
import math

import numpy as np
import jax
import jax.numpy as jnp
from jax import lax
from jax.experimental import pallas as pl
from jax.experimental.pallas import tpu as pltpu

F32 = jnp.float32
BF16 = jnp.bfloat16

D_MODEL = 1024
BR_W = 256
HEAD_DIM = 64
HEADS = 4
N_PARTS = 12
P_IN = N_PARTS * BR_W
GRID_W = 64
GLA_LR = 16
GLA_NORMALIZER = 16.0
RWKV_LR = 64
RWKV_G_LR = 160
D_FF = 2816
ROPE_BASE = 10000.0
EPS = 1e-6

CHUNK = 64
GLA_SUB = 16
RET_CHUNK = 128
HALO = 8
NEG_BIG = -1e30
LOG2E = 1.4426950408889634

(F_GQ, F_GK, F_GV, F_GG, F_RQ, F_RK, F_RV, F_RG, F_WR, F_WV, F_WKK, F_WG, F_BON,
 F_LAF, F_LAB, F_LWF, F_LWB, F_KDF, F_KDB, F_ASF, F_ASB) = range(21)
N_FEAT = 21

VMEM_LIMIT = 56 * 1024 * 1024


def _cparams(sem):
    return pltpu.CompilerParams(dimension_semantics=sem, vmem_limit_bytes=VMEM_LIMIT)


def _dot(a, b):
    return jnp.dot(a.astype(BF16), b.astype(BF16), preferred_element_type=F32)


def _dot_nt(a, b):
    return lax.dot_general(a.astype(BF16), b.astype(BF16), (((1,), (1,)), ((), ())),
                           preferred_element_type=F32)


def _dot_tn(a, b):
    return lax.dot_general(a.astype(BF16), b.astype(BF16), (((0,), (0,)), ((), ())),
                           preferred_element_type=F32)


def _split3(x):
    hi = x.astype(BF16)
    r1 = x - hi.astype(F32)
    mid = r1.astype(BF16)
    lo = (r1 - mid.astype(F32)).astype(BF16)
    return hi, mid, lo


def _dot_exact_lhs(a_bf16, x):
    hi, mid, lo = _split3(x)
    d = lambda t: jnp.dot(a_bf16, t, preferred_element_type=F32)
    return d(hi) + d(mid) + d(lo)


def _sigmoid(x):
    return 1.0 / (1.0 + jnp.exp(-x))


def _silu(x):
    return x * _sigmoid(x)


def _softplus(x):
    return jnp.maximum(x, 0.0) + jnp.log(1.0 + jnp.exp(-jnp.abs(x)))


def _iota(shape, dim):
    return lax.broadcasted_iota(jnp.int32, shape, dim)


def _head_ones():
    r = _iota((BR_W, BR_W), 0) // HEAD_DIM
    c = _iota((BR_W, BR_W), 1) // HEAD_DIM
    return jnp.where(r == c, 1.0, 0.0).astype(BF16)


def _same_head(n_rows, rows_per_head):
    r = _iota((n_rows, BR_W), 0) // rows_per_head
    c = _iota((n_rows, BR_W), 1) // HEAD_DIM
    return r == c


def _stack_heads(x):
    c = x.shape[0]
    xs = jnp.concatenate([x] * HEADS, axis=0)
    return jnp.where(_same_head(HEADS * c, c), xs, 0.0)


def _unstack_sum(y, c):
    out = y[0:c]
    for h in range(1, HEADS):
        out = out + y[h * c:(h + 1) * c]
    return out


def _tri(c, rev, strict=False):
    r = _iota((c, c), 0)
    s = _iota((c, c), 1)
    if rev:
        return (r < s) if strict else (r <= s)
    return (r > s) if strict else (r >= s)


def _tri_blockdiag(c, rev, strict):
    n = HEADS * c
    r = _iota((n, n), 0)
    s = _iota((n, n), 1)
    same = (r // c) == (s // c)
    rr = r % c
    ss = s % c
    if rev:
        t = (rr < ss) if strict else (rr <= ss)
    else:
        t = (rr > ss) if strict else (rr >= ss)
    return jnp.logical_and(same, t)


def _halo_keep(n_ext, tb, i, nt):
    row = _iota((n_ext, 1), 0)
    first = jnp.where(i > 0, 1.0, 0.0)
    last = jnp.where(i < nt - 1, 1.0, 0.0)
    return jnp.where(row < HALO, first, jnp.where(row >= tb + HALO, last, 1.0))


def _rms_rows(x):
    return x * lax.rsqrt(jnp.mean(x * x, axis=-1, keepdims=True) + EPS)


def _mod_kernel(c_ref, w_ref, b_ref, o_ref):
    s = _silu(c_ref[...])
    o_ref[0] = _dot(s, w_ref[0]) + b_ref[0]


def _modulation(cmat, w_ada, b_ada):
    L = w_ada.shape[0]
    nblk = 1536
    return pl.pallas_call(
        _mod_kernel,
        grid=(L, 6 * D_MODEL // nblk),
        in_specs=[
            pl.BlockSpec((16, D_MODEL), lambda l, j: (0, 0)),
            pl.BlockSpec((1, D_MODEL, nblk), lambda l, j: (l, 0, j)),
            pl.BlockSpec((1, 1, nblk), lambda l, j: (l, 0, j)),
        ],
        out_specs=pl.BlockSpec((1, 16, nblk), lambda l, j: (l, 0, j)),
        out_shape=jax.ShapeDtypeStruct((L, 16, 6 * D_MODEL), F32),
        compiler_params=_cparams(("arbitrary", "arbitrary")),
        name="mod",
    )(cmat, w_ada, b_ada.reshape(L, 1, 6 * D_MODEL))


def _pre_kernel(use_rope, tb, nt,
                x_ref, xp_ref, xn_ref, mod_ref, g1_ref, win_ref, wh_ref, wxw_ref, wxa_ref,
                wxg_ref, w2_ref, vec_ref, conv_ref, mu_ref, dft_ref, cos_ref, sin_ref,
                f_ref, hb_ref, fc_ref, fs_ref, hs_ref, ps_ref):
    i = pl.program_id(1)
    n_ext = tb + 2 * HALO
    x_ext = jnp.concatenate([xp_ref[0], x_ref[0], xn_ref[0]], axis=0)
    mod = mod_ref[0]
    shift = mod[:, 0:D_MODEL]
    scale = mod[:, D_MODEL:2 * D_MODEL]
    h_ext = (_rms_rows(x_ext) * g1_ref[...]) * (1.0 + scale) + shift
    h_ext = h_ext * _halo_keep(n_ext, tb, i, nt)
    hs_ref[...] = h_ext
    hb_ext = h_ext.astype(BF16)
    ps_ref[...] = jnp.dot(hb_ext, win_ref[...], preferred_element_type=F32)

    h = hs_ref[HALO:HALO + tb, :]
    prev = hs_ref[HALO - 1:HALO - 1 + tb, :]
    nxt = hs_ref[HALO + 1:HALO + 1 + tb, :]
    hb = h.astype(BF16)
    hb_ref[0] = hb
    xx = 0.5 * (prev + nxt) - h
    mu = mu_ref[...]
    xw = h + xx * mu[0:1]
    xa = h + xx * mu[1:2]
    xg = h + xx * mu[2:3]

    vec = vec_ref[...]
    ba_f, ba_b, w0_f, w0_b = vec[0:1], vec[1:2], vec[2:3], vec[3:4]
    a0_f, a0_b, kkw, kaw, rkw = vec[4:5], vec[5:6], vec[6:7], vec[7:8], vec[8:9]

    def col(j):
        return ps_ref[HALO:HALO + tb, j * BR_W:(j + 1) * BR_W]

    def put(j, val):
        f_ref[0, :, j * BR_W:(j + 1) * BR_W] = val

    put(F_GQ, col(0) * HEAD_DIM ** -0.5)
    put(F_GK, col(1))
    put(F_GV, col(2))
    put(F_GG, col(3))
    zh = jnp.dot(hb, wh_ref[...], preferred_element_type=F32)
    for dr, (ba, dst) in enumerate(((ba_f, F_LAF), (ba_b, F_LAB))):
        z = _dot(zh, w2_ref[dr, 0:128, :]) + ba
        put(dst, -_softplus(-z) * (1.0 / GLA_NORMALIZER))

    rq = col(4)
    rk = col(5) * HEAD_DIM ** -0.5
    if use_rope:
        cosf = cos_ref[...]
        sins = sin_ref[...]
        low = (_iota((tb, BR_W), 1) % HEAD_DIM) < (HEAD_DIM // 2)

        def rope(t):
            partner = jnp.where(low, pltpu.roll(t, BR_W - HEAD_DIM // 2, 1),
                                pltpu.roll(t, HEAD_DIM // 2, 1))
            return t * cosf + partner * sins

        rq = rope(rq)
        rk = rope(rk)
    put(F_RQ, rq)
    put(F_RK, rk)
    put(F_RV, col(6))
    put(F_RG, col(7))

    conv = conv_ref[...]

    def dwconv(j, part):
        lo, hi = (8 + part) * BR_W, (9 + part) * BR_W
        cw = conv[:, part * BR_W:(part + 1) * BR_W]
        return (ps_ref[HALO - 1:HALO - 1 + tb, lo:hi] * cw[0:1]
                + ps_ref[HALO:HALO + tb, lo:hi] * cw[1:2]
                + ps_ref[HALO + 1:HALO + 1 + tb, lo:hi] * cw[2:3])

    r_c = dwconv(8, 0)
    k_c = dwconv(9, 1)
    v_c = dwconv(10, 2)
    ones_h = _head_ones()
    g = _dot(_sigmoid(jnp.dot(xg.astype(BF16), wxg_ref[...], preferred_element_type=F32)),
             w2_ref[6])
    kk = k_c * kkw
    kk = kk * lax.rsqrt(_dot(kk * kk, ones_h) + EPS)
    tw = jnp.tanh(jnp.dot(xw.astype(BF16), wxw_ref[...], preferred_element_type=F32))
    ta = jnp.dot(xa.astype(BF16), wxa_ref[...], preferred_element_type=F32)
    bonus = jnp.zeros((tb, BR_W), F32)
    for dr, (w0, a0, d_lw, d_kd, d_as) in enumerate(
            ((w0_f, a0_f, F_LWF, F_KDF, F_ASF), (w0_b, a0_b, F_LWB, F_KDB, F_ASB))):
        w_raw = -_softplus(-(w0 + _dot(tw, w2_ref[2 + dr, 0:128, :]))) - 0.5
        put(d_lw, -jnp.exp(w_raw))
        a_sig = _sigmoid(a0 + _dot(ta, w2_ref[4 + dr, 0:128, :]))
        kd = k_c * (1.0 + (a_sig - 1.0) * kaw)
        put(d_kd, kd)
        put(d_as, a_sig)
        bonus = bonus + _dot(r_c * kd * rkw, ones_h) * v_c
    put(F_WR, r_c)
    put(F_WV, v_c)
    put(F_WKK, kk)
    put(F_WG, g)
    put(F_BON, bonus)

    fb = col(11).astype(BF16)
    fc_ref[0] = jnp.dot(fb, dft_ref[0], preferred_element_type=F32).astype(BF16)
    fs_ref[0] = jnp.dot(fb, dft_ref[1], preferred_element_type=F32).astype(BF16)


def _pre(x, mods, lw, use_rope, rope_tabs, dft_c):
    B, T, D = x.shape
    tb = min(256, T)
    nt = T // tb
    hb8 = tb // HALO
    n_h = T // HALO
    const2 = lambda b, i: (0, 0)
    const3 = lambda b, i: (0, 0, 0)
    kern = functools.partial(_pre_kernel, use_rope, tb, nt)
    return pl.pallas_call(
        kern,
        grid=(B, nt),
        in_specs=[
            pl.BlockSpec((1, tb, D), lambda b, i: (b, i, 0)),
            pl.BlockSpec((1, HALO, D), lambda b, i: (b, jnp.maximum(i * hb8 - 1, 0), 0)),
            pl.BlockSpec((1, HALO, D), lambda b, i: (b, jnp.minimum((i + 1) * hb8, n_h - 1), 0)),
            pl.BlockSpec((1, 1, 6 * D), lambda b, i: (b, 0, 0)),
            pl.BlockSpec((1, D), const2),
            pl.BlockSpec((D, P_IN), const2),
            pl.BlockSpec((D, 128), const2),
            pl.BlockSpec((D, 128), const2),
            pl.BlockSpec((D, 128), const2),
            pl.BlockSpec((D, 256), const2),
            pl.BlockSpec((7, 256, BR_W), const3),
            pl.BlockSpec((16, BR_W), const2),
            pl.BlockSpec((3, 3 * BR_W), const2),
            pl.BlockSpec((3, D), const2),
            pl.BlockSpec((2, BR_W, BR_W), const3),
            pl.BlockSpec((tb, BR_W), lambda b, i: (i, 0)),
            pl.BlockSpec((tb, BR_W), lambda b, i: (i, 0)),
        ],
        out_specs=[
            pl.BlockSpec((1, tb, N_FEAT * BR_W), lambda b, i: (b, i, 0)),
            pl.BlockSpec((1, tb, D), lambda b, i: (b, i, 0)),
            pl.BlockSpec((1, tb, BR_W), lambda b, i: (b, i, 0)),
            pl.BlockSpec((1, tb, BR_W), lambda b, i: (b, i, 0)),
        ],
        out_shape=[
            jax.ShapeDtypeStruct((B, T, N_FEAT * BR_W), F32),
            jax.ShapeDtypeStruct((B, T, D), BF16),
            jax.ShapeDtypeStruct((B, T, BR_W), BF16),
            jax.ShapeDtypeStruct((B, T, BR_W), BF16),
        ],
        scratch_shapes=[
            pltpu.VMEM((tb + 2 * HALO, D), F32),
            pltpu.VMEM((tb + 2 * HALO, P_IN), F32),
        ],
        compiler_params=_cparams(("parallel", "arbitrary")),
        name="pre",
    )(x, x, x, mods, lw["g1"], lw["w_in"], lw["w_h"], lw["w_xw"], lw["w_xa"], lw["w_xg"],
      lw["w2"], lw["vec"], lw["rwkv_conv"], lw["rwkv_mu"], dft_c, rope_tabs[0], rope_tabs[1])


def _cumsum_chunk(x, rev):
    c = x.shape[0]
    tri = jnp.where(_tri(c, rev), 1.0, 0.0).astype(BF16)
    return _dot_exact_lhs(tri, x)


def _gla_chunks(chains):
    c = chains[0][0].shape[0]
    nblk = c // GLA_SUB
    rowid = _iota((c, 1), 0)
    ones_h = _head_ones()
    same = _same_head(HEADS * GLA_SUB, GLA_SUB)
    bd = (_iota((BR_W, BR_W), 0) // HEAD_DIM) == (_iota((BR_W, BR_W), 1) // HEAD_DIM)
    cums = [_cumsum_chunk(ch[3], ch[5]) for ch in chains]

    offd, inter, st_new = [], [], []
    for (q, k, v, la, st, rev), cum in zip(chains, cums):
        cum_end = cum[0:1] if rev else cum[c - 1:c]
        parts = []
        for blk in range(nblk):
            base = blk * GLA_SUB
            if rev:
                k_lo, k_hi, ref = base + GLA_SUB, c, base + GLA_SUB
            else:
                k_lo, k_hi, ref = 0, base, base - 1
            if k_hi <= k_lo:
                parts.append(None)
                continue
            qt = q[base:base + GLA_SUB] * jnp.exp(cum[base:base + GLA_SUB] - cum[ref:ref + 1])
            kt = k[k_lo:k_hi] * jnp.exp(cum[ref:ref + 1] - cum[k_lo:k_hi])
            sc = _dot_nt(_stack_heads(qt), kt)
            ov = jnp.where(same, _dot(sc, v[k_lo:k_hi]), 0.0)
            parts.append(_unstack_sum(ov, GLA_SUB))
        offd.append(parts)
        inter.append(_dot_nt(q * jnp.exp(cum), st))
        st_new.append(st * jnp.exp(cum_end)
                      + jnp.where(bd, _dot_tn(v, k * jnp.exp(cum_end - cum)), 0.0))

    reds, spans = [], []
    for (q, k, v, la, st, rev), cum in zip(chains, cums):
        cum2 = cum * LOG2E
        pieces, sp = [], []
        for j in range(c):
            base = (j // GLA_SUB) * GLA_SUB
            own = (j // 8) * 8
            others = range(base, own, 8) if rev else range(own + 8, base + GLA_SUB, 8)
            for g0 in (own, *others):
                arg = cum2[g0:g0 + 8] - cum2[j:j + 1]
                if g0 == own:
                    valid = (rowid[g0:g0 + 8] <= j) if rev else (rowid[g0:g0 + 8] >= j)
                    arg = jnp.where(valid, arg, NEG_BIG)
                pieces.append(q[g0:g0 + 8] * (k[j:j + 1] * jnp.exp2(arg)))
                sp.append((j, g0))
        reds.append(jnp.dot(jnp.concatenate(pieces, axis=0).astype(BF16), ones_h,
                            preferred_element_type=F32))
        spans.append(sp)

    outs = []
    for ci, (q, k, v, la, st, rev) in enumerate(chains):
        groups = [None] * (c // 8)
        for idx, (j, g0) in enumerate(spans[ci]):
            piece = reds[ci][idx * 8:idx * 8 + 8] * v[j:j + 1]
            gi = g0 // 8
            groups[gi] = piece if groups[gi] is None else groups[gi] + piece
        blocks = []
        for blk in range(nblk):
            o_blk = jnp.concatenate(groups[blk * GLA_SUB // 8:(blk + 1) * GLA_SUB // 8], axis=0)
            if offd[ci][blk] is not None:
                o_blk = o_blk + offd[ci][blk]
            blocks.append(o_blk)
        outs.append((jnp.concatenate(blocks, axis=0) + inter[ci], st_new[ci]))
    return outs


def _ret_chunk(q, k, v, s, dmat, qdec, kdec, cdec):
    sc = _dot_nt(q, _stack_heads(k)) * dmat
    o = _dot(sc, _stack_heads(v)) + _dot(q * qdec, s)
    bd = (_iota((BR_W, BR_W), 0) // HEAD_DIM) == (_iota((BR_W, BR_W), 1) // HEAD_DIM)
    s_new = s * cdec + jnp.where(bd, _dot_tn(k * kdec, v), 0.0)
    return o, s_new


def _rwkv_chunks(chains):
    c = chains[0][0].shape[0]
    n = HEADS * c
    bf = lambda t: t.astype(BF16)
    masks = {}
    for ch in chains:
        rev = ch[7]
        if rev not in masks:
            masks[rev] = (_tri_blockdiag(c, rev, True), _tri_blockdiag(c, rev, False))
    eye = jnp.where(_iota((n, n), 0) == _iota((n, n), 1), 1.0, 0.0)
    cums = [_cumsum_chunk(ch[5], ch[7]) for ch in chains]
    st = []
    for (r, v, kk, kd, asig, lw, s, rev), cum in zip(chains, cums):
        cum_end = cum[0:1] if rev else cum[c - 1:c]
        einv = jnp.exp(-cum)
        ehat = jnp.exp(cum_end - cum)
        bvec = kk * asig
        st.append(dict(
            a_s=bf(_stack_heads(-kk * jnp.exp(cum - lw))), r_s=bf(_stack_heads(r * jnp.exp(cum))),
            b_s=bf(_stack_heads(bvec * einv)), k_s=bf(_stack_heads(kd * einv)),
            v_s=bf(_stack_heads(v)), bh_s=bf(_stack_heads(bvec * ehat)),
            kh_s=bf(_stack_heads(kd * ehat)), g_end=jnp.exp(cum_end), s=s, sb=bf(s), rev=rev))
    for d in st:
        d["m"] = _dot_nt(jnp.concatenate([d["a_s"], d["r_s"]], axis=0),
                         jnp.concatenate([d["b_s"], d["k_s"]], axis=0))
    for d in st:
        strict, incl = masks[d["rev"]]
        m = d.pop("m")
        d["p"] = jnp.where(strict, m[0:n, 0:n], 0.0)
        d["t"] = eye + d["p"]
        d["m_ak"] = bf(jnp.where(strict, m[0:n, n:2 * n], 0.0))
        d["n_rb"] = bf(jnp.where(incl, m[n:2 * n, 0:n], 0.0))
        d["n_rk"] = bf(jnp.where(incl, m[n:2 * n, n:2 * n], 0.0))
    for d in st:
        d["w"] = _dot_nt(d["a_s"], d["sb"]) + _dot(d["m_ak"], d["v_s"])
        d["y0"] = _dot_nt(d["r_s"], d["sb"]) + _dot(d["n_rk"], d["v_s"])
    for _ in range(int(math.log2(c)) - 1):
        for d in st:
            pb = bf(d["p"])
            d["p"] = jnp.dot(pb, pb, preferred_element_type=F32)
        for d in st:
            d["t"] = d["t"] + _dot(d["t"], d["p"])
    for d in st:
        d["u"] = bf(_dot(d["t"], d["w"]))
    outs = []
    for d in st:
        y = _unstack_sum(d["y0"] + _dot(d["n_rb"], d["u"]), c)
        s_new = d["s"] * d["g_end"] + _dot_tn(jnp.concatenate([d["u"], d["v_s"]], axis=0),
                                              jnp.concatenate([d["bh_s"], d["kh_s"]], axis=0))
        outs.append((y, s_new))
    return outs


def _scan_kernel(kind, n_in, n_tab, nc, bb, *refs):
    fwd = refs[0:n_in]
    bwd = refs[n_in:2 * n_in]
    tabs = refs[2 * n_in:2 * n_in + n_tab]
    s0_ref = refs[2 * n_in + n_tab]
    of_ref, ob_ref, sf_ref, st_ref = refs[2 * n_in + n_tab + 1:]
    i = pl.program_id(1)

    @pl.when(i == 0)
    def _():
        st_ref[...] = s0_ref[...]

    chains = [(dr, b) for b in range(bb) for dr in range(2)]
    ins_of = lambda dr, b: [t[b] for t in (fwd, bwd)[dr]]
    if kind == "rwkv":
        res = _rwkv_chunks([(*ins_of(dr, b), st_ref[dr, b], dr == 1) for dr, b in chains])
    elif kind == "gla":
        res = _gla_chunks([(*ins_of(dr, b), st_ref[dr, b], dr == 1) for dr, b in chains])
    else:
        res = []
        for dr, b in chains:
            dmat, qdec, kdec, cdec = (t[dr] for t in tabs)
            res.append(_ret_chunk(*ins_of(dr, b), st_ref[dr, b], dmat, qdec, kdec, cdec))
    for (dr, b), (o, s_new) in zip(chains, res):
        (of_ref, ob_ref)[dr][b] = o
        st_ref[dr, b] = s_new

    @pl.when(i == nc - 1)
    def _():
        sf_ref[...] = st_ref[...]


_SCAN_ROWS = {"gla": 2, "ret": 2, "rwkv": 2}


def _scan(kind, feats, cols_f, cols_b, s0, tabs=()):
    B, T, _ = feats.shape
    c = RET_CHUNK if kind == "ret" else CHUNK
    c = min(c, T)
    nc = T // c
    bb = _SCAN_ROWS[kind]
    n_in = len(cols_f)
    in_specs = []
    for j in cols_f:
        in_specs.append(pl.BlockSpec((bb, c, BR_W), lambda b, i, j=j: (b, i, j)))
    for j in cols_b:
        in_specs.append(pl.BlockSpec((bb, c, BR_W), lambda b, i, j=j: (b, nc - 1 - i, j)))
    for t in tabs:
        in_specs.append(pl.BlockSpec(t.shape, lambda b, i, nd=t.ndim: (0,) * nd))
    in_specs.append(pl.BlockSpec((2, bb, BR_W, BR_W), lambda b, i: (0, b, 0, 0)))
    kern = functools.partial(_scan_kernel, kind, n_in, len(tabs), nc, bb)
    return pl.pallas_call(
        kern,
        grid=(B // bb, nc),
        in_specs=in_specs,
        out_specs=[
            pl.BlockSpec((bb, c, BR_W), lambda b, i: (b, i, 0)),
            pl.BlockSpec((bb, c, BR_W), lambda b, i: (b, nc - 1 - i, 0)),
            pl.BlockSpec((2, bb, BR_W, BR_W), lambda b, i: (0, b, 0, 0)),
        ],
        out_shape=[
            jax.ShapeDtypeStruct((B, T, BR_W), F32),
            jax.ShapeDtypeStruct((B, T, BR_W), F32),
            jax.ShapeDtypeStruct((2, B, BR_W, BR_W), F32),
        ],
        scratch_shapes=[pltpu.VMEM((2, bb, BR_W, BR_W), F32)],
        compiler_params=_cparams(("parallel", "arbitrary")),
        name=kind,
    )(*([feats] * (2 * n_in)), *tabs, s0)


def _ret_tables(c):
    pos = np.arange(c, dtype=np.float64)
    lane_head = np.arange(BR_W) // HEAD_DIM
    dmats, qd, kd, cd = [], [], [], []
    for dr in range(2):
        expo = -5.0 - np.arange(HEADS, dtype=np.float64)
        if dr == 1:
            expo = expo[::-1]
        log_g = np.log1p(-np.exp2(expo))
        if dr == 0:
            rel = pos[:, None] - pos[None, :]
            qpow = pos + 1.0
            kpow = c - 1.0 - pos
        else:
            rel = pos[None, :] - pos[:, None]
            qpow = c - pos
            kpow = pos
        tri = rel >= 0
        dm = np.where(tri[None], np.exp(np.where(tri, rel, 0.0)[None] * log_g[:, None, None]), 0.0)
        dmats.append(np.concatenate([dm[h] for h in range(HEADS)], axis=1))
        qd.append(np.exp(qpow[:, None] * log_g[lane_head][None, :]))
        kd.append(np.exp(kpow[:, None] * log_g[lane_head][None, :]))
        cd.append(np.exp(c * log_g[lane_head])[None, :])
    f = lambda xs: jnp.asarray(np.stack(xs), dtype=F32)
    return f(dmats), f(qd), f(kd), f(cd)


def _fnet_dense_kernel(t_len, cs_ref, g1_ref, g2_ref, o_ref):
    o_ref[0] = (jnp.dot(cs_ref[:, 0:t_len], g1_ref[0], preferred_element_type=F32)
                + jnp.dot(cs_ref[:, t_len:2 * t_len], g2_ref[0], preferred_element_type=F32))


def _fnet_dense(g1, g2, cs):
    B, T, _ = g1.shape
    tb = min(512, T)
    return pl.pallas_call(
        functools.partial(_fnet_dense_kernel, T),
        grid=(T // tb, B),
        in_specs=[
            pl.BlockSpec((tb, 2 * T), lambda i, b: (i, 0)),
            pl.BlockSpec((1, T, BR_W), lambda i, b: (b, 0, 0)),
            pl.BlockSpec((1, T, BR_W), lambda i, b: (b, 0, 0)),
        ],
        out_specs=pl.BlockSpec((1, tb, BR_W), lambda i, b: (b, i, 0)),
        out_shape=jax.ShapeDtypeStruct((B, T, BR_W), F32),
        compiler_params=_cparams(("arbitrary", "arbitrary")),
        name="fnet",
    )(cs, g1, g2)


def _fft1_kernel(n, f1_ref, tc_ref, ts_ref, x1_ref, x2_ref, zr_ref, zi_ref):
    f1 = f1_ref[...]
    r1 = jnp.dot(f1, x1_ref[0], preferred_element_type=F32)
    r2 = jnp.dot(f1, x2_ref[0], preferred_element_type=F32)
    yr = r1[0:n] - r2[n:2 * n]
    yi = -r2[0:n] - r1[n:2 * n]
    tc = tc_ref[...]
    ts = ts_ref[...]
    zr_ref[0] = (tc * yr + ts * yi).astype(BF16)
    zi_ref[0] = (tc * yi - ts * yr).astype(BF16)


def _fft2_kernel(rb, f3_ref, zr_ref, zi_ref, o_ref):
    f3 = f3_ref[...]
    for j in range(rb):
        z = jnp.concatenate([zr_ref[0, j], zi_ref[0, j]], axis=0)
        o_ref[0, j] = jnp.dot(f3, z, preferred_element_type=F32)


def _fnet_fft(g1, g2):
    B, T, _ = g1.shape
    n = int(round(math.sqrt(T)))
    assert n * n == T
    idx = np.arange(n)
    a_n = 2.0 * np.pi * ((idx[:, None] * idx[None, :]) % n) / n
    f1 = jnp.asarray(np.concatenate([np.cos(a_n), np.sin(a_n)], axis=0), dtype=BF16)
    f3 = jnp.asarray(np.concatenate([np.cos(a_n), np.sin(a_n)], axis=1), dtype=BF16)
    a_t = 2.0 * np.pi * (idx[:, None] * idx[None, :]) / T
    tc = jnp.repeat(jnp.asarray(np.cos(a_t), dtype=F32), BR_W, axis=1)
    ts = jnp.repeat(jnp.asarray(np.sin(a_t), dtype=F32), BR_W, axis=1)
    lanes = n * BR_W
    nb = min(4096, lanes)
    x1 = g1.reshape(B, n, lanes)
    x2 = g2.reshape(B, n, lanes)
    zr, zi = pl.pallas_call(
        functools.partial(_fft1_kernel, n),
        grid=(lanes // nb, B),
        in_specs=[
            pl.BlockSpec((2 * n, n), lambda j, b: (0, 0)),
            pl.BlockSpec((n, nb), lambda j, b: (0, j)),
            pl.BlockSpec((n, nb), lambda j, b: (0, j)),
            pl.BlockSpec((1, n, nb), lambda j, b: (b, 0, j)),
            pl.BlockSpec((1, n, nb), lambda j, b: (b, 0, j)),
        ],
        out_specs=[pl.BlockSpec((1, n, nb), lambda j, b: (b, 0, j)),
                   pl.BlockSpec((1, n, nb), lambda j, b: (b, 0, j))],
        out_shape=[jax.ShapeDtypeStruct((B, n, lanes), BF16),
                   jax.ShapeDtypeStruct((B, n, lanes), BF16)],
        compiler_params=_cparams(("arbitrary", "arbitrary")),
        name="fft1",
    )(f1, tc, ts, x1, x2)
    rb = 16
    o = pl.pallas_call(
        functools.partial(_fft2_kernel, rb),
        grid=(B, n // rb),
        in_specs=[
            pl.BlockSpec((n, 2 * n), lambda b, j: (0, 0)),
            pl.BlockSpec((1, rb, n, BR_W), lambda b, j: (b, j, 0, 0)),
            pl.BlockSpec((1, rb, n, BR_W), lambda b, j: (b, j, 0, 0)),
        ],
        out_specs=pl.BlockSpec((1, rb, n, BR_W), lambda b, j: (b, j, 0, 0)),
        out_shape=jax.ShapeDtypeStruct((B, n, n, BR_W), F32),
        compiler_params=_cparams(("parallel", "arbitrary")),
        name="fft2",
    )(f3, zr.reshape(B, n, n, BR_W), zi.reshape(B, n, n, BR_W))
    return o.transpose(0, 2, 1, 3).reshape(B, T, BR_W)


def _time_dft_table(T):
    t = np.arange(T)
    ang = ((t[:, None] * t[None, :]) % T) * (2.0 * np.pi / T)
    return jnp.asarray(np.concatenate([np.cos(ang), -np.sin(ang)], axis=1), dtype=BF16)


def _channel_dft_tables(T):
    ch = np.arange(HEAD_DIM)
    a64 = 2.0 * np.pi * ((ch[:, None] * ch[None, :]) % HEAD_DIM) / HEAD_DIM
    scale = (T * HEAD_DIM) ** -0.5
    eye = np.eye(HEADS)
    cbd = np.kron(eye, np.cos(a64)) * scale
    sbd = np.kron(eye, np.sin(a64)) * scale
    return jnp.asarray(np.stack([cbd, sbd]), dtype=BF16)


def _head_norm(o, ones_h, center):
    if center:
        o = o - _dot(o, ones_h) * (1.0 / HEAD_DIM)
    var = _dot(o * o, ones_h) * (1.0 / HEAD_DIM)
    return o * lax.rsqrt(var + EPS)


def _merge_kernel(x_ref, hb_ref, mod_ref, gg_ref, rg_ref, wg_ref, bon_ref,
                  ogf_ref, ogb_ref, orf_ref, orb_ref, oyf_ref, oyb_ref, fn_ref,
                  gn_ref, wgate_ref, bgate_ref, wbr_ref, wout_ref, o_ref):
    ones_h = _head_ones()
    gn = gn_ref[...]
    gla = _head_norm(ogf_ref[0] + ogb_ref[0], ones_h, False) * gn[0:1] * _silu(gg_ref[0])
    ret = _head_norm(orf_ref[0] + orb_ref[0], ones_h, True) * gn[1:2] * _silu(rg_ref[0])
    rwkv = (_head_norm(oyf_ref[0] + oyb_ref[0], ones_h, True) * gn[2:3] + bon_ref[0]) * wg_ref[0]
    outs = (gla, ret, rwkv, fn_ref[0])
    hb = hb_ref[0]
    z = None
    for br in range(4):
        gate = _sigmoid(jnp.dot(hb, wgate_ref[br], preferred_element_type=F32) + bgate_ref[br])
        term = gate * _dot(outs[br], wbr_ref[br])
        z = term if z is None else z + term
    y = _dot(z, wout_ref[...])
    gate1 = mod_ref[0][:, 2 * D_MODEL:3 * D_MODEL]
    o_ref[0] = x_ref[0] + gate1 * y


def _merge(x, hb, mods, feats, scans, fnet_out, lw):
    B, T, D = x.shape
    tb = min(256, T)
    tok = lambda w: pl.BlockSpec((1, tb, w), lambda b, i: (b, i, 0))
    fcol = lambda j: pl.BlockSpec((1, tb, BR_W), lambda b, i, j=j: (b, i, j))
    const2 = lambda b, i: (0, 0)
    const3 = lambda b, i: (0, 0, 0)
    return pl.pallas_call(
        _merge_kernel,
        grid=(B, T // tb),
        in_specs=[
            tok(D), tok(D),
            pl.BlockSpec((1, 1, 6 * D), lambda b, i: (b, 0, 0)),
            fcol(F_GG), fcol(F_RG), fcol(F_WG), fcol(F_BON),
            tok(BR_W), tok(BR_W), tok(BR_W), tok(BR_W), tok(BR_W), tok(BR_W), tok(BR_W),
            pl.BlockSpec((3, BR_W), const2),
            pl.BlockSpec((4, D, D), const3),
            pl.BlockSpec((4, 1, D), const3),
            pl.BlockSpec((4, BR_W, D), const3),
            pl.BlockSpec((D, D), const2),
        ],
        out_specs=tok(D),
        out_shape=jax.ShapeDtypeStruct((B, T, D), F32),
        compiler_params=_cparams(("parallel", "arbitrary")),
        name="merge",
    )(x, hb, mods, feats, feats, feats, feats, *scans, fnet_out,
      lw["gn"], lw["w_gate"], lw["b_gate"], lw["w_br"], lw["w_out"])


FFN_COL_CHUNKS = (1024, 1024, 768)


def _ffn_kernel(final, tb, nt,
                x_ref, xp_ref, xn_ref, mod_ref, g2_ref, up_ref, cw_ref, cb_ref,
                down_ref, gfin_ref, o_ref, act_ref):
    i = pl.program_id(1)
    n_ext = tb + 2 * HALO
    x_ext = jnp.concatenate([xp_ref[0], x_ref[0], xn_ref[0]], axis=0)
    mod = mod_ref[0]
    shift = mod[:, 3 * D_MODEL:4 * D_MODEL]
    scale = mod[:, 4 * D_MODEL:5 * D_MODEL]
    h2 = (_rms_rows(x_ext) * g2_ref[...]) * (1.0 + scale) + shift
    h2 = h2 * _halo_keep(n_ext, tb, i, nt)
    h2_ext = h2.astype(BF16)
    h2_mid = h2[HALO:HALO + tb].astype(BF16)
    cw = cw_ref[...]
    cb = cb_ref[...]
    lo = 0
    for width in FFN_COL_CHUNKS:
        a_ext = jnp.dot(h2_ext, up_ref[:, lo:lo + width], preferred_element_type=F32)
        u = jnp.dot(h2_mid, up_ref[:, D_FF + lo:D_FF + lo + width], preferred_element_type=F32)
        a = (a_ext[HALO - 1:HALO - 1 + tb] * cw[0:1, lo:lo + width]
             + a_ext[HALO:HALO + tb] * cw[1:2, lo:lo + width]
             + a_ext[HALO + 1:HALO + 1 + tb] * cw[2:3, lo:lo + width] + cb[:, lo:lo + width])
        act_ref[:, lo:lo + width] = (_silu(a) * u).astype(BF16)
        lo += width
    y = jnp.dot(act_ref[...], down_ref[...], preferred_element_type=F32)
    gate2 = mod[:, 5 * D_MODEL:6 * D_MODEL]
    res = x_ref[0] + gate2 * y
    if final:
        res = _rms_rows(res) * gfin_ref[...]
    o_ref[0] = res


def _ffn(x, mods, lw, g_final, final):
    B, T, D = x.shape
    tb = min(512, T)
    nt = T // tb
    hb8 = tb // HALO
    n_h = T // HALO
    const2 = lambda b, i: (0, 0)
    resident = lambda shape: pl.BlockSpec(shape, const2, pipeline_mode=pl.Buffered(1))
    kern = functools.partial(_ffn_kernel, final, tb, nt)
    return pl.pallas_call(
        kern,
        grid=(B, nt),
        in_specs=[
            pl.BlockSpec((1, tb, D), lambda b, i: (b, i, 0)),
            pl.BlockSpec((1, HALO, D), lambda b, i: (b, jnp.maximum(i * hb8 - 1, 0), 0)),
            pl.BlockSpec((1, HALO, D), lambda b, i: (b, jnp.minimum((i + 1) * hb8, n_h - 1), 0)),
            pl.BlockSpec((1, 1, 6 * D), lambda b, i: (b, 0, 0)),
            pl.BlockSpec((1, D), const2),
            resident((D, 2 * D_FF)),
            pl.BlockSpec((3, D_FF), const2),
            pl.BlockSpec((1, D_FF), const2),
            resident((D_FF, D)),
            pl.BlockSpec((1, D), const2),
        ],
        out_specs=pl.BlockSpec((1, tb, D), lambda b, i: (b, i, 0)),
        out_shape=jax.ShapeDtypeStruct((B, T, D), F32),
        scratch_shapes=[pltpu.VMEM((tb, D_FF), BF16)],
        compiler_params=_cparams(("parallel", "arbitrary")),
        name="ffn",
    )(x, x, x, mods, lw["g2"], lw["ffn_up"], lw["ffn_conv"], lw["ffn_conv_b"],
      lw["ffn_down"], g_final)


def _rope_tables(T):
    rows = T // GRID_W
    row = jnp.repeat(jnp.arange(rows, dtype=F32), GRID_W)
    colp = jnp.tile(jnp.arange(GRID_W, dtype=F32), rows)
    n_freq = HEAD_DIM // 4
    inv = ROPE_BASE ** (-jnp.arange(n_freq, dtype=F32) / n_freq)
    ang = jnp.concatenate([row[:, None] * inv, colp[:, None] * inv], axis=-1)
    cos, sin = jnp.cos(ang), jnp.sin(ang)
    cos_h = jnp.concatenate([cos, cos], axis=-1)
    sin_h = jnp.concatenate([-sin, sin], axis=-1)
    return jnp.tile(cos_h, (1, HEADS)), jnp.tile(sin_h, (1, HEADS))


def _pad_rows(w, lo, n):
    return jnp.zeros((n, w.shape[1]), w.dtype).at[lo:lo + w.shape[0]].set(w)


def _layer_weights(l, p):
    D = D_MODEL
    bf = lambda t: t.astype(BF16)
    zcols = lambda w, n: jnp.concatenate([w, jnp.zeros((w.shape[0], n - w.shape[1]), w.dtype)], axis=1)
    w_h = zcols(jnp.concatenate([p["gla_wa1"][l, 0], p["gla_wa1"][l, 1]], axis=1), 128)
    w_xw = jnp.concatenate([p["rwkv_w1"][l, 0], p["rwkv_w1"][l, 1]], axis=1)
    w_xa = jnp.concatenate([p["rwkv_a1"][l, 0], p["rwkv_a1"][l, 1]], axis=1)
    w_xg = zcols(p["rwkv_g1"][l], 256)
    w2 = jnp.stack([
        _pad_rows(p["gla_wa2"][l, 0], 0, 256), _pad_rows(p["gla_wa2"][l, 1], GLA_LR, 256),
        _pad_rows(p["rwkv_w2"][l, 0], 0, 256), _pad_rows(p["rwkv_w2"][l, 1], RWKV_LR, 256),
        _pad_rows(p["rwkv_a2"][l, 0], 0, 256), _pad_rows(p["rwkv_a2"][l, 1], RWKV_LR, 256),
        _pad_rows(p["rwkv_g2"][l], 0, 256),
    ])
    vec = jnp.stack([p["gla_ba"][l, 0], p["gla_ba"][l, 1], p["rwkv_w0"][l, 0], p["rwkv_w0"][l, 1],
                     p["rwkv_a0"][l, 0], p["rwkv_a0"][l, 1], p["rwkv_kk"][l], p["rwkv_ka"][l],
                     p["rwkv_rk"][l]])
    vec = jnp.concatenate([vec, jnp.zeros((16 - vec.shape[0], BR_W), F32)], axis=0)
    return {
        "g1": p["g_norm1"][l].reshape(1, D), "g2": p["g_norm2"][l].reshape(1, D),
        "w_in": bf(p["w_in"][l]), "w_h": bf(w_h), "w_xw": bf(w_xw), "w_xa": bf(w_xa),
        "w_xg": bf(w_xg), "w2": bf(w2), "vec": vec,
        "rwkv_conv": p["rwkv_conv"][l], "rwkv_mu": p["rwkv_mu"][l],
        "gn": jnp.stack([p["gla_gn"][l], p["ret_gn"][l], p["rwkv_gn"][l]]),
        "w_gate": bf(p["w_gate"][l]), "b_gate": p["b_gate"][l].reshape(4, 1, D),
        "w_br": bf(p["w_br"][l]), "w_out": bf(p["w_out"][l]),
        "ffn_up": bf(p["ffn_up"][l]), "ffn_conv": p["ffn_conv"][l],
        "ffn_conv_b": p["ffn_conv_b"][l].reshape(1, D_FF), "ffn_down": bf(p["ffn_down"][l]),
    }


_GLA_F = (F_GQ, F_GK, F_GV, F_LAF)
_GLA_B = (F_GQ, F_GK, F_GV, F_LAB)
_RET = (F_RQ, F_RK, F_RV)
_RWKV_F = (F_WR, F_WV, F_WKK, F_KDF, F_ASF, F_LWF)
_RWKV_B = (F_WR, F_WV, F_WKK, F_KDB, F_ASB, F_LWB)


def _mixers(feats_c, feats_l, ret_tabs_c, ret_tabs_l, need_ctx):
    B = feats_l.shape[0]
    zero = jnp.zeros((2, B, BR_W, BR_W), F32)
    out_c, out_l = [], []
    for kind, cf, cb, tabs in (("gla", _GLA_F, _GLA_B, None), ("ret", _RET, _RET, True),
                               ("rwkv", _RWKV_F, _RWKV_B, None)):
        tc = ret_tabs_c if tabs else ()
        tl = ret_tabs_l if tabs else ()
        of_c, ob_c, s_c = _scan(kind, feats_c, cf, cb, zero, tc)
        of_l, ob_l, _ = _scan(kind, feats_l, cf, cb, s_c, tl)
        out_c += [of_c, ob_c]
        out_l += [of_l, ob_l]
    return out_l, (out_c if need_ctx else None)


def kernel(x, c, ctx, c_ctx, w_ada, b_ada, g_norm1, g_norm2, w_in, gla_wa1, gla_wa2, gla_ba, gla_gn, ret_gn, rwkv_conv, rwkv_mu, rwkv_w0, rwkv_w1, rwkv_w2, rwkv_a0, rwkv_a1, rwkv_a2, rwkv_g1, rwkv_g2, rwkv_kk, rwkv_ka, rwkv_rk, rwkv_gn, w_gate, b_gate, w_br, w_out, ffn_up, ffn_conv, ffn_conv_b, ffn_down, g_final):
    p = dict(g_norm1=g_norm1, g_norm2=g_norm2, w_in=w_in, gla_wa1=gla_wa1, gla_wa2=gla_wa2,
             gla_ba=gla_ba, gla_gn=gla_gn, ret_gn=ret_gn, rwkv_conv=rwkv_conv, rwkv_mu=rwkv_mu,
             rwkv_w0=rwkv_w0, rwkv_w1=rwkv_w1, rwkv_w2=rwkv_w2, rwkv_a0=rwkv_a0,
             rwkv_a1=rwkv_a1, rwkv_a2=rwkv_a2, rwkv_g1=rwkv_g1, rwkv_g2=rwkv_g2,
             rwkv_kk=rwkv_kk, rwkv_ka=rwkv_ka, rwkv_rk=rwkv_rk, rwkv_gn=rwkv_gn,
             w_gate=w_gate, b_gate=b_gate, w_br=w_br, w_out=w_out, ffn_up=ffn_up,
             ffn_conv=ffn_conv, ffn_conv_b=ffn_conv_b, ffn_down=ffn_down)
    B, T, D = x.shape
    Tc = ctx.shape[1]
    depth = w_ada.shape[0]

    cmat = jnp.concatenate([c, c_ctx[None, :], jnp.zeros((16 - B - 1, D), F32)], axis=0)
    mod_all = _modulation(cmat, w_ada, b_ada)

    rope_l = _rope_tables(T)
    rope_c = (jnp.zeros((Tc, BR_W), F32), jnp.zeros((Tc, BR_W), F32))
    dft_l = _channel_dft_tables(T)
    dft_c = _channel_dft_tables(Tc)
    cs_c = _time_dft_table(Tc)
    ret_tabs_l = _ret_tables(min(RET_CHUNK, T))
    ret_tabs_c = _ret_tables(min(RET_CHUNK, Tc))
    g_fin = g_final.reshape(1, D)

    for l in range(depth):
        last = l == depth - 1
        lw = _layer_weights(l, p)
        mods_l = mod_all[l, 0:B].reshape(B, 1, 6 * D)
        mods_c = jnp.broadcast_to(mod_all[l, B:B + 1].reshape(1, 1, 6 * D), (B, 1, 6 * D))

        feats_l, hb_l, g1_l, g2_l = _pre(x, mods_l, lw, True, rope_l, dft_l)
        feats_c, hb_c, g1_c, g2_c = _pre(ctx, mods_c, lw, False, rope_c, dft_c)
        scans_l, scans_c = _mixers(feats_c, feats_l, ret_tabs_c, ret_tabs_l, not last)

        fn_l = _fnet_fft(g1_l, g2_l)
        x = _merge(x, hb_l, mods_l, feats_l, scans_l, fn_l, lw)
        x = _ffn(x, mods_l, lw, g_fin, last)
        if not last:
            fn_c = _fnet_dense(g1_c, g2_c, cs_c)
            ctx = _merge(ctx, hb_c, mods_c, feats_c, scans_c, fn_c, lw)
            ctx = _ffn(ctx, mods_c, lw, g_fin, False)
    return x
```

```python
import functools
import math

import numpy as np
import jax
import jax.numpy as jnp
from jax import lax
from jax.experimental import pallas as pl
from jax.experimental.pallas import tpu as pltpu

F32 = jnp.float32
BF16 = jnp.bfloat16

D_MODEL = 1024
BR_W = 256
HEAD_DIM = 64
HEADS = 4
N_PARTS = 12
P_IN = N_PARTS * BR_W
GRID_W = 64
GLA_LR = 16
GLA_NORMALIZER = 16.0
RWKV_LR = 64
RWKV_G_LR = 160
D_FF = 2816
ROPE_BASE = 10000.0
EPS = 1e-6

SCAN_CHUNK = {"gla": 64, "ret": 128, "rwkv": 64}
GLA_SUB = 16
HALO = 8
NEG_BIG = -1e30
LOG2E = 1.4426950408889634

(F_GQ, F_GK, F_GV, F_GG, F_RQ, F_RK, F_RV, F_RG, F_WR, F_WV, F_WKK, F_WG, F_BON,
 F_LAF, F_LAB, F_LWF, F_LWB, F_KDF, F_KDB, F_ASF, F_ASB) = range(21)
N_FEAT = 21

LR_G, LR_GS, LR_H, LR_W, LR_A, LR_WS, LR_AS = 0, 256, 512, 640, 768, 896, 1024
LR_COLS = 1280

VMEM_LIMIT = 56 * 1024 * 1024


def _cparams(sem):
    return pltpu.CompilerParams(dimension_semantics=sem, vmem_limit_bytes=VMEM_LIMIT)


def _dot(a, b):
    return jnp.dot(a.astype(BF16), b.astype(BF16), preferred_element_type=F32)


def _dot_nt(a, b):
    return lax.dot_general(a.astype(BF16), b.astype(BF16), (((1,), (1,)), ((), ())),
                           preferred_element_type=F32)


def _dot_tn(a, b):
    return lax.dot_general(a.astype(BF16), b.astype(BF16), (((0,), (0,)), ((), ())),
                           preferred_element_type=F32)


def _split3(x):
    hi = x.astype(BF16)
    r1 = x - hi.astype(F32)
    mid = r1.astype(BF16)
    lo = (r1 - mid.astype(F32)).astype(BF16)
    return hi, mid, lo


def _dot_exact_lhs(a_bf16, x):
    hi, mid, lo = _split3(x)
    d = lambda t: jnp.dot(a_bf16, t, preferred_element_type=F32)
    return d(hi) + d(mid) + d(lo)


def _sigmoid(x):
    return 1.0 / (1.0 + jnp.exp(-x))


def _silu(x):
    return x * _sigmoid(x)


def _softplus(x):
    return jnp.maximum(x, 0.0) + jnp.log(1.0 + jnp.exp(-jnp.abs(x)))


def _iota(shape, dim):
    return lax.broadcasted_iota(jnp.int32, shape, dim)


def _head_ones():
    r = _iota((BR_W, BR_W), 0) // HEAD_DIM
    c = _iota((BR_W, BR_W), 1) // HEAD_DIM
    return jnp.where(r == c, 1.0, 0.0).astype(BF16)


def _same_head(n_rows, rows_per_head):
    r = _iota((n_rows, BR_W), 0) // rows_per_head
    c = _iota((n_rows, BR_W), 1) // HEAD_DIM
    return r == c


def _stack_heads(x):
    c = x.shape[0]
    xs = jnp.concatenate([x] * HEADS, axis=0)
    return jnp.where(_same_head(HEADS * c, c), xs, 0.0)


def _unstack_sum(y, c):
    out = y[0:c]
    for h in range(1, HEADS):
        out = out + y[h * c:(h + 1) * c]
    return out


def _tri(c, rev, strict=False):
    r = _iota((c, c), 0)
    s = _iota((c, c), 1)
    if rev:
        return (r < s) if strict else (r <= s)
    return (r > s) if strict else (r >= s)


def _tri_blockdiag(c, rev, strict):
    n = HEADS * c
    r = _iota((n, n), 0)
    s = _iota((n, n), 1)
    same = (r // c) == (s // c)
    rr = r % c
    ss = s % c
    if rev:
        t = (rr < ss) if strict else (rr <= ss)
    else:
        t = (rr > ss) if strict else (rr >= ss)
    return jnp.logical_and(same, t)


def _halo_keep(n_ext, tb, i, nt):
    row = _iota((n_ext, 1), 0)
    first = jnp.where(i > 0, 1.0, 0.0)
    last = jnp.where(i < nt - 1, 1.0, 0.0)
    return jnp.where(row < HALO, first, jnp.where(row >= tb + HALO, last, 1.0))


def _rms_rows(x):
    return x * lax.rsqrt(jnp.mean(x * x, axis=-1, keepdims=True) + EPS)


def _mod_kernel(c_ref, w_ref, b_ref, o_ref):
    s = _silu(c_ref[...])
    o_ref[0] = _dot(s, w_ref[0]) + b_ref[0]


def _modulation(cmat, w_ada, b_ada):
    L = w_ada.shape[0]
    nblk = 1536
    return pl.pallas_call(
        _mod_kernel,
        grid=(L, 6 * D_MODEL // nblk),
        in_specs=[
            pl.BlockSpec((16, D_MODEL), lambda l, j: (0, 0)),
            pl.BlockSpec((1, D_MODEL, nblk), lambda l, j: (l, 0, j)),
            pl.BlockSpec((1, 1, nblk), lambda l, j: (l, 0, j)),
        ],
        out_specs=pl.BlockSpec((1, 16, nblk), lambda l, j: (l, 0, j)),
        out_shape=jax.ShapeDtypeStruct((L, 16, 6 * D_MODEL), F32),
        compiler_params=_cparams(("arbitrary", "arbitrary")),
        name="mod",
    )(cmat, w_ada, b_ada.reshape(L, 1, 6 * D_MODEL))


def _pre_kernel(use_rope, tb, nt,
                x_ref, xp_ref, xn_ref, mod_ref, g1_ref, win_ref, wlr_ref,
                w2_ref, vec_ref, conv_ref, dft_ref, cos_ref, sin_ref,
                f_ref, hb_ref, fc_ref, fs_ref, lr_ref, ps_ref):
    i = pl.program_id(1)
    n_ext = tb + 2 * HALO
    x_ext = jnp.concatenate([xp_ref[0], x_ref[0], xn_ref[0]], axis=0)
    mod = mod_ref[0]
    shift = mod[:, 0:D_MODEL]
    scale = mod[:, D_MODEL:2 * D_MODEL]
    h_ext = (_rms_rows(x_ext) * g1_ref[...]) * (1.0 + scale) + shift
    h_ext = h_ext * _halo_keep(n_ext, tb, i, nt)
    hb_ext = h_ext.astype(BF16)
    hb = h_ext[HALO:HALO + tb].astype(BF16)
    hb_ref[0] = hb
    lr_ref[...] = jnp.dot(hb_ext, wlr_ref[...], preferred_element_type=F32)
    ps_ref[...] = jnp.dot(hb_ext, win_ref[:, 8 * BR_W:11 * BR_W], preferred_element_type=F32)

    def mid(lo, hi):
        return lr_ref[HALO:HALO + tb, lo:hi]

    def shifted(lo, hi):
        return (0.5 * (lr_ref[HALO - 1:HALO - 1 + tb, lo:hi] + lr_ref[HALO + 1:HALO + 1 + tb, lo:hi])
                - lr_ref[HALO:HALO + tb, lo:hi])

    vec = vec_ref[...]
    ba_f, ba_b, w0_f, w0_b = vec[0:1], vec[1:2], vec[2:3], vec[3:4]
    a0_f, a0_b, kkw, kaw, rkw = vec[4:5], vec[5:6], vec[6:7], vec[7:8], vec[8:9]

    def put(j, val):
        f_ref[0, :, j * BR_W:(j + 1) * BR_W] = val

    zh = mid(LR_H, LR_H + 128)
    tw = jnp.tanh(mid(LR_W, LR_W + 128) + shifted(LR_WS, LR_WS + 128))
    ta = mid(LR_A, LR_A + 128) + shifted(LR_AS, LR_AS + 128)
    sg = _sigmoid(mid(LR_G, LR_G + 256) + shifted(LR_GS, LR_GS + 256))

    p_gla = jnp.dot(hb, win_ref[:, 0:4 * BR_W], preferred_element_type=F32)
    put(F_GQ, p_gla[:, 0:BR_W] * HEAD_DIM ** -0.5)
    put(F_GK, p_gla[:, BR_W:2 * BR_W])
    put(F_GV, p_gla[:, 2 * BR_W:3 * BR_W])
    put(F_GG, p_gla[:, 3 * BR_W:4 * BR_W])

    z_la = [_dot(zh, w2_ref[dr, 0:128, :]) for dr in range(2)]
    z_w = [_dot(tw, w2_ref[2 + dr, 0:128, :]) for dr in range(2)]
    z_a = [_dot(ta, w2_ref[4 + dr, 0:128, :]) for dr in range(2)]
    put(F_WG, _dot(sg, w2_ref[6]))

    for dr, (ba, dst) in enumerate(((ba_f, F_LAF), (ba_b, F_LAB))):
        put(dst, -_softplus(-(z_la[dr] + ba)) * (1.0 / GLA_NORMALIZER))

    p_ret = jnp.dot(hb, win_ref[:, 4 * BR_W:8 * BR_W], preferred_element_type=F32)

    conv = conv_ref[...]

    def dwconv(part):
        lo, hi = part * BR_W, (part + 1) * BR_W
        cw = conv[:, lo:hi]
        return (ps_ref[HALO - 1:HALO - 1 + tb, lo:hi] * cw[0:1]
                + ps_ref[HALO:HALO + tb, lo:hi] * cw[1:2]
                + ps_ref[HALO + 1:HALO + 1 + tb, lo:hi] * cw[2:3])

    r_c = dwconv(0)
    k_c = dwconv(1)
    v_c = dwconv(2)
    ones_h = _head_ones()
    kk = k_c * kkw
    kk = kk * lax.rsqrt(_dot(kk * kk, ones_h) + EPS)
    put(F_WR, r_c)
    put(F_WV, v_c)
    put(F_WKK, kk)

    p_fnet = jnp.dot(hb, win_ref[:, 11 * BR_W:12 * BR_W], preferred_element_type=F32)

    bonus = jnp.zeros((tb, BR_W), F32)
    for dr, (w0, a0, d_lw, d_kd, d_as) in enumerate(
            ((w0_f, a0_f, F_LWF, F_KDF, F_ASF), (w0_b, a0_b, F_LWB, F_KDB, F_ASB))):
        w_raw = -_softplus(-(w0 + z_w[dr])) - 0.5
        put(d_lw, -jnp.exp(w_raw))
        a_sig = _sigmoid(a0 + z_a[dr])
        kd = k_c * (1.0 + (a_sig - 1.0) * kaw)
        put(d_kd, kd)
        put(d_as, a_sig)
        bonus = bonus + _dot(r_c * kd * rkw, ones_h) * v_c
    put(F_BON, bonus)

    rq = p_ret[:, 0:BR_W]
    rk = p_ret[:, BR_W:2 * BR_W] * HEAD_DIM ** -0.5
    if use_rope:
        cosf = cos_ref[...]
        sins = sin_ref[...]
        low = (_iota((tb, BR_W), 1) % HEAD_DIM) < (HEAD_DIM // 2)

        def rope(t):
            partner = jnp.where(low, pltpu.roll(t, BR_W - HEAD_DIM // 2, 1),
                                pltpu.roll(t, HEAD_DIM // 2, 1))
            return t * cosf + partner * sins

        rq = rope(rq)
        rk = rope(rk)
    put(F_RQ, rq)
    put(F_RK, rk)
    put(F_RV, p_ret[:, 2 * BR_W:3 * BR_W])
    put(F_RG, p_ret[:, 3 * BR_W:4 * BR_W])

    fb = p_fnet.astype(BF16)
    fc_ref[0] = jnp.dot(fb, dft_ref[0], preferred_element_type=F32).astype(BF16)
    fs_ref[0] = jnp.dot(fb, dft_ref[1], preferred_element_type=F32).astype(BF16)


def _pre(x, mods, lw, use_rope, rope_tabs, dft_c):
    B, T, D = x.shape
    tb = min(256, T)
    nt = T // tb
    hb8 = tb // HALO
    n_h = T // HALO
    const2 = lambda b, i: (0, 0)
    const3 = lambda b, i: (0, 0, 0)
    kern = functools.partial(_pre_kernel, use_rope, tb, nt)
    return pl.pallas_call(
        kern,
        grid=(B, nt),
        in_specs=[
            pl.BlockSpec((1, tb, D), lambda b, i: (b, i, 0)),
            pl.BlockSpec((1, HALO, D), lambda b, i: (b, jnp.maximum(i * hb8 - 1, 0), 0)),
            pl.BlockSpec((1, HALO, D), lambda b, i: (b, jnp.minimum((i + 1) * hb8, n_h - 1), 0)),
            pl.BlockSpec((1, 1, 6 * D), lambda b, i: (b, 0, 0)),
            pl.BlockSpec((1, D), const2),
            pl.BlockSpec((D, P_IN), const2),
            pl.BlockSpec((D, LR_COLS), const2),
            pl.BlockSpec((7, 256, BR_W), const3),
            pl.BlockSpec((16, BR_W), const2),
            pl.BlockSpec((3, 3 * BR_W), const2),
            pl.BlockSpec((2, BR_W, BR_W), const3),
            pl.BlockSpec((tb, BR_W), lambda b, i: (i, 0)),
            pl.BlockSpec((tb, BR_W), lambda b, i: (i, 0)),
        ],
        out_specs=[
            pl.BlockSpec((1, tb, N_FEAT * BR_W), lambda b, i: (b, i, 0)),
            pl.BlockSpec((1, tb, D), lambda b, i: (b, i, 0)),
            pl.BlockSpec((1, tb, BR_W), lambda b, i: (b, i, 0)),
            pl.BlockSpec((1, tb, BR_W), lambda b, i: (b, i, 0)),
        ],
        out_shape=[
            jax.ShapeDtypeStruct((B, T, N_FEAT * BR_W), F32),
            jax.ShapeDtypeStruct((B, T, D), BF16),
            jax.ShapeDtypeStruct((B, T, BR_W), BF16),
            jax.ShapeDtypeStruct((B, T, BR_W), BF16),
        ],
        scratch_shapes=[
            pltpu.VMEM((tb + 2 * HALO, LR_COLS), F32),
            pltpu.VMEM((tb + 2 * HALO, 3 * BR_W), F32),
        ],
        compiler_params=_cparams(("parallel", "arbitrary")),
        name="pre",
    )(x, x, x, mods, lw["g1"], lw["w_in"], lw["w_lr"],
      lw["w2"], lw["vec"], lw["rwkv_conv"], dft_c, rope_tabs[0], rope_tabs[1])


def _cumsum_chunk(x, rev):
    c = x.shape[0]
    tri = jnp.where(_tri(c, rev), 1.0, 0.0).astype(BF16)
    return _dot_exact_lhs(tri, x)


def _gla_chunks(chains):
    c = chains[0][0].shape[0]
    nblk = c // GLA_SUB
    rowid = _iota((c, 1), 0)
    ones_h = _head_ones()
    same = _same_head(HEADS * GLA_SUB, GLA_SUB)
    bd = (_iota((BR_W, BR_W), 0) // HEAD_DIM) == (_iota((BR_W, BR_W), 1) // HEAD_DIM)
    cums = [_cumsum_chunk(ch[3], ch[5]) for ch in chains]

    offd, inter, st_new = [], [], []
    for (q, k, v, la, st, rev), cum in zip(chains, cums):
        cum_end = cum[0:1] if rev else cum[c - 1:c]
        parts = []
        for blk in range(nblk):
            base = blk * GLA_SUB
            if rev:
                k_lo, k_hi, ref = base + GLA_SUB, c, base + GLA_SUB
            else:
                k_lo, k_hi, ref = 0, base, base - 1
            if k_hi <= k_lo:
                parts.append(None)
                continue
            qt = q[base:base + GLA_SUB] * jnp.exp(cum[base:base + GLA_SUB] - cum[ref:ref + 1])
            kt = k[k_lo:k_hi] * jnp.exp(cum[ref:ref + 1] - cum[k_lo:k_hi])
            sc = _dot_nt(_stack_heads(qt), kt)
            ov = jnp.where(same, _dot(sc, v[k_lo:k_hi]), 0.0)
            parts.append(_unstack_sum(ov, GLA_SUB))
        offd.append(parts)
        inter.append(_dot_nt(q * jnp.exp(cum), st))
        st_new.append(st * jnp.exp(cum_end)
                      + jnp.where(bd, _dot_tn(v, k * jnp.exp(cum_end - cum)), 0.0))

    reds, spans = [], []
    for (q, k, v, la, st, rev), cum in zip(chains, cums):
        cum2 = cum * LOG2E
        pieces, sp = [], []
        for j in range(c):
            base = (j // GLA_SUB) * GLA_SUB
            own = (j // 8) * 8
            others = range(base, own, 8) if rev else range(own + 8, base + GLA_SUB, 8)
            for g0 in (own, *others):
                arg = cum2[g0:g0 + 8] - cum2[j:j + 1]
                if g0 == own:
                    valid = (rowid[g0:g0 + 8] <= j) if rev else (rowid[g0:g0 + 8] >= j)
                    arg = jnp.where(valid, arg, NEG_BIG)
                pieces.append(q[g0:g0 + 8] * (k[j:j + 1] * jnp.exp2(arg)))
                sp.append((j, g0))
        reds.append(jnp.dot(jnp.concatenate(pieces, axis=0).astype(BF16), ones_h,
                            preferred_element_type=F32))
        spans.append(sp)

    outs = []
    for ci, (q, k, v, la, st, rev) in enumerate(chains):
        groups = [None] * (c // 8)
        for idx, (j, g0) in enumerate(spans[ci]):
            piece = reds[ci][idx * 8:idx * 8 + 8] * v[j:j + 1]
            gi = g0 // 8
            groups[gi] = piece if groups[gi] is None else groups[gi] + piece
        blocks = []
        for blk in range(nblk):
            o_blk = jnp.concatenate(groups[blk * GLA_SUB // 8:(blk + 1) * GLA_SUB // 8], axis=0)
            if offd[ci][blk] is not None:
                o_blk = o_blk + offd[ci][blk]
            blocks.append(o_blk)
        outs.append((jnp.concatenate(blocks, axis=0) + inter[ci], st_new[ci]))
    return outs


def _ret_chunk(q, k, v, s, dmat, qdec, kdec, cdec):
    sc = _dot_nt(q, _stack_heads(k)) * dmat
    o = _dot(sc, _stack_heads(v)) + _dot(q * qdec, s)
    bd = (_iota((BR_W, BR_W), 0) // HEAD_DIM) == (_iota((BR_W, BR_W), 1) // HEAD_DIM)
    s_new = s * cdec + jnp.where(bd, _dot_tn(k * kdec, v), 0.0)
    return o, s_new


def _rwkv_chunks(chains):
    c = chains[0][0].shape[0]
    n = HEADS * c
    bf = lambda t: t.astype(BF16)
    masks = {}
    for ch in chains:
        rev = ch[7]
        if rev not in masks:
            masks[rev] = (_tri_blockdiag(c, rev, True), _tri_blockdiag(c, rev, False))
    eye = jnp.where(_iota((n, n), 0) == _iota((n, n), 1), 1.0, 0.0)
    cums = [_cumsum_chunk(ch[5], ch[7]) for ch in chains]
    st = []
    for (r, v, kk, kd, asig, lw, s, rev), cum in zip(chains, cums):
        cum_end = cum[0:1] if rev else cum[c - 1:c]
        einv = jnp.exp(-cum)
        ehat = jnp.exp(cum_end - cum)
        bvec = kk * asig
        st.append(dict(
            a_s=bf(_stack_heads(-kk * jnp.exp(cum - lw))), r_s=bf(_stack_heads(r * jnp.exp(cum))),
            b_s=bf(_stack_heads(bvec * einv)), k_s=bf(_stack_heads(kd * einv)),
            v_s=bf(_stack_heads(v)), bh_s=bf(_stack_heads(bvec * ehat)),
            kh_s=bf(_stack_heads(kd * ehat)), g_end=jnp.exp(cum_end), s=s, sb=bf(s), rev=rev))
    for d in st:
        d["m"] = _dot_nt(jnp.concatenate([d["a_s"], d["r_s"]], axis=0),
                         jnp.concatenate([d["b_s"], d["k_s"]], axis=0))
    for d in st:
        strict, incl = masks[d["rev"]]
        m = d.pop("m")
        d["p"] = jnp.where(strict, m[0:n, 0:n], 0.0)
        d["t"] = eye + d["p"]
        d["m_ak"] = bf(jnp.where(strict, m[0:n, n:2 * n], 0.0))
        d["n_rb"] = bf(jnp.where(incl, m[n:2 * n, 0:n], 0.0))
        d["n_rk"] = bf(jnp.where(incl, m[n:2 * n, n:2 * n], 0.0))
    for d in st:
        d["w"] = _dot_nt(d["a_s"], d["sb"]) + _dot(d["m_ak"], d["v_s"])
        d["y0"] = _dot_nt(d["r_s"], d["sb"]) + _dot(d["n_rk"], d["v_s"])
    for _ in range(int(math.log2(c)) - 1):
        for d in st:
            pb = bf(d["p"])
            d["p"] = jnp.dot(pb, pb, preferred_element_type=F32)
        for d in st:
            d["t"] = d["t"] + _dot(d["t"], d["p"])
    for d in st:
        d["u"] = bf(_dot(d["t"], d["w"]))
    outs = []
    for d in st:
        y = _unstack_sum(d["y0"] + _dot(d["n_rb"], d["u"]), c)
        s_new = d["s"] * d["g_end"] + _dot_tn(jnp.concatenate([d["u"], d["v_s"]], axis=0),
                                              jnp.concatenate([d["bh_s"], d["kh_s"]], axis=0))
        outs.append((y, s_new))
    return outs


def _scan_kernel(kind, n_in, n_tab, nc, bb, *refs):
    fwd = refs[0:n_in]
    bwd = refs[n_in:2 * n_in]
    tabs = refs[2 * n_in:2 * n_in + n_tab]
    s0_ref = refs[2 * n_in + n_tab]
    of_ref, ob_ref, sf_ref, st_ref = refs[2 * n_in + n_tab + 1:]
    i = pl.program_id(1)

    @pl.when(i == 0)
    def _():
        st_ref[...] = s0_ref[...]

    chains = [(dr, b) for b in range(bb) for dr in range(2)]
    ins_of = lambda dr, b: [t[b] for t in (fwd, bwd)[dr]]
    if kind == "rwkv":
        res = _rwkv_chunks([(*ins_of(dr, b), st_ref[dr, b], dr == 1) for dr, b in chains])
    elif kind == "gla":
        res = _gla_chunks([(*ins_of(dr, b), st_ref[dr, b], dr == 1) for dr, b in chains])
    else:
        res = []
        for dr, b in chains:
            dmat, qdec, kdec, cdec = (t[dr] for t in tabs)
            res.append(_ret_chunk(*ins_of(dr, b), st_ref[dr, b], dmat, qdec, kdec, cdec))
    for (dr, b), (o, s_new) in zip(chains, res):
        (of_ref, ob_ref)[dr][b] = o
        st_ref[dr, b] = s_new

    @pl.when(i == nc - 1)
    def _():
        sf_ref[...] = st_ref[...]


_SCAN_ROWS = {"gla": 4, "ret": 2, "rwkv": 4}


def _scan(kind, feats, cols_f, cols_b, s0, tabs=()):
    B, T, _ = feats.shape
    c = min(SCAN_CHUNK[kind], T)
    nc = T // c
    bb = _SCAN_ROWS[kind]
    n_in = len(cols_f)
    in_specs = []
    for j in cols_f:
        in_specs.append(pl.BlockSpec((bb, c, BR_W), lambda b, i, j=j: (b, i, j)))
    for j in cols_b:
        in_specs.append(pl.BlockSpec((bb, c, BR_W), lambda b, i, j=j: (b, nc - 1 - i, j)))
    for t in tabs:
        in_specs.append(pl.BlockSpec(t.shape, lambda b, i, nd=t.ndim: (0,) * nd))
    in_specs.append(pl.BlockSpec((2, bb, BR_W, BR_W), lambda b, i: (0, b, 0, 0)))
    kern = functools.partial(_scan_kernel, kind, n_in, len(tabs), nc, bb)
    return pl.pallas_call(
        kern,
        grid=(B // bb, nc),
        in_specs=in_specs,
        out_specs=[
            pl.BlockSpec((bb, c, BR_W), lambda b, i: (b, i, 0)),
            pl.BlockSpec((bb, c, BR_W), lambda b, i: (b, nc - 1 - i, 0)),
            pl.BlockSpec((2, bb, BR_W, BR_W), lambda b, i: (0, b, 0, 0)),
        ],
        out_shape=[
            jax.ShapeDtypeStruct((B, T, BR_W), F32),
            jax.ShapeDtypeStruct((B, T, BR_W), F32),
            jax.ShapeDtypeStruct((2, B, BR_W, BR_W), F32),
        ],
        scratch_shapes=[pltpu.VMEM((2, bb, BR_W, BR_W), F32)],
        compiler_params=_cparams(("parallel", "arbitrary")),
        name=kind,
    )(*([feats] * (2 * n_in)), *tabs, s0)


def _ret_tables(c):
    pos = np.arange(c, dtype=np.float64)
    lane_head = np.arange(BR_W) // HEAD_DIM
    dmats, qd, kd, cd = [], [], [], []
    for dr in range(2):
        expo = -5.0 - np.arange(HEADS, dtype=np.float64)
        if dr == 1:
            expo = expo[::-1]
        log_g = np.log1p(-np.exp2(expo))
        if dr == 0:
            rel = pos[:, None] - pos[None, :]
            qpow = pos + 1.0
            kpow = c - 1.0 - pos
        else:
            rel = pos[None, :] - pos[:, None]
            qpow = c - pos
            kpow = pos
        tri = rel >= 0
        dm = np.where(tri[None], np.exp(np.where(tri, rel, 0.0)[None] * log_g[:, None, None]), 0.0)
        dmats.append(np.concatenate([dm[h] for h in range(HEADS)], axis=1))
        qd.append(np.exp(qpow[:, None] * log_g[lane_head][None, :]))
        kd.append(np.exp(kpow[:, None] * log_g[lane_head][None, :]))
        cd.append(np.exp(c * log_g[lane_head])[None, :])
    f = lambda xs: jnp.asarray(np.stack(xs), dtype=F32)
    return f(dmats), f(qd), f(kd), f(cd)


def _fnet_dense_kernel(t_len, cs_ref, g1_ref, g2_ref, o_ref):
    o_ref[0] = (jnp.dot(cs_ref[:, 0:t_len], g1_ref[0], preferred_element_type=F32)
                + jnp.dot(cs_ref[:, t_len:2 * t_len], g2_ref[0], preferred_element_type=F32))


def _fnet_dense(g1, g2, cs):
    B, T, _ = g1.shape
    tb = min(512, T)
    return pl.pallas_call(
        functools.partial(_fnet_dense_kernel, T),
        grid=(T // tb, B),
        in_specs=[
            pl.BlockSpec((tb, 2 * T), lambda i, b: (i, 0)),
            pl.BlockSpec((1, T, BR_W), lambda i, b: (b, 0, 0)),
            pl.BlockSpec((1, T, BR_W), lambda i, b: (b, 0, 0)),
        ],
        out_specs=pl.BlockSpec((1, tb, BR_W), lambda i, b: (b, i, 0)),
        out_shape=jax.ShapeDtypeStruct((B, T, BR_W), F32),
        compiler_params=_cparams(("arbitrary", "arbitrary")),
        name="fnet",
    )(cs, g1, g2)


def _fft1_kernel(n, f1_ref, tc_ref, ts_ref, x1_ref, x2_ref, zr_ref, zi_ref):
    f1 = f1_ref[...]
    r1 = jnp.dot(f1, x1_ref[0], preferred_element_type=F32)
    r2 = jnp.dot(f1, x2_ref[0], preferred_element_type=F32)
    yr = r1[0:n] - r2[n:2 * n]
    yi = -r2[0:n] - r1[n:2 * n]
    tc = tc_ref[...]
    ts = ts_ref[...]
    zr_ref[0] = (tc * yr + ts * yi).astype(BF16)
    zi_ref[0] = (tc * yi - ts * yr).astype(BF16)


def _fft2_kernel(rb, f3_ref, zr_ref, zi_ref, o_ref):
    f3 = f3_ref[...]
    for j in range(rb):
        z = jnp.concatenate([zr_ref[0, j], zi_ref[0, j]], axis=0)
        o_ref[0, j] = jnp.dot(f3, z, preferred_element_type=F32)


def _fnet_fft(g1, g2):
    B, T, _ = g1.shape
    n = int(round(math.sqrt(T)))
    assert n * n == T
    idx = np.arange(n)
    a_n = 2.0 * np.pi * ((idx[:, None] * idx[None, :]) % n) / n
    f1 = jnp.asarray(np.concatenate([np.cos(a_n), np.sin(a_n)], axis=0), dtype=BF16)
    f3 = jnp.asarray(np.concatenate([np.cos(a_n), np.sin(a_n)], axis=1), dtype=BF16)
    a_t = 2.0 * np.pi * (idx[:, None] * idx[None, :]) / T
    tc = jnp.repeat(jnp.asarray(np.cos(a_t), dtype=F32), BR_W, axis=1)
    ts = jnp.repeat(jnp.asarray(np.sin(a_t), dtype=F32), BR_W, axis=1)
    lanes = n * BR_W
    nb = min(4096, lanes)
    x1 = g1.reshape(B, n, lanes)
    x2 = g2.reshape(B, n, lanes)
    zr, zi = pl.pallas_call(
        functools.partial(_fft1_kernel, n),
        grid=(lanes // nb, B),
        in_specs=[
            pl.BlockSpec((2 * n, n), lambda j, b: (0, 0)),
            pl.BlockSpec((n, nb), lambda j, b: (0, j)),
            pl.BlockSpec((n, nb), lambda j, b: (0, j)),
            pl.BlockSpec((1, n, nb), lambda j, b: (b, 0, j)),
            pl.BlockSpec((1, n, nb), lambda j, b: (b, 0, j)),
        ],
        out_specs=[pl.BlockSpec((1, n, nb), lambda j, b: (b, 0, j)),
                   pl.BlockSpec((1, n, nb), lambda j, b: (b, 0, j))],
        out_shape=[jax.ShapeDtypeStruct((B, n, lanes), BF16),
                   jax.ShapeDtypeStruct((B, n, lanes), BF16)],
        compiler_params=_cparams(("arbitrary", "arbitrary")),
        name="fft1",
    )(f1, tc, ts, x1, x2)
    rb = 16
    o = pl.pallas_call(
        functools.partial(_fft2_kernel, rb),
        grid=(B, n // rb),
        in_specs=[
            pl.BlockSpec((n, 2 * n), lambda b, j: (0, 0)),
            pl.BlockSpec((1, rb, n, BR_W), lambda b, j: (b, j, 0, 0)),
            pl.BlockSpec((1, rb, n, BR_W), lambda b, j: (b, j, 0, 0)),
        ],
        out_specs=pl.BlockSpec((1, rb, n, BR_W), lambda b, j: (b, j, 0, 0)),
        out_shape=jax.ShapeDtypeStruct((B, n, n, BR_W), F32),
        compiler_params=_cparams(("parallel", "arbitrary")),
        name="fft2",
    )(f3, zr.reshape(B, n, n, BR_W), zi.reshape(B, n, n, BR_W))
    return o.transpose(0, 2, 1, 3).reshape(B, T, BR_W)


def _time_dft_table(T):
    t = np.arange(T)
    ang = ((t[:, None] * t[None, :]) % T) * (2.0 * np.pi / T)
    return jnp.asarray(np.concatenate([np.cos(ang), -np.sin(ang)], axis=1), dtype=BF16)


def _channel_dft_tables(T):
    ch = np.arange(HEAD_DIM)
    a64 = 2.0 * np.pi * ((ch[:, None] * ch[None, :]) % HEAD_DIM) / HEAD_DIM
    scale = (T * HEAD_DIM) ** -0.5
    eye = np.eye(HEADS)
    cbd = np.kron(eye, np.cos(a64)) * scale
    sbd = np.kron(eye, np.sin(a64)) * scale
    return jnp.asarray(np.stack([cbd, sbd]), dtype=BF16)


def _head_norm(o, ones_h, center):
    if center:
        o = o - _dot(o, ones_h) * (1.0 / HEAD_DIM)
    var = _dot(o * o, ones_h) * (1.0 / HEAD_DIM)
    return o * lax.rsqrt(var + EPS)


def _merge_kernel(x_ref, hb_ref, mod_ref, gg_ref, rg_ref, wg_ref, bon_ref,
                  ogf_ref, ogb_ref, orf_ref, orb_ref, oyf_ref, oyb_ref, fn_ref,
                  gn_ref, wgate_ref, bgate_ref, wbr_ref, wout_ref, o_ref):
    ones_h = _head_ones()
    gn = gn_ref[...]
    gla = _head_norm(ogf_ref[0] + ogb_ref[0], ones_h, False) * gn[0:1] * _silu(gg_ref[0])
    ret = _head_norm(orf_ref[0] + orb_ref[0], ones_h, True) * gn[1:2] * _silu(rg_ref[0])
    rwkv = (_head_norm(oyf_ref[0] + oyb_ref[0], ones_h, True) * gn[2:3] + bon_ref[0]) * wg_ref[0]
    outs = (gla, ret, rwkv, fn_ref[0])
    hb = hb_ref[0]
    z = None
    for br in range(4):
        gate = _sigmoid(jnp.dot(hb, wgate_ref[br], preferred_element_type=F32) + bgate_ref[br])
        term = gate * _dot(outs[br], wbr_ref[br])
        z = term if z is None else z + term
    y = _dot(z, wout_ref[...])
    gate1 = mod_ref[0][:, 2 * D_MODEL:3 * D_MODEL]
    o_ref[0] = x_ref[0] + gate1 * y


def _merge(x, hb, mods, feats, scans, fnet_out, lw):
    B, T, D = x.shape
    tb = min(256, T)
    tok = lambda w: pl.BlockSpec((1, tb, w), lambda b, i: (b, i, 0))
    fcol = lambda j: pl.BlockSpec((1, tb, BR_W), lambda b, i, j=j: (b, i, j))
    const2 = lambda b, i: (0, 0)
    const3 = lambda b, i: (0, 0, 0)
    return pl.pallas_call(
        _merge_kernel,
        grid=(B, T // tb),
        in_specs=[
            tok(D), tok(D),
            pl.BlockSpec((1, 1, 6 * D), lambda b, i: (b, 0, 0)),
            fcol(F_GG), fcol(F_RG), fcol(F_WG), fcol(F_BON),
            tok(BR_W), tok(BR_W), tok(BR_W), tok(BR_W), tok(BR_W), tok(BR_W), tok(BR_W),
            pl.BlockSpec((3, BR_W), const2),
            pl.BlockSpec((4, D, D), const3),
            pl.BlockSpec((4, 1, D), const3),
            pl.BlockSpec((4, BR_W, D), const3),
            pl.BlockSpec((D, D), const2),
        ],
        out_specs=tok(D),
        out_shape=jax.ShapeDtypeStruct((B, T, D), F32),
        compiler_params=_cparams(("parallel", "arbitrary")),
        name="merge",
    )(x, hb, mods, feats, feats, feats, feats, *scans, fnet_out,
      lw["gn"], lw["w_gate"], lw["b_gate"], lw["w_br"], lw["w_out"])


FFN_COL_CHUNKS = (1024, 1024, 768)


def _ffn_kernel(final, tb, nt,
                x_ref, xp_ref, xn_ref, mod_ref, g2_ref, up_ref, cw_ref, cb_ref,
                down_ref, gfin_ref, o_ref, act_ref):
    i = pl.program_id(1)
    n_ext = tb + 2 * HALO
    x_ext = jnp.concatenate([xp_ref[0], x_ref[0], xn_ref[0]], axis=0)
    mod = mod_ref[0]
    shift = mod[:, 3 * D_MODEL:4 * D_MODEL]
    scale = mod[:, 4 * D_MODEL:5 * D_MODEL]
    h2 = (_rms_rows(x_ext) * g2_ref[...]) * (1.0 + scale) + shift
    h2 = h2 * _halo_keep(n_ext, tb, i, nt)
    h2_ext = h2.astype(BF16)
    h2_mid = h2[HALO:HALO + tb].astype(BF16)
    cw = cw_ref[...]
    cb = cb_ref[...]
    lo = 0
    for width in FFN_COL_CHUNKS:
        a_ext = jnp.dot(h2_ext, up_ref[:, lo:lo + width], preferred_element_type=F32)
        u = jnp.dot(h2_mid, up_ref[:, D_FF + lo:D_FF + lo + width], preferred_element_type=F32)
        a = (a_ext[HALO - 1:HALO - 1 + tb] * cw[0:1, lo:lo + width]
             + a_ext[HALO:HALO + tb] * cw[1:2, lo:lo + width]
             + a_ext[HALO + 1:HALO + 1 + tb] * cw[2:3, lo:lo + width] + cb[:, lo:lo + width])
        act_ref[:, lo:lo + width] = (_silu(a) * u).astype(BF16)
        lo += width
    y = jnp.dot(act_ref[...], down_ref[...], preferred_element_type=F32)
    gate2 = mod[:, 5 * D_MODEL:6 * D_MODEL]
    res = x_ref[0] + gate2 * y
    if final:
        res = _rms_rows(res) * gfin_ref[...]
    o_ref[0] = res


def _ffn(x, mods, lw, g_final, final):
    B, T, D = x.shape
    tb = min(512, T)
    nt = T // tb
    hb8 = tb // HALO
    n_h = T // HALO
    const2 = lambda b, i: (0, 0)
    resident = lambda shape: pl.BlockSpec(shape, const2, pipeline_mode=pl.Buffered(1))
    kern = functools.partial(_ffn_kernel, final, tb, nt)
    return pl.pallas_call(
        kern,
        grid=(B, nt),
        in_specs=[
            pl.BlockSpec((1, tb, D), lambda b, i: (b, i, 0)),
            pl.BlockSpec((1, HALO, D), lambda b, i: (b, jnp.maximum(i * hb8 - 1, 0), 0)),
            pl.BlockSpec((1, HALO, D), lambda b, i: (b, jnp.minimum((i + 1) * hb8, n_h - 1), 0)),
            pl.BlockSpec((1, 1, 6 * D), lambda b, i: (b, 0, 0)),
            pl.BlockSpec((1, D), const2),
            resident((D, 2 * D_FF)),
            pl.BlockSpec((3, D_FF), const2),
            pl.BlockSpec((1, D_FF), const2),
            resident((D_FF, D)),
            pl.BlockSpec((1, D), const2),
        ],
        out_specs=pl.BlockSpec((1, tb, D), lambda b, i: (b, i, 0)),
        out_shape=jax.ShapeDtypeStruct((B, T, D), F32),
        scratch_shapes=[pltpu.VMEM((tb, D_FF), BF16)],
        compiler_params=_cparams(("parallel", "arbitrary")),
        name="ffn",
    )(x, x, x, mods, lw["g2"], lw["ffn_up"], lw["ffn_conv"], lw["ffn_conv_b"],
      lw["ffn_down"], g_final)


def _rope_tables(T):
    rows = T // GRID_W
    row = jnp.repeat(jnp.arange(rows, dtype=F32), GRID_W)
    colp = jnp.tile(jnp.arange(GRID_W, dtype=F32), rows)
    n_freq = HEAD_DIM // 4
    inv = ROPE_BASE ** (-jnp.arange(n_freq, dtype=F32) / n_freq)
    ang = jnp.concatenate([row[:, None] * inv, colp[:, None] * inv], axis=-1)
    cos, sin = jnp.cos(ang), jnp.sin(ang)
    cos_h = jnp.concatenate([cos, cos], axis=-1)
    sin_h = jnp.concatenate([-sin, sin], axis=-1)
    return jnp.tile(cos_h, (1, HEADS)), jnp.tile(sin_h, (1, HEADS))


def _pad_rows(w, lo, n):
    return jnp.zeros((n, w.shape[1]), w.dtype).at[lo:lo + w.shape[0]].set(w)


def _layer_weights(l, p):
    D = D_MODEL
    bf = lambda t: t.astype(BF16)
    zcols = lambda w, n: jnp.concatenate([w, jnp.zeros((w.shape[0], n - w.shape[1]), w.dtype)], axis=1)
    w_h = zcols(jnp.concatenate([p["gla_wa1"][l, 0], p["gla_wa1"][l, 1]], axis=1), 128)
    w_xw = jnp.concatenate([p["rwkv_w1"][l, 0], p["rwkv_w1"][l, 1]], axis=1)
    w_xa = jnp.concatenate([p["rwkv_a1"][l, 0], p["rwkv_a1"][l, 1]], axis=1)
    w_xg = zcols(p["rwkv_g1"][l], 256)
    mu = p["rwkv_mu"][l]
    w_lr = jnp.concatenate([w_xg, mu[2][:, None] * w_xg, w_h, w_xw, w_xa,
                            mu[0][:, None] * w_xw, mu[1][:, None] * w_xa,
                            jnp.zeros((D, LR_COLS - LR_AS - 128), F32)], axis=1)
    w2 = jnp.stack([
        _pad_rows(p["gla_wa2"][l, 0], 0, 256), _pad_rows(p["gla_wa2"][l, 1], GLA_LR, 256),
        _pad_rows(p["rwkv_w2"][l, 0], 0, 256), _pad_rows(p["rwkv_w2"][l, 1], RWKV_LR, 256),
        _pad_rows(p["rwkv_a2"][l, 0], 0, 256), _pad_rows(p["rwkv_a2"][l, 1], RWKV_LR, 256),
        _pad_rows(p["rwkv_g2"][l], 0, 256),
    ])
    vec = jnp.stack([p["gla_ba"][l, 0], p["gla_ba"][l, 1], p["rwkv_w0"][l, 0], p["rwkv_w0"][l, 1],
                     p["rwkv_a0"][l, 0], p["rwkv_a0"][l, 1], p["rwkv_kk"][l], p["rwkv_ka"][l],
                     p["rwkv_rk"][l]])
    vec = jnp.concatenate([vec, jnp.zeros((16 - vec.shape[0], BR_W), F32)], axis=0)
    return {
        "g1": p["g_norm1"][l].reshape(1, D), "g2": p["g_norm2"][l].reshape(1, D),
        "w_in": bf(p["w_in"][l]), "w_lr": bf(w_lr), "w2": bf(w2), "vec": vec,
        "rwkv_conv": p["rwkv_conv"][l],
        "gn": jnp.stack([p["gla_gn"][l], p["ret_gn"][l], p["rwkv_gn"][l]]),
        "w_gate": bf(p["w_gate"][l]), "b_gate": p["b_gate"][l].reshape(4, 1, D),
        "w_br": bf(p["w_br"][l]), "w_out": bf(p["w_out"][l]),
        "ffn_up": bf(p["ffn_up"][l]), "ffn_conv": p["ffn_conv"][l],
        "ffn_conv_b": p["ffn_conv_b"][l].reshape(1, D_FF), "ffn_down": bf(p["ffn_down"][l]),
    }


_GLA_F = (F_GQ, F_GK, F_GV, F_LAF)
_GLA_B = (F_GQ, F_GK, F_GV, F_LAB)
_RET = (F_RQ, F_RK, F_RV)
_RWKV_F = (F_WR, F_WV, F_WKK, F_KDF, F_ASF, F_LWF)
_RWKV_B = (F_WR, F_WV, F_WKK, F_KDB, F_ASB, F_LWB)


def _mixers(feats_c, feats_l, ret_tabs_c, ret_tabs_l, need_ctx):
    B = feats_l.shape[0]
    zero = jnp.zeros((2, B, BR_W, BR_W), F32)
    out_c, out_l = [], []
    for kind, cf, cb, tabs in (("gla", _GLA_F, _GLA_B, None), ("ret", _RET, _RET, True),
                               ("rwkv", _RWKV_F, _RWKV_B, None)):
        tc = ret_tabs_c if tabs else ()
        tl = ret_tabs_l if tabs else ()
        of_c, ob_c, s_c = _scan(kind, feats_c, cf, cb, zero, tc)
        of_l, ob_l, _ = _scan(kind, feats_l, cf, cb, s_c, tl)
        out_c += [of_c, ob_c]
        out_l += [of_l, ob_l]
    return out_l, (out_c if need_ctx else None)


def kernel(x, c, ctx, c_ctx, w_ada, b_ada, g_norm1, g_norm2, w_in, gla_wa1, gla_wa2, gla_ba, gla_gn, ret_gn, rwkv_conv, rwkv_mu, rwkv_w0, rwkv_w1, rwkv_w2, rwkv_a0, rwkv_a1, rwkv_a2, rwkv_g1, rwkv_g2, rwkv_kk, rwkv_ka, rwkv_rk, rwkv_gn, w_gate, b_gate, w_br, w_out, ffn_up, ffn_conv, ffn_conv_b, ffn_down, g_final):
    p = dict(g_norm1=g_norm1, g_norm2=g_norm2, w_in=w_in, gla_wa1=gla_wa1, gla_wa2=gla_wa2,
             gla_ba=gla_ba, gla_gn=gla_gn, ret_gn=ret_gn, rwkv_conv=rwkv_conv, rwkv_mu=rwkv_mu,
             rwkv_w0=rwkv_w0, rwkv_w1=rwkv_w1, rwkv_w2=rwkv_w2, rwkv_a0=rwkv_a0,
             rwkv_a1=rwkv_a1, rwkv_a2=rwkv_a2, rwkv_g1=rwkv_g1, rwkv_g2=rwkv_g2,
             rwkv_kk=rwkv_kk, rwkv_ka=rwkv_ka, rwkv_rk=rwkv_rk, rwkv_gn=rwkv_gn,
             w_gate=w_gate, b_gate=b_gate, w_br=w_br, w_out=w_out, ffn_up=ffn_up,
             ffn_conv=ffn_conv, ffn_conv_b=ffn_conv_b, ffn_down=ffn_down)
    B, T, D = x.shape
    Tc = ctx.shape[1]
    depth = w_ada.shape[0]

    cmat = jnp.concatenate([c, c_ctx[None, :], jnp.zeros((16 - B - 1, D), F32)], axis=0)
    mod_all = _modulation(cmat, w_ada, b_ada)

    rope_l = _rope_tables(T)
    rope_c = (jnp.zeros((Tc, BR_W), F32), jnp.zeros((Tc, BR_W), F32))
    dft_l = _channel_dft_tables(T)
    dft_c = _channel_dft_tables(Tc)
    cs_c = _time_dft_table(Tc)
    ret_tabs_l = _ret_tables(min(SCAN_CHUNK["ret"], T))
    ret_tabs_c = _ret_tables(min(SCAN_CHUNK["ret"], Tc))
    g_fin = g_final.reshape(1, D)

    for l in range(depth):
        last = l == depth - 1
        lw = _layer_weights(l, p)
        mods_l = mod_all[l, 0:B].reshape(B, 1, 6 * D)
        mods_c = jnp.broadcast_to(mod_all[l, B:B + 1].reshape(1, 1, 6 * D), (B, 1, 6 * D))

        feats_l, hb_l, g1_l, g2_l = _pre(x, mods_l, lw, True, rope_l, dft_l)
        feats_c, hb_c, g1_c, g2_c = _pre(ctx, mods_c, lw, False, rope_c, dft_c)
        scans_l, scans_c = _mixers(feats_c, feats_l, ret_tabs_c, ret_tabs_l, not last)

        fn_l = _fnet_fft(g1_l, g2_l)
        x = _merge(x, hb_l, mods_l, feats_l, scans_l, fn_l, lw)
        x = _ffn(x, mods_l, lw, g_fin, last)
        if not last:
            fn_c = _fnet_dense(g1_c, g2_c, cs_c)
            ctx = _merge(ctx, hb_c, mods_c, feats_c, scans_c, fn_c, lw)
            ctx = _ffn(ctx, mods_c, lw, g_fin, False)
    return x
```

```python
import functools
import math

import numpy as np
import jax
import jax.numpy as jnp
from jax import lax
from jax.experimental import pallas as pl
from jax.experimental.pallas import tpu as pltpu

F32 = jnp.float32
BF16 = jnp.bfloat16

D_MODEL = 1024
BR_W = 256
HEAD_DIM = 64
HEADS = 4
N_PARTS = 12
P_IN = N_PARTS * BR_W
GRID_W = 64
GLA_LR = 16
GLA_NORMALIZER = 16.0
RWKV_LR = 64
RWKV_G_LR = 160
D_FF = 2816
ROPE_BASE = 10000.0
EPS = 1e-6

SCAN_CHUNK = {"gla": 64, "ret": 128, "rwkv": 64}
GLA_SUB = 16
HALO = 8
NEG_BIG = -1e30
LOG2E = 1.4426950408889634

(F_GQ, F_GK, F_GV, F_GG, F_RQ, F_RK, F_RV, F_RG, F_WR, F_WV, F_WKK, F_WG, F_BON,
 F_LAF, F_LAB, F_LWF, F_LWB, F_KDF, F_KDB, F_ASF, F_ASB) = range(21)
N_FEAT = 21

LR_G, LR_GS, LR_H, LR_W, LR_A, LR_WS, LR_AS = 0, 256, 512, 640, 768, 896, 1024
LR_COLS = 1280

VMEM_LIMIT = 56 * 1024 * 1024


def _cparams(sem):
    return pltpu.CompilerParams(dimension_semantics=sem, vmem_limit_bytes=VMEM_LIMIT)


def _dot(a, b):
    return jnp.dot(a.astype(BF16), b.astype(BF16), preferred_element_type=F32)


def _dot_nt(a, b):
    return lax.dot_general(a.astype(BF16), b.astype(BF16), (((1,), (1,)), ((), ())),
                           preferred_element_type=F32)


def _dot_tn(a, b):
    return lax.dot_general(a.astype(BF16), b.astype(BF16), (((0,), (0,)), ((), ())),
                           preferred_element_type=F32)


def _split3(x):
    hi = x.astype(BF16)
    r1 = x - hi.astype(F32)
    mid = r1.astype(BF16)
    lo = (r1 - mid.astype(F32)).astype(BF16)
    return hi, mid, lo


def _dot_exact_lhs(a_bf16, x):
    hi, mid, lo = _split3(x)
    d = lambda t: jnp.dot(a_bf16, t, preferred_element_type=F32)
    return d(hi) + d(mid) + d(lo)


def _sigmoid(x):
    return 1.0 / (1.0 + jnp.exp(-x))


def _silu(x):
    return x * _sigmoid(x)


def _softplus(x):
    return jnp.maximum(x, 0.0) + jnp.log(1.0 + jnp.exp(-jnp.abs(x)))


def _iota(shape, dim):
    return lax.broadcasted_iota(jnp.int32, shape, dim)


def _head_ones():
    r = _iota((BR_W, BR_W), 0) // HEAD_DIM
    c = _iota((BR_W, BR_W), 1) // HEAD_DIM
    return jnp.where(r == c, 1.0, 0.0).astype(BF16)


def _same_head(n_rows, rows_per_head):
    r = _iota((n_rows, BR_W), 0) // rows_per_head
    c = _iota((n_rows, BR_W), 1) // HEAD_DIM
    return r == c


def _stack_heads(x):
    c = x.shape[0]
    xs = jnp.concatenate([x] * HEADS, axis=0)
    return jnp.where(_same_head(HEADS * c, c), xs, 0.0)


def _unstack_sum(y, c):
    out = y[0:c]
    for h in range(1, HEADS):
        out = out + y[h * c:(h + 1) * c]
    return out


def _tri(c, rev, strict=False):
    r = _iota((c, c), 0)
    s = _iota((c, c), 1)
    if rev:
        return (r < s) if strict else (r <= s)
    return (r > s) if strict else (r >= s)


def _tri_blockdiag(c, rev, strict):
    n = HEADS * c
    r = _iota((n, n), 0)
    s = _iota((n, n), 1)
    same = (r // c) == (s // c)
    rr = r % c
    ss = s % c
    if rev:
        t = (rr < ss) if strict else (rr <= ss)
    else:
        t = (rr > ss) if strict else (rr >= ss)
    return jnp.logical_and(same, t)


def _halo_keep(n_ext, tb, i, nt):
    row = _iota((n_ext, 1), 0)
    first = jnp.where(i > 0, 1.0, 0.0)
    last = jnp.where(i < nt - 1, 1.0, 0.0)
    return jnp.where(row < HALO, first, jnp.where(row >= tb + HALO, last, 1.0))


def _rms_rows(x):
    return x * lax.rsqrt(jnp.mean(x * x, axis=-1, keepdims=True) + EPS)


def _mod_kernel(c_ref, w_ref, b_ref, o_ref):
    s = _silu(c_ref[...])
    o_ref[0] = _dot(s, w_ref[0]) + b_ref[0]


def _modulation(cmat, w_ada, b_ada):
    L = w_ada.shape[0]
    nblk = 1536
    return pl.pallas_call(
        _mod_kernel,
        grid=(L, 6 * D_MODEL // nblk),
        in_specs=[
            pl.BlockSpec((16, D_MODEL), lambda l, j: (0, 0)),
            pl.BlockSpec((1, D_MODEL, nblk), lambda l, j: (l, 0, j)),
            pl.BlockSpec((1, 1, nblk), lambda l, j: (l, 0, j)),
        ],
        out_specs=pl.BlockSpec((1, 16, nblk), lambda l, j: (l, 0, j)),
        out_shape=jax.ShapeDtypeStruct((L, 16, 6 * D_MODEL), F32),
        compiler_params=_cparams(("arbitrary", "arbitrary")),
        name="mod",
    )(cmat, w_ada, b_ada.reshape(L, 1, 6 * D_MODEL))


def _pre_kernel(use_rope, tb, nt,
                x_ref, xp_ref, xn_ref, mod_ref, g1_ref, win_ref, wlr_ref,
                w2_ref, vec_ref, conv_ref, dft_ref, cos_ref, sin_ref,
                f_ref, hb_ref, fc_ref, fs_ref, lr_ref, ps_ref):
    i = pl.program_id(1)
    n_ext = tb + 2 * HALO
    x_ext = jnp.concatenate([xp_ref[0], x_ref[0], xn_ref[0]], axis=0)
    mod = mod_ref[0]
    shift = mod[:, 0:D_MODEL]
    scale = mod[:, D_MODEL:2 * D_MODEL]
    h_ext = (_rms_rows(x_ext) * g1_ref[...]) * (1.0 + scale) + shift
    h_ext = h_ext * _halo_keep(n_ext, tb, i, nt)
    hb_ext = h_ext.astype(BF16)
    hb = h_ext[HALO:HALO + tb].astype(BF16)
    hb_ref[0] = hb
    lr_ref[...] = jnp.dot(hb_ext, wlr_ref[...], preferred_element_type=F32)
    ps_ref[...] = jnp.dot(hb_ext, win_ref[:, 8 * BR_W:11 * BR_W], preferred_element_type=F32)

    def mid(lo, hi):
        return lr_ref[HALO:HALO + tb, lo:hi]

    def shifted(lo, hi):
        return (0.5 * (lr_ref[HALO - 1:HALO - 1 + tb, lo:hi] + lr_ref[HALO + 1:HALO + 1 + tb, lo:hi])
                - lr_ref[HALO:HALO + tb, lo:hi])

    vec = vec_ref[...]
    ba_f, ba_b, w0_f, w0_b = vec[0:1], vec[1:2], vec[2:3], vec[3:4]
    a0_f, a0_b, kkw, kaw, rkw = vec[4:5], vec[5:6], vec[6:7], vec[7:8], vec[8:9]

    def put(j, val):
        f_ref[0, :, j * BR_W:(j + 1) * BR_W] = val

    zh = mid(LR_H, LR_H + 128)
    tw = jnp.tanh(mid(LR_W, LR_W + 128) + shifted(LR_WS, LR_WS + 128))
    ta = mid(LR_A, LR_A + 128) + shifted(LR_AS, LR_AS + 128)
    sg = _sigmoid(mid(LR_G, LR_G + 256) + shifted(LR_GS, LR_GS + 256))

    p_gla = jnp.dot(hb, win_ref[:, 0:4 * BR_W], preferred_element_type=F32)
    put(F_GQ, p_gla[:, 0:BR_W] * HEAD_DIM ** -0.5)
    put(F_GK, p_gla[:, BR_W:2 * BR_W])
    put(F_GV, p_gla[:, 2 * BR_W:3 * BR_W])
    put(F_GG, p_gla[:, 3 * BR_W:4 * BR_W])

    z_la = [_dot(zh, w2_ref[dr, 0:128, :]) for dr in range(2)]
    z_w = [_dot(tw, w2_ref[2 + dr, 0:128, :]) for dr in range(2)]
    z_a = [_dot(ta, w2_ref[4 + dr, 0:128, :]) for dr in range(2)]
    put(F_WG, _dot(sg, w2_ref[6]))

    for dr, (ba, dst) in enumerate(((ba_f, F_LAF), (ba_b, F_LAB))):
        put(dst, -_softplus(-(z_la[dr] + ba)) * (1.0 / GLA_NORMALIZER))

    p_ret = jnp.dot(hb, win_ref[:, 4 * BR_W:8 * BR_W], preferred_element_type=F32)

    conv = conv_ref[...]

    def dwconv(part):
        lo, hi = part * BR_W, (part + 1) * BR_W
        cw = conv[:, lo:hi]
        return (ps_ref[HALO - 1:HALO - 1 + tb, lo:hi] * cw[0:1]
                + ps_ref[HALO:HALO + tb, lo:hi] * cw[1:2]
                + ps_ref[HALO + 1:HALO + 1 + tb, lo:hi] * cw[2:3])

    r_c = dwconv(0)
    k_c = dwconv(1)
    v_c = dwconv(2)
    ones_h = _head_ones()
    kk = k_c * kkw
    kk = kk * lax.rsqrt(_dot(kk * kk, ones_h) + EPS)
    put(F_WR, r_c)
    put(F_WV, v_c)
    put(F_WKK, kk)

    p_fnet = jnp.dot(hb, win_ref[:, 11 * BR_W:12 * BR_W], preferred_element_type=F32)

    bonus = jnp.zeros((tb, BR_W), F32)
    for dr, (w0, a0, d_lw, d_kd, d_as) in enumerate(
            ((w0_f, a0_f, F_LWF, F_KDF, F_ASF), (w0_b, a0_b, F_LWB, F_KDB, F_ASB))):
        w_raw = -_softplus(-(w0 + z_w[dr])) - 0.5
        put(d_lw, -jnp.exp(w_raw))
        a_sig = _sigmoid(a0 + z_a[dr])
        kd = k_c * (1.0 + (a_sig - 1.0) * kaw)
        put(d_kd, kd)
        put(d_as, a_sig)
        bonus = bonus + _dot(r_c * kd * rkw, ones_h) * v_c
    put(F_BON, bonus)

    rq = p_ret[:, 0:BR_W]
    rk = p_ret[:, BR_W:2 * BR_W] * HEAD_DIM ** -0.5
    if use_rope:
        cosf = cos_ref[...]
        sins = sin_ref[...]
        low = (_iota((tb, BR_W), 1) % HEAD_DIM) < (HEAD_DIM // 2)

        def rope(t):
            partner = jnp.where(low, pltpu.roll(t, BR_W - HEAD_DIM // 2, 1),
                                pltpu.roll(t, HEAD_DIM // 2, 1))
            return t * cosf + partner * sins

        rq = rope(rq)
        rk = rope(rk)
    put(F_RQ, rq)
    put(F_RK, rk)
    put(F_RV, p_ret[:, 2 * BR_W:3 * BR_W])
    put(F_RG, p_ret[:, 3 * BR_W:4 * BR_W])

    fb = p_fnet.astype(BF16)
    fc_ref[0] = jnp.dot(fb, dft_ref[0], preferred_element_type=F32)
    fs_ref[0] = jnp.dot(fb, dft_ref[1], preferred_element_type=F32)


def _pre(x, mods, lw, use_rope, rope_tabs, dft_c):
    B, T, D = x.shape
    tb = min(256, T)
    nt = T // tb
    hb8 = tb // HALO
    n_h = T // HALO
    const2 = lambda b, i: (0, 0)
    const3 = lambda b, i: (0, 0, 0)
    kern = functools.partial(_pre_kernel, use_rope, tb, nt)
    return pl.pallas_call(
        kern,
        grid=(B, nt),
        in_specs=[
            pl.BlockSpec((1, tb, D), lambda b, i: (b, i, 0)),
            pl.BlockSpec((1, HALO, D), lambda b, i: (b, jnp.maximum(i * hb8 - 1, 0), 0)),
            pl.BlockSpec((1, HALO, D), lambda b, i: (b, jnp.minimum((i + 1) * hb8, n_h - 1), 0)),
            pl.BlockSpec((1, 1, 6 * D), lambda b, i: (b, 0, 0)),
            pl.BlockSpec((1, D), const2),
            pl.BlockSpec((D, P_IN), const2),
            pl.BlockSpec((D, LR_COLS), const2),
            pl.BlockSpec((7, 256, BR_W), const3),
            pl.BlockSpec((16, BR_W), const2),
            pl.BlockSpec((3, 3 * BR_W), const2),
            pl.BlockSpec((2, BR_W, BR_W), const3),
            pl.BlockSpec((tb, BR_W), lambda b, i: (i, 0)),
            pl.BlockSpec((tb, BR_W), lambda b, i: (i, 0)),
        ],
        out_specs=[
            pl.BlockSpec((1, tb, N_FEAT * BR_W), lambda b, i: (b, i, 0)),
            pl.BlockSpec((1, tb, D), lambda b, i: (b, i, 0)),
            pl.BlockSpec((1, tb, BR_W), lambda b, i: (b, i, 0)),
            pl.BlockSpec((1, tb, BR_W), lambda b, i: (b, i, 0)),
        ],
        out_shape=[
            jax.ShapeDtypeStruct((B, T, N_FEAT * BR_W), F32),
            jax.ShapeDtypeStruct((B, T, D), BF16),
            jax.ShapeDtypeStruct((B, T, BR_W), F32),
            jax.ShapeDtypeStruct((B, T, BR_W), F32),
        ],
        scratch_shapes=[
            pltpu.VMEM((tb + 2 * HALO, LR_COLS), F32),
            pltpu.VMEM((tb + 2 * HALO, 3 * BR_W), F32),
        ],
        compiler_params=_cparams(("parallel", "arbitrary")),
        name="pre",
    )(x, x, x, mods, lw["g1"], lw["w_in"], lw["w_lr"],
      lw["w2"], lw["vec"], lw["rwkv_conv"], dft_c, rope_tabs[0], rope_tabs[1])


def _cumsum_chunk(x, rev):
    c = x.shape[0]
    tri = jnp.where(_tri(c, rev), 1.0, 0.0).astype(BF16)
    return _dot_exact_lhs(tri, x)


def _gla_chunks(chains):
    c = chains[0][0].shape[0]
    nblk = c // GLA_SUB
    rowid = _iota((c, 1), 0)
    ones_h = _head_ones()
    same = _same_head(HEADS * GLA_SUB, GLA_SUB)
    bd = (_iota((BR_W, BR_W), 0) // HEAD_DIM) == (_iota((BR_W, BR_W), 1) // HEAD_DIM)
    cums = [_cumsum_chunk(ch[3], ch[5]) for ch in chains]

    offd, inter, st_new = [], [], []
    for (q, k, v, la, st, rev), cum in zip(chains, cums):
        cum_end = cum[0:1] if rev else cum[c - 1:c]
        parts = []
        for blk in range(nblk):
            base = blk * GLA_SUB
            if rev:
                k_lo, k_hi, ref = base + GLA_SUB, c, base + GLA_SUB
            else:
                k_lo, k_hi, ref = 0, base, base - 1
            if k_hi <= k_lo:
                parts.append(None)
                continue
            qt = q[base:base + GLA_SUB] * jnp.exp(cum[base:base + GLA_SUB] - cum[ref:ref + 1])
            kt = k[k_lo:k_hi] * jnp.exp(cum[ref:ref + 1] - cum[k_lo:k_hi])
            sc = _dot_nt(_stack_heads(qt), kt)
            ov = jnp.where(same, _dot(sc, v[k_lo:k_hi]), 0.0)
            parts.append(_unstack_sum(ov, GLA_SUB))
        offd.append(parts)
        inter.append(_dot_nt(q * jnp.exp(cum), st))
        st_new.append(st * jnp.exp(cum_end)
                      + jnp.where(bd, _dot_tn(v, k * jnp.exp(cum_end - cum)), 0.0))

    reds, spans = [], []
    for (q, k, v, la, st, rev), cum in zip(chains, cums):
        cum2 = cum * LOG2E
        pieces, sp = [], []
        for j in range(c):
            base = (j // GLA_SUB) * GLA_SUB
            own = (j // 8) * 8
            others = range(base, own, 8) if rev else range(own + 8, base + GLA_SUB, 8)
            for g0 in (own, *others):
                arg = cum2[g0:g0 + 8] - cum2[j:j + 1]
                if g0 == own:
                    valid = (rowid[g0:g0 + 8] <= j) if rev else (rowid[g0:g0 + 8] >= j)
                    arg = jnp.where(valid, arg, NEG_BIG)
                pieces.append(q[g0:g0 + 8] * (k[j:j + 1] * jnp.exp2(arg)))
                sp.append((j, g0))
        reds.append(jnp.dot(jnp.concatenate(pieces, axis=0).astype(BF16), ones_h,
                            preferred_element_type=F32))
        spans.append(sp)

    outs = []
    for ci, (q, k, v, la, st, rev) in enumerate(chains):
        groups = [None] * (c // 8)
        for idx, (j, g0) in enumerate(spans[ci]):
            piece = reds[ci][idx * 8:idx * 8 + 8] * v[j:j + 1]
            gi = g0 // 8
            groups[gi] = piece if groups[gi] is None else groups[gi] + piece
        blocks = []
        for blk in range(nblk):
            o_blk = jnp.concatenate(groups[blk * GLA_SUB // 8:(blk + 1) * GLA_SUB // 8], axis=0)
            if offd[ci][blk] is not None:
                o_blk = o_blk + offd[ci][blk]
            blocks.append(o_blk)
        outs.append((jnp.concatenate(blocks, axis=0) + inter[ci], st_new[ci]))
    return outs


def _ret_chunks(chains):
    bd = (_iota((BR_W, BR_W), 0) // HEAD_DIM) == (_iota((BR_W, BR_W), 1) // HEAD_DIM)
    sc = [_dot_nt(q, _stack_heads(k)) * dmat for q, k, v, s, dmat, _, _, _ in chains]
    inter = [_dot(q * qdec, s) for q, k, v, s, _, qdec, _, _ in chains]
    upd = [_dot_tn(k * kdec, v) for q, k, v, s, _, _, kdec, _ in chains]
    outs = []
    for ch, sc_i, inter_i, upd_i in zip(chains, sc, inter, upd):
        q, k, v, s, _, _, _, cdec = ch
        o = _dot(sc_i, _stack_heads(v)) + inter_i
        outs.append((o, s * cdec + jnp.where(bd, upd_i, 0.0)))
    return outs


def _rwkv_chunks(chains):
    c = chains[0][0].shape[0]
    n = HEADS * c
    bf = lambda t: t.astype(BF16)
    masks = {}
    for ch in chains:
        rev = ch[7]
        if rev not in masks:
            masks[rev] = (_tri_blockdiag(c, rev, True), _tri_blockdiag(c, rev, False))
    eye = jnp.where(_iota((n, n), 0) == _iota((n, n), 1), 1.0, 0.0)
    cums = [_cumsum_chunk(ch[5], ch[7]) for ch in chains]
    st = []
    for (r, v, kk, kd, asig, lw, s, rev), cum in zip(chains, cums):
        cum_end = cum[0:1] if rev else cum[c - 1:c]
        einv = jnp.exp(-cum)
        ehat = jnp.exp(cum_end - cum)
        bvec = kk * asig
        st.append(dict(
            a_s=bf(_stack_heads(-kk * jnp.exp(cum - lw))), r_s=bf(_stack_heads(r * jnp.exp(cum))),
            b_s=bf(_stack_heads(bvec * einv)), k_s=bf(_stack_heads(kd * einv)),
            v_s=bf(_stack_heads(v)), bh_s=bf(_stack_heads(bvec * ehat)),
            kh_s=bf(_stack_heads(kd * ehat)), g_end=jnp.exp(cum_end), s=s, sb=bf(s), rev=rev))
    for d in st:
        d["m"] = _dot_nt(jnp.concatenate([d["a_s"], d["r_s"]], axis=0),
                         jnp.concatenate([d["b_s"], d["k_s"]], axis=0))
    for d in st:
        strict, incl = masks[d["rev"]]
        m = d.pop("m")
        d["p"] = jnp.where(strict, m[0:n, 0:n], 0.0)
        d["t"] = eye + d["p"]
        d["m_ak"] = bf(jnp.where(strict, m[0:n, n:2 * n], 0.0))
        d["n_rb"] = bf(jnp.where(incl, m[n:2 * n, 0:n], 0.0))
        d["n_rk"] = bf(jnp.where(incl, m[n:2 * n, n:2 * n], 0.0))
    for d in st:
        d["w"] = _dot_nt(d["a_s"], d["sb"]) + _dot(d["m_ak"], d["v_s"])
        d["y0"] = _dot_nt(d["r_s"], d["sb"]) + _dot(d["n_rk"], d["v_s"])
    for _ in range(int(math.log2(c)) - 1):
        for d in st:
            pb = bf(d["p"])
            d["p"] = jnp.dot(pb, pb, preferred_element_type=F32)
        for d in st:
            d["t"] = d["t"] + _dot(d["t"], d["p"])
    for d in st:
        d["u"] = bf(_dot(d["t"], d["w"]))
    outs = []
    for d in st:
        y = _unstack_sum(d["y0"] + _dot(d["n_rb"], d["u"]), c)
        s_new = d["s"] * d["g_end"] + _dot_tn(jnp.concatenate([d["u"], d["v_s"]], axis=0),
                                              jnp.concatenate([d["bh_s"], d["kh_s"]], axis=0))
        outs.append((y, s_new))
    return outs


def _scan_kernel(kind, n_in, n_tab, nc, bb, *refs):
    fwd = refs[0:n_in]
    bwd = refs[n_in:2 * n_in]
    tabs = refs[2 * n_in:2 * n_in + n_tab]
    s0_ref = refs[2 * n_in + n_tab]
    of_ref, ob_ref, sf_ref, st_ref = refs[2 * n_in + n_tab + 1:]
    i = pl.program_id(1)

    @pl.when(i == 0)
    def _():
        st_ref[...] = s0_ref[...]

    chains = [(dr, b) for b in range(bb) for dr in range(2)]
    ins_of = lambda dr, b: [t[b] for t in (fwd, bwd)[dr]]
    if kind == "rwkv":
        res = _rwkv_chunks([(*ins_of(dr, b), st_ref[dr, b], dr == 1) for dr, b in chains])
    elif kind == "gla":
        res = _gla_chunks([(*ins_of(dr, b), st_ref[dr, b], dr == 1) for dr, b in chains])
    else:
        res = _ret_chunks([(*ins_of(dr, b), st_ref[dr, b], *(t[dr] for t in tabs))
                           for dr, b in chains])
    for (dr, b), (o, s_new) in zip(chains, res):
        (of_ref, ob_ref)[dr][b] = o
        st_ref[dr, b] = s_new

    @pl.when(i == nc - 1)
    def _():
        sf_ref[...] = st_ref[...]


_SCAN_ROWS = {"gla": 4, "ret": 4, "rwkv": 4}


def _scan(kind, feats, cols_f, cols_b, s0, tabs=()):
    B, T, _ = feats.shape
    c = min(SCAN_CHUNK[kind], T)
    nc = T // c
    bb = _SCAN_ROWS[kind]
    n_in = len(cols_f)
    in_specs = []
    for j in cols_f:
        in_specs.append(pl.BlockSpec((bb, c, BR_W), lambda b, i, j=j: (b, i, j)))
    for j in cols_b:
        in_specs.append(pl.BlockSpec((bb, c, BR_W), lambda b, i, j=j: (b, nc - 1 - i, j)))
    for t in tabs:
        in_specs.append(pl.BlockSpec(t.shape, lambda b, i, nd=t.ndim: (0,) * nd))
    in_specs.append(pl.BlockSpec((2, bb, BR_W, BR_W), lambda b, i: (0, b, 0, 0)))
    kern = functools.partial(_scan_kernel, kind, n_in, len(tabs), nc, bb)
    return pl.pallas_call(
        kern,
        grid=(B // bb, nc),
        in_specs=in_specs,
        out_specs=[
            pl.BlockSpec((bb, c, BR_W), lambda b, i: (b, i, 0)),
            pl.BlockSpec((bb, c, BR_W), lambda b, i: (b, nc - 1 - i, 0)),
            pl.BlockSpec((2, bb, BR_W, BR_W), lambda b, i: (0, b, 0, 0)),
        ],
        out_shape=[
            jax.ShapeDtypeStruct((B, T, BR_W), F32),
            jax.ShapeDtypeStruct((B, T, BR_W), F32),
            jax.ShapeDtypeStruct((2, B, BR_W, BR_W), F32),
        ],
        scratch_shapes=[pltpu.VMEM((2, bb, BR_W, BR_W), F32)],
        compiler_params=_cparams(("parallel", "arbitrary")),
        name=kind,
    )(*([feats] * (2 * n_in)), *tabs, s0)


def _ret_tables(c):
    pos = np.arange(c, dtype=np.float64)
    lane_head = np.arange(BR_W) // HEAD_DIM
    dmats, qd, kd, cd = [], [], [], []
    for dr in range(2):
        expo = -5.0 - np.arange(HEADS, dtype=np.float64)
        if dr == 1:
            expo = expo[::-1]
        log_g = np.log1p(-np.exp2(expo))
        if dr == 0:
            rel = pos[:, None] - pos[None, :]
            qpow = pos + 1.0
            kpow = c - 1.0 - pos
        else:
            rel = pos[None, :] - pos[:, None]
            qpow = c - pos
            kpow = pos
        tri = rel >= 0
        dm = np.where(tri[None], np.exp(np.where(tri, rel, 0.0)[None] * log_g[:, None, None]), 0.0)
        dmats.append(np.concatenate([dm[h] for h in range(HEADS)], axis=1))
        qd.append(np.exp(qpow[:, None] * log_g[lane_head][None, :]))
        kd.append(np.exp(kpow[:, None] * log_g[lane_head][None, :]))
        cd.append(np.exp(c * log_g[lane_head])[None, :])
    f = lambda xs: jnp.asarray(np.stack(xs), dtype=F32)
    return f(dmats), f(qd), f(kd), f(cd)


def _fnet_dense_kernel(t_len, cs_ref, g1_ref, g2_ref, o_ref):
    o_ref[0] = (jnp.dot(cs_ref[:, 0:t_len], g1_ref[0].astype(BF16), preferred_element_type=F32)
                + jnp.dot(cs_ref[:, t_len:2 * t_len], g2_ref[0].astype(BF16),
                          preferred_element_type=F32))


def _fnet_dense(g1, g2, cs):
    B, T, _ = g1.shape
    tb = min(512, T)
    return pl.pallas_call(
        functools.partial(_fnet_dense_kernel, T),
        grid=(T // tb, B),
        in_specs=[
            pl.BlockSpec((tb, 2 * T), lambda i, b: (i, 0)),
            pl.BlockSpec((1, T, BR_W), lambda i, b: (b, 0, 0)),
            pl.BlockSpec((1, T, BR_W), lambda i, b: (b, 0, 0)),
        ],
        out_specs=pl.BlockSpec((1, tb, BR_W), lambda i, b: (b, i, 0)),
        out_shape=jax.ShapeDtypeStruct((B, T, BR_W), F32),
        compiler_params=_cparams(("arbitrary", "arbitrary")),
        name="fnet",
    )(cs, g1, g2)


FFT_R = 8


def _fft_kernel(n, k1_ref, k2_ref, tc_ref, ts_ref, g1_ref, g2_ref, o_ref, zr_ref, zi_ref):
    m = n * FFT_R
    for j in range(n // FFT_R):
        sl = slice(j * FFT_R, (j + 1) * FFT_R)
        x = jnp.concatenate([g1_ref[0, :, sl, :].reshape(m, BR_W),
                             g2_ref[0, :, sl, :].reshape(m, BR_W)], axis=1).astype(BF16)
        y = jnp.dot(k1_ref[...], x, preferred_element_type=F32)
        yr = y[0:m, 0:BR_W] - y[m:2 * m, BR_W:2 * BR_W]
        yi = -y[0:m, BR_W:2 * BR_W] - y[m:2 * m, 0:BR_W]
        tc = tc_ref[:, sl, :].reshape(m, BR_W)
        ts = ts_ref[:, sl, :].reshape(m, BR_W)
        zr_ref[:, sl, :] = (tc * yr + ts * yi).reshape(n, FFT_R, BR_W)
        zi_ref[:, sl, :] = (tc * yi - ts * yr).reshape(n, FFT_R, BR_W)
    for j in range(n // FFT_R):
        sl = slice(j * FFT_R, (j + 1) * FFT_R)
        z = jnp.concatenate([zr_ref[sl].reshape(m, BR_W), zi_ref[sl].reshape(m, BR_W)],
                            axis=0).astype(BF16)
        o = jnp.dot(k2_ref[...], z, preferred_element_type=F32)
        o_ref[0, :, sl, :] = o.reshape(n, FFT_R, BR_W)


def _fnet_fft(g1, g2):
    B, T, _ = g1.shape
    n = int(round(math.sqrt(T)))
    assert n * n == T and n % FFT_R == 0
    m = n * FFT_R
    idx = np.arange(n)
    a_n = 2.0 * np.pi * ((idx[:, None] * idx[None, :]) % n) / n
    eye = np.eye(FFT_R)
    k1 = np.concatenate([np.einsum("ap,ij->aipj", f, eye).reshape(m, m)
                         for f in (np.cos(a_n), np.sin(a_n))], axis=0)
    k2 = np.concatenate([np.einsum("pr,ij->pijr", f, eye).reshape(m, m)
                         for f in (np.cos(a_n), np.sin(a_n))], axis=1)
    a_t = 2.0 * np.pi * (idx[:, None] * idx[None, :]) / T
    tc = jnp.broadcast_to(jnp.asarray(np.cos(a_t), dtype=F32)[:, :, None], (n, n, BR_W))
    ts = jnp.broadcast_to(jnp.asarray(np.sin(a_t), dtype=F32)[:, :, None], (n, n, BR_W))
    const2 = lambda b: (0, 0)
    const3 = lambda b: (0, 0, 0)
    one = pl.Buffered(1)
    o = pl.pallas_call(
        functools.partial(_fft_kernel, n),
        grid=(B,),
        in_specs=[
            pl.BlockSpec((2 * m, m), const2, pipeline_mode=one),
            pl.BlockSpec((m, 2 * m), const2, pipeline_mode=one),
            pl.BlockSpec((n, n, BR_W), const3, pipeline_mode=one),
            pl.BlockSpec((n, n, BR_W), const3, pipeline_mode=one),
            pl.BlockSpec((1, n, n, BR_W), lambda b: (b, 0, 0, 0)),
            pl.BlockSpec((1, n, n, BR_W), lambda b: (b, 0, 0, 0)),
        ],
        out_specs=pl.BlockSpec((1, n, n, BR_W), lambda b: (b, 0, 0, 0)),
        out_shape=jax.ShapeDtypeStruct((B, n, n, BR_W), F32),
        scratch_shapes=[pltpu.VMEM((n, n, BR_W), F32), pltpu.VMEM((n, n, BR_W), F32)],
        compiler_params=_cparams(("arbitrary",)),
        name="fft",
    )(jnp.asarray(k1, dtype=BF16), jnp.asarray(k2, dtype=BF16), tc, ts,
      g1.reshape(B, n, n, BR_W), g2.reshape(B, n, n, BR_W))
    return o.reshape(B, T, BR_W)


def _time_dft_table(T):
    t = np.arange(T)
    ang = ((t[:, None] * t[None, :]) % T) * (2.0 * np.pi / T)
    return jnp.asarray(np.concatenate([np.cos(ang), -np.sin(ang)], axis=1), dtype=BF16)


def _channel_dft_tables(T):
    ch = np.arange(HEAD_DIM)
    a64 = 2.0 * np.pi * ((ch[:, None] * ch[None, :]) % HEAD_DIM) / HEAD_DIM
    scale = (T * HEAD_DIM) ** -0.5
    eye = np.eye(HEADS)
    cbd = np.kron(eye, np.cos(a64)) * scale
    sbd = np.kron(eye, np.sin(a64)) * scale
    return jnp.asarray(np.stack([cbd, sbd]), dtype=BF16)


def _head_norm(o, ones_h, center):
    if center:
        o = o - _dot(o, ones_h) * (1.0 / HEAD_DIM)
    var = _dot(o * o, ones_h) * (1.0 / HEAD_DIM)
    return o * lax.rsqrt(var + EPS)


def _merge_kernel(x_ref, hb_ref, mod_ref, gg_ref, rg_ref, wg_ref, bon_ref,
                  ogf_ref, ogb_ref, orf_ref, orb_ref, oyf_ref, oyb_ref, fn_ref,
                  gn_ref, wgate_ref, bgate_ref, wbr_ref, wout_ref, o_ref):
    ones_h = _head_ones()
    gn = gn_ref[...]
    gla = _head_norm(ogf_ref[0] + ogb_ref[0], ones_h, False) * gn[0:1] * _silu(gg_ref[0])
    ret = _head_norm(orf_ref[0] + orb_ref[0], ones_h, True) * gn[1:2] * _silu(rg_ref[0])
    rwkv = (_head_norm(oyf_ref[0] + oyb_ref[0], ones_h, True) * gn[2:3] + bon_ref[0]) * wg_ref[0]
    outs = (gla, ret, rwkv, fn_ref[0])
    hb = hb_ref[0]
    z = None
    for br in range(4):
        gate = _sigmoid(jnp.dot(hb, wgate_ref[br], preferred_element_type=F32) + bgate_ref[br])
        term = gate * _dot(outs[br], wbr_ref[br])
        z = term if z is None else z + term
    y = _dot(z, wout_ref[...])
    gate1 = mod_ref[0][:, 2 * D_MODEL:3 * D_MODEL]
    o_ref[0] = x_ref[0] + gate1 * y


def _merge(x, hb, mods, feats, scans, fnet_out, lw):
    B, T, D = x.shape
    tb = min(256, T)
    tok = lambda w: pl.BlockSpec((1, tb, w), lambda b, i: (b, i, 0))
    fcol = lambda j: pl.BlockSpec((1, tb, BR_W), lambda b, i, j=j: (b, i, j))
    const2 = lambda b, i: (0, 0)
    const3 = lambda b, i: (0, 0, 0)
    return pl.pallas_call(
        _merge_kernel,
        grid=(B, T // tb),
        in_specs=[
            tok(D), tok(D),
            pl.BlockSpec((1, 1, 6 * D), lambda b, i: (b, 0, 0)),
            fcol(F_GG), fcol(F_RG), fcol(F_WG), fcol(F_BON),
            tok(BR_W), tok(BR_W), tok(BR_W), tok(BR_W), tok(BR_W), tok(BR_W), tok(BR_W),
            pl.BlockSpec((3, BR_W), const2),
            pl.BlockSpec((4, D, D), const3),
            pl.BlockSpec((4, 1, D), const3),
            pl.BlockSpec((4, BR_W, D), const3),
            pl.BlockSpec((D, D), const2),
        ],
        out_specs=tok(D),
        out_shape=jax.ShapeDtypeStruct((B, T, D), F32),
        compiler_params=_cparams(("parallel", "arbitrary")),
        name="merge",
    )(x, hb, mods, feats, feats, feats, feats, *scans, fnet_out,
      lw["gn"], lw["w_gate"], lw["b_gate"], lw["w_br"], lw["w_out"])


FFN_COL_CHUNKS = (1024, 1024, 768)


def _ffn_kernel(final, tb, nt,
                x_ref, xp_ref, xn_ref, mod_ref, g2_ref, up_ref, cw_ref, cb_ref,
                down_ref, gfin_ref, o_ref, act_ref):
    i = pl.program_id(1)
    n_ext = tb + 2 * HALO
    x_ext = jnp.concatenate([xp_ref[0], x_ref[0], xn_ref[0]], axis=0)
    mod = mod_ref[0]
    shift = mod[:, 3 * D_MODEL:4 * D_MODEL]
    scale = mod[:, 4 * D_MODEL:5 * D_MODEL]
    h2 = (_rms_rows(x_ext) * g2_ref[...]) * (1.0 + scale) + shift
    h2 = h2 * _halo_keep(n_ext, tb, i, nt)
    h2_ext = h2.astype(BF16)
    h2_mid = h2[HALO:HALO + tb].astype(BF16)
    cw = cw_ref[...]
    cb = cb_ref[...]
    lo = 0
    for width in FFN_COL_CHUNKS:
        a_ext = jnp.dot(h2_ext, up_ref[:, lo:lo + width], preferred_element_type=F32)
        u = jnp.dot(h2_mid, up_ref[:, D_FF + lo:D_FF + lo + width], preferred_element_type=F32)
        a = (a_ext[HALO - 1:HALO - 1 + tb] * cw[0:1, lo:lo + width]
             + a_ext[HALO:HALO + tb] * cw[1:2, lo:lo + width]
             + a_ext[HALO + 1:HALO + 1 + tb] * cw[2:3, lo:lo + width] + cb[:, lo:lo + width])
        act_ref[:, lo:lo + width] = (_silu(a) * u).astype(BF16)
        lo += width
    y = jnp.dot(act_ref[...], down_ref[...], preferred_element_type=F32)
    gate2 = mod[:, 5 * D_MODEL:6 * D_MODEL]
    res = x_ref[0] + gate2 * y
    if final:
        res = _rms_rows(res) * gfin_ref[...]
    o_ref[0] = res


def _ffn(x, mods, lw, g_final, final):
    B, T, D = x.shape
    tb = min(512, T)
    nt = T // tb
    hb8 = tb // HALO
    n_h = T // HALO
    const2 = lambda b, i: (0, 0)
    resident = lambda shape: pl.BlockSpec(shape, const2, pipeline_mode=pl.Buffered(1))
    kern = functools.partial(_ffn_kernel, final, tb, nt)
    return pl.pallas_call(
        kern,
        grid=(B, nt),
        in_specs=[
            pl.BlockSpec((1, tb, D), lambda b, i: (b, i, 0)),
            pl.BlockSpec((1, HALO, D), lambda b, i: (b, jnp.maximum(i * hb8 - 1, 0), 0)),
            pl.BlockSpec((1, HALO, D), lambda b, i: (b, jnp.minimum((i + 1) * hb8, n_h - 1), 0)),
            pl.BlockSpec((1, 1, 6 * D), lambda b, i: (b, 0, 0)),
            pl.BlockSpec((1, D), const2),
            resident((D, 2 * D_FF)),
            pl.BlockSpec((3, D_FF), const2),
            pl.BlockSpec((1, D_FF), const2),
            resident((D_FF, D)),
            pl.BlockSpec((1, D), const2),
        ],
        out_specs=pl.BlockSpec((1, tb, D), lambda b, i: (b, i, 0)),
        out_shape=jax.ShapeDtypeStruct((B, T, D), F32),
        scratch_shapes=[pltpu.VMEM((tb, D_FF), BF16)],
        compiler_params=_cparams(("parallel", "arbitrary")),
        name="ffn",
    )(x, x, x, mods, lw["g2"], lw["ffn_up"], lw["ffn_conv"], lw["ffn_conv_b"],
      lw["ffn_down"], g_final)


def _rope_tables(T):
    rows = T // GRID_W
    row = jnp.repeat(jnp.arange(rows, dtype=F32), GRID_W)
    colp = jnp.tile(jnp.arange(GRID_W, dtype=F32), rows)
    n_freq = HEAD_DIM // 4
    inv = ROPE_BASE ** (-jnp.arange(n_freq, dtype=F32) / n_freq)
    ang = jnp.concatenate([row[:, None] * inv, colp[:, None] * inv], axis=-1)
    cos, sin = jnp.cos(ang), jnp.sin(ang)
    cos_h = jnp.concatenate([cos, cos], axis=-1)
    sin_h = jnp.concatenate([-sin, sin], axis=-1)
    return jnp.tile(cos_h, (1, HEADS)), jnp.tile(sin_h, (1, HEADS))


def _pad_rows(w, lo, n):
    return jnp.zeros((n, w.shape[1]), w.dtype).at[lo:lo + w.shape[0]].set(w)


def _layer_weights(l, p):
    D = D_MODEL
    bf = lambda t: t.astype(BF16)
    zcols = lambda w, n: jnp.concatenate([w, jnp.zeros((w.shape[0], n - w.shape[1]), w.dtype)], axis=1)
    w_h = zcols(jnp.concatenate([p["gla_wa1"][l, 0], p["gla_wa1"][l, 1]], axis=1), 128)
    w_xw = jnp.concatenate([p["rwkv_w1"][l, 0], p["rwkv_w1"][l, 1]], axis=1)
    w_xa = jnp.concatenate([p["rwkv_a1"][l, 0], p["rwkv_a1"][l, 1]], axis=1)
    w_xg = zcols(p["rwkv_g1"][l], 256)
    mu = p["rwkv_mu"][l]
    w_lr = jnp.concatenate([w_xg, mu[2][:, None] * w_xg, w_h, w_xw, w_xa,
                            mu[0][:, None] * w_xw, mu[1][:, None] * w_xa,
                            jnp.zeros((D, LR_COLS - LR_AS - 128), F32)], axis=1)
    w2 = jnp.stack([
        _pad_rows(p["gla_wa2"][l, 0], 0, 256), _pad_rows(p["gla_wa2"][l, 1], GLA_LR, 256),
        _pad_rows(p["rwkv_w2"][l, 0], 0, 256), _pad_rows(p["rwkv_w2"][l, 1], RWKV_LR, 256),
        _pad_rows(p["rwkv_a2"][l, 0], 0, 256), _pad_rows(p["rwkv_a2"][l, 1], RWKV_LR, 256),
        _pad_rows(p["rwkv_g2"][l], 0, 256),
    ])
    vec = jnp.stack([p["gla_ba"][l, 0], p["gla_ba"][l, 1], p["rwkv_w0"][l, 0], p["rwkv_w0"][l, 1],
                     p["rwkv_a0"][l, 0], p["rwkv_a0"][l, 1], p["rwkv_kk"][l], p["rwkv_ka"][l],
                     p["rwkv_rk"][l]])
    vec = jnp.concatenate([vec, jnp.zeros((16 - vec.shape[0], BR_W), F32)], axis=0)
    return {
        "g1": p["g_norm1"][l].reshape(1, D), "g2": p["g_norm2"][l].reshape(1, D),
        "w_in": bf(p["w_in"][l]), "w_lr": bf(w_lr), "w2": bf(w2), "vec": vec,
        "rwkv_conv": p["rwkv_conv"][l],
        "gn": jnp.stack([p["gla_gn"][l], p["ret_gn"][l], p["rwkv_gn"][l]]),
        "w_gate": bf(p["w_gate"][l]), "b_gate": p["b_gate"][l].reshape(4, 1, D),
        "w_br": bf(p["w_br"][l]), "w_out": bf(p["w_out"][l]),
        "ffn_up": bf(p["ffn_up"][l]), "ffn_conv": p["ffn_conv"][l],
        "ffn_conv_b": p["ffn_conv_b"][l].reshape(1, D_FF), "ffn_down": bf(p["ffn_down"][l]),
    }


_GLA_F = (F_GQ, F_GK, F_GV, F_LAF)
_GLA_B = (F_GQ, F_GK, F_GV, F_LAB)
_RET = (F_RQ, F_RK, F_RV)
_RWKV_F = (F_WR, F_WV, F_WKK, F_KDF, F_ASF, F_LWF)
_RWKV_B = (F_WR, F_WV, F_WKK, F_KDB, F_ASB, F_LWB)


def _mixers(feats_c, feats_l, ret_tabs_c, ret_tabs_l, need_ctx):
    B = feats_l.shape[0]
    zero = jnp.zeros((2, B, BR_W, BR_W), F32)
    out_c, out_l = [], []
    for kind, cf, cb, tabs in (("gla", _GLA_F, _GLA_B, None), ("ret", _RET, _RET, True),
                               ("rwkv", _RWKV_F, _RWKV_B, None)):
        tc = ret_tabs_c if tabs else ()
        tl = ret_tabs_l if tabs else ()
        of_c, ob_c, s_c = _scan(kind, feats_c, cf, cb, zero, tc)
        of_l, ob_l, _ = _scan(kind, feats_l, cf, cb, s_c, tl)
        out_c += [of_c, ob_c]
        out_l += [of_l, ob_l]
    return out_l, (out_c if need_ctx else None)


def kernel(x, c, ctx, c_ctx, w_ada, b_ada, g_norm1, g_norm2, w_in, gla_wa1, gla_wa2, gla_ba, gla_gn, ret_gn, rwkv_conv, rwkv_mu, rwkv_w0, rwkv_w1, rwkv_w2, rwkv_a0, rwkv_a1, rwkv_a2, rwkv_g1, rwkv_g2, rwkv_kk, rwkv_ka, rwkv_rk, rwkv_gn, w_gate, b_gate, w_br, w_out, ffn_up, ffn_conv, ffn_conv_b, ffn_down, g_final):
    p = dict(g_norm1=g_norm1, g_norm2=g_norm2, w_in=w_in, gla_wa1=gla_wa1, gla_wa2=gla_wa2,
             gla_ba=gla_ba, gla_gn=gla_gn, ret_gn=ret_gn, rwkv_conv=rwkv_conv, rwkv_mu=rwkv_mu,
             rwkv_w0=rwkv_w0, rwkv_w1=rwkv_w1, rwkv_w2=rwkv_w2, rwkv_a0=rwkv_a0,
             rwkv_a1=rwkv_a1, rwkv_a2=rwkv_a2, rwkv_g1=rwkv_g1, rwkv_g2=rwkv_g2,
             rwkv_kk=rwkv_kk, rwkv_ka=rwkv_ka, rwkv_rk=rwkv_rk, rwkv_gn=rwkv_gn,
             w_gate=w_gate, b_gate=b_gate, w_br=w_br, w_out=w_out, ffn_up=ffn_up,
             ffn_conv=ffn_conv, ffn_conv_b=ffn_conv_b, ffn_down=ffn_down)
    B, T, D = x.shape
    Tc = ctx.shape[1]
    depth = w_ada.shape[0]

    cmat = jnp.concatenate([c, c_ctx[None, :], jnp.zeros((16 - B - 1, D), F32)], axis=0)
    mod_all = _modulation(cmat, w_ada, b_ada)

    rope_l = _rope_tables(T)
    rope_c = (jnp.zeros((Tc, BR_W), F32), jnp.zeros((Tc, BR_W), F32))
    dft_l = _channel_dft_tables(T)
    dft_c = _channel_dft_tables(Tc)
    cs_c = _time_dft_table(Tc)
    ret_tabs_l = _ret_tables(min(SCAN_CHUNK["ret"], T))
    ret_tabs_c = _ret_tables(min(SCAN_CHUNK["ret"], Tc))
    g_fin = g_final.reshape(1, D)

    for l in range(depth):
        last = l == depth - 1
        lw = _layer_weights(l, p)
        mods_l = mod_all[l, 0:B].reshape(B, 1, 6 * D)
        mods_c = jnp.broadcast_to(mod_all[l, B:B + 1].reshape(1, 1, 6 * D), (B, 1, 6 * D))

        feats_l, hb_l, g1_l, g2_l = _pre(x, mods_l, lw, True, rope_l, dft_l)
        feats_c, hb_c, g1_c, g2_c = _pre(ctx, mods_c, lw, False, rope_c, dft_c)
        scans_l, scans_c = _mixers(feats_c, feats_l, ret_tabs_c, ret_tabs_l, not last)

        fn_l = _fnet_fft(g1_l, g2_l)
        x = _merge(x, hb_l, mods_l, feats_l, scans_l, fn_l, lw)
        x = _ffn(x, mods_l, lw, g_fin, last)
        if not last:
            fn_c = _fnet_dense(g1_c, g2_c, cs_c)
            ctx = _merge(ctx, hb_c, mods_c, feats_c, scans_c, fn_c, lw)
            ctx = _ffn(ctx, mods_c, lw, g_fin, False)
    return x
```

```python
import functools
import math

import numpy as np
import jax
import jax.numpy as jnp
from jax import lax
from jax.experimental import pallas as pl
from jax.experimental.pallas import tpu as pltpu

F32 = jnp.float32
BF16 = jnp.bfloat16

D_MODEL = 1024
BR_W = 256
HEAD_DIM = 64
HEADS = 4
N_PARTS = 12
P_IN = N_PARTS * BR_W
GRID_W = 64
GLA_LR = 16
GLA_NORMALIZER = 16.0
RWKV_LR = 64
RWKV_G_LR = 160
D_FF = 2816
ROPE_BASE = 10000.0
EPS = 1e-6

SCAN_CHUNK = {"gla": 64, "ret": 128, "rwkv": 64}
GLA_SUB = 16
HALO = 8
NEG_BIG = -1e30
LOG2E = 1.4426950408889634

(F_GQ, F_GK, F_GV, F_GG, F_RQ, F_RK, F_RV, F_RG, F_WR, F_WV, F_WKK, F_WG, F_BON,
 F_LAF, F_LAB, F_LWF, F_LWB, F_KDF, F_KDB, F_ASF, F_ASB) = range(21)
N_FEAT = 21

LR_G, LR_GS, LR_H, LR_W, LR_A, LR_WS, LR_AS = 0, 256, 512, 640, 768, 896, 1024
LR_COLS = 1280

VMEM_LIMIT = 56 * 1024 * 1024


def _cparams(sem):
    return pltpu.CompilerParams(dimension_semantics=sem, vmem_limit_bytes=VMEM_LIMIT)


def _dot(a, b):
    return jnp.dot(a.astype(BF16), b.astype(BF16), preferred_element_type=F32)


def _dot_nt(a, b):
    return lax.dot_general(a.astype(BF16), b.astype(BF16), (((1,), (1,)), ((), ())),
                           preferred_element_type=F32)


def _dot_tn(a, b):
    return lax.dot_general(a.astype(BF16), b.astype(BF16), (((0,), (0,)), ((), ())),
                           preferred_element_type=F32)


def _split3(x):
    hi = x.astype(BF16)
    r1 = x - hi.astype(F32)
    mid = r1.astype(BF16)
    lo = (r1 - mid.astype(F32)).astype(BF16)
    return hi, mid, lo


def _dot_exact_lhs(a_bf16, x):
    hi, mid, lo = _split3(x)
    d = lambda t: jnp.dot(a_bf16, t, preferred_element_type=F32)
    return d(hi) + d(mid) + d(lo)


def _sigmoid(x):
    return 1.0 / (1.0 + jnp.exp(-x))


def _silu(x):
    return x * _sigmoid(x)


def _softplus(x):
    return jnp.maximum(x, 0.0) + jnp.log(1.0 + jnp.exp(-jnp.abs(x)))


def _iota(shape, dim):
    return lax.broadcasted_iota(jnp.int32, shape, dim)


def _head_ones():
    r = _iota((BR_W, BR_W), 0) // HEAD_DIM
    c = _iota((BR_W, BR_W), 1) // HEAD_DIM
    return jnp.where(r == c, 1.0, 0.0).astype(BF16)


def _same_head(n_rows, rows_per_head):
    r = _iota((n_rows, BR_W), 0) // rows_per_head
    c = _iota((n_rows, BR_W), 1) // HEAD_DIM
    return r == c


def _stack_heads(x):
    c, w = x.shape
    xs = jnp.concatenate([x] * HEADS, axis=0)
    keep = (_iota((HEADS * c, w), 0) // c) == (_iota((HEADS * c, w), 1) // (w // HEADS))
    return jnp.where(keep, xs, 0.0)


def _unstack_sum(y, c):
    out = y[0:c]
    for h in range(1, HEADS):
        out = out + y[h * c:(h + 1) * c]
    return out


def _tri(c, rev, strict=False):
    r = _iota((c, c), 0)
    s = _iota((c, c), 1)
    if rev:
        return (r < s) if strict else (r <= s)
    return (r > s) if strict else (r >= s)


def _halo_keep(n_ext, tb, i, nt):
    row = _iota((n_ext, 1), 0)
    first = jnp.where(i > 0, 1.0, 0.0)
    last = jnp.where(i < nt - 1, 1.0, 0.0)
    return jnp.where(row < HALO, first, jnp.where(row >= tb + HALO, last, 1.0))


def _rms_rows(x):
    return x * lax.rsqrt(jnp.mean(x * x, axis=-1, keepdims=True) + EPS)


def _mod_kernel(c_ref, w_ref, b_ref, o_ref):
    s = _silu(c_ref[...])
    o_ref[0] = _dot(s, w_ref[0]) + b_ref[0]


def _modulation(cmat, w_ada, b_ada):
    L = w_ada.shape[0]
    nblk = 1536
    return pl.pallas_call(
        _mod_kernel,
        grid=(L, 6 * D_MODEL // nblk),
        in_specs=[
            pl.BlockSpec((16, D_MODEL), lambda l, j: (0, 0)),
            pl.BlockSpec((1, D_MODEL, nblk), lambda l, j: (l, 0, j)),
            pl.BlockSpec((1, 1, nblk), lambda l, j: (l, 0, j)),
        ],
        out_specs=pl.BlockSpec((1, 16, nblk), lambda l, j: (l, 0, j)),
        out_shape=jax.ShapeDtypeStruct((L, 16, 6 * D_MODEL), F32),
        compiler_params=_cparams(("arbitrary", "arbitrary")),
        name="mod",
    )(cmat, w_ada, b_ada.reshape(L, 1, 6 * D_MODEL))


def _pre_kernel(use_rope, tb, nt,
                x_ref, xp_ref, xn_ref, mod_ref, g1_ref, win_ref, wlr_ref,
                w2_ref, vec_ref, conv_ref, dft_ref, cos_ref, sin_ref,
                f_ref, hb_ref, fc_ref, fs_ref, lr_ref, ps_ref):
    i = pl.program_id(1)
    n_ext = tb + 2 * HALO
    x_ext = jnp.concatenate([xp_ref[0], x_ref[0], xn_ref[0]], axis=0)
    mod = mod_ref[0]
    shift = mod[:, 0:D_MODEL]
    scale = mod[:, D_MODEL:2 * D_MODEL]
    h_ext = (_rms_rows(x_ext) * g1_ref[...]) * (1.0 + scale) + shift
    h_ext = h_ext * _halo_keep(n_ext, tb, i, nt)
    hb_ext = h_ext.astype(BF16)
    hb = h_ext[HALO:HALO + tb].astype(BF16)
    hb_ref[0] = hb
    lr_ref[...] = jnp.dot(hb_ext, wlr_ref[...], preferred_element_type=F32)
    ps_ref[...] = jnp.dot(hb_ext, win_ref[:, 8 * BR_W:11 * BR_W], preferred_element_type=F32)

    def mid(lo, hi):
        return lr_ref[HALO:HALO + tb, lo:hi]

    def shifted(lo, hi):
        return (0.5 * (lr_ref[HALO - 1:HALO - 1 + tb, lo:hi] + lr_ref[HALO + 1:HALO + 1 + tb, lo:hi])
                - lr_ref[HALO:HALO + tb, lo:hi])

    vec = vec_ref[...]
    ba_f, ba_b, w0_f, w0_b = vec[0:1], vec[1:2], vec[2:3], vec[3:4]
    a0_f, a0_b, kkw, kaw, rkw = vec[4:5], vec[5:6], vec[6:7], vec[7:8], vec[8:9]

    def put(j, val):
        f_ref[0, :, j * BR_W:(j + 1) * BR_W] = val

    zh = mid(LR_H, LR_H + 128)
    tw = jnp.tanh(mid(LR_W, LR_W + 128) + shifted(LR_WS, LR_WS + 128))
    ta = mid(LR_A, LR_A + 128) + shifted(LR_AS, LR_AS + 128)
    sg = _sigmoid(mid(LR_G, LR_G + 256) + shifted(LR_GS, LR_GS + 256))

    z_la = [_dot(zh, w2_ref[dr, 0:128, :]) for dr in range(2)]
    z_w = [_dot(tw, w2_ref[2 + dr, 0:128, :]) for dr in range(2)]
    z_a = [_dot(ta, w2_ref[4 + dr, 0:128, :]) for dr in range(2)]
    put(F_WG, _dot(sg, w2_ref[6]))

    p_gla = jnp.dot(hb, win_ref[:, 0:4 * BR_W], preferred_element_type=F32)
    put(F_GQ, p_gla[:, 0:BR_W] * HEAD_DIM ** -0.5)
    put(F_GK, p_gla[:, BR_W:2 * BR_W])
    put(F_GV, p_gla[:, 2 * BR_W:3 * BR_W])
    put(F_GG, p_gla[:, 3 * BR_W:4 * BR_W])

    for dr, (ba, dst) in enumerate(((ba_f, F_LAF), (ba_b, F_LAB))):
        put(dst, -_softplus(-(z_la[dr] + ba)) * (1.0 / GLA_NORMALIZER))

    p_ret = jnp.dot(hb, win_ref[:, 4 * BR_W:8 * BR_W], preferred_element_type=F32)

    conv = conv_ref[...]

    def dwconv(part):
        lo, hi = part * BR_W, (part + 1) * BR_W
        cw = conv[:, lo:hi]
        return (ps_ref[HALO - 1:HALO - 1 + tb, lo:hi] * cw[0:1]
                + ps_ref[HALO:HALO + tb, lo:hi] * cw[1:2]
                + ps_ref[HALO + 1:HALO + 1 + tb, lo:hi] * cw[2:3])

    r_c = dwconv(0)
    k_c = dwconv(1)
    v_c = dwconv(2)
    ones_h = _head_ones()
    kk = k_c * kkw
    kk = kk * lax.rsqrt(_dot(kk * kk, ones_h) + EPS)
    put(F_WR, r_c)
    put(F_WV, v_c)
    put(F_WKK, kk)

    p_fnet = jnp.dot(hb, win_ref[:, 11 * BR_W:12 * BR_W], preferred_element_type=F32)

    bonus = jnp.zeros((tb, BR_W), F32)
    for dr, (w0, a0, d_lw, d_kd, d_as) in enumerate(
            ((w0_f, a0_f, F_LWF, F_KDF, F_ASF), (w0_b, a0_b, F_LWB, F_KDB, F_ASB))):
        w_raw = -_softplus(-(w0 + z_w[dr])) - 0.5
        put(d_lw, -jnp.exp(w_raw))
        a_sig = _sigmoid(a0 + z_a[dr])
        kd = k_c * (1.0 + (a_sig - 1.0) * kaw)
        put(d_kd, kd)
        put(d_as, a_sig)
        bonus = bonus + _dot(r_c * kd * rkw, ones_h) * v_c
    put(F_BON, bonus)

    rq = p_ret[:, 0:BR_W]
    rk = p_ret[:, BR_W:2 * BR_W] * HEAD_DIM ** -0.5
    if use_rope:
        cosf = cos_ref[...]
        sins = sin_ref[...]
        low = (_iota((tb, BR_W), 1) % HEAD_DIM) < (HEAD_DIM // 2)

        def rope(t):
            partner = jnp.where(low, pltpu.roll(t, BR_W - HEAD_DIM // 2, 1),
                                pltpu.roll(t, HEAD_DIM // 2, 1))
            return t * cosf + partner * sins

        rq = rope(rq)
        rk = rope(rk)
    put(F_RQ, rq)
    put(F_RK, rk)
    put(F_RV, p_ret[:, 2 * BR_W:3 * BR_W])
    put(F_RG, p_ret[:, 3 * BR_W:4 * BR_W])

    fb = p_fnet.astype(BF16)
    fc_ref[0] = jnp.dot(fb, dft_ref[0], preferred_element_type=F32)
    fs_ref[0] = jnp.dot(fb, dft_ref[1], preferred_element_type=F32)


def _pre(x, mods, lw, use_rope, rope_tabs, dft_c):
    B, T, D = x.shape
    tb = min(256, T)
    nt = T // tb
    hb8 = tb // HALO
    n_h = T // HALO
    const2 = lambda b, i: (0, 0)
    const3 = lambda b, i: (0, 0, 0)
    kern = functools.partial(_pre_kernel, use_rope, tb, nt)
    return pl.pallas_call(
        kern,
        grid=(B, nt),
        in_specs=[
            pl.BlockSpec((1, tb, D), lambda b, i: (b, i, 0)),
            pl.BlockSpec((1, HALO, D), lambda b, i: (b, jnp.maximum(i * hb8 - 1, 0), 0)),
            pl.BlockSpec((1, HALO, D), lambda b, i: (b, jnp.minimum((i + 1) * hb8, n_h - 1), 0)),
            pl.BlockSpec((1, 1, 6 * D), lambda b, i: (b, 0, 0)),
            pl.BlockSpec((1, D), const2),
            pl.BlockSpec((D, P_IN), const2),
            pl.BlockSpec((D, LR_COLS), const2),
            pl.BlockSpec((7, 256, BR_W), const3),
            pl.BlockSpec((16, BR_W), const2),
            pl.BlockSpec((3, 3 * BR_W), const2),
            pl.BlockSpec((2, BR_W, BR_W), const3),
            pl.BlockSpec((tb, BR_W), lambda b, i: (i, 0)),
            pl.BlockSpec((tb, BR_W), lambda b, i: (i, 0)),
        ],
        out_specs=[
            pl.BlockSpec((1, tb, N_FEAT * BR_W), lambda b, i: (b, i, 0)),
            pl.BlockSpec((1, tb, D), lambda b, i: (b, i, 0)),
            pl.BlockSpec((1, tb, BR_W), lambda b, i: (b, i, 0)),
            pl.BlockSpec((1, tb, BR_W), lambda b, i: (b, i, 0)),
        ],
        out_shape=[
            jax.ShapeDtypeStruct((B, T, N_FEAT * BR_W), F32),
            jax.ShapeDtypeStruct((B, T, D), BF16),
            jax.ShapeDtypeStruct((B, T, BR_W), F32),
            jax.ShapeDtypeStruct((B, T, BR_W), F32),
        ],
        scratch_shapes=[
            pltpu.VMEM((tb + 2 * HALO, LR_COLS), F32),
            pltpu.VMEM((tb + 2 * HALO, 3 * BR_W), F32),
        ],
        compiler_params=_cparams(("parallel", "arbitrary")),
        name="pre",
    )(x, x, x, mods, lw["g1"], lw["w_in"], lw["w_lr"],
      lw["w2"], lw["vec"], lw["rwkv_conv"], dft_c, rope_tabs[0], rope_tabs[1])


def _cumsum_chunk(x, rev):
    c = x.shape[0]
    tri = jnp.where(_tri(c, rev), 1.0, 0.0).astype(BF16)
    return _dot_exact_lhs(tri, x)


def _gla_chunks(chains):
    c = chains[0][0].shape[0]
    nblk = c // GLA_SUB
    rowid = _iota((c, 1), 0)
    ones_h = _head_ones()
    same = _same_head(HEADS * GLA_SUB, GLA_SUB)
    bd = (_iota((BR_W, BR_W), 0) // HEAD_DIM) == (_iota((BR_W, BR_W), 1) // HEAD_DIM)
    cums = [_cumsum_chunk(ch[3], ch[5]) for ch in chains]

    offd, inter, st_new = [], [], []
    for (q, k, v, la, st, rev), cum in zip(chains, cums):
        cum_end = cum[0:1] if rev else cum[c - 1:c]
        parts = []
        for blk in range(nblk):
            base = blk * GLA_SUB
            if rev:
                k_lo, k_hi, ref = base + GLA_SUB, c, base + GLA_SUB
            else:
                k_lo, k_hi, ref = 0, base, base - 1
            if k_hi <= k_lo:
                parts.append(None)
                continue
            qt = q[base:base + GLA_SUB] * jnp.exp(cum[base:base + GLA_SUB] - cum[ref:ref + 1])
            kt = k[k_lo:k_hi] * jnp.exp(cum[ref:ref + 1] - cum[k_lo:k_hi])
            sc = _dot_nt(_stack_heads(qt), kt)
            ov = jnp.where(same, _dot(sc, v[k_lo:k_hi]), 0.0)
            parts.append(_unstack_sum(ov, GLA_SUB))
        offd.append(parts)
        inter.append(_dot_nt(q * jnp.exp(cum), st))
        st_new.append(st * jnp.exp(cum_end)
                      + jnp.where(bd, _dot_tn(v, k * jnp.exp(cum_end - cum)), 0.0))

    sub = _iota((8, BR_W), 0)
    bias = {rev: [jnp.where((sub <= jj) if rev else (sub >= jj), 0.0, NEG_BIG) for jj in range(8)]
            for rev in {ch[5] for ch in chains}}
    reds, spans = [], []
    for (q, k, v, la, st, rev), cum in zip(chains, cums):
        cum2 = cum * LOG2E
        pieces, sp = [], []
        for j in range(c):
            base = (j // GLA_SUB) * GLA_SUB
            own = (j // 8) * 8
            others = range(base, own, 8) if rev else range(own + 8, base + GLA_SUB, 8)
            for g0 in (own, *others):
                arg = cum2[g0:g0 + 8] - cum2[j:j + 1]
                if g0 == own:
                    arg = arg + bias[rev][j - own]
                pieces.append(q[g0:g0 + 8] * (k[j:j + 1] * jnp.exp2(arg)))
                sp.append((j, g0))
        reds.append(jnp.dot(jnp.concatenate(pieces, axis=0).astype(BF16), ones_h,
                            preferred_element_type=F32))
        spans.append(sp)

    outs = []
    for ci, (q, k, v, la, st, rev) in enumerate(chains):
        groups = [None] * (c // 8)
        for idx, (j, g0) in enumerate(spans[ci]):
            piece = reds[ci][idx * 8:idx * 8 + 8] * v[j:j + 1]
            gi = g0 // 8
            groups[gi] = piece if groups[gi] is None else groups[gi] + piece
        blocks = []
        for blk in range(nblk):
            o_blk = jnp.concatenate(groups[blk * GLA_SUB // 8:(blk + 1) * GLA_SUB // 8], axis=0)
            if offd[ci][blk] is not None:
                o_blk = o_blk + offd[ci][blk]
            blocks.append(o_blk)
        outs.append((jnp.concatenate(blocks, axis=0) + inter[ci], st_new[ci]))
    return outs


def _ret_chunks(chains):
    bd = (_iota((BR_W, BR_W), 0) // HEAD_DIM) == (_iota((BR_W, BR_W), 1) // HEAD_DIM)
    sc = [_dot_nt(q, _stack_heads(k)) * dmat for q, k, v, s, dmat, _, _, _ in chains]
    inter = [_dot(q * qdec, s) for q, k, v, s, _, qdec, _, _ in chains]
    upd = [_dot_tn(k * kdec, v) for q, k, v, s, _, _, kdec, _ in chains]
    outs = []
    for ch, sc_i, inter_i, upd_i in zip(chains, sc, inter, upd):
        q, k, v, s, _, _, _, cdec = ch
        o = _dot(sc_i, _stack_heads(v)) + inter_i
        outs.append((o, s * cdec + jnp.where(bd, upd_i, 0.0)))
    return outs


def _tri_wide(c, rev, strict):
    t = _iota((c, HEADS * c), 0)
    s = _iota((c, HEADS * c), 1) % c
    if rev:
        return (s > t) if strict else (s >= t)
    return (s < t) if strict else (s <= t)


def _rwkv_chunks(chains):
    c = chains[0][0].shape[0]
    n = HEADS * c
    bf = lambda t: t.astype(BF16)
    stack = lambda t: bf(_stack_heads(t))
    masks = {rev: (_tri_wide(c, rev, True), _tri_wide(c, rev, False))
             for rev in {ch[7] for ch in chains}}
    eye = jnp.where(_iota((c, n), 1) % c == _iota((c, n), 0), 1.0, 0.0)
    bd = (_iota((BR_W, BR_W), 0) // HEAD_DIM) == (_iota((BR_W, BR_W), 1) // HEAD_DIM)
    cums = [_cumsum_chunk(ch[5], ch[7]) for ch in chains]
    st = []
    for (r, v, kk, kd, asig, lw, s, rev), cum in zip(chains, cums):
        cum_end = cum[0:1] if rev else cum[c - 1:c]
        einv = jnp.exp(-cum)
        ehat = jnp.exp(cum_end - cum)
        bvec = kk * asig
        st.append(dict(
            a=bf(-kk * jnp.exp(cum - lw)), r=bf(r * jnp.exp(cum)),
            b_s=stack(bvec * einv), k_s=stack(kd * einv), v=bf(v), v_s=stack(v),
            bh=bf(bvec * ehat), kh=bf(kd * ehat), g_end=jnp.exp(cum_end), s=s, sb=bf(s), rev=rev))
    for d in st:
        d["m"] = _dot_nt(jnp.concatenate([d["a"], d["r"]], axis=0),
                         jnp.concatenate([d["b_s"], d["k_s"]], axis=0))
    for d in st:
        strict, incl = masks[d["rev"]]
        m = d.pop("m")
        d["p"] = jnp.where(strict, m[0:c, 0:n], 0.0)
        d["t"] = eye + d["p"]
        d["ps"] = stack(d["p"])
        d["m_ak"] = bf(jnp.where(strict, m[0:c, n:2 * n], 0.0))
        d["n_rb"] = bf(jnp.where(incl, m[c:2 * c, 0:n], 0.0))
        d["n_rk"] = bf(jnp.where(incl, m[c:2 * c, n:2 * n], 0.0))
    for d in st:
        d["w"] = _dot_nt(d["a"], d["sb"]) + _dot(d["m_ak"], d["v_s"])
        d["y0"] = _dot_nt(d["r"], d["sb"]) + _dot(d["n_rk"], d["v_s"])
    for _ in range(int(math.log2(c)) - 1):
        for d in st:
            d["p"] = _dot(d["p"], d["ps"])
            d["ps"] = stack(d["p"])
        for d in st:
            d["t"] = d["t"] + _dot(d["t"], d["ps"])
    for d in st:
        d["u"] = _dot(d["t"], stack(d["w"]))
    outs = []
    for d in st:
        y = d["y0"] + _dot(d["n_rb"], stack(d["u"]))
        upd = _dot_tn(jnp.concatenate([bf(d["u"]), d["v"]], axis=0),
                      jnp.concatenate([d["bh"], d["kh"]], axis=0))
        outs.append((y, d["s"] * d["g_end"] + jnp.where(bd, upd, 0.0)))
    return outs


def _scan_kernel(kind, n_in, n_tab, nc, bb, *refs):
    fwd = refs[0:n_in]
    bwd = refs[n_in:2 * n_in]
    tabs = refs[2 * n_in:2 * n_in + n_tab]
    s0_ref = refs[2 * n_in + n_tab]
    of_ref, ob_ref, sf_ref, st_ref = refs[2 * n_in + n_tab + 1:]
    i = pl.program_id(1)

    @pl.when(i == 0)
    def _():
        st_ref[...] = s0_ref[...]

    chains = [(dr, b) for b in range(bb) for dr in range(2)]
    ins_of = lambda dr, b: [t[b] for t in (fwd, bwd)[dr]]
    if kind == "rwkv":
        res = _rwkv_chunks([(*ins_of(dr, b), st_ref[dr, b], dr == 1) for dr, b in chains])
    elif kind == "gla":
        res = _gla_chunks([(*ins_of(dr, b), st_ref[dr, b], dr == 1) for dr, b in chains])
    else:
        res = _ret_chunks([(*ins_of(dr, b), st_ref[dr, b], *(t[dr] for t in tabs))
                           for dr, b in chains])
    for (dr, b), (o, s_new) in zip(chains, res):
        (of_ref, ob_ref)[dr][b] = o
        st_ref[dr, b] = s_new

    @pl.when(i == nc - 1)
    def _():
        sf_ref[...] = st_ref[...]


_SCAN_ROWS = {"gla": 4, "ret": 4, "rwkv": 8}


def _scan(kind, feats, cols_f, cols_b, s0, tabs=()):
    B, T, _ = feats.shape
    c = min(SCAN_CHUNK[kind], T)
    nc = T // c
    bb = math.gcd(_SCAN_ROWS[kind], B)
    n_in = len(cols_f)
    in_specs = []
    for j in cols_f:
        in_specs.append(pl.BlockSpec((bb, c, BR_W), lambda b, i, j=j: (b, i, j)))
    for j in cols_b:
        in_specs.append(pl.BlockSpec((bb, c, BR_W), lambda b, i, j=j: (b, nc - 1 - i, j)))
    for t in tabs:
        in_specs.append(pl.BlockSpec(t.shape, lambda b, i, nd=t.ndim: (0,) * nd))
    in_specs.append(pl.BlockSpec((2, bb, BR_W, BR_W), lambda b, i: (0, b, 0, 0)))
    kern = functools.partial(_scan_kernel, kind, n_in, len(tabs), nc, bb)
    return pl.pallas_call(
        kern,
        grid=(B // bb, nc),
        in_specs=in_specs,
        out_specs=[
            pl.BlockSpec((bb, c, BR_W), lambda b, i: (b, i, 0)),
            pl.BlockSpec((bb, c, BR_W), lambda b, i: (b, nc - 1 - i, 0)),
            pl.BlockSpec((2, bb, BR_W, BR_W), lambda b, i: (0, b, 0, 0)),
        ],
        out_shape=[
            jax.ShapeDtypeStruct((B, T, BR_W), F32),
            jax.ShapeDtypeStruct((B, T, BR_W), F32),
            jax.ShapeDtypeStruct((2, B, BR_W, BR_W), F32),
        ],
        scratch_shapes=[pltpu.VMEM((2, bb, BR_W, BR_W), F32)],
        compiler_params=_cparams(("parallel", "arbitrary")),
        name=kind,
    )(*([feats] * (2 * n_in)), *tabs, s0)


def _ret_tables(c):
    pos = np.arange(c, dtype=np.float64)
    lane_head = np.arange(BR_W) // HEAD_DIM
    dmats, qd, kd, cd = [], [], [], []
    for dr in range(2):
        expo = -5.0 - np.arange(HEADS, dtype=np.float64)
        if dr == 1:
            expo = expo[::-1]
        log_g = np.log1p(-np.exp2(expo))
        if dr == 0:
            rel = pos[:, None] - pos[None, :]
            qpow = pos + 1.0
            kpow = c - 1.0 - pos
        else:
            rel = pos[None, :] - pos[:, None]
            qpow = c - pos
            kpow = pos
        tri = rel >= 0
        dm = np.where(tri[None], np.exp(np.where(tri, rel, 0.0)[None] * log_g[:, None, None]), 0.0)
        dmats.append(np.concatenate([dm[h] for h in range(HEADS)], axis=1))
        qd.append(np.exp(qpow[:, None] * log_g[lane_head][None, :]))
        kd.append(np.exp(kpow[:, None] * log_g[lane_head][None, :]))
        cd.append(np.exp(c * log_g[lane_head])[None, :])
    f = lambda xs: jnp.asarray(np.stack(xs), dtype=F32)
    return f(dmats), f(qd), f(kd), f(cd)


def _fnet_dense_kernel(t_len, cs_ref, g1_ref, g2_ref, o_ref):
    o_ref[0] = (jnp.dot(cs_ref[:, 0:t_len], g1_ref[0].astype(BF16), preferred_element_type=F32)
                + jnp.dot(cs_ref[:, t_len:2 * t_len], g2_ref[0].astype(BF16),
                          preferred_element_type=F32))


def _fnet_dense(g1, g2, cs):
    B, T, _ = g1.shape
    tb = min(512, T)
    return pl.pallas_call(
        functools.partial(_fnet_dense_kernel, T),
        grid=(T // tb, B),
        in_specs=[
            pl.BlockSpec((tb, 2 * T), lambda i, b: (i, 0)),
            pl.BlockSpec((1, T, BR_W), lambda i, b: (b, 0, 0)),
            pl.BlockSpec((1, T, BR_W), lambda i, b: (b, 0, 0)),
        ],
        out_specs=pl.BlockSpec((1, tb, BR_W), lambda i, b: (b, i, 0)),
        out_shape=jax.ShapeDtypeStruct((B, T, BR_W), F32),
        compiler_params=_cparams(("arbitrary", "arbitrary")),
        name="fnet",
    )(cs, g1, g2)


FFT_R = 8


def _fft_kernel(n, k1_ref, k2_ref, tc_ref, ts_ref, g1_ref, g2_ref, o_ref, zr_ref, zi_ref):
    m = n * FFT_R
    for j in range(n // FFT_R):
        sl = slice(j * FFT_R, (j + 1) * FFT_R)
        x = jnp.concatenate([g1_ref[0, :, sl, :].reshape(m, BR_W),
                             g2_ref[0, :, sl, :].reshape(m, BR_W)], axis=1).astype(BF16)
        y = jnp.dot(k1_ref[...], x, preferred_element_type=F32)
        yr = y[0:m, 0:BR_W] - y[m:2 * m, BR_W:2 * BR_W]
        yi = -y[0:m, BR_W:2 * BR_W] - y[m:2 * m, 0:BR_W]
        tc = tc_ref[:, sl, :].reshape(m, BR_W)
        ts = ts_ref[:, sl, :].reshape(m, BR_W)
        zr_ref[:, sl, :] = (tc * yr + ts * yi).reshape(n, FFT_R, BR_W)
        zi_ref[:, sl, :] = (tc * yi - ts * yr).reshape(n, FFT_R, BR_W)
    for j in range(n // FFT_R):
        sl = slice(j * FFT_R, (j + 1) * FFT_R)
        z = jnp.concatenate([zr_ref[sl].reshape(m, BR_W), zi_ref[sl].reshape(m, BR_W)],
                            axis=0).astype(BF16)
        o = jnp.dot(k2_ref[...], z, preferred_element_type=F32)
        o_ref[0, :, sl, :] = o.reshape(n, FFT_R, BR_W)


def _fnet_fft(g1, g2):
    B, T, _ = g1.shape
    n = int(round(math.sqrt(T)))
    assert n * n == T and n % FFT_R == 0
    m = n * FFT_R
    idx = np.arange(n)
    a_n = 2.0 * np.pi * ((idx[:, None] * idx[None, :]) % n) / n
    eye = np.eye(FFT_R)
    k1 = np.concatenate([np.einsum("ap,ij->aipj", f, eye).reshape(m, m)
                         for f in (np.cos(a_n), np.sin(a_n))], axis=0)
    k2 = np.concatenate([np.einsum("pr,ij->pijr", f, eye).reshape(m, m)
                         for f in (np.cos(a_n), np.sin(a_n))], axis=1)
    a_t = 2.0 * np.pi * (idx[:, None] * idx[None, :]) / T
    tc = jnp.broadcast_to(jnp.asarray(np.cos(a_t), dtype=F32)[:, :, None], (n, n, BR_W))
    ts = jnp.broadcast_to(jnp.asarray(np.sin(a_t), dtype=F32)[:, :, None], (n, n, BR_W))
    const2 = lambda b: (0, 0)
    const3 = lambda b: (0, 0, 0)
    one = pl.Buffered(1)
    o = pl.pallas_call(
        functools.partial(_fft_kernel, n),
        grid=(B,),
        in_specs=[
            pl.BlockSpec((2 * m, m), const2, pipeline_mode=one),
            pl.BlockSpec((m, 2 * m), const2, pipeline_mode=one),
            pl.BlockSpec((n, n, BR_W), const3, pipeline_mode=one),
            pl.BlockSpec((n, n, BR_W), const3, pipeline_mode=one),
            pl.BlockSpec((1, n, n, BR_W), lambda b: (b, 0, 0, 0)),
            pl.BlockSpec((1, n, n, BR_W), lambda b: (b, 0, 0, 0)),
        ],
        out_specs=pl.BlockSpec((1, n, n, BR_W), lambda b: (b, 0, 0, 0)),
        out_shape=jax.ShapeDtypeStruct((B, n, n, BR_W), F32),
        scratch_shapes=[pltpu.VMEM((n, n, BR_W), F32), pltpu.VMEM((n, n, BR_W), F32)],
        compiler_params=_cparams(("arbitrary",)),
        name="fft",
    )(jnp.asarray(k1, dtype=BF16), jnp.asarray(k2, dtype=BF16), tc, ts,
      g1.reshape(B, n, n, BR_W), g2.reshape(B, n, n, BR_W))
    return o.reshape(B, T, BR_W)


def _time_dft_table(T):
    t = np.arange(T)
    ang = ((t[:, None] * t[None, :]) % T) * (2.0 * np.pi / T)
    return jnp.asarray(np.concatenate([np.cos(ang), -np.sin(ang)], axis=1), dtype=BF16)


def _channel_dft_tables(T):
    ch = np.arange(HEAD_DIM)
    a64 = 2.0 * np.pi * ((ch[:, None] * ch[None, :]) % HEAD_DIM) / HEAD_DIM
    scale = (T * HEAD_DIM) ** -0.5
    eye = np.eye(HEADS)
    cbd = np.kron(eye, np.cos(a64)) * scale
    sbd = np.kron(eye, np.sin(a64)) * scale
    return jnp.asarray(np.stack([cbd, sbd]), dtype=BF16)


def _head_norm(o, ones_h, center):
    if center:
        o = o - _dot(o, ones_h) * (1.0 / HEAD_DIM)
    var = _dot(o * o, ones_h) * (1.0 / HEAD_DIM)
    return o * lax.rsqrt(var + EPS)


def _merge_kernel(x_ref, hb_ref, mod_ref, gg_ref, rg_ref, wg_ref, bon_ref,
                  ogf_ref, ogb_ref, orf_ref, orb_ref, oyf_ref, oyb_ref, fn_ref,
                  gn_ref, wgate_ref, bgate_ref, wbr_ref, wout_ref, o_ref):
    ones_h = _head_ones()
    gn = gn_ref[...]
    gla = _head_norm(ogf_ref[0] + ogb_ref[0], ones_h, False) * gn[0:1] * _silu(gg_ref[0])
    ret = _head_norm(orf_ref[0] + orb_ref[0], ones_h, True) * gn[1:2] * _silu(rg_ref[0])
    rwkv = (_head_norm(oyf_ref[0] + oyb_ref[0], ones_h, True) * gn[2:3] + bon_ref[0]) * wg_ref[0]
    outs = (gla, ret, rwkv, fn_ref[0])
    hb = hb_ref[0]
    z = None
    for br in range(4):
        gate = _sigmoid(jnp.dot(hb, wgate_ref[br], preferred_element_type=F32) + bgate_ref[br])
        term = gate * _dot(outs[br], wbr_ref[br])
        z = term if z is None else z + term
    y = _dot(z, wout_ref[...])
    gate1 = mod_ref[0][:, 2 * D_MODEL:3 * D_MODEL]
    o_ref[0] = x_ref[0] + gate1 * y


def _merge(x, hb, mods, feats, scans, fnet_out, lw):
    B, T, D = x.shape
    tb = min(256, T)
    tok = lambda w: pl.BlockSpec((1, tb, w), lambda b, i: (b, i, 0))
    fcol = lambda j: pl.BlockSpec((1, tb, BR_W), lambda b, i, j=j: (b, i, j))
    const2 = lambda b, i: (0, 0)
    const3 = lambda b, i: (0, 0, 0)
    return pl.pallas_call(
        _merge_kernel,
        grid=(B, T // tb),
        in_specs=[
            tok(D), tok(D),
            pl.BlockSpec((1, 1, 6 * D), lambda b, i: (b, 0, 0)),
            fcol(F_GG), fcol(F_RG), fcol(F_WG), fcol(F_BON),
            tok(BR_W), tok(BR_W), tok(BR_W), tok(BR_W), tok(BR_W), tok(BR_W), tok(BR_W),
            pl.BlockSpec((3, BR_W), const2),
            pl.BlockSpec((4, D, D), const3),
            pl.BlockSpec((4, 1, D), const3),
            pl.BlockSpec((4, BR_W, D), const3),
            pl.BlockSpec((D, D), const2),
        ],
        out_specs=tok(D),
        out_shape=jax.ShapeDtypeStruct((B, T, D), F32),
        compiler_params=_cparams(("parallel", "arbitrary")),
        name="merge",
    )(x, hb, mods, feats, feats, feats, feats, *scans, fnet_out,
      lw["gn"], lw["w_gate"], lw["b_gate"], lw["w_br"], lw["w_out"])


FFN_COL_CHUNKS = (1024, 1024, 768)


def _ffn_kernel(final, tb, nt,
                x_ref, xp_ref, xn_ref, mod_ref, g2_ref, up_ref, cw_ref, cb_ref,
                down_ref, gfin_ref, o_ref, act_ref):
    i = pl.program_id(1)
    n_ext = tb + 2 * HALO
    x_ext = jnp.concatenate([xp_ref[0], x_ref[0], xn_ref[0]], axis=0)
    mod = mod_ref[0]
    shift = mod[:, 3 * D_MODEL:4 * D_MODEL]
    scale = mod[:, 4 * D_MODEL:5 * D_MODEL]
    h2 = (_rms_rows(x_ext) * g2_ref[...]) * (1.0 + scale) + shift
    h2 = h2 * _halo_keep(n_ext, tb, i, nt)
    h2_ext = h2.astype(BF16)
    h2_mid = h2[HALO:HALO + tb].astype(BF16)
    cw = cw_ref[...]
    cb = cb_ref[...]
    lo = 0
    for width in FFN_COL_CHUNKS:
        a_ext = jnp.dot(h2_ext, up_ref[:, lo:lo + width], preferred_element_type=F32)
        u = jnp.dot(h2_mid, up_ref[:, D_FF + lo:D_FF + lo + width], preferred_element_type=F32)
        a = (a_ext[HALO - 1:HALO - 1 + tb] * cw[0:1, lo:lo + width]
             + a_ext[HALO:HALO + tb] * cw[1:2, lo:lo + width]
             + a_ext[HALO + 1:HALO + 1 + tb] * cw[2:3, lo:lo + width] + cb[:, lo:lo + width])
        act_ref[:, lo:lo + width] = (_silu(a) * u).astype(BF16)
        lo += width
    y = jnp.dot(act_ref[...], down_ref[...], preferred_element_type=F32)
    gate2 = mod[:, 5 * D_MODEL:6 * D_MODEL]
    res = x_ref[0] + gate2 * y
    if final:
        res = _rms_rows(res) * gfin_ref[...]
    o_ref[0] = res


def _ffn(x, mods, lw, g_final, final):
    B, T, D = x.shape
    tb = min(512, T)
    nt = T // tb
    hb8 = tb // HALO
    n_h = T // HALO
    const2 = lambda b, i: (0, 0)
    resident = lambda shape: pl.BlockSpec(shape, const2, pipeline_mode=pl.Buffered(1))
    kern = functools.partial(_ffn_kernel, final, tb, nt)
    return pl.pallas_call(
        kern,
        grid=(B, nt),
        in_specs=[
            pl.BlockSpec((1, tb, D), lambda b, i: (b, i, 0)),
            pl.BlockSpec((1, HALO, D), lambda b, i: (b, jnp.maximum(i * hb8 - 1, 0), 0)),
            pl.BlockSpec((1, HALO, D), lambda b, i: (b, jnp.minimum((i + 1) * hb8, n_h - 1), 0)),
            pl.BlockSpec((1, 1, 6 * D), lambda b, i: (b, 0, 0)),
            pl.BlockSpec((1, D), const2),
            resident((D, 2 * D_FF)),
            pl.BlockSpec((3, D_FF), const2),
            pl.BlockSpec((1, D_FF), const2),
            resident((D_FF, D)),
            pl.BlockSpec((1, D), const2),
        ],
        out_specs=pl.BlockSpec((1, tb, D), lambda b, i: (b, i, 0)),
        out_shape=jax.ShapeDtypeStruct((B, T, D), F32),
        scratch_shapes=[pltpu.VMEM((tb, D_FF), BF16)],
        compiler_params=_cparams(("parallel", "arbitrary")),
        name="ffn",
    )(x, x, x, mods, lw["g2"], lw["ffn_up"], lw["ffn_conv"], lw["ffn_conv_b"],
      lw["ffn_down"], g_final)


def _rope_tables(T):
    rows = T // GRID_W
    row = jnp.repeat(jnp.arange(rows, dtype=F32), GRID_W)
    colp = jnp.tile(jnp.arange(GRID_W, dtype=F32), rows)
    n_freq = HEAD_DIM // 4
    inv = ROPE_BASE ** (-jnp.arange(n_freq, dtype=F32) / n_freq)
    ang = jnp.concatenate([row[:, None] * inv, colp[:, None] * inv], axis=-1)
    cos, sin = jnp.cos(ang), jnp.sin(ang)
    cos_h = jnp.concatenate([cos, cos], axis=-1)
    sin_h = jnp.concatenate([-sin, sin], axis=-1)
    return jnp.tile(cos_h, (1, HEADS)), jnp.tile(sin_h, (1, HEADS))


def _pad_rows(w, lo, n):
    return jnp.zeros((n, w.shape[1]), w.dtype).at[lo:lo + w.shape[0]].set(w)


def _layer_weights(l, p):
    D = D_MODEL
    bf = lambda t: t.astype(BF16)
    zcols = lambda w, n: jnp.concatenate([w, jnp.zeros((w.shape[0], n - w.shape[1]), w.dtype)], axis=1)
    w_h = zcols(jnp.concatenate([p["gla_wa1"][l, 0], p["gla_wa1"][l, 1]], axis=1), 128)
    w_xw = jnp.concatenate([p["rwkv_w1"][l, 0], p["rwkv_w1"][l, 1]], axis=1)
    w_xa = jnp.concatenate([p["rwkv_a1"][l, 0], p["rwkv_a1"][l, 1]], axis=1)
    w_xg = zcols(p["rwkv_g1"][l], 256)
    mu = p["rwkv_mu"][l]
    w_lr = jnp.concatenate([w_xg, mu[2][:, None] * w_xg, w_h, w_xw, w_xa,
                            mu[0][:, None] * w_xw, mu[1][:, None] * w_xa,
                            jnp.zeros((D, LR_COLS - LR_AS - 128), F32)], axis=1)
    w2 = jnp.stack([
        _pad_rows(p["gla_wa2"][l, 0], 0, 256), _pad_rows(p["gla_wa2"][l, 1], GLA_LR, 256),
        _pad_rows(p["rwkv_w2"][l, 0], 0, 256), _pad_rows(p["rwkv_w2"][l, 1], RWKV_LR, 256),
        _pad_rows(p["rwkv_a2"][l, 0], 0, 256), _pad_rows(p["rwkv_a2"][l, 1], RWKV_LR, 256),
        _pad_rows(p["rwkv_g2"][l], 0, 256),
    ])
    vec = jnp.stack([p["gla_ba"][l, 0], p["gla_ba"][l, 1], p["rwkv_w0"][l, 0], p["rwkv_w0"][l, 1],
                     p["rwkv_a0"][l, 0], p["rwkv_a0"][l, 1], p["rwkv_kk"][l], p["rwkv_ka"][l],
                     p["rwkv_rk"][l]])
    vec = jnp.concatenate([vec, jnp.zeros((16 - vec.shape[0], BR_W), F32)], axis=0)
    return {
        "g1": p["g_norm1"][l].reshape(1, D), "g2": p["g_norm2"][l].reshape(1, D),
        "w_in": bf(p["w_in"][l]), "w_lr": bf(w_lr), "w2": bf(w2), "vec": vec,
        "rwkv_conv": p["rwkv_conv"][l],
        "gn": jnp.stack([p["gla_gn"][l], p["ret_gn"][l], p["rwkv_gn"][l]]),
        "w_gate": bf(p["w_gate"][l]), "b_gate": p["b_gate"][l].reshape(4, 1, D),
        "w_br": bf(p["w_br"][l]), "w_out": bf(p["w_out"][l]),
        "ffn_up": bf(p["ffn_up"][l]), "ffn_conv": p["ffn_conv"][l],
        "ffn_conv_b": p["ffn_conv_b"][l].reshape(1, D_FF), "ffn_down": bf(p["ffn_down"][l]),
    }


_GLA_F = (F_GQ, F_GK, F_GV, F_LAF)
_GLA_B = (F_GQ, F_GK, F_GV, F_LAB)
_RET = (F_RQ, F_RK, F_RV)
_RWKV_F = (F_WR, F_WV, F_WKK, F_KDF, F_ASF, F_LWF)
_RWKV_B = (F_WR, F_WV, F_WKK, F_KDB, F_ASB, F_LWB)


def _mixers(feats_c, feats_l, ret_tabs_c, ret_tabs_l, need_ctx):
    B = feats_l.shape[0]
    zero = jnp.zeros((2, B, BR_W, BR_W), F32)
    out_c, out_l = [], []
    for kind, cf, cb, tabs in (("gla", _GLA_F, _GLA_B, None), ("ret", _RET, _RET, True),
                               ("rwkv", _RWKV_F, _RWKV_B, None)):
        tc = ret_tabs_c if tabs else ()
        tl = ret_tabs_l if tabs else ()
        of_c, ob_c, s_c = _scan(kind, feats_c, cf, cb, zero, tc)
        of_l, ob_l, _ = _scan(kind, feats_l, cf, cb, s_c, tl)
        out_c += [of_c, ob_c]
        out_l += [of_l, ob_l]
    return out_l, (out_c if need_ctx else None)


def kernel(x, c, ctx, c_ctx, w_ada, b_ada, g_norm1, g_norm2, w_in, gla_wa1, gla_wa2, gla_ba, gla_gn, ret_gn, rwkv_conv, rwkv_mu, rwkv_w0, rwkv_w1, rwkv_w2, rwkv_a0, rwkv_a1, rwkv_a2, rwkv_g1, rwkv_g2, rwkv_kk, rwkv_ka, rwkv_rk, rwkv_gn, w_gate, b_gate, w_br, w_out, ffn_up, ffn_conv, ffn_conv_b, ffn_down, g_final):
    p = dict(g_norm1=g_norm1, g_norm2=g_norm2, w_in=w_in, gla_wa1=gla_wa1, gla_wa2=gla_wa2,
             gla_ba=gla_ba, gla_gn=gla_gn, ret_gn=ret_gn, rwkv_conv=rwkv_conv, rwkv_mu=rwkv_mu,
             rwkv_w0=rwkv_w0, rwkv_w1=rwkv_w1, rwkv_w2=rwkv_w2, rwkv_a0=rwkv_a0,
             rwkv_a1=rwkv_a1, rwkv_a2=rwkv_a2, rwkv_g1=rwkv_g1, rwkv_g2=rwkv_g2,
             rwkv_kk=rwkv_kk, rwkv_ka=rwkv_ka, rwkv_rk=rwkv_rk, rwkv_gn=rwkv_gn,
             w_gate=w_gate, b_gate=b_gate, w_br=w_br, w_out=w_out, ffn_up=ffn_up,
             ffn_conv=ffn_conv, ffn_conv_b=ffn_conv_b, ffn_down=ffn_down)
    B, T, D = x.shape
    Tc = ctx.shape[1]
    depth = w_ada.shape[0]

    cmat = jnp.concatenate([c, c_ctx[None, :], jnp.zeros((16 - B - 1, D), F32)], axis=0)
    mod_all = _modulation(cmat, w_ada, b_ada)

    rope_l = _rope_tables(T)
    rope_c = (jnp.zeros((Tc, BR_W), F32), jnp.zeros((Tc, BR_W), F32))
    dft_l = _channel_dft_tables(T)
    dft_c = _channel_dft_tables(Tc)
    cs_c = _time_dft_table(Tc)
    ret_tabs_l = _ret_tables(min(SCAN_CHUNK["ret"], T))
    ret_tabs_c = _ret_tables(min(SCAN_CHUNK["ret"], Tc))
    g_fin = g_final.reshape(1, D)

    for l in range(depth):
        last = l == depth - 1
        lw = _layer_weights(l, p)
        mods_l = mod_all[l, 0:B].reshape(B, 1, 6 * D)
        mods_c = jnp.broadcast_to(mod_all[l, B:B + 1].reshape(1, 1, 6 * D), (B, 1, 6 * D))

        feats_l, hb_l, g1_l, g2_l = _pre(x, mods_l, lw, True, rope_l, dft_l)
        feats_c, hb_c, g1_c, g2_c = _pre(ctx, mods_c, lw, False, rope_c, dft_c)
        scans_l, scans_c = _mixers(feats_c, feats_l, ret_tabs_c, ret_tabs_l, not last)

        fn_l = _fnet_fft(g1_l, g2_l)
        x = _merge(x, hb_l, mods_l, feats_l, scans_l, fn_l, lw)
        x = _ffn(x, mods_l, lw, g_fin, last)
        if not last:
            fn_c = _fnet_dense(g1_c, g2_c, cs_c)
            ctx = _merge(ctx, hb_c, mods_c, feats_c, scans_c, fn_c, lw)
            ctx = _ffn(ctx, mods_c, lw, g_fin, False)
    return x
```

```python
import functools
import math

import numpy as np
import jax
import jax.numpy as jnp
from jax import lax
from jax.experimental import pallas as pl
from jax.experimental.pallas import tpu as pltpu

F32 = jnp.float32
BF16 = jnp.bfloat16

D_MODEL = 1024
BR_W = 256
HEAD_DIM = 64
HEADS = 4
N_PARTS = 12
P_IN = N_PARTS * BR_W
GRID_W = 64
GLA_LR = 16
GLA_NORMALIZER = 16.0
RWKV_LR = 64
RWKV_G_LR = 160
D_FF = 2816
ROPE_BASE = 10000.0
EPS = 1e-6

SCAN_CHUNK = {"gla": 64, "ret": 128, "rwkv": 64}
GLA_SUB = 16
HALO = 8
NEG_BIG = -1e30
LOG2E = 1.4426950408889634

(F_GQ, F_GK, F_GV, F_GG, F_RQ, F_RK, F_RV, F_RG, F_WR, F_WV, F_WKK, F_WG, F_BON,
 F_LAF, F_LAB, F_LWF, F_LWB, F_KDF, F_KDB, F_ASF, F_ASB) = range(21)
N_FEAT = 21

LR_G, LR_GS, LR_H, LR_W, LR_A, LR_WS, LR_AS = 0, 256, 512, 640, 768, 896, 1024
LR_COLS = 1280

VMEM_LIMIT = 56 * 1024 * 1024


def _cparams(sem):
    return pltpu.CompilerParams(dimension_semantics=sem, vmem_limit_bytes=VMEM_LIMIT)


def _dot(a, b):
    return jnp.dot(a.astype(BF16), b.astype(BF16), preferred_element_type=F32)


def _dot_nt(a, b):
    return lax.dot_general(a.astype(BF16), b.astype(BF16), (((1,), (1,)), ((), ())),
                           preferred_element_type=F32)


def _dot_tn(a, b):
    return lax.dot_general(a.astype(BF16), b.astype(BF16), (((0,), (0,)), ((), ())),
                           preferred_element_type=F32)


def _split3(x):
    hi = x.astype(BF16)
    r1 = x - hi.astype(F32)
    mid = r1.astype(BF16)
    lo = (r1 - mid.astype(F32)).astype(BF16)
    return hi, mid, lo


def _dot_exact_lhs(a_bf16, x):
    hi, mid, lo = _split3(x)
    d = lambda t: jnp.dot(a_bf16, t, preferred_element_type=F32)
    return d(hi) + d(mid) + d(lo)


def _sigmoid(x):
    return 1.0 / (1.0 + jnp.exp(-x))


def _silu(x):
    return x * _sigmoid(x)


def _softplus(x):
    return jnp.maximum(x, 0.0) + jnp.log(1.0 + jnp.exp(-jnp.abs(x)))


def _iota(shape, dim):
    return lax.broadcasted_iota(jnp.int32, shape, dim)


def _head_ones():
    r = _iota((BR_W, BR_W), 0) // HEAD_DIM
    c = _iota((BR_W, BR_W), 1) // HEAD_DIM
    return jnp.where(r == c, 1.0, 0.0).astype(BF16)


def _stack_heads(x):
    c, w = x.shape
    xs = jnp.concatenate([x] * HEADS, axis=0)
    keep = (_iota((HEADS * c, w), 0) // c) == (_iota((HEADS * c, w), 1) // (w // HEADS))
    return jnp.where(keep, xs, 0.0)


def _tri(c, rev, strict=False):
    r = _iota((c, c), 0)
    s = _iota((c, c), 1)
    if rev:
        return (r < s) if strict else (r <= s)
    return (r > s) if strict else (r >= s)


def _halo_keep(n_ext, tb, i, nt):
    row = _iota((n_ext, 1), 0)
    first = jnp.where(i > 0, 1.0, 0.0)
    last = jnp.where(i < nt - 1, 1.0, 0.0)
    return jnp.where(row < HALO, first, jnp.where(row >= tb + HALO, last, 1.0))


def _rms_rows(x):
    return x * lax.rsqrt(jnp.mean(x * x, axis=-1, keepdims=True) + EPS)


def _mod_kernel(c_ref, w_ref, b_ref, o_ref):
    s = _silu(c_ref[...])
    o_ref[0] = _dot(s, w_ref[0]) + b_ref[0]


def _modulation(cmat, w_ada, b_ada):
    L = w_ada.shape[0]
    nblk = 1536
    return pl.pallas_call(
        _mod_kernel,
        grid=(L, 6 * D_MODEL // nblk),
        in_specs=[
            pl.BlockSpec((16, D_MODEL), lambda l, j: (0, 0)),
            pl.BlockSpec((1, D_MODEL, nblk), lambda l, j: (l, 0, j)),
            pl.BlockSpec((1, 1, nblk), lambda l, j: (l, 0, j)),
        ],
        out_specs=pl.BlockSpec((1, 16, nblk), lambda l, j: (l, 0, j)),
        out_shape=jax.ShapeDtypeStruct((L, 16, 6 * D_MODEL), F32),
        compiler_params=_cparams(("arbitrary", "arbitrary")),
        name="mod",
    )(cmat, w_ada, b_ada.reshape(L, 1, 6 * D_MODEL))


def _pre_kernel(use_rope, tb, nt,
                x_ref, xp_ref, xn_ref, mod_ref, g1_ref, win_ref, wlr_ref,
                w2_ref, vec_ref, conv_ref, dft_ref, cos_ref, sin_ref,
                f_ref, hb_ref, fc_ref, fs_ref, lr_ref, ps_ref):
    i = pl.program_id(1)
    n_ext = tb + 2 * HALO
    x_ext = jnp.concatenate([xp_ref[0], x_ref[0], xn_ref[0]], axis=0)
    mod = mod_ref[0]
    shift = mod[:, 0:D_MODEL]
    scale = mod[:, D_MODEL:2 * D_MODEL]
    h_ext = (_rms_rows(x_ext) * g1_ref[...]) * (1.0 + scale) + shift
    h_ext = h_ext * _halo_keep(n_ext, tb, i, nt)
    hb_ext = h_ext.astype(BF16)
    hb = h_ext[HALO:HALO + tb].astype(BF16)
    hb_ref[0] = hb
    lr_ref[...] = jnp.dot(hb_ext, wlr_ref[...], preferred_element_type=F32)
    ps_ref[...] = jnp.dot(hb_ext, win_ref[:, 8 * BR_W:11 * BR_W], preferred_element_type=F32)

    def mid(lo, hi):
        return lr_ref[HALO:HALO + tb, lo:hi]

    def shifted(lo, hi):
        return (0.5 * (lr_ref[HALO - 1:HALO - 1 + tb, lo:hi] + lr_ref[HALO + 1:HALO + 1 + tb, lo:hi])
                - lr_ref[HALO:HALO + tb, lo:hi])

    vec = vec_ref[...]
    ba_f, ba_b, w0_f, w0_b = vec[0:1], vec[1:2], vec[2:3], vec[3:4]
    a0_f, a0_b, kkw, kaw, rkw = vec[4:5], vec[5:6], vec[6:7], vec[7:8], vec[8:9]

    def put(j, val):
        f_ref[0, :, j * BR_W:(j + 1) * BR_W] = val

    zh = mid(LR_H, LR_H + 128)
    tw = jnp.tanh(mid(LR_W, LR_W + 128) + shifted(LR_WS, LR_WS + 128))
    ta = mid(LR_A, LR_A + 128) + shifted(LR_AS, LR_AS + 128)
    sg = _sigmoid(mid(LR_G, LR_G + 256) + shifted(LR_GS, LR_GS + 256))

    p_gla = jnp.dot(hb, win_ref[:, 0:4 * BR_W], preferred_element_type=F32)
    put(F_GQ, p_gla[:, 0:BR_W] * HEAD_DIM ** -0.5)
    put(F_GK, p_gla[:, BR_W:2 * BR_W])
    put(F_GV, p_gla[:, 2 * BR_W:3 * BR_W])
    put(F_GG, p_gla[:, 3 * BR_W:4 * BR_W])

    z_la = [_dot(zh, w2_ref[dr, 0:128, :]) for dr in range(2)]
    z_w = [_dot(tw, w2_ref[2 + dr, 0:128, :]) for dr in range(2)]
    z_a = [_dot(ta, w2_ref[4 + dr, 0:128, :]) for dr in range(2)]
    put(F_WG, _dot(sg, w2_ref[6]))

    for dr, (ba, dst) in enumerate(((ba_f, F_LAF), (ba_b, F_LAB))):
        put(dst, -_softplus(-(z_la[dr] + ba)) * (1.0 / GLA_NORMALIZER))

    p_ret = jnp.dot(hb, win_ref[:, 4 * BR_W:8 * BR_W], preferred_element_type=F32)

    conv = conv_ref[...]

    def dwconv(part):
        lo, hi = part * BR_W, (part + 1) * BR_W
        cw = conv[:, lo:hi]
        return (ps_ref[HALO - 1:HALO - 1 + tb, lo:hi] * cw[0:1]
                + ps_ref[HALO:HALO + tb, lo:hi] * cw[1:2]
                + ps_ref[HALO + 1:HALO + 1 + tb, lo:hi] * cw[2:3])

    r_c = dwconv(0)
    k_c = dwconv(1)
    v_c = dwconv(2)
    ones_h = _head_ones()
    kk = k_c * kkw
    kk = kk * lax.rsqrt(_dot(kk * kk, ones_h) + EPS)
    put(F_WR, r_c)
    put(F_WV, v_c)
    put(F_WKK, kk)

    p_fnet = jnp.dot(hb, win_ref[:, 11 * BR_W:12 * BR_W], preferred_element_type=F32)

    bonus = jnp.zeros((tb, BR_W), F32)
    for dr, (w0, a0, d_lw, d_kd, d_as) in enumerate(
            ((w0_f, a0_f, F_LWF, F_KDF, F_ASF), (w0_b, a0_b, F_LWB, F_KDB, F_ASB))):
        w_raw = -_softplus(-(w0 + z_w[dr])) - 0.5
        put(d_lw, -jnp.exp(w_raw))
        a_sig = _sigmoid(a0 + z_a[dr])
        kd = k_c * (1.0 + (a_sig - 1.0) * kaw)
        put(d_kd, kd)
        put(d_as, a_sig)
        bonus = bonus + _dot(r_c * kd * rkw, ones_h) * v_c
    put(F_BON, bonus)

    rq = p_ret[:, 0:BR_W]
    rk = p_ret[:, BR_W:2 * BR_W] * HEAD_DIM ** -0.5
    if use_rope:
        cosf = cos_ref[...]
        sins = sin_ref[...]
        low = (_iota((tb, BR_W), 1) % HEAD_DIM) < (HEAD_DIM // 2)

        def rope(t):
            partner = jnp.where(low, pltpu.roll(t, BR_W - HEAD_DIM // 2, 1),
                                pltpu.roll(t, HEAD_DIM // 2, 1))
            return t * cosf + partner * sins

        rq = rope(rq)
        rk = rope(rk)
    put(F_RQ, rq)
    put(F_RK, rk)
    put(F_RV, p_ret[:, 2 * BR_W:3 * BR_W])
    put(F_RG, p_ret[:, 3 * BR_W:4 * BR_W])

    fb = p_fnet.astype(BF16)
    fc_ref[0] = jnp.dot(fb, dft_ref[0], preferred_element_type=F32)
    fs_ref[0] = jnp.dot(fb, dft_ref[1], preferred_element_type=F32)


def _pre(x, mods, lw, use_rope, rope_tabs, dft_c):
    B, T, D = x.shape
    tb = min(256, T)
    nt = T // tb
    hb8 = tb // HALO
    n_h = T // HALO
    const2 = lambda b, i: (0, 0)
    const3 = lambda b, i: (0, 0, 0)
    kern = functools.partial(_pre_kernel, use_rope, tb, nt)
    return pl.pallas_call(
        kern,
        grid=(B, nt),
        in_specs=[
            pl.BlockSpec((1, tb, D), lambda b, i: (b, i, 0)),
            pl.BlockSpec((1, HALO, D), lambda b, i: (b, jnp.maximum(i * hb8 - 1, 0), 0)),
            pl.BlockSpec((1, HALO, D), lambda b, i: (b, jnp.minimum((i + 1) * hb8, n_h - 1), 0)),
            pl.BlockSpec((1, 1, 6 * D), lambda b, i: (b, 0, 0)),
            pl.BlockSpec((1, D), const2),
            pl.BlockSpec((D, P_IN), const2),
            pl.BlockSpec((D, LR_COLS), const2),
            pl.BlockSpec((7, 256, BR_W), const3),
            pl.BlockSpec((16, BR_W), const2),
            pl.BlockSpec((3, 3 * BR_W), const2),
            pl.BlockSpec((2, BR_W, BR_W), const3),
            pl.BlockSpec((tb, BR_W), lambda b, i: (i, 0)),
            pl.BlockSpec((tb, BR_W), lambda b, i: (i, 0)),
        ],
        out_specs=[
            pl.BlockSpec((1, tb, N_FEAT * BR_W), lambda b, i: (b, i, 0)),
            pl.BlockSpec((1, tb, D), lambda b, i: (b, i, 0)),
            pl.BlockSpec((1, tb, BR_W), lambda b, i: (b, i, 0)),
            pl.BlockSpec((1, tb, BR_W), lambda b, i: (b, i, 0)),
        ],
        out_shape=[
            jax.ShapeDtypeStruct((B, T, N_FEAT * BR_W), F32),
            jax.ShapeDtypeStruct((B, T, D), BF16),
            jax.ShapeDtypeStruct((B, T, BR_W), F32),
            jax.ShapeDtypeStruct((B, T, BR_W), F32),
        ],
        scratch_shapes=[
            pltpu.VMEM((tb + 2 * HALO, LR_COLS), F32),
            pltpu.VMEM((tb + 2 * HALO, 3 * BR_W), F32),
        ],
        compiler_params=_cparams(("parallel", "arbitrary")),
        name="pre",
    )(x, x, x, mods, lw["g1"], lw["w_in"], lw["w_lr"],
      lw["w2"], lw["vec"], lw["rwkv_conv"], dft_c, rope_tabs[0], rope_tabs[1])


def _cumsum_chunk(x, rev):
    c = x.shape[0]
    tri = jnp.where(_tri(c, rev), 1.0, 0.0).astype(BF16)
    return _dot_exact_lhs(tri, x)


def _gla_chunks(chains):
    c = chains[0][0].shape[0]
    nblk = c // GLA_SUB
    rowid = _iota((c, 1), 0)
    ones_h = _head_ones()
    bd = (_iota((BR_W, BR_W), 0) // HEAD_DIM) == (_iota((BR_W, BR_W), 1) // HEAD_DIM)
    cums = [_cumsum_chunk(ch[3], ch[5]) for ch in chains]

    same = ((_iota((HEADS * GLA_SUB, BR_W), 0) // GLA_SUB)
            == (_iota((HEADS * GLA_SUB, BR_W), 1) // HEAD_DIM))
    offd, inter, st_new = [], [], []
    for (q, k, v, la, st, rev), cum in zip(chains, cums):
        cum_end = cum[0:1] if rev else cum[c - 1:c]
        parts = []
        for blk in range(nblk):
            base = blk * GLA_SUB
            if rev:
                k_lo, k_hi, ref = base + GLA_SUB, c, base + GLA_SUB
            else:
                k_lo, k_hi, ref = 0, base, base - 1
            if k_hi <= k_lo:
                parts.append(None)
                continue
            qt = q[base:base + GLA_SUB] * jnp.exp(cum[base:base + GLA_SUB] - cum[ref:ref + 1])
            kt = k[k_lo:k_hi] * jnp.exp(cum[ref:ref + 1] - cum[k_lo:k_hi])
            sc = _dot_nt(_stack_heads(qt), kt)
            ov = jnp.where(same, _dot(sc, v[k_lo:k_hi]), 0.0)
            parts.append(sum(ov[h * GLA_SUB:(h + 1) * GLA_SUB] for h in range(HEADS)))
        offd.append(parts)
        inter.append(_dot_nt(q * jnp.exp(cum), st))
        st_new.append(st * jnp.exp(cum_end)
                      + jnp.where(bd, _dot_tn(v, k * jnp.exp(cum_end - cum)), 0.0))

    sub = _iota((8, BR_W), 0)
    bias = {rev: [jnp.where((sub <= jj) if rev else (sub >= jj), 0.0, NEG_BIG) for jj in range(8)]
            for rev in {ch[5] for ch in chains}}
    reds, spans = [], []
    for (q, k, v, la, st, rev), cum in zip(chains, cums):
        cum2 = cum * LOG2E
        pieces, sp = [], []
        for j in range(c):
            base = (j // GLA_SUB) * GLA_SUB
            own = (j // 8) * 8
            others = range(base, own, 8) if rev else range(own + 8, base + GLA_SUB, 8)
            for g0 in (own, *others):
                arg = cum2[g0:g0 + 8] - cum2[j:j + 1]
                if g0 == own:
                    arg = arg + bias[rev][j - own]
                pieces.append(q[g0:g0 + 8] * (k[j:j + 1] * jnp.exp2(arg)))
                sp.append((j, g0))
        reds.append(jnp.dot(jnp.concatenate(pieces, axis=0).astype(BF16), ones_h,
                            preferred_element_type=F32))
        spans.append(sp)

    outs = []
    for ci, (q, k, v, la, st, rev) in enumerate(chains):
        groups = [None] * (c // 8)
        for idx, (j, g0) in enumerate(spans[ci]):
            piece = reds[ci][idx * 8:idx * 8 + 8] * v[j:j + 1]
            gi = g0 // 8
            groups[gi] = piece if groups[gi] is None else groups[gi] + piece
        blocks = []
        for blk in range(nblk):
            o_blk = jnp.concatenate(groups[blk * GLA_SUB // 8:(blk + 1) * GLA_SUB // 8], axis=0)
            if offd[ci][blk] is not None:
                o_blk = o_blk + offd[ci][blk]
            blocks.append(o_blk)
        outs.append((jnp.concatenate(blocks, axis=0) + inter[ci], st_new[ci]))
    return outs


def _ret_chunks(chains):
    bd = (_iota((BR_W, BR_W), 0) // HEAD_DIM) == (_iota((BR_W, BR_W), 1) // HEAD_DIM)
    sc = [_dot_nt(q, _stack_heads(k)) * dmat for q, k, v, s, dmat, _, _, _ in chains]
    inter = [_dot(q * qdec, s) for q, k, v, s, _, qdec, _, _ in chains]
    upd = [_dot_tn(k * kdec, v) for q, k, v, s, _, _, kdec, _ in chains]
    outs = []
    for ch, sc_i, inter_i, upd_i in zip(chains, sc, inter, upd):
        q, k, v, s, _, _, _, cdec = ch
        o = _dot(sc_i, _stack_heads(v)) + inter_i
        outs.append((o, s * cdec + jnp.where(bd, upd_i, 0.0)))
    return outs


def _tri_wide(c, rev, strict):
    t = _iota((c, HEADS * c), 0)
    s = _iota((c, HEADS * c), 1) % c
    if rev:
        return (s > t) if strict else (s >= t)
    return (s < t) if strict else (s <= t)


def _rwkv_chunks(chains):
    c = chains[0][0].shape[0]
    n = HEADS * c
    bf = lambda t: t.astype(BF16)
    stack = lambda t: bf(_stack_heads(t))
    masks = {rev: (_tri_wide(c, rev, True), _tri_wide(c, rev, False))
             for rev in {ch[7] for ch in chains}}
    eye = jnp.where(_iota((c, n), 1) % c == _iota((c, n), 0), 1.0, 0.0)
    bd = (_iota((BR_W, BR_W), 0) // HEAD_DIM) == (_iota((BR_W, BR_W), 1) // HEAD_DIM)
    cums = [_cumsum_chunk(ch[5], ch[7]) for ch in chains]
    st = []
    for (r, v, kk, kd, asig, lw, s, rev), cum in zip(chains, cums):
        cum_end = cum[0:1] if rev else cum[c - 1:c]
        einv = jnp.exp(-cum)
        ehat = jnp.exp(cum_end - cum)
        bvec = kk * asig
        st.append(dict(
            a=bf(-kk * jnp.exp(cum - lw)), r=bf(r * jnp.exp(cum)),
            b_s=stack(bvec * einv), k_s=stack(kd * einv), v=bf(v), v_s=stack(v),
            bh=bf(bvec * ehat), kh=bf(kd * ehat), g_end=jnp.exp(cum_end), s=s, sb=bf(s), rev=rev))
    for d in st:
        d["m"] = _dot_nt(jnp.concatenate([d["a"], d["r"]], axis=0),
                         jnp.concatenate([d["b_s"], d["k_s"]], axis=0))
    for d in st:
        strict, incl = masks[d["rev"]]
        m = d.pop("m")
        d["p"] = jnp.where(strict, m[0:c, 0:n], 0.0)
        d["t"] = eye + d["p"]
        d["ps"] = stack(d["p"])
        d["m_ak"] = bf(jnp.where(strict, m[0:c, n:2 * n], 0.0))
        d["n_rb"] = bf(jnp.where(incl, m[c:2 * c, 0:n], 0.0))
        d["n_rk"] = bf(jnp.where(incl, m[c:2 * c, n:2 * n], 0.0))
    for d in st:
        wy = (_dot_nt(jnp.concatenate([d["a"], d["r"]], axis=0), d["sb"])
              + _dot(jnp.concatenate([d["m_ak"], d["n_rk"]], axis=0), d["v_s"]))
        d["w"], d["y0"] = wy[0:c], wy[c:2 * c]
    for d in st:
        d["p"] = _dot(d["p"], d["ps"])
        d["ps"] = stack(d["p"])
    for _ in range(int(math.log2(c)) - 2):
        for d in st:
            tp = _dot(jnp.concatenate([d["t"], d["p"]], axis=0), d["ps"])
            d["t"] = d["t"] + tp[0:c]
            d["p"] = tp[c:2 * c]
            d["ps"] = stack(d["p"])
    for d in st:
        d["t"] = d["t"] + _dot(d["t"], d["ps"])
    for d in st:
        d["u"] = _dot(d["t"], stack(d["w"]))
    outs = []
    for d in st:
        y = d["y0"] + _dot(d["n_rb"], stack(d["u"]))
        upd = _dot_tn(jnp.concatenate([bf(d["u"]), d["v"]], axis=0),
                      jnp.concatenate([d["bh"], d["kh"]], axis=0))
        outs.append((y, d["s"] * d["g_end"] + jnp.where(bd, upd, 0.0)))
    return outs


def _scan_kernel(kind, n_in, n_tab, nc, bb, *refs):
    fwd = refs[0:n_in]
    bwd = refs[n_in:2 * n_in]
    tabs = refs[2 * n_in:2 * n_in + n_tab]
    s0_ref = refs[2 * n_in + n_tab]
    of_ref, ob_ref, sf_ref, st_ref = refs[2 * n_in + n_tab + 1:]
    i = pl.program_id(1)

    @pl.when(i == 0)
    def _():
        st_ref[...] = s0_ref[...]

    chains = [(dr, b) for b in range(bb) for dr in range(2)]
    ins_of = lambda dr, b: [t[b] for t in (fwd, bwd)[dr]]
    if kind == "rwkv":
        res = _rwkv_chunks([(*ins_of(dr, b), st_ref[dr, b], dr == 1) for dr, b in chains])
    elif kind == "gla":
        res = _gla_chunks([(*ins_of(dr, b), st_ref[dr, b], dr == 1) for dr, b in chains])
    else:
        res = _ret_chunks([(*ins_of(dr, b), st_ref[dr, b], *(t[dr] for t in tabs))
                           for dr, b in chains])
    for (dr, b), (o, s_new) in zip(chains, res):
        (of_ref, ob_ref)[dr][b] = o
        st_ref[dr, b] = s_new

    @pl.when(i == nc - 1)
    def _():
        sf_ref[...] = st_ref[...]


_SCAN_ROWS = {"gla": 4, "ret": 4, "rwkv": 8}


def _scan(kind, feats, cols_f, cols_b, s0, tabs=()):
    B, T, _ = feats.shape
    c = min(SCAN_CHUNK[kind], T)
    nc = T // c
    bb = math.gcd(_SCAN_ROWS[kind], B)
    n_in = len(cols_f)
    in_specs = []
    for j in cols_f:
        in_specs.append(pl.BlockSpec((bb, c, BR_W), lambda b, i, j=j: (b, i, j)))
    for j in cols_b:
        in_specs.append(pl.BlockSpec((bb, c, BR_W), lambda b, i, j=j: (b, nc - 1 - i, j)))
    for t in tabs:
        in_specs.append(pl.BlockSpec(t.shape, lambda b, i, nd=t.ndim: (0,) * nd))
    in_specs.append(pl.BlockSpec((2, bb, BR_W, BR_W), lambda b, i: (0, b, 0, 0)))
    kern = functools.partial(_scan_kernel, kind, n_in, len(tabs), nc, bb)
    return pl.pallas_call(
        kern,
        grid=(B // bb, nc),
        in_specs=in_specs,
        out_specs=[
            pl.BlockSpec((bb, c, BR_W), lambda b, i: (b, i, 0)),
            pl.BlockSpec((bb, c, BR_W), lambda b, i: (b, nc - 1 - i, 0)),
            pl.BlockSpec((2, bb, BR_W, BR_W), lambda b, i: (0, b, 0, 0)),
        ],
        out_shape=[
            jax.ShapeDtypeStruct((B, T, BR_W), F32),
            jax.ShapeDtypeStruct((B, T, BR_W), F32),
            jax.ShapeDtypeStruct((2, B, BR_W, BR_W), F32),
        ],
        scratch_shapes=[pltpu.VMEM((2, bb, BR_W, BR_W), F32)],
        compiler_params=_cparams(("parallel", "arbitrary")),
        name=kind,
    )(*([feats] * (2 * n_in)), *tabs, s0)


def _ret_tables(c):
    pos = np.arange(c, dtype=np.float64)
    lane_head = np.arange(BR_W) // HEAD_DIM
    dmats, qd, kd, cd = [], [], [], []
    for dr in range(2):
        expo = -5.0 - np.arange(HEADS, dtype=np.float64)
        if dr == 1:
            expo = expo[::-1]
        log_g = np.log1p(-np.exp2(expo))
        if dr == 0:
            rel = pos[:, None] - pos[None, :]
            qpow = pos + 1.0
            kpow = c - 1.0 - pos
        else:
            rel = pos[None, :] - pos[:, None]
            qpow = c - pos
            kpow = pos
        tri = rel >= 0
        dm = np.where(tri[None], np.exp(np.where(tri, rel, 0.0)[None] * log_g[:, None, None]), 0.0)
        dmats.append(np.concatenate([dm[h] for h in range(HEADS)], axis=1))
        qd.append(np.exp(qpow[:, None] * log_g[lane_head][None, :]))
        kd.append(np.exp(kpow[:, None] * log_g[lane_head][None, :]))
        cd.append(np.exp(c * log_g[lane_head])[None, :])
    f = lambda xs: jnp.asarray(np.stack(xs), dtype=F32)
    return f(dmats), f(qd), f(kd), f(cd)


def _fnet_dense_kernel(t_len, cs_ref, g1_ref, g2_ref, o_ref):
    o_ref[0] = (jnp.dot(cs_ref[:, 0:t_len], g1_ref[0].astype(BF16), preferred_element_type=F32)
                + jnp.dot(cs_ref[:, t_len:2 * t_len], g2_ref[0].astype(BF16),
                          preferred_element_type=F32))


def _fnet_dense(g1, g2, cs):
    B, T, _ = g1.shape
    tb = min(512, T)
    return pl.pallas_call(
        functools.partial(_fnet_dense_kernel, T),
        grid=(T // tb, B),
        in_specs=[
            pl.BlockSpec((tb, 2 * T), lambda i, b: (i, 0)),
            pl.BlockSpec((1, T, BR_W), lambda i, b: (b, 0, 0)),
            pl.BlockSpec((1, T, BR_W), lambda i, b: (b, 0, 0)),
        ],
        out_specs=pl.BlockSpec((1, tb, BR_W), lambda i, b: (b, i, 0)),
        out_shape=jax.ShapeDtypeStruct((B, T, BR_W), F32),
        compiler_params=_cparams(("arbitrary", "arbitrary")),
        name="fnet",
    )(cs, g1, g2)


FFT_R = 8


def _fft_kernel(n, k1_ref, k2_ref, tc_ref, ts_ref, g1_ref, g2_ref, o_ref, zr_ref, zi_ref):
    m = n * FFT_R
    for j in range(n // FFT_R):
        sl = slice(j * FFT_R, (j + 1) * FFT_R)
        x = jnp.concatenate([g1_ref[0, :, sl, :].reshape(m, BR_W),
                             g2_ref[0, :, sl, :].reshape(m, BR_W)], axis=1).astype(BF16)
        y = jnp.dot(k1_ref[...], x, preferred_element_type=F32)
        yr = y[0:m, 0:BR_W] - y[m:2 * m, BR_W:2 * BR_W]
        yi = -y[0:m, BR_W:2 * BR_W] - y[m:2 * m, 0:BR_W]
        tc = tc_ref[:, sl, :].reshape(m, BR_W)
        ts = ts_ref[:, sl, :].reshape(m, BR_W)
        zr_ref[:, sl, :] = (tc * yr + ts * yi).reshape(n, FFT_R, BR_W)
        zi_ref[:, sl, :] = (tc * yi - ts * yr).reshape(n, FFT_R, BR_W)
    for j in range(n // FFT_R):
        sl = slice(j * FFT_R, (j + 1) * FFT_R)
        z = jnp.concatenate([zr_ref[sl].reshape(m, BR_W), zi_ref[sl].reshape(m, BR_W)],
                            axis=0).astype(BF16)
        o = jnp.dot(k2_ref[...], z, preferred_element_type=F32)
        o_ref[0, :, sl, :] = o.reshape(n, FFT_R, BR_W)


def _fnet_fft(g1, g2):
    B, T, _ = g1.shape
    n = int(round(math.sqrt(T)))
    assert n * n == T and n % FFT_R == 0
    m = n * FFT_R
    idx = np.arange(n)
    a_n = 2.0 * np.pi * ((idx[:, None] * idx[None, :]) % n) / n
    eye = np.eye(FFT_R)
    k1 = np.concatenate([np.einsum("ap,ij->aipj", f, eye).reshape(m, m)
                         for f in (np.cos(a_n), np.sin(a_n))], axis=0)
    k2 = np.concatenate([np.einsum("pr,ij->pijr", f, eye).reshape(m, m)
                         for f in (np.cos(a_n), np.sin(a_n))], axis=1)
    a_t = 2.0 * np.pi * (idx[:, None] * idx[None, :]) / T
    tc = jnp.broadcast_to(jnp.asarray(np.cos(a_t), dtype=F32)[:, :, None], (n, n, BR_W))
    ts = jnp.broadcast_to(jnp.asarray(np.sin(a_t), dtype=F32)[:, :, None], (n, n, BR_W))
    const2 = lambda b: (0, 0)
    const3 = lambda b: (0, 0, 0)
    one = pl.Buffered(1)
    o = pl.pallas_call(
        functools.partial(_fft_kernel, n),
        grid=(B,),
        in_specs=[
            pl.BlockSpec((2 * m, m), const2, pipeline_mode=one),
            pl.BlockSpec((m, 2 * m), const2, pipeline_mode=one),
            pl.BlockSpec((n, n, BR_W), const3, pipeline_mode=one),
            pl.BlockSpec((n, n, BR_W), const3, pipeline_mode=one),
            pl.BlockSpec((1, n, n, BR_W), lambda b: (b, 0, 0, 0)),
            pl.BlockSpec((1, n, n, BR_W), lambda b: (b, 0, 0, 0)),
        ],
        out_specs=pl.BlockSpec((1, n, n, BR_W), lambda b: (b, 0, 0, 0)),
        out_shape=jax.ShapeDtypeStruct((B, n, n, BR_W), F32),
        scratch_shapes=[pltpu.VMEM((n, n, BR_W), F32), pltpu.VMEM((n, n, BR_W), F32)],
        compiler_params=_cparams(("arbitrary",)),
        name="fft",
    )(jnp.asarray(k1, dtype=BF16), jnp.asarray(k2, dtype=BF16), tc, ts,
      g1.reshape(B, n, n, BR_W), g2.reshape(B, n, n, BR_W))
    return o.reshape(B, T, BR_W)


def _time_dft_table(T):
    t = np.arange(T)
    ang = ((t[:, None] * t[None, :]) % T) * (2.0 * np.pi / T)
    return jnp.asarray(np.concatenate([np.cos(ang), -np.sin(ang)], axis=1), dtype=BF16)


def _channel_dft_tables(T):
    ch = np.arange(HEAD_DIM)
    a64 = 2.0 * np.pi * ((ch[:, None] * ch[None, :]) % HEAD_DIM) / HEAD_DIM
    scale = (T * HEAD_DIM) ** -0.5
    eye = np.eye(HEADS)
    cbd = np.kron(eye, np.cos(a64)) * scale
    sbd = np.kron(eye, np.sin(a64)) * scale
    return jnp.asarray(np.stack([cbd, sbd]), dtype=BF16)


def _head_norm(o, ones_h, center):
    if center:
        o = o - _dot(o, ones_h) * (1.0 / HEAD_DIM)
    var = _dot(o * o, ones_h) * (1.0 / HEAD_DIM)
    return o * lax.rsqrt(var + EPS)


def _merge_kernel(x_ref, hb_ref, mod_ref, gg_ref, rg_ref, wg_ref, bon_ref,
                  ogf_ref, ogb_ref, orf_ref, orb_ref, oyf_ref, oyb_ref, fn_ref,
                  gn_ref, wgate_ref, bgate_ref, wbr_ref, wout_ref, o_ref):
    ones_h = _head_ones()
    gn = gn_ref[...]
    gla = _head_norm(ogf_ref[0] + ogb_ref[0], ones_h, False) * gn[0:1] * _silu(gg_ref[0])
    ret = _head_norm(orf_ref[0] + orb_ref[0], ones_h, True) * gn[1:2] * _silu(rg_ref[0])
    rwkv = (_head_norm(oyf_ref[0] + oyb_ref[0], ones_h, True) * gn[2:3] + bon_ref[0]) * wg_ref[0]
    outs = (gla, ret, rwkv, fn_ref[0])
    hb = hb_ref[0]
    z = None
    for br in range(4):
        gate = _sigmoid(jnp.dot(hb, wgate_ref[br], preferred_element_type=F32) + bgate_ref[br])
        term = gate * _dot(outs[br], wbr_ref[br])
        z = term if z is None else z + term
    y = _dot(z, wout_ref[...])
    gate1 = mod_ref[0][:, 2 * D_MODEL:3 * D_MODEL]
    o_ref[0] = x_ref[0] + gate1 * y


def _merge(x, hb, mods, feats, scans, fnet_out, lw):
    B, T, D = x.shape
    tb = min(256, T)
    tok = lambda w: pl.BlockSpec((1, tb, w), lambda b, i: (b, i, 0))
    fcol = lambda j: pl.BlockSpec((1, tb, BR_W), lambda b, i, j=j: (b, i, j))
    const2 = lambda b, i: (0, 0)
    const3 = lambda b, i: (0, 0, 0)
    return pl.pallas_call(
        _merge_kernel,
        grid=(B, T // tb),
        in_specs=[
            tok(D), tok(D),
            pl.BlockSpec((1, 1, 6 * D), lambda b, i: (b, 0, 0)),
            fcol(F_GG), fcol(F_RG), fcol(F_WG), fcol(F_BON),
            tok(BR_W), tok(BR_W), tok(BR_W), tok(BR_W), tok(BR_W), tok(BR_W), tok(BR_W),
            pl.BlockSpec((3, BR_W), const2),
            pl.BlockSpec((4, D, D), const3),
            pl.BlockSpec((4, 1, D), const3),
            pl.BlockSpec((4, BR_W, D), const3),
            pl.BlockSpec((D, D), const2),
        ],
        out_specs=tok(D),
        out_shape=jax.ShapeDtypeStruct((B, T, D), F32),
        compiler_params=_cparams(("parallel", "arbitrary")),
        name="merge",
    )(x, hb, mods, feats, feats, feats, feats, *scans, fnet_out,
      lw["gn"], lw["w_gate"], lw["b_gate"], lw["w_br"], lw["w_out"])


FFN_COL_CHUNKS = (1024, 1024, 768)


def _ffn_kernel(final, tb, nt,
                x_ref, xp_ref, xn_ref, mod_ref, g2_ref, up_ref, cw_ref, cb_ref,
                down_ref, gfin_ref, o_ref, act_ref):
    i = pl.program_id(1)
    n_ext = tb + 2 * HALO
    x_ext = jnp.concatenate([xp_ref[0], x_ref[0], xn_ref[0]], axis=0)
    mod = mod_ref[0]
    shift = mod[:, 3 * D_MODEL:4 * D_MODEL]
    scale = mod[:, 4 * D_MODEL:5 * D_MODEL]
    h2 = (_rms_rows(x_ext) * g2_ref[...]) * (1.0 + scale) + shift
    h2 = h2 * _halo_keep(n_ext, tb, i, nt)
    h2_ext = h2.astype(BF16)
    h2_mid = h2[HALO:HALO + tb].astype(BF16)
    cw = cw_ref[...]
    cb = cb_ref[...]
    lo = 0
    for width in FFN_COL_CHUNKS:
        a_ext = jnp.dot(h2_ext, up_ref[:, lo:lo + width], preferred_element_type=F32)
        u = jnp.dot(h2_mid, up_ref[:, D_FF + lo:D_FF + lo + width], preferred_element_type=F32)
        a = (a_ext[HALO - 1:HALO - 1 + tb] * cw[0:1, lo:lo + width]
             + a_ext[HALO:HALO + tb] * cw[1:2, lo:lo + width]
             + a_ext[HALO + 1:HALO + 1 + tb] * cw[2:3, lo:lo + width] + cb[:, lo:lo + width])
        act_ref[:, lo:lo + width] = (_silu(a) * u).astype(BF16)
        lo += width
    y = jnp.dot(act_ref[...], down_ref[...], preferred_element_type=F32)
    gate2 = mod[:, 5 * D_MODEL:6 * D_MODEL]
    res = x_ref[0] + gate2 * y
    if final:
        res = _rms_rows(res) * gfin_ref[...]
    o_ref[0] = res


def _ffn(x, mods, lw, g_final, final):
    B, T, D = x.shape
    tb = min(512, T)
    nt = T // tb
    hb8 = tb // HALO
    n_h = T // HALO
    const2 = lambda b, i: (0, 0)
    resident = lambda shape: pl.BlockSpec(shape, const2, pipeline_mode=pl.Buffered(1))
    kern = functools.partial(_ffn_kernel, final, tb, nt)
    return pl.pallas_call(
        kern,
        grid=(B, nt),
        in_specs=[
            pl.BlockSpec((1, tb, D), lambda b, i: (b, i, 0)),
            pl.BlockSpec((1, HALO, D), lambda b, i: (b, jnp.maximum(i * hb8 - 1, 0), 0)),
            pl.BlockSpec((1, HALO, D), lambda b, i: (b, jnp.minimum((i + 1) * hb8, n_h - 1), 0)),
            pl.BlockSpec((1, 1, 6 * D), lambda b, i: (b, 0, 0)),
            pl.BlockSpec((1, D), const2),
            resident((D, 2 * D_FF)),
            pl.BlockSpec((3, D_FF), const2),
            pl.BlockSpec((1, D_FF), const2),
            resident((D_FF, D)),
            pl.BlockSpec((1, D), const2),
        ],
        out_specs=pl.BlockSpec((1, tb, D), lambda b, i: (b, i, 0)),
        out_shape=jax.ShapeDtypeStruct((B, T, D), F32),
        scratch_shapes=[pltpu.VMEM((tb, D_FF), BF16)],
        compiler_params=_cparams(("parallel", "arbitrary")),
        name="ffn",
    )(x, x, x, mods, lw["g2"], lw["ffn_up"], lw["ffn_conv"], lw["ffn_conv_b"],
      lw["ffn_down"], g_final)


def _rope_tables(T):
    rows = T // GRID_W
    row = jnp.repeat(jnp.arange(rows, dtype=F32), GRID_W)
    colp = jnp.tile(jnp.arange(GRID_W, dtype=F32), rows)
    n_freq = HEAD_DIM // 4
    inv = ROPE_BASE ** (-jnp.arange(n_freq, dtype=F32) / n_freq)
    ang = jnp.concatenate([row[:, None] * inv, colp[:, None] * inv], axis=-1)
    cos, sin = jnp.cos(ang), jnp.sin(ang)
    cos_h = jnp.concatenate([cos, cos], axis=-1)
    sin_h = jnp.concatenate([-sin, sin], axis=-1)
    return jnp.tile(cos_h, (1, HEADS)), jnp.tile(sin_h, (1, HEADS))


def _pad_rows(w, lo, n):
    return jnp.zeros((n, w.shape[1]), w.dtype).at[lo:lo + w.shape[0]].set(w)


def _layer_weights(l, p):
    D = D_MODEL
    bf = lambda t: t.astype(BF16)
    zcols = lambda w, n: jnp.concatenate([w, jnp.zeros((w.shape[0], n - w.shape[1]), w.dtype)], axis=1)
    w_h = zcols(jnp.concatenate([p["gla_wa1"][l, 0], p["gla_wa1"][l, 1]], axis=1), 128)
    w_xw = jnp.concatenate([p["rwkv_w1"][l, 0], p["rwkv_w1"][l, 1]], axis=1)
    w_xa = jnp.concatenate([p["rwkv_a1"][l, 0], p["rwkv_a1"][l, 1]], axis=1)
    w_xg = zcols(p["rwkv_g1"][l], 256)
    mu = p["rwkv_mu"][l]
    w_lr = jnp.concatenate([w_xg, mu[2][:, None] * w_xg, w_h, w_xw, w_xa,
                            mu[0][:, None] * w_xw, mu[1][:, None] * w_xa,
                            jnp.zeros((D, LR_COLS - LR_AS - 128), F32)], axis=1)
    w2 = jnp.stack([
        _pad_rows(p["gla_wa2"][l, 0], 0, 256), _pad_rows(p["gla_wa2"][l, 1], GLA_LR, 256),
        _pad_rows(p["rwkv_w2"][l, 0], 0, 256), _pad_rows(p["rwkv_w2"][l, 1], RWKV_LR, 256),
        _pad_rows(p["rwkv_a2"][l, 0], 0, 256), _pad_rows(p["rwkv_a2"][l, 1], RWKV_LR, 256),
        _pad_rows(p["rwkv_g2"][l], 0, 256),
    ])
    vec = jnp.stack([p["gla_ba"][l, 0], p["gla_ba"][l, 1], p["rwkv_w0"][l, 0], p["rwkv_w0"][l, 1],
                     p["rwkv_a0"][l, 0], p["rwkv_a0"][l, 1], p["rwkv_kk"][l], p["rwkv_ka"][l],
                     p["rwkv_rk"][l]])
    vec = jnp.concatenate([vec, jnp.zeros((16 - vec.shape[0], BR_W), F32)], axis=0)
    return {
        "g1": p["g_norm1"][l].reshape(1, D), "g2": p["g_norm2"][l].reshape(1, D),
        "w_in": bf(p["w_in"][l]), "w_lr": bf(w_lr), "w2": bf(w2), "vec": vec,
        "rwkv_conv": p["rwkv_conv"][l],
        "gn": jnp.stack([p["gla_gn"][l], p["ret_gn"][l], p["rwkv_gn"][l]]),
        "w_gate": bf(p["w_gate"][l]), "b_gate": p["b_gate"][l].reshape(4, 1, D),
        "w_br": bf(p["w_br"][l]), "w_out": bf(p["w_out"][l]),
        "ffn_up": bf(p["ffn_up"][l]), "ffn_conv": p["ffn_conv"][l],
        "ffn_conv_b": p["ffn_conv_b"][l].reshape(1, D_FF), "ffn_down": bf(p["ffn_down"][l]),
    }


_GLA_F = (F_GQ, F_GK, F_GV, F_LAF)
_GLA_B = (F_GQ, F_GK, F_GV, F_LAB)
_RET = (F_RQ, F_RK, F_RV)
_RWKV_F = (F_WR, F_WV, F_WKK, F_KDF, F_ASF, F_LWF)
_RWKV_B = (F_WR, F_WV, F_WKK, F_KDB, F_ASB, F_LWB)


def _mixers(feats_c, feats_l, ret_tabs_c, ret_tabs_l, need_ctx):
    B = feats_l.shape[0]
    zero = jnp.zeros((2, B, BR_W, BR_W), F32)
    out_c, out_l = [], []
    for kind, cf, cb, tabs in (("gla", _GLA_F, _GLA_B, None), ("ret", _RET, _RET, True),
                               ("rwkv", _RWKV_F, _RWKV_B, None)):
        tc = ret_tabs_c if tabs else ()
        tl = ret_tabs_l if tabs else ()
        of_c, ob_c, s_c = _scan(kind, feats_c, cf, cb, zero, tc)
        of_l, ob_l, _ = _scan(kind, feats_l, cf, cb, s_c, tl)
        out_c += [of_c, ob_c]
        out_l += [of_l, ob_l]
    return out_l, (out_c if need_ctx else None)


def kernel(x, c, ctx, c_ctx, w_ada, b_ada, g_norm1, g_norm2, w_in, gla_wa1, gla_wa2, gla_ba, gla_gn, ret_gn, rwkv_conv, rwkv_mu, rwkv_w0, rwkv_w1, rwkv_w2, rwkv_a0, rwkv_a1, rwkv_a2, rwkv_g1, rwkv_g2, rwkv_kk, rwkv_ka, rwkv_rk, rwkv_gn, w_gate, b_gate, w_br, w_out, ffn_up, ffn_conv, ffn_conv_b, ffn_down, g_final):
    p = dict(g_norm1=g_norm1, g_norm2=g_norm2, w_in=w_in, gla_wa1=gla_wa1, gla_wa2=gla_wa2,
             gla_ba=gla_ba, gla_gn=gla_gn, ret_gn=ret_gn, rwkv_conv=rwkv_conv, rwkv_mu=rwkv_mu,
             rwkv_w0=rwkv_w0, rwkv_w1=rwkv_w1, rwkv_w2=rwkv_w2, rwkv_a0=rwkv_a0,
             rwkv_a1=rwkv_a1, rwkv_a2=rwkv_a2, rwkv_g1=rwkv_g1, rwkv_g2=rwkv_g2,
             rwkv_kk=rwkv_kk, rwkv_ka=rwkv_ka, rwkv_rk=rwkv_rk, rwkv_gn=rwkv_gn,
             w_gate=w_gate, b_gate=b_gate, w_br=w_br, w_out=w_out, ffn_up=ffn_up,
             ffn_conv=ffn_conv, ffn_conv_b=ffn_conv_b, ffn_down=ffn_down)
    B, T, D = x.shape
    Tc = ctx.shape[1]
    depth = w_ada.shape[0]

    cmat = jnp.concatenate([c, c_ctx[None, :], jnp.zeros((16 - B - 1, D), F32)], axis=0)
    mod_all = _modulation(cmat, w_ada, b_ada)

    rope_l = _rope_tables(T)
    rope_c = (jnp.zeros((Tc, BR_W), F32), jnp.zeros((Tc, BR_W), F32))
    dft_l = _channel_dft_tables(T)
    dft_c = _channel_dft_tables(Tc)
    cs_c = _time_dft_table(Tc)
    ret_tabs_l = _ret_tables(min(SCAN_CHUNK["ret"], T))
    ret_tabs_c = _ret_tables(min(SCAN_CHUNK["ret"], Tc))
    g_fin = g_final.reshape(1, D)

    for l in range(depth):
        last = l == depth - 1
        lw = _layer_weights(l, p)
        mods_l = mod_all[l, 0:B].reshape(B, 1, 6 * D)
        mods_c = jnp.broadcast_to(mod_all[l, B:B + 1].reshape(1, 1, 6 * D), (B, 1, 6 * D))

        feats_l, hb_l, g1_l, g2_l = _pre(x, mods_l, lw, True, rope_l, dft_l)
        feats_c, hb_c, g1_c, g2_c = _pre(ctx, mods_c, lw, False, rope_c, dft_c)
        scans_l, scans_c = _mixers(feats_c, feats_l, ret_tabs_c, ret_tabs_l, not last)

        fn_l = _fnet_fft(g1_l, g2_l)
        x = _merge(x, hb_l, mods_l, feats_l, scans_l, fn_l, lw)
        x = _ffn(x, mods_l, lw, g_fin, last)
        if not last:
            fn_c = _fnet_dense(g1_c, g2_c, cs_c)
            ctx = _merge(ctx, hb_c, mods_c, feats_c, scans_c, fn_c, lw)
            ctx = _ffn(ctx, mods_c, lw, g_fin, False)
    return x
```

```python
import functools
import math

import numpy as np
import jax
import jax.numpy as jnp
from jax import lax
from jax.experimental import pallas as pl
from jax.experimental.pallas import tpu as pltpu

F32 = jnp.float32
BF16 = jnp.bfloat16

D_MODEL = 1024
BR_W = 256
HEAD_DIM = 64
HEADS = 4
N_PARTS = 12
P_IN = N_PARTS * BR_W
GRID_W = 64
GLA_LR = 16
GLA_NORMALIZER = 16.0
RWKV_LR = 64
RWKV_G_LR = 160
D_FF = 2816
ROPE_BASE = 10000.0
EPS = 1e-6

SCAN_CHUNK = {"gla": 64, "ret": 128, "rwkv": 64}
GLA_SUB = 16
HALO = 8
NEG_BIG = -1e30
LOG2E = 1.4426950408889634

(F_GQ, F_GK, F_GV, F_GG, F_RQ, F_RK, F_RV, F_RG, F_WR, F_WV, F_WKK, F_WG, F_BON,
 F_LAF, F_LAB, F_LWF, F_LWB, F_KDF, F_KDB, F_ASF, F_ASB) = range(21)
N_FEAT = 21

LR_G, LR_GS, LR_H, LR_W, LR_A, LR_WS, LR_AS = 0, 256, 512, 640, 768, 896, 1024
LR_COLS = 1280

VMEM_LIMIT = 56 * 1024 * 1024


def _cparams(sem):
    return pltpu.CompilerParams(dimension_semantics=sem, vmem_limit_bytes=VMEM_LIMIT)


def _dot(a, b):
    return jnp.dot(a.astype(BF16), b.astype(BF16), preferred_element_type=F32)


def _dot_nt(a, b):
    return lax.dot_general(a.astype(BF16), b.astype(BF16), (((1,), (1,)), ((), ())),
                           preferred_element_type=F32)


def _dot_tn(a, b):
    return lax.dot_general(a.astype(BF16), b.astype(BF16), (((0,), (0,)), ((), ())),
                           preferred_element_type=F32)


def _split3(x):
    hi = x.astype(BF16)
    r1 = x - hi.astype(F32)
    mid = r1.astype(BF16)
    lo = (r1 - mid.astype(F32)).astype(BF16)
    return hi, mid, lo


def _dot_exact_lhs(a_bf16, x):
    hi, mid, lo = _split3(x)
    d = lambda t: jnp.dot(a_bf16, t, preferred_element_type=F32)
    return d(hi) + d(mid) + d(lo)


def _sigmoid(x):
    return 1.0 / (1.0 + jnp.exp(-x))


def _silu(x):
    return x * _sigmoid(x)


def _softplus(x):
    return jnp.maximum(x, 0.0) + jnp.log(1.0 + jnp.exp(-jnp.abs(x)))


def _iota(shape, dim):
    return lax.broadcasted_iota(jnp.int32, shape, dim)


def _head_ones():
    r = _iota((BR_W, BR_W), 0) // HEAD_DIM
    c = _iota((BR_W, BR_W), 1) // HEAD_DIM
    return jnp.where(r == c, 1.0, 0.0).astype(BF16)


def _stack_heads(x):
    c, w = x.shape
    xs = jnp.concatenate([x] * HEADS, axis=0)
    keep = (_iota((HEADS * c, w), 0) // c) == (_iota((HEADS * c, w), 1) // (w // HEADS))
    return jnp.where(keep, xs, 0.0)


def _tri(c, rev, strict=False):
    r = _iota((c, c), 0)
    s = _iota((c, c), 1)
    if rev:
        return (r < s) if strict else (r <= s)
    return (r > s) if strict else (r >= s)


def _halo_keep(n_ext, tb, i, nt):
    row = _iota((n_ext, 1), 0)
    first = jnp.where(i > 0, 1.0, 0.0)
    last = jnp.where(i < nt - 1, 1.0, 0.0)
    return jnp.where(row < HALO, first, jnp.where(row >= tb + HALO, last, 1.0))


def _rms_rows(x):
    return x * lax.rsqrt(jnp.mean(x * x, axis=-1, keepdims=True) + EPS)


def _mod_kernel(c_ref, w_ref, b_ref, o_ref):
    s = _silu(c_ref[...])
    o_ref[0] = _dot(s, w_ref[0]) + b_ref[0]


def _modulation(cmat, w_ada, b_ada):
    L = w_ada.shape[0]
    nblk = 1536
    return pl.pallas_call(
        _mod_kernel,
        grid=(L, 6 * D_MODEL // nblk),
        in_specs=[
            pl.BlockSpec((16, D_MODEL), lambda l, j: (0, 0)),
            pl.BlockSpec((1, D_MODEL, nblk), lambda l, j: (l, 0, j)),
            pl.BlockSpec((1, 1, nblk), lambda l, j: (l, 0, j)),
        ],
        out_specs=pl.BlockSpec((1, 16, nblk), lambda l, j: (l, 0, j)),
        out_shape=jax.ShapeDtypeStruct((L, 16, 6 * D_MODEL), F32),
        compiler_params=_cparams(("arbitrary", "arbitrary")),
        name="mod",
    )(cmat, w_ada, b_ada.reshape(L, 1, 6 * D_MODEL))


def _pre_kernel(use_rope, tb, nt,
                x_ref, xp_ref, xn_ref, mod_ref, g1_ref, win_ref, wlr_ref,
                w2_ref, vec_ref, conv_ref, dft_ref, cos_ref, sin_ref,
                f_ref, hb_ref, fc_ref, fs_ref, lr_ref, ps_ref):
    i = pl.program_id(1)
    n_ext = tb + 2 * HALO
    x_ext = jnp.concatenate([xp_ref[0], x_ref[0], xn_ref[0]], axis=0)
    mod = mod_ref[0]
    shift = mod[:, 0:D_MODEL]
    scale = mod[:, D_MODEL:2 * D_MODEL]
    h_ext = (_rms_rows(x_ext) * g1_ref[...]) * (1.0 + scale) + shift
    h_ext = h_ext * _halo_keep(n_ext, tb, i, nt)
    hb_ext = h_ext.astype(BF16)
    hb = h_ext[HALO:HALO + tb].astype(BF16)
    hb_ref[0] = hb
    lr_ref[...] = jnp.dot(hb_ext, wlr_ref[...], preferred_element_type=F32)
    ps_ref[...] = jnp.dot(hb_ext, win_ref[:, 8 * BR_W:11 * BR_W], preferred_element_type=F32)

    def mid(lo, hi):
        return lr_ref[HALO:HALO + tb, lo:hi]

    def shifted(lo, hi):
        return (0.5 * (lr_ref[HALO - 1:HALO - 1 + tb, lo:hi] + lr_ref[HALO + 1:HALO + 1 + tb, lo:hi])
                - lr_ref[HALO:HALO + tb, lo:hi])

    vec = vec_ref[...]
    ba_f, ba_b, w0_f, w0_b = vec[0:1], vec[1:2], vec[2:3], vec[3:4]
    a0_f, a0_b, kkw, kaw, rkw = vec[4:5], vec[5:6], vec[6:7], vec[7:8], vec[8:9]

    def put(j, val):
        f_ref[0, :, j * BR_W:(j + 1) * BR_W] = val

    zh = mid(LR_H, LR_H + 128)
    tw = jnp.tanh(mid(LR_W, LR_W + 128) + shifted(LR_WS, LR_WS + 128))
    ta = mid(LR_A, LR_A + 128) + shifted(LR_AS, LR_AS + 128)
    sg = _sigmoid(mid(LR_G, LR_G + 256) + shifted(LR_GS, LR_GS + 256))

    p_gla = jnp.dot(hb, win_ref[:, 0:4 * BR_W], preferred_element_type=F32)
    put(F_GQ, p_gla[:, 0:BR_W] * HEAD_DIM ** -0.5)
    put(F_GK, p_gla[:, BR_W:2 * BR_W])
    put(F_GV, p_gla[:, 2 * BR_W:3 * BR_W])
    put(F_GG, p_gla[:, 3 * BR_W:4 * BR_W])

    z_la = [_dot(zh, w2_ref[dr, 0:128, :]) for dr in range(2)]
    z_w = [_dot(tw, w2_ref[2 + dr, 0:128, :]) for dr in range(2)]
    z_a = [_dot(ta, w2_ref[4 + dr, 0:128, :]) for dr in range(2)]
    put(F_WG, _dot(sg, w2_ref[6]))

    for dr, (ba, dst) in enumerate(((ba_f, F_LAF), (ba_b, F_LAB))):
        put(dst, -_softplus(-(z_la[dr] + ba)) * (1.0 / GLA_NORMALIZER))

    p_ret = jnp.dot(hb, win_ref[:, 4 * BR_W:8 * BR_W], preferred_element_type=F32)

    conv = conv_ref[...]

    def dwconv(part):
        lo, hi = part * BR_W, (part + 1) * BR_W
        cw = conv[:, lo:hi]
        return (ps_ref[HALO - 1:HALO - 1 + tb, lo:hi] * cw[0:1]
                + ps_ref[HALO:HALO + tb, lo:hi] * cw[1:2]
                + ps_ref[HALO + 1:HALO + 1 + tb, lo:hi] * cw[2:3])

    r_c = dwconv(0)
    k_c = dwconv(1)
    v_c = dwconv(2)
    ones_h = _head_ones()
    kk = k_c * kkw
    kk = kk * lax.rsqrt(_dot(kk * kk, ones_h) + EPS)
    put(F_WR, r_c)
    put(F_WV, v_c)
    put(F_WKK, kk)

    p_fnet = jnp.dot(hb, win_ref[:, 11 * BR_W:12 * BR_W], preferred_element_type=F32)

    bonus = jnp.zeros((tb, BR_W), F32)
    for dr, (w0, a0, d_lw, d_kd, d_as) in enumerate(
            ((w0_f, a0_f, F_LWF, F_KDF, F_ASF), (w0_b, a0_b, F_LWB, F_KDB, F_ASB))):
        w_raw = -_softplus(-(w0 + z_w[dr])) - 0.5
        put(d_lw, -jnp.exp(w_raw))
        a_sig = _sigmoid(a0 + z_a[dr])
        kd = k_c * (1.0 + (a_sig - 1.0) * kaw)
        put(d_kd, kd)
        put(d_as, a_sig)
        bonus = bonus + _dot(r_c * kd * rkw, ones_h) * v_c
    put(F_BON, bonus)

    rq = p_ret[:, 0:BR_W]
    rk = p_ret[:, BR_W:2 * BR_W] * HEAD_DIM ** -0.5
    if use_rope:
        cosf = cos_ref[...]
        sins = sin_ref[...]
        low = (_iota((tb, BR_W), 1) % HEAD_DIM) < (HEAD_DIM // 2)

        def rope(t):
            partner = jnp.where(low, pltpu.roll(t, BR_W - HEAD_DIM // 2, 1),
                                pltpu.roll(t, HEAD_DIM // 2, 1))
            return t * cosf + partner * sins

        rq = rope(rq)
        rk = rope(rk)
    put(F_RQ, rq)
    put(F_RK, rk)
    put(F_RV, p_ret[:, 2 * BR_W:3 * BR_W])
    put(F_RG, p_ret[:, 3 * BR_W:4 * BR_W])

    fb = p_fnet.astype(BF16)
    fc_ref[0] = jnp.dot(fb, dft_ref[0], preferred_element_type=F32)
    fs_ref[0] = jnp.dot(fb, dft_ref[1], preferred_element_type=F32)


def _pre(x, mods, lw, use_rope, rope_tabs, dft_c):
    B, T, D = x.shape
    tb = min(512, T)
    nt = T // tb
    hb8 = tb // HALO
    n_h = T // HALO
    const2 = lambda b, i: (0, 0)
    const3 = lambda b, i: (0, 0, 0)
    kern = functools.partial(_pre_kernel, use_rope, tb, nt)
    return pl.pallas_call(
        kern,
        grid=(B, nt),
        in_specs=[
            pl.BlockSpec((1, tb, D), lambda b, i: (b, i, 0)),
            pl.BlockSpec((1, HALO, D), lambda b, i: (b, jnp.maximum(i * hb8 - 1, 0), 0)),
            pl.BlockSpec((1, HALO, D), lambda b, i: (b, jnp.minimum((i + 1) * hb8, n_h - 1), 0)),
            pl.BlockSpec((1, 1, 6 * D), lambda b, i: (b, 0, 0)),
            pl.BlockSpec((1, D), const2),
            pl.BlockSpec((D, P_IN), const2, pipeline_mode=pl.Buffered(1)),
            pl.BlockSpec((D, LR_COLS), const2, pipeline_mode=pl.Buffered(1)),
            pl.BlockSpec((7, 256, BR_W), const3, pipeline_mode=pl.Buffered(1)),
            pl.BlockSpec((16, BR_W), const2),
            pl.BlockSpec((3, 3 * BR_W), const2),
            pl.BlockSpec((2, BR_W, BR_W), const3),
            pl.BlockSpec((tb, BR_W), lambda b, i: (i, 0)),
            pl.BlockSpec((tb, BR_W), lambda b, i: (i, 0)),
        ],
        out_specs=[
            pl.BlockSpec((1, tb, N_FEAT * BR_W), lambda b, i: (b, i, 0)),
            pl.BlockSpec((1, tb, D), lambda b, i: (b, i, 0)),
            pl.BlockSpec((1, tb, BR_W), lambda b, i: (b, i, 0)),
            pl.BlockSpec((1, tb, BR_W), lambda b, i: (b, i, 0)),
        ],
        out_shape=[
            jax.ShapeDtypeStruct((B, T, N_FEAT * BR_W), F32),
            jax.ShapeDtypeStruct((B, T, D), BF16),
            jax.ShapeDtypeStruct((B, T, BR_W), F32),
            jax.ShapeDtypeStruct((B, T, BR_W), F32),
        ],
        scratch_shapes=[
            pltpu.VMEM((tb + 2 * HALO, LR_COLS), F32),
            pltpu.VMEM((tb + 2 * HALO, 3 * BR_W), F32),
        ],
        compiler_params=_cparams(("parallel", "arbitrary")),
        name="pre",
    )(x, x, x, mods, lw["g1"], lw["w_in"], lw["w_lr"],
      lw["w2"], lw["vec"], lw["rwkv_conv"], dft_c, rope_tabs[0], rope_tabs[1])


def _cumsum_chunk(x, rev):
    c = x.shape[0]
    tri = jnp.where(_tri(c, rev), 1.0, 0.0).astype(BF16)
    return _dot_exact_lhs(tri, x)


def _gla_chunks(chains):
    c = chains[0][0].shape[0]
    nblk = c // GLA_SUB
    rowid = _iota((c, 1), 0)
    ones_h = _head_ones()
    bd = (_iota((BR_W, BR_W), 0) // HEAD_DIM) == (_iota((BR_W, BR_W), 1) // HEAD_DIM)
    cums = [_cumsum_chunk(ch[3], ch[5]) for ch in chains]

    same = ((_iota((HEADS * GLA_SUB, BR_W), 0) // GLA_SUB)
            == (_iota((HEADS * GLA_SUB, BR_W), 1) // HEAD_DIM))
    offd, inter, st_new = [], [], []
    for (q, k, v, la, st, rev), cum in zip(chains, cums):
        cum_end = cum[0:1] if rev else cum[c - 1:c]
        parts = []
        for blk in range(nblk):
            base = blk * GLA_SUB
            if rev:
                k_lo, k_hi, ref = base + GLA_SUB, c, base + GLA_SUB
            else:
                k_lo, k_hi, ref = 0, base, base - 1
            if k_hi <= k_lo:
                parts.append(None)
                continue
            qt = q[base:base + GLA_SUB] * jnp.exp(cum[base:base + GLA_SUB] - cum[ref:ref + 1])
            kt = k[k_lo:k_hi] * jnp.exp(cum[ref:ref + 1] - cum[k_lo:k_hi])
            sc = _dot_nt(_stack_heads(qt), kt)
            ov = jnp.where(same, _dot(sc, v[k_lo:k_hi]), 0.0)
            parts.append(sum(ov[h * GLA_SUB:(h + 1) * GLA_SUB] for h in range(HEADS)))
        offd.append(parts)
        inter.append(_dot_nt(q * jnp.exp(cum), st))
        st_new.append(st * jnp.exp(cum_end)
                      + jnp.where(bd, _dot_tn(v, k * jnp.exp(cum_end - cum)), 0.0))

    sub = _iota((8, BR_W), 0)
    bias = {rev: [jnp.where((sub <= jj) if rev else (sub >= jj), 0.0, NEG_BIG) for jj in range(8)]
            for rev in {ch[5] for ch in chains}}
    reds, spans = [], []
    for (q, k, v, la, st, rev), cum in zip(chains, cums):
        cum2 = cum * LOG2E
        pieces, sp = [], []
        for j in range(c):
            base = (j // GLA_SUB) * GLA_SUB
            own = (j // 8) * 8
            others = range(base, own, 8) if rev else range(own + 8, base + GLA_SUB, 8)
            for g0 in (own, *others):
                arg = cum2[g0:g0 + 8] - cum2[j:j + 1]
                if g0 == own:
                    arg = arg + bias[rev][j - own]
                pieces.append(q[g0:g0 + 8] * (k[j:j + 1] * jnp.exp2(arg)))
                sp.append((j, g0))
        reds.append(jnp.dot(jnp.concatenate(pieces, axis=0).astype(BF16), ones_h,
                            preferred_element_type=F32))
        spans.append(sp)

    outs = []
    for ci, (q, k, v, la, st, rev) in enumerate(chains):
        groups = [None] * (c // 8)
        for idx, (j, g0) in enumerate(spans[ci]):
            piece = reds[ci][idx * 8:idx * 8 + 8] * v[j:j + 1]
            gi = g0 // 8
            groups[gi] = piece if groups[gi] is None else groups[gi] + piece
        blocks = []
        for blk in range(nblk):
            o_blk = jnp.concatenate(groups[blk * GLA_SUB // 8:(blk + 1) * GLA_SUB // 8], axis=0)
            if offd[ci][blk] is not None:
                o_blk = o_blk + offd[ci][blk]
            blocks.append(o_blk)
        outs.append((jnp.concatenate(blocks, axis=0) + inter[ci], st_new[ci]))
    return outs


def _ret_chunks(chains):
    bd = (_iota((BR_W, BR_W), 0) // HEAD_DIM) == (_iota((BR_W, BR_W), 1) // HEAD_DIM)
    sc = [_dot_nt(q, _stack_heads(k)) * dmat for q, k, v, s, dmat, _, _, _ in chains]
    inter = [_dot(q * qdec, s) for q, k, v, s, _, qdec, _, _ in chains]
    upd = [_dot_tn(k * kdec, v) for q, k, v, s, _, _, kdec, _ in chains]
    outs = []
    for ch, sc_i, inter_i, upd_i in zip(chains, sc, inter, upd):
        q, k, v, s, _, _, _, cdec = ch
        o = _dot(sc_i, _stack_heads(v)) + inter_i
        outs.append((o, s * cdec + jnp.where(bd, upd_i, 0.0)))
    return outs


def _tri_wide(c, rev, strict):
    t = _iota((c, HEADS * c), 0)
    s = _iota((c, HEADS * c), 1) % c
    if rev:
        return (s > t) if strict else (s >= t)
    return (s < t) if strict else (s <= t)


def _rwkv_chunks(chains):
    c = chains[0][0].shape[0]
    n = HEADS * c
    bf = lambda t: t.astype(BF16)
    stack = lambda t: bf(_stack_heads(t))
    masks = {rev: (_tri_wide(c, rev, True), _tri_wide(c, rev, False))
             for rev in {ch[7] for ch in chains}}
    eye = jnp.where(_iota((c, n), 1) % c == _iota((c, n), 0), 1.0, 0.0)
    bd = (_iota((BR_W, BR_W), 0) // HEAD_DIM) == (_iota((BR_W, BR_W), 1) // HEAD_DIM)
    cums = [_cumsum_chunk(ch[5], ch[7]) for ch in chains]
    st = []
    for (r, v, kk, kd, asig, lw, s, rev), cum in zip(chains, cums):
        cum_end = cum[0:1] if rev else cum[c - 1:c]
        einv = jnp.exp(-cum)
        ehat = jnp.exp(cum_end - cum)
        bvec = kk * asig
        st.append(dict(
            a=bf(-kk * jnp.exp(cum - lw)), r=bf(r * jnp.exp(cum)),
            b_s=stack(bvec * einv), k_s=stack(kd * einv), v=bf(v), v_s=stack(v),
            bh=bf(bvec * ehat), kh=bf(kd * ehat), g_end=jnp.exp(cum_end), s=s, sb=bf(s), rev=rev))
    for d in st:
        d["m"] = _dot_nt(jnp.concatenate([d["a"], d["r"]], axis=0),
                         jnp.concatenate([d["b_s"], d["k_s"]], axis=0))
    for d in st:
        strict, incl = masks[d["rev"]]
        m = d.pop("m")
        d["p"] = jnp.where(strict, m[0:c, 0:n], 0.0)
        d["t"] = eye + d["p"]
        d["ps"] = stack(d["p"])
        d["m_ak"] = bf(jnp.where(strict, m[0:c, n:2 * n], 0.0))
        d["n_rb"] = bf(jnp.where(incl, m[c:2 * c, 0:n], 0.0))
        d["n_rk"] = bf(jnp.where(incl, m[c:2 * c, n:2 * n], 0.0))
    for d in st:
        wy = (_dot_nt(jnp.concatenate([d["a"], d["r"]], axis=0), d["sb"])
              + _dot(jnp.concatenate([d["m_ak"], d["n_rk"]], axis=0), d["v_s"]))
        d["w"], d["y0"] = wy[0:c], wy[c:2 * c]
    for d in st:
        d["p"] = _dot(d["p"], d["ps"])
        d["ps"] = stack(d["p"])
    for _ in range(int(math.log2(c)) - 2):
        for d in st:
            tp = _dot(jnp.concatenate([d["t"], d["p"]], axis=0), d["ps"])
            d["t"] = d["t"] + tp[0:c]
            d["p"] = tp[c:2 * c]
            d["ps"] = stack(d["p"])
    for d in st:
        d["t"] = d["t"] + _dot(d["t"], d["ps"])
    for d in st:
        d["u"] = _dot(d["t"], stack(d["w"]))
    outs = []
    for d in st:
        y = d["y0"] + _dot(d["n_rb"], stack(d["u"]))
        upd = _dot_tn(jnp.concatenate([bf(d["u"]), d["v"]], axis=0),
                      jnp.concatenate([d["bh"], d["kh"]], axis=0))
        outs.append((y, d["s"] * d["g_end"] + jnp.where(bd, upd, 0.0)))
    return outs


def _scan_kernel(kind, n_in, n_tab, nc, bb, *refs):
    fwd = refs[0:n_in]
    bwd = refs[n_in:2 * n_in]
    tabs = refs[2 * n_in:2 * n_in + n_tab]
    s0_ref = refs[2 * n_in + n_tab]
    of_ref, ob_ref, sf_ref, st_ref = refs[2 * n_in + n_tab + 1:]
    i = pl.program_id(1)

    @pl.when(i == 0)
    def _():
        st_ref[...] = s0_ref[...]

    chains = [(dr, b) for b in range(bb) for dr in range(2)]
    ins_of = lambda dr, b: [t[b] for t in (fwd, bwd)[dr]]
    if kind == "rwkv":
        res = _rwkv_chunks([(*ins_of(dr, b), st_ref[dr, b], dr == 1) for dr, b in chains])
    elif kind == "gla":
        res = _gla_chunks([(*ins_of(dr, b), st_ref[dr, b], dr == 1) for dr, b in chains])
    else:
        res = _ret_chunks([(*ins_of(dr, b), st_ref[dr, b], *(t[dr] for t in tabs))
                           for dr, b in chains])
    for (dr, b), (o, s_new) in zip(chains, res):
        (of_ref, ob_ref)[dr][b] = o
        st_ref[dr, b] = s_new

    @pl.when(i == nc - 1)
    def _():
        sf_ref[...] = st_ref[...]


_SCAN_ROWS = {"gla": 4, "ret": 4, "rwkv": 8}


def _scan(kind, feats, cols_f, cols_b, s0, tabs=()):
    B, T, _ = feats.shape
    c = min(SCAN_CHUNK[kind], T)
    nc = T // c
    bb = math.gcd(_SCAN_ROWS[kind], B)
    n_in = len(cols_f)
    in_specs = []
    for j in cols_f:
        in_specs.append(pl.BlockSpec((bb, c, BR_W), lambda b, i, j=j: (b, i, j)))
    for j in cols_b:
        in_specs.append(pl.BlockSpec((bb, c, BR_W), lambda b, i, j=j: (b, nc - 1 - i, j)))
    for t in tabs:
        in_specs.append(pl.BlockSpec(t.shape, lambda b, i, nd=t.ndim: (0,) * nd))
    in_specs.append(pl.BlockSpec((2, bb, BR_W, BR_W), lambda b, i: (0, b, 0, 0)))
    kern = functools.partial(_scan_kernel, kind, n_in, len(tabs), nc, bb)
    return pl.pallas_call(
        kern,
        grid=(B // bb, nc),
        in_specs=in_specs,
        out_specs=[
            pl.BlockSpec((bb, c, BR_W), lambda b, i: (b, i, 0)),
            pl.BlockSpec((bb, c, BR_W), lambda b, i: (b, nc - 1 - i, 0)),
            pl.BlockSpec((2, bb, BR_W, BR_W), lambda b, i: (0, b, 0, 0)),
        ],
        out_shape=[
            jax.ShapeDtypeStruct((B, T, BR_W), F32),
            jax.ShapeDtypeStruct((B, T, BR_W), F32),
            jax.ShapeDtypeStruct((2, B, BR_W, BR_W), F32),
        ],
        scratch_shapes=[pltpu.VMEM((2, bb, BR_W, BR_W), F32)],
        compiler_params=_cparams(("parallel", "arbitrary")),
        name=kind,
    )(*([feats] * (2 * n_in)), *tabs, s0)


def _ret_tables(c):
    pos = np.arange(c, dtype=np.float64)
    lane_head = np.arange(BR_W) // HEAD_DIM
    dmats, qd, kd, cd = [], [], [], []
    for dr in range(2):
        expo = -5.0 - np.arange(HEADS, dtype=np.float64)
        if dr == 1:
            expo = expo[::-1]
        log_g = np.log1p(-np.exp2(expo))
        if dr == 0:
            rel = pos[:, None] - pos[None, :]
            qpow = pos + 1.0
            kpow = c - 1.0 - pos
        else:
            rel = pos[None, :] - pos[:, None]
            qpow = c - pos
            kpow = pos
        tri = rel >= 0
        dm = np.where(tri[None], np.exp(np.where(tri, rel, 0.0)[None] * log_g[:, None, None]), 0.0)
        dmats.append(np.concatenate([dm[h] for h in range(HEADS)], axis=1))
        qd.append(np.exp(qpow[:, None] * log_g[lane_head][None, :]))
        kd.append(np.exp(kpow[:, None] * log_g[lane_head][None, :]))
        cd.append(np.exp(c * log_g[lane_head])[None, :])
    f = lambda xs: jnp.asarray(np.stack(xs), dtype=F32)
    return f(dmats), f(qd), f(kd), f(cd)


def _fnet_dense_kernel(t_len, cs_ref, g1_ref, g2_ref, o_ref):
    o_ref[0] = (jnp.dot(cs_ref[:, 0:t_len], g1_ref[0].astype(BF16), preferred_element_type=F32)
                + jnp.dot(cs_ref[:, t_len:2 * t_len], g2_ref[0].astype(BF16),
                          preferred_element_type=F32))


def _fnet_dense(g1, g2, cs):
    B, T, _ = g1.shape
    tb = min(512, T)
    return pl.pallas_call(
        functools.partial(_fnet_dense_kernel, T),
        grid=(T // tb, B),
        in_specs=[
            pl.BlockSpec((tb, 2 * T), lambda i, b: (i, 0)),
            pl.BlockSpec((1, T, BR_W), lambda i, b: (b, 0, 0)),
            pl.BlockSpec((1, T, BR_W), lambda i, b: (b, 0, 0)),
        ],
        out_specs=pl.BlockSpec((1, tb, BR_W), lambda i, b: (b, i, 0)),
        out_shape=jax.ShapeDtypeStruct((B, T, BR_W), F32),
        compiler_params=_cparams(("arbitrary", "arbitrary")),
        name="fnet",
    )(cs, g1, g2)


FFT_R = 8


def _fft_kernel(n, k1_ref, k2_ref, tc_ref, ts_ref, g1_ref, g2_ref, o_ref, zr_ref, zi_ref):
    m = n * FFT_R
    for j in range(n // FFT_R):
        sl = slice(j * FFT_R, (j + 1) * FFT_R)
        x = jnp.concatenate([g1_ref[0, :, sl, :].reshape(m, BR_W),
                             g2_ref[0, :, sl, :].reshape(m, BR_W)], axis=1).astype(BF16)
        y = jnp.dot(k1_ref[...], x, preferred_element_type=F32)
        yr = y[0:m, 0:BR_W] - y[m:2 * m, BR_W:2 * BR_W]
        yi = -y[0:m, BR_W:2 * BR_W] - y[m:2 * m, 0:BR_W]
        tc = tc_ref[:, sl, :].reshape(m, BR_W)
        ts = ts_ref[:, sl, :].reshape(m, BR_W)
        zr_ref[:, sl, :] = (tc * yr + ts * yi).reshape(n, FFT_R, BR_W)
        zi_ref[:, sl, :] = (tc * yi - ts * yr).reshape(n, FFT_R, BR_W)
    for j in range(n // FFT_R):
        sl = slice(j * FFT_R, (j + 1) * FFT_R)
        z = jnp.concatenate([zr_ref[sl].reshape(m, BR_W), zi_ref[sl].reshape(m, BR_W)],
                            axis=0).astype(BF16)
        o = jnp.dot(k2_ref[...], z, preferred_element_type=F32)
        o_ref[0, :, sl, :] = o.reshape(n, FFT_R, BR_W)


def _fnet_fft(g1, g2):
    B, T, _ = g1.shape
    n = int(round(math.sqrt(T)))
    assert n * n == T and n % FFT_R == 0
    m = n * FFT_R
    idx = np.arange(n)
    a_n = 2.0 * np.pi * ((idx[:, None] * idx[None, :]) % n) / n
    eye = np.eye(FFT_R)
    k1 = np.concatenate([np.einsum("ap,ij->aipj", f, eye).reshape(m, m)
                         for f in (np.cos(a_n), np.sin(a_n))], axis=0)
    k2 = np.concatenate([np.einsum("pr,ij->pijr", f, eye).reshape(m, m)
                         for f in (np.cos(a_n), np.sin(a_n))], axis=1)
    a_t = 2.0 * np.pi * (idx[:, None] * idx[None, :]) / T
    tc = jnp.broadcast_to(jnp.asarray(np.cos(a_t), dtype=F32)[:, :, None], (n, n, BR_W))
    ts = jnp.broadcast_to(jnp.asarray(np.sin(a_t), dtype=F32)[:, :, None], (n, n, BR_W))
    const2 = lambda b: (0, 0)
    const3 = lambda b: (0, 0, 0)
    one = pl.Buffered(1)
    o = pl.pallas_call(
        functools.partial(_fft_kernel, n),
        grid=(B,),
        in_specs=[
            pl.BlockSpec((2 * m, m), const2, pipeline_mode=one),
            pl.BlockSpec((m, 2 * m), const2, pipeline_mode=one),
            pl.BlockSpec((n, n, BR_W), const3, pipeline_mode=one),
            pl.BlockSpec((n, n, BR_W), const3, pipeline_mode=one),
            pl.BlockSpec((1, n, n, BR_W), lambda b: (b, 0, 0, 0)),
            pl.BlockSpec((1, n, n, BR_W), lambda b: (b, 0, 0, 0)),
        ],
        out_specs=pl.BlockSpec((1, n, n, BR_W), lambda b: (b, 0, 0, 0)),
        out_shape=jax.ShapeDtypeStruct((B, n, n, BR_W), F32),
        scratch_shapes=[pltpu.VMEM((n, n, BR_W), F32), pltpu.VMEM((n, n, BR_W), F32)],
        compiler_params=_cparams(("arbitrary",)),
        name="fft",
    )(jnp.asarray(k1, dtype=BF16), jnp.asarray(k2, dtype=BF16), tc, ts,
      g1.reshape(B, n, n, BR_W), g2.reshape(B, n, n, BR_W))
    return o.reshape(B, T, BR_W)


def _time_dft_table(T):
    t = np.arange(T)
    ang = ((t[:, None] * t[None, :]) % T) * (2.0 * np.pi / T)
    return jnp.asarray(np.concatenate([np.cos(ang), -np.sin(ang)], axis=1), dtype=BF16)


def _channel_dft_tables(T):
    ch = np.arange(HEAD_DIM)
    a64 = 2.0 * np.pi * ((ch[:, None] * ch[None, :]) % HEAD_DIM) / HEAD_DIM
    scale = (T * HEAD_DIM) ** -0.5
    eye = np.eye(HEADS)
    cbd = np.kron(eye, np.cos(a64)) * scale
    sbd = np.kron(eye, np.sin(a64)) * scale
    return jnp.asarray(np.stack([cbd, sbd]), dtype=BF16)


def _head_norm(o, ones_h, center):
    if center:
        o = o - _dot(o, ones_h) * (1.0 / HEAD_DIM)
    var = _dot(o * o, ones_h) * (1.0 / HEAD_DIM)
    return o * lax.rsqrt(var + EPS)


def _merge_kernel(x_ref, hb_ref, mod_ref, gg_ref, rg_ref, wg_ref, bon_ref,
                  ogf_ref, ogb_ref, orf_ref, orb_ref, oyf_ref, oyb_ref, fn_ref,
                  gn_ref, wgate_ref, bgate_ref, wbr_ref, wout_ref, o_ref):
    ones_h = _head_ones()
    gn = gn_ref[...]
    gla = _head_norm(ogf_ref[0] + ogb_ref[0], ones_h, False) * gn[0:1] * _silu(gg_ref[0])
    ret = _head_norm(orf_ref[0] + orb_ref[0], ones_h, True) * gn[1:2] * _silu(rg_ref[0])
    rwkv = (_head_norm(oyf_ref[0] + oyb_ref[0], ones_h, True) * gn[2:3] + bon_ref[0]) * wg_ref[0]
    outs = (gla, ret, rwkv, fn_ref[0])
    hb = hb_ref[0]
    z = None
    for br in range(4):
        gate = _sigmoid(jnp.dot(hb, wgate_ref[br], preferred_element_type=F32) + bgate_ref[br])
        term = gate * _dot(outs[br], wbr_ref[br])
        z = term if z is None else z + term
    y = _dot(z, wout_ref[...])
    gate1 = mod_ref[0][:, 2 * D_MODEL:3 * D_MODEL]
    o_ref[0] = x_ref[0] + gate1 * y


def _merge(x, hb, mods, feats, scans, fnet_out, lw):
    B, T, D = x.shape
    tb = min(512, T)
    tok = lambda w: pl.BlockSpec((1, tb, w), lambda b, i: (b, i, 0))
    fcol = lambda j: pl.BlockSpec((1, tb, BR_W), lambda b, i, j=j: (b, i, j))
    const2 = lambda b, i: (0, 0)
    const3 = lambda b, i: (0, 0, 0)
    return pl.pallas_call(
        _merge_kernel,
        grid=(B, T // tb),
        in_specs=[
            tok(D), tok(D),
            pl.BlockSpec((1, 1, 6 * D), lambda b, i: (b, 0, 0)),
            fcol(F_GG), fcol(F_RG), fcol(F_WG), fcol(F_BON),
            tok(BR_W), tok(BR_W), tok(BR_W), tok(BR_W), tok(BR_W), tok(BR_W), tok(BR_W),
            pl.BlockSpec((3, BR_W), const2),
            pl.BlockSpec((4, D, D), const3, pipeline_mode=pl.Buffered(1)),
            pl.BlockSpec((4, 1, D), const3),
            pl.BlockSpec((4, BR_W, D), const3, pipeline_mode=pl.Buffered(1)),
            pl.BlockSpec((D, D), const2, pipeline_mode=pl.Buffered(1)),
        ],
        out_specs=tok(D),
        out_shape=jax.ShapeDtypeStruct((B, T, D), F32),
        compiler_params=_cparams(("parallel", "arbitrary")),
        name="merge",
    )(x, hb, mods, feats, feats, feats, feats, *scans, fnet_out,
      lw["gn"], lw["w_gate"], lw["b_gate"], lw["w_br"], lw["w_out"])


FFN_COL_CHUNKS = (1024, 1024, 768)


def _ffn_kernel(final, tb, nt,
                x_ref, xp_ref, xn_ref, mod_ref, g2_ref, up_ref, cw_ref, cb_ref,
                down_ref, gfin_ref, o_ref, act_ref):
    i = pl.program_id(1)
    n_ext = tb + 2 * HALO
    x_ext = jnp.concatenate([xp_ref[0], x_ref[0], xn_ref[0]], axis=0)
    mod = mod_ref[0]
    shift = mod[:, 3 * D_MODEL:4 * D_MODEL]
    scale = mod[:, 4 * D_MODEL:5 * D_MODEL]
    h2 = (_rms_rows(x_ext) * g2_ref[...]) * (1.0 + scale) + shift
    h2 = h2 * _halo_keep(n_ext, tb, i, nt)
    h2_ext = h2.astype(BF16)
    h2_mid = h2[HALO:HALO + tb].astype(BF16)
    cw = cw_ref[...]
    cb = cb_ref[...]
    lo = 0
    for width in FFN_COL_CHUNKS:
        a_ext = jnp.dot(h2_ext, up_ref[:, lo:lo + width], preferred_element_type=F32)
        u = jnp.dot(h2_mid, up_ref[:, D_FF + lo:D_FF + lo + width], preferred_element_type=F32)
        a = (a_ext[HALO - 1:HALO - 1 + tb] * cw[0:1, lo:lo + width]
             + a_ext[HALO:HALO + tb] * cw[1:2, lo:lo + width]
             + a_ext[HALO + 1:HALO + 1 + tb] * cw[2:3, lo:lo + width] + cb[:, lo:lo + width])
        act_ref[:, lo:lo + width] = (_silu(a) * u).astype(BF16)
        lo += width
    y = jnp.dot(act_ref[...], down_ref[...], preferred_element_type=F32)
    gate2 = mod[:, 5 * D_MODEL:6 * D_MODEL]
    res = x_ref[0] + gate2 * y
    if final:
        res = _rms_rows(res) * gfin_ref[...]
    o_ref[0] = res


def _ffn(x, mods, lw, g_final, final):
    B, T, D = x.shape
    tb = min(512, T)
    nt = T // tb
    hb8 = tb // HALO
    n_h = T // HALO
    const2 = lambda b, i: (0, 0)
    resident = lambda shape: pl.BlockSpec(shape, const2, pipeline_mode=pl.Buffered(1))
    kern = functools.partial(_ffn_kernel, final, tb, nt)
    return pl.pallas_call(
        kern,
        grid=(B, nt),
        in_specs=[
            pl.BlockSpec((1, tb, D), lambda b, i: (b, i, 0)),
            pl.BlockSpec((1, HALO, D), lambda b, i: (b, jnp.maximum(i * hb8 - 1, 0), 0)),
            pl.BlockSpec((1, HALO, D), lambda b, i: (b, jnp.minimum((i + 1) * hb8, n_h - 1), 0)),
            pl.BlockSpec((1, 1, 6 * D), lambda b, i: (b, 0, 0)),
            pl.BlockSpec((1, D), const2),
            resident((D, 2 * D_FF)),
            pl.BlockSpec((3, D_FF), const2),
            pl.BlockSpec((1, D_FF), const2),
            resident((D_FF, D)),
            pl.BlockSpec((1, D), const2),
        ],
        out_specs=pl.BlockSpec((1, tb, D), lambda b, i: (b, i, 0)),
        out_shape=jax.ShapeDtypeStruct((B, T, D), F32),
        scratch_shapes=[pltpu.VMEM((tb, D_FF), BF16)],
        compiler_params=_cparams(("parallel", "arbitrary")),
        name="ffn",
    )(x, x, x, mods, lw["g2"], lw["ffn_up"], lw["ffn_conv"], lw["ffn_conv_b"],
      lw["ffn_down"], g_final)


def _rope_tables(T):
    rows = T // GRID_W
    row = jnp.repeat(jnp.arange(rows, dtype=F32), GRID_W)
    colp = jnp.tile(jnp.arange(GRID_W, dtype=F32), rows)
    n_freq = HEAD_DIM // 4
    inv = ROPE_BASE ** (-jnp.arange(n_freq, dtype=F32) / n_freq)
    ang = jnp.concatenate([row[:, None] * inv, colp[:, None] * inv], axis=-1)
    cos, sin = jnp.cos(ang), jnp.sin(ang)
    cos_h = jnp.concatenate([cos, cos], axis=-1)
    sin_h = jnp.concatenate([-sin, sin], axis=-1)
    return jnp.tile(cos_h, (1, HEADS)), jnp.tile(sin_h, (1, HEADS))


def _pad_rows(w, lo, n):
    return jnp.zeros((n, w.shape[1]), w.dtype).at[lo:lo + w.shape[0]].set(w)


def _layer_weights(l, p):
    D = D_MODEL
    bf = lambda t: t.astype(BF16)
    zcols = lambda w, n: jnp.concatenate([w, jnp.zeros((w.shape[0], n - w.shape[1]), w.dtype)], axis=1)
    w_h = zcols(jnp.concatenate([p["gla_wa1"][l, 0], p["gla_wa1"][l, 1]], axis=1), 128)
    w_xw = jnp.concatenate([p["rwkv_w1"][l, 0], p["rwkv_w1"][l, 1]], axis=1)
    w_xa = jnp.concatenate([p["rwkv_a1"][l, 0], p["rwkv_a1"][l, 1]], axis=1)
    w_xg = zcols(p["rwkv_g1"][l], 256)
    mu = p["rwkv_mu"][l]
    w_lr = jnp.concatenate([w_xg, mu[2][:, None] * w_xg, w_h, w_xw, w_xa,
                            mu[0][:, None] * w_xw, mu[1][:, None] * w_xa,
                            jnp.zeros((D, LR_COLS - LR_AS - 128), F32)], axis=1)
    w2 = jnp.stack([
        _pad_rows(p["gla_wa2"][l, 0], 0, 256), _pad_rows(p["gla_wa2"][l, 1], GLA_LR, 256),
        _pad_rows(p["rwkv_w2"][l, 0], 0, 256), _pad_rows(p["rwkv_w2"][l, 1], RWKV_LR, 256),
        _pad_rows(p["rwkv_a2"][l, 0], 0, 256), _pad_rows(p["rwkv_a2"][l, 1], RWKV_LR, 256),
        _pad_rows(p["rwkv_g2"][l], 0, 256),
    ])
    vec = jnp.stack([p["gla_ba"][l, 0], p["gla_ba"][l, 1], p["rwkv_w0"][l, 0], p["rwkv_w0"][l, 1],
                     p["rwkv_a0"][l, 0], p["rwkv_a0"][l, 1], p["rwkv_kk"][l], p["rwkv_ka"][l],
                     p["rwkv_rk"][l]])
    vec = jnp.concatenate([vec, jnp.zeros((16 - vec.shape[0], BR_W), F32)], axis=0)
    return {
        "g1": p["g_norm1"][l].reshape(1, D), "g2": p["g_norm2"][l].reshape(1, D),
        "w_in": bf(p["w_in"][l]), "w_lr": bf(w_lr), "w2": bf(w2), "vec": vec,
        "rwkv_conv": p["rwkv_conv"][l],
        "gn": jnp.stack([p["gla_gn"][l], p["ret_gn"][l], p["rwkv_gn"][l]]),
        "w_gate": bf(p["w_gate"][l]), "b_gate": p["b_gate"][l].reshape(4, 1, D),
        "w_br": bf(p["w_br"][l]), "w_out": bf(p["w_out"][l]),
        "ffn_up": bf(p["ffn_up"][l]), "ffn_conv": p["ffn_conv"][l],
        "ffn_conv_b": p["ffn_conv_b"][l].reshape(1, D_FF), "ffn_down": bf(p["ffn_down"][l]),
    }


_GLA_F = (F_GQ, F_GK, F_GV, F_LAF)
_GLA_B = (F_GQ, F_GK, F_GV, F_LAB)
_RET = (F_RQ, F_RK, F_RV)
_RWKV_F = (F_WR, F_WV, F_WKK, F_KDF, F_ASF, F_LWF)
_RWKV_B = (F_WR, F_WV, F_WKK, F_KDB, F_ASB, F_LWB)


def _mixers(feats_c, feats_l, ret_tabs_c, ret_tabs_l, need_ctx):
    B = feats_l.shape[0]
    zero = jnp.zeros((2, B, BR_W, BR_W), F32)
    out_c, out_l = [], []
    for kind, cf, cb, tabs in (("gla", _GLA_F, _GLA_B, None), ("ret", _RET, _RET, True),
                               ("rwkv", _RWKV_F, _RWKV_B, None)):
        tc = ret_tabs_c if tabs else ()
        tl = ret_tabs_l if tabs else ()
        of_c, ob_c, s_c = _scan(kind, feats_c, cf, cb, zero, tc)
        of_l, ob_l, _ = _scan(kind, feats_l, cf, cb, s_c, tl)
        out_c += [of_c, ob_c]
        out_l += [of_l, ob_l]
    return out_l, (out_c if need_ctx else None)


def kernel(x, c, ctx, c_ctx, w_ada, b_ada, g_norm1, g_norm2, w_in, gla_wa1, gla_wa2, gla_ba, gla_gn, ret_gn, rwkv_conv, rwkv_mu, rwkv_w0, rwkv_w1, rwkv_w2, rwkv_a0, rwkv_a1, rwkv_a2, rwkv_g1, rwkv_g2, rwkv_kk, rwkv_ka, rwkv_rk, rwkv_gn, w_gate, b_gate, w_br, w_out, ffn_up, ffn_conv, ffn_conv_b, ffn_down, g_final):
    p = dict(g_norm1=g_norm1, g_norm2=g_norm2, w_in=w_in, gla_wa1=gla_wa1, gla_wa2=gla_wa2,
             gla_ba=gla_ba, gla_gn=gla_gn, ret_gn=ret_gn, rwkv_conv=rwkv_conv, rwkv_mu=rwkv_mu,
             rwkv_w0=rwkv_w0, rwkv_w1=rwkv_w1, rwkv_w2=rwkv_w2, rwkv_a0=rwkv_a0,
             rwkv_a1=rwkv_a1, rwkv_a2=rwkv_a2, rwkv_g1=rwkv_g1, rwkv_g2=rwkv_g2,
             rwkv_kk=rwkv_kk, rwkv_ka=rwkv_ka, rwkv_rk=rwkv_rk, rwkv_gn=rwkv_gn,
             w_gate=w_gate, b_gate=b_gate, w_br=w_br, w_out=w_out, ffn_up=ffn_up,
             ffn_conv=ffn_conv, ffn_conv_b=ffn_conv_b, ffn_down=ffn_down)
    B, T, D = x.shape
    Tc = ctx.shape[1]
    depth = w_ada.shape[0]

    cmat = jnp.concatenate([c, c_ctx[None, :], jnp.zeros((16 - B - 1, D), F32)], axis=0)
    mod_all = _modulation(cmat, w_ada, b_ada)

    rope_l = _rope_tables(T)
    rope_c = (jnp.zeros((Tc, BR_W), F32), jnp.zeros((Tc, BR_W), F32))
    dft_l = _channel_dft_tables(T)
    dft_c = _channel_dft_tables(Tc)
    cs_c = _time_dft_table(Tc)
    ret_tabs_l = _ret_tables(min(SCAN_CHUNK["ret"], T))
    ret_tabs_c = _ret_tables(min(SCAN_CHUNK["ret"], Tc))
    g_fin = g_final.reshape(1, D)

    for l in range(depth):
        last = l == depth - 1
        lw = _layer_weights(l, p)
        mods_l = mod_all[l, 0:B].reshape(B, 1, 6 * D)
        mods_c = jnp.broadcast_to(mod_all[l, B:B + 1].reshape(1, 1, 6 * D), (B, 1, 6 * D))

        feats_l, hb_l, g1_l, g2_l = _pre(x, mods_l, lw, True, rope_l, dft_l)
        feats_c, hb_c, g1_c, g2_c = _pre(ctx, mods_c, lw, False, rope_c, dft_c)
        scans_l, scans_c = _mixers(feats_c, feats_l, ret_tabs_c, ret_tabs_l, not last)

        fn_l = _fnet_fft(g1_l, g2_l)
        x = _merge(x, hb_l, mods_l, feats_l, scans_l, fn_l, lw)
        x = _ffn(x, mods_l, lw, g_fin, last)
        if not last:
            fn_c = _fnet_dense(g1_c, g2_c, cs_c)
            ctx = _merge(ctx, hb_c, mods_c, feats_c, scans_c, fn_c, lw)
            ctx = _ffn(ctx, mods_c, lw, g_fin, False)
    return x
```

```python
import functools
import math

import numpy as np
import jax
import jax.numpy as jnp
from jax import lax
from jax.experimental import pallas as pl
from jax.experimental.pallas import tpu as pltpu

F32 = jnp.float32
BF16 = jnp.bfloat16

D_MODEL = 1024
BR_W = 256
HEAD_DIM = 64
HEADS = 4
N_PARTS = 12
P_IN = N_PARTS * BR_W
GRID_W = 64
GLA_LR = 16
GLA_NORMALIZER = 16.0
RWKV_LR = 64
RWKV_G_LR = 160
D_FF = 2816
ROPE_BASE = 10000.0
EPS = 1e-6

SCAN_CHUNK = {"gla": 64, "ret": 128, "rwkv": 64}
GLA_SUB = 16
HALO = 8
NEG_BIG = -1e30
LOG2E = 1.4426950408889634

(F_GQ, F_GK, F_GV, F_GG, F_RQ, F_RK, F_RV, F_RG, F_WR, F_WV, F_WKK, F_WG, F_BON,
 F_LAF, F_LAB, F_LWF, F_LWB, F_KDF, F_KDB, F_ASF, F_ASB) = range(21)
N_FEAT = 21

LR_G, LR_GS, LR_H, LR_W, LR_A, LR_WS, LR_AS = 0, 256, 512, 640, 768, 896, 1024
LR_COLS = 1280

TOKEN_TILE = 512
MOD_ROWS = 16
MOD_COL_BLOCK = 1536
VMEM_LIMIT = 56 * 1024 * 1024


def _cparams(sem):
    return pltpu.CompilerParams(dimension_semantics=sem, vmem_limit_bytes=VMEM_LIMIT)


def _dot(a, b):
    return jnp.dot(a.astype(BF16), b.astype(BF16), preferred_element_type=F32)


def _dot_nt(a, b):
    return lax.dot_general(a.astype(BF16), b.astype(BF16), (((1,), (1,)), ((), ())),
                           preferred_element_type=F32)


def _dot_tn(a, b):
    return lax.dot_general(a.astype(BF16), b.astype(BF16), (((0,), (0,)), ((), ())),
                           preferred_element_type=F32)


def _split3(x):
    hi = x.astype(BF16)
    r1 = x - hi.astype(F32)
    mid = r1.astype(BF16)
    lo = (r1 - mid.astype(F32)).astype(BF16)
    return hi, mid, lo


def _dot_exact_lhs(a_bf16, x):
    hi, mid, lo = _split3(x)
    d = lambda t: jnp.dot(a_bf16, t, preferred_element_type=F32)
    return d(hi) + d(mid) + d(lo)


def _sigmoid(x):
    return 1.0 / (1.0 + jnp.exp(-x))


def _silu(x):
    return x * _sigmoid(x)


def _softplus(x):
    return jnp.maximum(x, 0.0) + jnp.log(1.0 + jnp.exp(-jnp.abs(x)))


def _iota(shape, dim):
    return lax.broadcasted_iota(jnp.int32, shape, dim)


def _head_ones():
    r = _iota((BR_W, BR_W), 0) // HEAD_DIM
    c = _iota((BR_W, BR_W), 1) // HEAD_DIM
    return jnp.where(r == c, 1.0, 0.0).astype(BF16)


def _stack_heads(x):
    c, w = x.shape
    xs = jnp.concatenate([x] * HEADS, axis=0)
    keep = (_iota((HEADS * c, w), 0) // c) == (_iota((HEADS * c, w), 1) // (w // HEADS))
    return jnp.where(keep, xs, 0.0)


def _tri(c, rev, strict=False):
    r = _iota((c, c), 0)
    s = _iota((c, c), 1)
    if rev:
        return (r < s) if strict else (r <= s)
    return (r > s) if strict else (r >= s)


def _halo_keep(n_ext, tb, i, nt):
    row = _iota((n_ext, 1), 0)
    first = jnp.where(i > 0, 1.0, 0.0)
    last = jnp.where(i < nt - 1, 1.0, 0.0)
    return jnp.where(row < HALO, first, jnp.where(row >= tb + HALO, last, 1.0))


def _rms_rows(x):
    return x * lax.rsqrt(jnp.mean(x * x, axis=-1, keepdims=True) + EPS)


def _mod_kernel(c_ref, w_ref, b_ref, o_ref):
    s = _silu(c_ref[...])
    o_ref[0] = _dot(s, w_ref[0]) + b_ref[0]


def _modulation(cmat, w_ada, b_ada):
    L = w_ada.shape[0]
    nblk = MOD_COL_BLOCK
    return pl.pallas_call(
        _mod_kernel,
        grid=(L, 6 * D_MODEL // nblk),
        in_specs=[
            pl.BlockSpec((MOD_ROWS, D_MODEL), lambda l, j: (0, 0)),
            pl.BlockSpec((1, D_MODEL, nblk), lambda l, j: (l, 0, j)),
            pl.BlockSpec((1, 1, nblk), lambda l, j: (l, 0, j)),
        ],
        out_specs=pl.BlockSpec((1, MOD_ROWS, nblk), lambda l, j: (l, 0, j)),
        out_shape=jax.ShapeDtypeStruct((L, MOD_ROWS, 6 * D_MODEL), F32),
        compiler_params=_cparams(("arbitrary", "arbitrary")),
        name="mod",
    )(cmat, w_ada, b_ada.reshape(L, 1, 6 * D_MODEL))


def _pre_kernel(use_rope, tb, nt,
                x_ref, xp_ref, xn_ref, mod_ref, g1_ref, win_ref, wlr_ref,
                w2_ref, vec_ref, conv_ref, dft_ref, cos_ref, sin_ref,
                f_ref, hb_ref, fc_ref, fs_ref, lr_ref, ps_ref):
    i = pl.program_id(1)
    n_ext = tb + 2 * HALO
    x_ext = jnp.concatenate([xp_ref[0], x_ref[0], xn_ref[0]], axis=0)
    mod = mod_ref[0]
    shift = mod[:, 0:D_MODEL]
    scale = mod[:, D_MODEL:2 * D_MODEL]
    h_ext = (_rms_rows(x_ext) * g1_ref[...]) * (1.0 + scale) + shift
    h_ext = h_ext * _halo_keep(n_ext, tb, i, nt)
    hb_ext = h_ext.astype(BF16)
    hb = h_ext[HALO:HALO + tb].astype(BF16)
    hb_ref[0] = hb
    lr_ref[...] = jnp.dot(hb_ext, wlr_ref[...], preferred_element_type=F32)
    ps_ref[...] = jnp.dot(hb_ext, win_ref[:, 8 * BR_W:11 * BR_W], preferred_element_type=F32)

    def mid(lo, hi):
        return lr_ref[HALO:HALO + tb, lo:hi]

    def shifted(lo, hi):
        return (0.5 * (lr_ref[HALO - 1:HALO - 1 + tb, lo:hi] + lr_ref[HALO + 1:HALO + 1 + tb, lo:hi])
                - lr_ref[HALO:HALO + tb, lo:hi])

    vec = vec_ref[...]
    ba_f, ba_b, w0_f, w0_b = vec[0:1], vec[1:2], vec[2:3], vec[3:4]
    a0_f, a0_b, kkw, kaw, rkw = vec[4:5], vec[5:6], vec[6:7], vec[7:8], vec[8:9]

    def put(j, val):
        f_ref[0, :, j * BR_W:(j + 1) * BR_W] = val

    zh = mid(LR_H, LR_H + 128)
    tw = jnp.tanh(mid(LR_W, LR_W + 128) + shifted(LR_WS, LR_WS + 128))
    ta = mid(LR_A, LR_A + 128) + shifted(LR_AS, LR_AS + 128)
    sg = _sigmoid(mid(LR_G, LR_G + 256) + shifted(LR_GS, LR_GS + 256))

    p_gla = jnp.dot(hb, win_ref[:, 0:4 * BR_W], preferred_element_type=F32)
    put(F_GQ, p_gla[:, 0:BR_W] * HEAD_DIM ** -0.5)
    put(F_GK, p_gla[:, BR_W:2 * BR_W])
    put(F_GV, p_gla[:, 2 * BR_W:3 * BR_W])
    put(F_GG, p_gla[:, 3 * BR_W:4 * BR_W])

    z_la = [_dot(zh, w2_ref[dr, 0:128, :]) for dr in range(2)]
    z_w = [_dot(tw, w2_ref[2 + dr, 0:128, :]) for dr in range(2)]
    z_a = [_dot(ta, w2_ref[4 + dr, 0:128, :]) for dr in range(2)]
    put(F_WG, _dot(sg, w2_ref[6]))

    for dr, (ba, dst) in enumerate(((ba_f, F_LAF), (ba_b, F_LAB))):
        put(dst, -_softplus(-(z_la[dr] + ba)) * (1.0 / GLA_NORMALIZER))

    p_ret = jnp.dot(hb, win_ref[:, 4 * BR_W:8 * BR_W], preferred_element_type=F32)

    conv = conv_ref[...]

    def dwconv(part):
        lo, hi = part * BR_W, (part + 1) * BR_W
        cw = conv[:, lo:hi]
        return (ps_ref[HALO - 1:HALO - 1 + tb, lo:hi] * cw[0:1]
                + ps_ref[HALO:HALO + tb, lo:hi] * cw[1:2]
                + ps_ref[HALO + 1:HALO + 1 + tb, lo:hi] * cw[2:3])

    r_c = dwconv(0)
    k_c = dwconv(1)
    v_c = dwconv(2)
    ones_h = _head_ones()
    kk = k_c * kkw
    kk = kk * lax.rsqrt(_dot(kk * kk, ones_h) + EPS)
    put(F_WR, r_c)
    put(F_WV, v_c)
    put(F_WKK, kk)

    p_fnet = jnp.dot(hb, win_ref[:, 11 * BR_W:12 * BR_W], preferred_element_type=F32)

    bonus = jnp.zeros((tb, BR_W), F32)
    for dr, (w0, a0, d_lw, d_kd, d_as) in enumerate(
            ((w0_f, a0_f, F_LWF, F_KDF, F_ASF), (w0_b, a0_b, F_LWB, F_KDB, F_ASB))):
        w_raw = -_softplus(-(w0 + z_w[dr])) - 0.5
        put(d_lw, -jnp.exp(w_raw))
        a_sig = _sigmoid(a0 + z_a[dr])
        kd = k_c * (1.0 + (a_sig - 1.0) * kaw)
        put(d_kd, kd)
        put(d_as, a_sig)
        bonus = bonus + _dot(r_c * kd * rkw, ones_h) * v_c
    put(F_BON, bonus)

    rq = p_ret[:, 0:BR_W]
    rk = p_ret[:, BR_W:2 * BR_W] * HEAD_DIM ** -0.5
    if use_rope:
        cosf = cos_ref[...]
        sins = sin_ref[...]
        low = (_iota((tb, BR_W), 1) % HEAD_DIM) < (HEAD_DIM // 2)

        def rope(t):
            partner = jnp.where(low, pltpu.roll(t, BR_W - HEAD_DIM // 2, 1),
                                pltpu.roll(t, HEAD_DIM // 2, 1))
            return t * cosf + partner * sins

        rq = rope(rq)
        rk = rope(rk)
    put(F_RQ, rq)
    put(F_RK, rk)
    put(F_RV, p_ret[:, 2 * BR_W:3 * BR_W])
    put(F_RG, p_ret[:, 3 * BR_W:4 * BR_W])

    fb = p_fnet.astype(BF16)
    fc_ref[0] = jnp.dot(fb, dft_ref[0], preferred_element_type=F32)
    fs_ref[0] = jnp.dot(fb, dft_ref[1], preferred_element_type=F32)


def _pre(x, mods, lw, use_rope, rope_tabs, dft_c):
    B, T, D = x.shape
    tb = min(TOKEN_TILE, T)
    nt = T // tb
    hb8 = tb // HALO
    n_h = T // HALO
    const2 = lambda b, i: (0, 0)
    const3 = lambda b, i: (0, 0, 0)
    kern = functools.partial(_pre_kernel, use_rope, tb, nt)
    return pl.pallas_call(
        kern,
        grid=(B, nt),
        in_specs=[
            pl.BlockSpec((1, tb, D), lambda b, i: (b, i, 0)),
            pl.BlockSpec((1, HALO, D), lambda b, i: (b, jnp.maximum(i * hb8 - 1, 0), 0)),
            pl.BlockSpec((1, HALO, D), lambda b, i: (b, jnp.minimum((i + 1) * hb8, n_h - 1), 0)),
            pl.BlockSpec((1, 1, 6 * D), lambda b, i: (b, 0, 0)),
            pl.BlockSpec((1, D), const2),
            pl.BlockSpec((D, P_IN), const2, pipeline_mode=pl.Buffered(1)),
            pl.BlockSpec((D, LR_COLS), const2, pipeline_mode=pl.Buffered(1)),
            pl.BlockSpec((7, 256, BR_W), const3, pipeline_mode=pl.Buffered(1)),
            pl.BlockSpec((16, BR_W), const2),
            pl.BlockSpec((3, 3 * BR_W), const2),
            pl.BlockSpec((2, BR_W, BR_W), const3),
            pl.BlockSpec((tb, BR_W), lambda b, i: (i, 0)),
            pl.BlockSpec((tb, BR_W), lambda b, i: (i, 0)),
        ],
        out_specs=[
            pl.BlockSpec((1, tb, N_FEAT * BR_W), lambda b, i: (b, i, 0)),
            pl.BlockSpec((1, tb, D), lambda b, i: (b, i, 0)),
            pl.BlockSpec((1, tb, BR_W), lambda b, i: (b, i, 0)),
            pl.BlockSpec((1, tb, BR_W), lambda b, i: (b, i, 0)),
        ],
        out_shape=[
            jax.ShapeDtypeStruct((B, T, N_FEAT * BR_W), F32),
            jax.ShapeDtypeStruct((B, T, D), BF16),
            jax.ShapeDtypeStruct((B, T, BR_W), F32),
            jax.ShapeDtypeStruct((B, T, BR_W), F32),
        ],
        scratch_shapes=[
            pltpu.VMEM((tb + 2 * HALO, LR_COLS), F32),
            pltpu.VMEM((tb + 2 * HALO, 3 * BR_W), F32),
        ],
        compiler_params=_cparams(("parallel", "arbitrary")),
        name="pre",
    )(x, x, x, mods, lw["g1"], lw["w_in"], lw["w_lr"],
      lw["w2"], lw["vec"], lw["rwkv_conv"], dft_c, rope_tabs[0], rope_tabs[1])


def _cumsum_chunk(x, rev):
    c = x.shape[0]
    tri = jnp.where(_tri(c, rev), 1.0, 0.0).astype(BF16)
    return _dot_exact_lhs(tri, x)


def _gla_chunks(chains):
    c = chains[0][0].shape[0]
    nblk = c // GLA_SUB
    ones_h = _head_ones()
    bd = (_iota((BR_W, BR_W), 0) // HEAD_DIM) == (_iota((BR_W, BR_W), 1) // HEAD_DIM)
    cums = [_cumsum_chunk(ch[3], ch[5]) for ch in chains]

    same = ((_iota((HEADS * GLA_SUB, BR_W), 0) // GLA_SUB)
            == (_iota((HEADS * GLA_SUB, BR_W), 1) // HEAD_DIM))
    offd, inter, st_new = [], [], []
    for (q, k, v, la, st, rev), cum in zip(chains, cums):
        cum_end = cum[0:1] if rev else cum[c - 1:c]
        parts = []
        for blk in range(nblk):
            base = blk * GLA_SUB
            if rev:
                k_lo, k_hi, ref = base + GLA_SUB, c, base + GLA_SUB
            else:
                k_lo, k_hi, ref = 0, base, base - 1
            if k_hi <= k_lo:
                parts.append(None)
                continue
            qt = q[base:base + GLA_SUB] * jnp.exp(cum[base:base + GLA_SUB] - cum[ref:ref + 1])
            kt = k[k_lo:k_hi] * jnp.exp(cum[ref:ref + 1] - cum[k_lo:k_hi])
            sc = _dot_nt(_stack_heads(qt), kt)
            ov = jnp.where(same, _dot(sc, v[k_lo:k_hi]), 0.0)
            parts.append(sum(ov[h * GLA_SUB:(h + 1) * GLA_SUB] for h in range(HEADS)))
        offd.append(parts)
        inter.append(_dot_nt(q * jnp.exp(cum), st))
        st_new.append(st * jnp.exp(cum_end)
                      + jnp.where(bd, _dot_tn(v, k * jnp.exp(cum_end - cum)), 0.0))

    sub = _iota((8, BR_W), 0)
    bias = {rev: [jnp.where((sub <= jj) if rev else (sub >= jj), 0.0, NEG_BIG) for jj in range(8)]
            for rev in {ch[5] for ch in chains}}
    reds, spans = [], []
    for (q, k, v, la, st, rev), cum in zip(chains, cums):
        cum2 = cum * LOG2E
        pieces, sp = [], []
        for j in range(c):
            base = (j // GLA_SUB) * GLA_SUB
            own = (j // 8) * 8
            others = range(base, own, 8) if rev else range(own + 8, base + GLA_SUB, 8)
            for g0 in (own, *others):
                arg = cum2[g0:g0 + 8] - cum2[j:j + 1]
                if g0 == own:
                    arg = arg + bias[rev][j - own]
                pieces.append(q[g0:g0 + 8] * (k[j:j + 1] * jnp.exp2(arg)))
                sp.append((j, g0))
        reds.append(jnp.dot(jnp.concatenate(pieces, axis=0).astype(BF16), ones_h,
                            preferred_element_type=F32))
        spans.append(sp)

    outs = []
    for ci, (q, k, v, la, st, rev) in enumerate(chains):
        groups = [None] * (c // 8)
        for idx, (j, g0) in enumerate(spans[ci]):
            piece = reds[ci][idx * 8:idx * 8 + 8] * v[j:j + 1]
            gi = g0 // 8
            groups[gi] = piece if groups[gi] is None else groups[gi] + piece
        blocks = []
        for blk in range(nblk):
            o_blk = jnp.concatenate(groups[blk * GLA_SUB // 8:(blk + 1) * GLA_SUB // 8], axis=0)
            if offd[ci][blk] is not None:
                o_blk = o_blk + offd[ci][blk]
            blocks.append(o_blk)
        outs.append((jnp.concatenate(blocks, axis=0) + inter[ci], st_new[ci]))
    return outs


def _ret_chunks(chains):
    bd = (_iota((BR_W, BR_W), 0) // HEAD_DIM) == (_iota((BR_W, BR_W), 1) // HEAD_DIM)
    sc = [_dot_nt(q, _stack_heads(k)) * dmat for q, k, v, s, dmat, _, _, _ in chains]
    inter = [_dot(q * qdec, s) for q, k, v, s, _, qdec, _, _ in chains]
    upd = [_dot_tn(k * kdec, v) for q, k, v, s, _, _, kdec, _ in chains]
    outs = []
    for ch, sc_i, inter_i, upd_i in zip(chains, sc, inter, upd):
        q, k, v, s, _, _, _, cdec = ch
        o = _dot(sc_i, _stack_heads(v)) + inter_i
        outs.append((o, s * cdec + jnp.where(bd, upd_i, 0.0)))
    return outs


def _tri_wide(c, rev, strict):
    t = _iota((c, HEADS * c), 0)
    s = _iota((c, HEADS * c), 1) % c
    if rev:
        return (s > t) if strict else (s >= t)
    return (s < t) if strict else (s <= t)


def _rwkv_chunks(chains):
    c = chains[0][0].shape[0]
    n = HEADS * c
    bf = lambda t: t.astype(BF16)
    stack = lambda t: bf(_stack_heads(t))
    masks = {rev: (_tri_wide(c, rev, True), _tri_wide(c, rev, False))
             for rev in {ch[7] for ch in chains}}
    eye = jnp.where(_iota((c, n), 1) % c == _iota((c, n), 0), 1.0, 0.0)
    bd = (_iota((BR_W, BR_W), 0) // HEAD_DIM) == (_iota((BR_W, BR_W), 1) // HEAD_DIM)
    cums = [_cumsum_chunk(ch[5], ch[7]) for ch in chains]
    st = []
    for (r, v, kk, kd, asig, lw, s, rev), cum in zip(chains, cums):
        cum_end = cum[0:1] if rev else cum[c - 1:c]
        einv = jnp.exp(-cum)
        ehat = jnp.exp(cum_end - cum)
        bvec = kk * asig
        st.append(dict(
            a=bf(-kk * jnp.exp(cum - lw)), r=bf(r * jnp.exp(cum)),
            b_s=stack(bvec * einv), k_s=stack(kd * einv), v=bf(v), v_s=stack(v),
            bh=bf(bvec * ehat), kh=bf(kd * ehat), g_end=jnp.exp(cum_end), s=s, sb=bf(s), rev=rev))
    for d in st:
        d["m"] = _dot_nt(jnp.concatenate([d["a"], d["r"]], axis=0),
                         jnp.concatenate([d["b_s"], d["k_s"]], axis=0))
    for d in st:
        strict, incl = masks[d["rev"]]
        m = d.pop("m")
        d["p"] = jnp.where(strict, m[0:c, 0:n], 0.0)
        d["t"] = eye + d["p"]
        d["ps"] = stack(d["p"])
        d["m_ak"] = bf(jnp.where(strict, m[0:c, n:2 * n], 0.0))
        d["n_rb"] = bf(jnp.where(incl, m[c:2 * c, 0:n], 0.0))
        d["n_rk"] = bf(jnp.where(incl, m[c:2 * c, n:2 * n], 0.0))
    for d in st:
        wy = (_dot_nt(jnp.concatenate([d["a"], d["r"]], axis=0), d["sb"])
              + _dot(jnp.concatenate([d["m_ak"], d["n_rk"]], axis=0), d["v_s"]))
        d["w"], d["y0"] = wy[0:c], wy[c:2 * c]
    for d in st:
        d["p"] = _dot(d["p"], d["ps"])
        d["ps"] = stack(d["p"])
    for _ in range(int(math.log2(c)) - 2):
        for d in st:
            tp = _dot(jnp.concatenate([d["t"], d["p"]], axis=0), d["ps"])
            d["t"] = d["t"] + tp[0:c]
            d["p"] = tp[c:2 * c]
            d["ps"] = stack(d["p"])
    for d in st:
        d["t"] = d["t"] + _dot(d["t"], d["ps"])
    for d in st:
        d["u"] = _dot(d["t"], stack(d["w"]))
    outs = []
    for d in st:
        y = d["y0"] + _dot(d["n_rb"], stack(d["u"]))
        upd = _dot_tn(jnp.concatenate([bf(d["u"]), d["v"]], axis=0),
                      jnp.concatenate([d["bh"], d["kh"]], axis=0))
        outs.append((y, d["s"] * d["g_end"] + jnp.where(bd, upd, 0.0)))
    return outs


def _scan_kernel(kind, n_in, n_tab, nc, bb, *refs):
    fwd = refs[0:n_in]
    bwd = refs[n_in:2 * n_in]
    tabs = refs[2 * n_in:2 * n_in + n_tab]
    s0_ref = refs[2 * n_in + n_tab]
    of_ref, ob_ref, sf_ref, st_ref = refs[2 * n_in + n_tab + 1:]
    i = pl.program_id(1)

    @pl.when(i == 0)
    def _():
        st_ref[...] = s0_ref[...]

    chains = [(dr, b) for b in range(bb) for dr in range(2)]
    ins_of = lambda dr, b: [t[b] for t in (fwd, bwd)[dr]]
    if kind == "rwkv":
        res = _rwkv_chunks([(*ins_of(dr, b), st_ref[dr, b], dr == 1) for dr, b in chains])
    elif kind == "gla":
        res = _gla_chunks([(*ins_of(dr, b), st_ref[dr, b], dr == 1) for dr, b in chains])
    else:
        res = _ret_chunks([(*ins_of(dr, b), st_ref[dr, b], *(t[dr] for t in tabs))
                           for dr, b in chains])
    for (dr, b), (o, s_new) in zip(chains, res):
        (of_ref, ob_ref)[dr][b] = o
        st_ref[dr, b] = s_new

    @pl.when(i == nc - 1)
    def _():
        sf_ref[...] = st_ref[...]


_SCAN_ROWS = {"gla": 4, "ret": 8, "rwkv": 8}


def _scan(kind, feats, cols_f, cols_b, s0, tabs=()):
    B, T, _ = feats.shape
    c = min(SCAN_CHUNK[kind], T)
    nc = T // c
    bb = math.gcd(_SCAN_ROWS[kind], B)
    n_in = len(cols_f)
    in_specs = []
    for j in cols_f:
        in_specs.append(pl.BlockSpec((bb, c, BR_W), lambda b, i, j=j: (b, i, j)))
    for j in cols_b:
        in_specs.append(pl.BlockSpec((bb, c, BR_W), lambda b, i, j=j: (b, nc - 1 - i, j)))
    for t in tabs:
        in_specs.append(pl.BlockSpec(t.shape, lambda b, i, nd=t.ndim: (0,) * nd))
    in_specs.append(pl.BlockSpec((2, bb, BR_W, BR_W), lambda b, i: (0, b, 0, 0)))
    kern = functools.partial(_scan_kernel, kind, n_in, len(tabs), nc, bb)
    return pl.pallas_call(
        kern,
        grid=(B // bb, nc),
        in_specs=in_specs,
        out_specs=[
            pl.BlockSpec((bb, c, BR_W), lambda b, i: (b, i, 0)),
            pl.BlockSpec((bb, c, BR_W), lambda b, i: (b, nc - 1 - i, 0)),
            pl.BlockSpec((2, bb, BR_W, BR_W), lambda b, i: (0, b, 0, 0)),
        ],
        out_shape=[
            jax.ShapeDtypeStruct((B, T, BR_W), F32),
            jax.ShapeDtypeStruct((B, T, BR_W), F32),
            jax.ShapeDtypeStruct((2, B, BR_W, BR_W), F32),
        ],
        scratch_shapes=[pltpu.VMEM((2, bb, BR_W, BR_W), F32)],
        compiler_params=_cparams(("parallel", "arbitrary")),
        name=kind,
    )(*([feats] * (2 * n_in)), *tabs, s0)


def _ret_tables(c):
    pos = np.arange(c, dtype=np.float64)
    lane_head = np.arange(BR_W) // HEAD_DIM
    dmats, qd, kd, cd = [], [], [], []
    for dr in range(2):
        expo = -5.0 - np.arange(HEADS, dtype=np.float64)
        if dr == 1:
            expo = expo[::-1]
        log_g = np.log1p(-np.exp2(expo))
        if dr == 0:
            rel = pos[:, None] - pos[None, :]
            qpow = pos + 1.0
            kpow = c - 1.0 - pos
        else:
            rel = pos[None, :] - pos[:, None]
            qpow = c - pos
            kpow = pos
        tri = rel >= 0
        dm = np.where(tri[None], np.exp(np.where(tri, rel, 0.0)[None] * log_g[:, None, None]), 0.0)
        dmats.append(np.concatenate([dm[h] for h in range(HEADS)], axis=1))
        qd.append(np.exp(qpow[:, None] * log_g[lane_head][None, :]))
        kd.append(np.exp(kpow[:, None] * log_g[lane_head][None, :]))
        cd.append(np.exp(c * log_g[lane_head])[None, :])
    f = lambda xs: jnp.asarray(np.stack(xs), dtype=F32)
    return f(dmats), f(qd), f(kd), f(cd)


def _fnet_dense_kernel(t_len, cs_ref, g1_ref, g2_ref, o_ref):
    o_ref[0] = (jnp.dot(cs_ref[:, 0:t_len], g1_ref[0].astype(BF16), preferred_element_type=F32)
                + jnp.dot(cs_ref[:, t_len:2 * t_len], g2_ref[0].astype(BF16),
                          preferred_element_type=F32))


def _fnet_dense(g1, g2, cs):
    B, T, _ = g1.shape
    tb = min(TOKEN_TILE, T)
    return pl.pallas_call(
        functools.partial(_fnet_dense_kernel, T),
        grid=(T // tb, B),
        in_specs=[
            pl.BlockSpec((tb, 2 * T), lambda i, b: (i, 0)),
            pl.BlockSpec((1, T, BR_W), lambda i, b: (b, 0, 0)),
            pl.BlockSpec((1, T, BR_W), lambda i, b: (b, 0, 0)),
        ],
        out_specs=pl.BlockSpec((1, tb, BR_W), lambda i, b: (b, i, 0)),
        out_shape=jax.ShapeDtypeStruct((B, T, BR_W), F32),
        compiler_params=_cparams(("arbitrary", "arbitrary")),
        name="fnet",
    )(cs, g1, g2)


def _bf16_const(table):
    return jnp.asarray(table, dtype=F32).astype(BF16)


FFT_R = 8


def _fft_kernel(n, k1_ref, k2_ref, tc_ref, ts_ref, g1_ref, g2_ref, o_ref, zr_ref, zi_ref):
    m = n * FFT_R
    for j in range(n // FFT_R):
        sl = slice(j * FFT_R, (j + 1) * FFT_R)
        x = jnp.concatenate([g1_ref[0, :, sl, :].reshape(m, BR_W),
                             g2_ref[0, :, sl, :].reshape(m, BR_W)], axis=1).astype(BF16)
        y = jnp.dot(k1_ref[...], x, preferred_element_type=F32)
        yr = y[0:m, 0:BR_W] - y[m:2 * m, BR_W:2 * BR_W]
        yi = -y[0:m, BR_W:2 * BR_W] - y[m:2 * m, 0:BR_W]
        tc = tc_ref[:, sl, :].reshape(m, BR_W)
        ts = ts_ref[:, sl, :].reshape(m, BR_W)
        zr_ref[:, sl, :] = (tc * yr + ts * yi).reshape(n, FFT_R, BR_W)
        zi_ref[:, sl, :] = (tc * yi - ts * yr).reshape(n, FFT_R, BR_W)
    for j in range(n // FFT_R):
        sl = slice(j * FFT_R, (j + 1) * FFT_R)
        z = jnp.concatenate([zr_ref[sl].reshape(m, BR_W), zi_ref[sl].reshape(m, BR_W)],
                            axis=0).astype(BF16)
        o = jnp.dot(k2_ref[...], z, preferred_element_type=F32)
        o_ref[0, :, sl, :] = o.reshape(n, FFT_R, BR_W)


def _fnet_fft(g1, g2):
    B, T, _ = g1.shape
    n = int(round(math.sqrt(T)))
    assert n * n == T and n % FFT_R == 0
    m = n * FFT_R
    idx = np.arange(n)
    a_n = 2.0 * np.pi * ((idx[:, None] * idx[None, :]) % n) / n
    eye = np.eye(FFT_R)
    k1 = np.concatenate([np.einsum("ap,ij->aipj", f, eye).reshape(m, m)
                         for f in (np.cos(a_n), np.sin(a_n))], axis=0)
    k2 = np.concatenate([np.einsum("pr,ij->pijr", f, eye).reshape(m, m)
                         for f in (np.cos(a_n), np.sin(a_n))], axis=1)
    a_t = 2.0 * np.pi * (idx[:, None] * idx[None, :]) / T
    tc = jnp.broadcast_to(jnp.asarray(np.cos(a_t), dtype=F32)[:, :, None], (n, n, BR_W))
    ts = jnp.broadcast_to(jnp.asarray(np.sin(a_t), dtype=F32)[:, :, None], (n, n, BR_W))
    const2 = lambda b: (0, 0)
    const3 = lambda b: (0, 0, 0)
    one = pl.Buffered(1)
    o = pl.pallas_call(
        functools.partial(_fft_kernel, n),
        grid=(B,),
        in_specs=[
            pl.BlockSpec((2 * m, m), const2, pipeline_mode=one),
            pl.BlockSpec((m, 2 * m), const2, pipeline_mode=one),
            pl.BlockSpec((n, n, BR_W), const3, pipeline_mode=one),
            pl.BlockSpec((n, n, BR_W), const3, pipeline_mode=one),
            pl.BlockSpec((1, n, n, BR_W), lambda b: (b, 0, 0, 0)),
            pl.BlockSpec((1, n, n, BR_W), lambda b: (b, 0, 0, 0)),
        ],
        out_specs=pl.BlockSpec((1, n, n, BR_W), lambda b: (b, 0, 0, 0)),
        out_shape=jax.ShapeDtypeStruct((B, n, n, BR_W), F32),
        scratch_shapes=[pltpu.VMEM((n, n, BR_W), F32), pltpu.VMEM((n, n, BR_W), F32)],
        compiler_params=_cparams(("arbitrary",)),
        name="fft",
    )(_bf16_const(k1), _bf16_const(k2), tc, ts,
      g1.reshape(B, n, n, BR_W), g2.reshape(B, n, n, BR_W))
    return o.reshape(B, T, BR_W)


def _time_dft_table(T):
    t = np.arange(T)
    ang = ((t[:, None] * t[None, :]) % T) * (2.0 * np.pi / T)
    return _bf16_const(np.concatenate([np.cos(ang), -np.sin(ang)], axis=1))


def _channel_dft_tables(T):
    ch = np.arange(HEAD_DIM)
    a64 = 2.0 * np.pi * ((ch[:, None] * ch[None, :]) % HEAD_DIM) / HEAD_DIM
    scale = (T * HEAD_DIM) ** -0.5
    eye = np.eye(HEADS)
    cbd = np.kron(eye, np.cos(a64)) * scale
    sbd = np.kron(eye, np.sin(a64)) * scale
    return _bf16_const(np.stack([cbd, sbd]))


def _head_norm(o, ones_h, center):
    if center:
        o = o - _dot(o, ones_h) * (1.0 / HEAD_DIM)
    var = _dot(o * o, ones_h) * (1.0 / HEAD_DIM)
    return o * lax.rsqrt(var + EPS)


def _merge_kernel(x_ref, hb_ref, mod_ref, gg_ref, rg_ref, wg_ref, bon_ref,
                  ogf_ref, ogb_ref, orf_ref, orb_ref, oyf_ref, oyb_ref, fn_ref,
                  gn_ref, wgate_ref, bgate_ref, wbr_ref, wout_ref, o_ref):
    ones_h = _head_ones()
    gn = gn_ref[...]
    gla = _head_norm(ogf_ref[0] + ogb_ref[0], ones_h, False) * gn[0:1] * _silu(gg_ref[0])
    ret = _head_norm(orf_ref[0] + orb_ref[0], ones_h, True) * gn[1:2] * _silu(rg_ref[0])
    rwkv = (_head_norm(oyf_ref[0] + oyb_ref[0], ones_h, True) * gn[2:3] + bon_ref[0]) * wg_ref[0]
    outs = (gla, ret, rwkv, fn_ref[0])
    hb = hb_ref[0]
    z = None
    for br in range(4):
        gate = _sigmoid(jnp.dot(hb, wgate_ref[br], preferred_element_type=F32) + bgate_ref[br])
        term = gate * _dot(outs[br], wbr_ref[br])
        z = term if z is None else z + term
    y = _dot(z, wout_ref[...])
    gate1 = mod_ref[0][:, 2 * D_MODEL:3 * D_MODEL]
    o_ref[0] = x_ref[0] + gate1 * y


def _merge(x, hb, mods, feats, scans, fnet_out, lw):
    B, T, D = x.shape
    tb = min(TOKEN_TILE, T)
    tok = lambda w: pl.BlockSpec((1, tb, w), lambda b, i: (b, i, 0))
    fcol = lambda j: pl.BlockSpec((1, tb, BR_W), lambda b, i, j=j: (b, i, j))
    const2 = lambda b, i: (0, 0)
    const3 = lambda b, i: (0, 0, 0)
    return pl.pallas_call(
        _merge_kernel,
        grid=(B, T // tb),
        in_specs=[
            tok(D), tok(D),
            pl.BlockSpec((1, 1, 6 * D), lambda b, i: (b, 0, 0)),
            fcol(F_GG), fcol(F_RG), fcol(F_WG), fcol(F_BON),
            tok(BR_W), tok(BR_W), tok(BR_W), tok(BR_W), tok(BR_W), tok(BR_W), tok(BR_W),
            pl.BlockSpec((3, BR_W), const2),
            pl.BlockSpec((4, D, D), const3, pipeline_mode=pl.Buffered(1)),
            pl.BlockSpec((4, 1, D), const3),
            pl.BlockSpec((4, BR_W, D), const3, pipeline_mode=pl.Buffered(1)),
            pl.BlockSpec((D, D), const2, pipeline_mode=pl.Buffered(1)),
        ],
        out_specs=tok(D),
        out_shape=jax.ShapeDtypeStruct((B, T, D), F32),
        compiler_params=_cparams(("parallel", "arbitrary")),
        name="merge",
    )(x, hb, mods, feats, feats, feats, feats, *scans, fnet_out,
      lw["gn"], lw["w_gate"], lw["b_gate"], lw["w_br"], lw["w_out"])


FFN_COL_CHUNKS = (1024, 1024, 768)


def _ffn_kernel(final, tb, nt,
                x_ref, xp_ref, xn_ref, mod_ref, g2_ref, up_ref, cw_ref, cb_ref,
                down_ref, gfin_ref, o_ref, act_ref):
    i = pl.program_id(1)
    n_ext = tb + 2 * HALO
    x_ext = jnp.concatenate([xp_ref[0], x_ref[0], xn_ref[0]], axis=0)
    mod = mod_ref[0]
    shift = mod[:, 3 * D_MODEL:4 * D_MODEL]
    scale = mod[:, 4 * D_MODEL:5 * D_MODEL]
    h2 = (_rms_rows(x_ext) * g2_ref[...]) * (1.0 + scale) + shift
    h2 = h2 * _halo_keep(n_ext, tb, i, nt)
    h2_ext = h2.astype(BF16)
    h2_mid = h2[HALO:HALO + tb].astype(BF16)
    cw = cw_ref[...]
    cb = cb_ref[...]
    lo = 0
    for width in FFN_COL_CHUNKS:
        a_ext = jnp.dot(h2_ext, up_ref[:, lo:lo + width], preferred_element_type=F32)
        u = jnp.dot(h2_mid, up_ref[:, D_FF + lo:D_FF + lo + width], preferred_element_type=F32)
        a = (a_ext[HALO - 1:HALO - 1 + tb] * cw[0:1, lo:lo + width]
             + a_ext[HALO:HALO + tb] * cw[1:2, lo:lo + width]
             + a_ext[HALO + 1:HALO + 1 + tb] * cw[2:3, lo:lo + width] + cb[:, lo:lo + width])
        act_ref[:, lo:lo + width] = (_silu(a) * u).astype(BF16)
        lo += width
    y = jnp.dot(act_ref[...], down_ref[...], preferred_element_type=F32)
    gate2 = mod[:, 5 * D_MODEL:6 * D_MODEL]
    res = x_ref[0] + gate2 * y
    if final:
        res = _rms_rows(res) * gfin_ref[...]
    o_ref[0] = res


def _ffn(x, mods, lw, g_final, final):
    B, T, D = x.shape
    tb = min(TOKEN_TILE, T)
    nt = T // tb
    hb8 = tb // HALO
    n_h = T // HALO
    const2 = lambda b, i: (0, 0)
    resident = lambda shape: pl.BlockSpec(shape, const2, pipeline_mode=pl.Buffered(1))
    kern = functools.partial(_ffn_kernel, final, tb, nt)
    return pl.pallas_call(
        kern,
        grid=(B, nt),
        in_specs=[
            pl.BlockSpec((1, tb, D), lambda b, i: (b, i, 0)),
            pl.BlockSpec((1, HALO, D), lambda b, i: (b, jnp.maximum(i * hb8 - 1, 0), 0)),
            pl.BlockSpec((1, HALO, D), lambda b, i: (b, jnp.minimum((i + 1) * hb8, n_h - 1), 0)),
            pl.BlockSpec((1, 1, 6 * D), lambda b, i: (b, 0, 0)),
            pl.BlockSpec((1, D), const2),
            resident((D, 2 * D_FF)),
            pl.BlockSpec((3, D_FF), const2),
            pl.BlockSpec((1, D_FF), const2),
            resident((D_FF, D)),
            pl.BlockSpec((1, D), const2),
        ],
        out_specs=pl.BlockSpec((1, tb, D), lambda b, i: (b, i, 0)),
        out_shape=jax.ShapeDtypeStruct((B, T, D), F32),
        scratch_shapes=[pltpu.VMEM((tb, D_FF), BF16)],
        compiler_params=_cparams(("parallel", "arbitrary")),
        name="ffn",
    )(x, x, x, mods, lw["g2"], lw["ffn_up"], lw["ffn_conv"], lw["ffn_conv_b"],
      lw["ffn_down"], g_final)


def _rope_tables(T):
    rows = T // GRID_W
    row = jnp.repeat(jnp.arange(rows, dtype=F32), GRID_W)
    colp = jnp.tile(jnp.arange(GRID_W, dtype=F32), rows)
    n_freq = HEAD_DIM // 4
    inv = ROPE_BASE ** (-jnp.arange(n_freq, dtype=F32) / n_freq)
    ang = jnp.concatenate([row[:, None] * inv, colp[:, None] * inv], axis=-1)
    cos, sin = jnp.cos(ang), jnp.sin(ang)
    cos_h = jnp.concatenate([cos, cos], axis=-1)
    sin_h = jnp.concatenate([-sin, sin], axis=-1)
    return jnp.tile(cos_h, (1, HEADS)), jnp.tile(sin_h, (1, HEADS))


def _pad_rows(w, lo, n):
    return jnp.zeros((n, w.shape[1]), w.dtype).at[lo:lo + w.shape[0]].set(w)


def _layer_weights(l, p):
    D = D_MODEL
    bf = lambda t: t.astype(BF16)
    zcols = lambda w, n: jnp.concatenate([w, jnp.zeros((w.shape[0], n - w.shape[1]), w.dtype)], axis=1)
    w_h = zcols(jnp.concatenate([p["gla_wa1"][l, 0], p["gla_wa1"][l, 1]], axis=1), 128)
    w_xw = jnp.concatenate([p["rwkv_w1"][l, 0], p["rwkv_w1"][l, 1]], axis=1)
    w_xa = jnp.concatenate([p["rwkv_a1"][l, 0], p["rwkv_a1"][l, 1]], axis=1)
    w_xg = zcols(p["rwkv_g1"][l], 256)
    mu = p["rwkv_mu"][l]
    w_lr = jnp.concatenate([w_xg, mu[2][:, None] * w_xg, w_h, w_xw, w_xa,
                            mu[0][:, None] * w_xw, mu[1][:, None] * w_xa,
                            jnp.zeros((D, LR_COLS - LR_AS - 128), F32)], axis=1)
    w2 = jnp.stack([
        _pad_rows(p["gla_wa2"][l, 0], 0, 256), _pad_rows(p["gla_wa2"][l, 1], GLA_LR, 256),
        _pad_rows(p["rwkv_w2"][l, 0], 0, 256), _pad_rows(p["rwkv_w2"][l, 1], RWKV_LR, 256),
        _pad_rows(p["rwkv_a2"][l, 0], 0, 256), _pad_rows(p["rwkv_a2"][l, 1], RWKV_LR, 256),
        _pad_rows(p["rwkv_g2"][l], 0, 256),
    ])
    vec = jnp.stack([p["gla_ba"][l, 0], p["gla_ba"][l, 1], p["rwkv_w0"][l, 0], p["rwkv_w0"][l, 1],
                     p["rwkv_a0"][l, 0], p["rwkv_a0"][l, 1], p["rwkv_kk"][l], p["rwkv_ka"][l],
                     p["rwkv_rk"][l]])
    vec = jnp.concatenate([vec, jnp.zeros((16 - vec.shape[0], BR_W), F32)], axis=0)
    return {
        "g1": p["g_norm1"][l].reshape(1, D), "g2": p["g_norm2"][l].reshape(1, D),
        "w_in": bf(p["w_in"][l]), "w_lr": bf(w_lr), "w2": bf(w2), "vec": vec,
        "rwkv_conv": p["rwkv_conv"][l],
        "gn": jnp.stack([p["gla_gn"][l], p["ret_gn"][l], p["rwkv_gn"][l]]),
        "w_gate": bf(p["w_gate"][l]), "b_gate": p["b_gate"][l].reshape(4, 1, D),
        "w_br": bf(p["w_br"][l]), "w_out": bf(p["w_out"][l]),
        "ffn_up": bf(p["ffn_up"][l]), "ffn_conv": p["ffn_conv"][l],
        "ffn_conv_b": p["ffn_conv_b"][l].reshape(1, D_FF), "ffn_down": bf(p["ffn_down"][l]),
    }


_GLA_F = (F_GQ, F_GK, F_GV, F_LAF)
_GLA_B = (F_GQ, F_GK, F_GV, F_LAB)
_RET = (F_RQ, F_RK, F_RV)
_RWKV_F = (F_WR, F_WV, F_WKK, F_KDF, F_ASF, F_LWF)
_RWKV_B = (F_WR, F_WV, F_WKK, F_KDB, F_ASB, F_LWB)


def _mixers(feats_c, feats_l, ret_tabs_c, ret_tabs_l, need_ctx):
    B = feats_l.shape[0]
    zero = jnp.zeros((2, B, BR_W, BR_W), F32)
    out_c, out_l = [], []
    for kind, cf, cb, tabs in (("gla", _GLA_F, _GLA_B, None), ("ret", _RET, _RET, True),
                               ("rwkv", _RWKV_F, _RWKV_B, None)):
        tc = ret_tabs_c if tabs else ()
        tl = ret_tabs_l if tabs else ()
        of_c, ob_c, s_c = _scan(kind, feats_c, cf, cb, zero, tc)
        of_l, ob_l, _ = _scan(kind, feats_l, cf, cb, s_c, tl)
        out_c += [of_c, ob_c]
        out_l += [of_l, ob_l]
    return out_l, (out_c if need_ctx else None)


def kernel(x, c, ctx, c_ctx, w_ada, b_ada, g_norm1, g_norm2, w_in, gla_wa1, gla_wa2, gla_ba, gla_gn, ret_gn, rwkv_conv, rwkv_mu, rwkv_w0, rwkv_w1, rwkv_w2, rwkv_a0, rwkv_a1, rwkv_a2, rwkv_g1, rwkv_g2, rwkv_kk, rwkv_ka, rwkv_rk, rwkv_gn, w_gate, b_gate, w_br, w_out, ffn_up, ffn_conv, ffn_conv_b, ffn_down, g_final):
    p = dict(g_norm1=g_norm1, g_norm2=g_norm2, w_in=w_in, gla_wa1=gla_wa1, gla_wa2=gla_wa2,
             gla_ba=gla_ba, gla_gn=gla_gn, ret_gn=ret_gn, rwkv_conv=rwkv_conv, rwkv_mu=rwkv_mu,
             rwkv_w0=rwkv_w0, rwkv_w1=rwkv_w1, rwkv_w2=rwkv_w2, rwkv_a0=rwkv_a0,
             rwkv_a1=rwkv_a1, rwkv_a2=rwkv_a2, rwkv_g1=rwkv_g1, rwkv_g2=rwkv_g2,
             rwkv_kk=rwkv_kk, rwkv_ka=rwkv_ka, rwkv_rk=rwkv_rk, rwkv_gn=rwkv_gn,
             w_gate=w_gate, b_gate=b_gate, w_br=w_br, w_out=w_out, ffn_up=ffn_up,
             ffn_conv=ffn_conv, ffn_conv_b=ffn_conv_b, ffn_down=ffn_down)
    B, T, D = x.shape
    Tc = ctx.shape[1]
    depth = w_ada.shape[0]

    cmat = jnp.concatenate([c, c_ctx[None, :], jnp.zeros((MOD_ROWS - B - 1, D), F32)], axis=0)
    mod_all = _modulation(cmat, w_ada, b_ada)

    rope_l = _rope_tables(T)
    rope_c = (jnp.zeros((Tc, BR_W), F32), jnp.zeros((Tc, BR_W), F32))
    dft_l = _channel_dft_tables(T)
    dft_c = _channel_dft_tables(Tc)
    cs_c = _time_dft_table(Tc)
    ret_tabs_l = _ret_tables(min(SCAN_CHUNK["ret"], T))
    ret_tabs_c = _ret_tables(min(SCAN_CHUNK["ret"], Tc))
    g_fin = g_final.reshape(1, D)

    for l in range(depth):
        last = l == depth - 1
        lw = _layer_weights(l, p)
        mods_l = mod_all[l, 0:B].reshape(B, 1, 6 * D)
        mods_c = jnp.broadcast_to(mod_all[l, B:B + 1].reshape(1, 1, 6 * D), (B, 1, 6 * D))

        feats_l, hb_l, g1_l, g2_l = _pre(x, mods_l, lw, True, rope_l, dft_l)
        feats_c, hb_c, g1_c, g2_c = _pre(ctx, mods_c, lw, False, rope_c, dft_c)
        scans_l, scans_c = _mixers(feats_c, feats_l, ret_tabs_c, ret_tabs_l, not last)

        fn_l = _fnet_fft(g1_l, g2_l)
        x = _merge(x, hb_l, mods_l, feats_l, scans_l, fn_l, lw)
        x = _ffn(x, mods_l, lw, g_fin, last)
        if not last:
            fn_c = _fnet_dense(g1_c, g2_c, cs_c)
            ctx = _merge(ctx, hb_c, mods_c, feats_c, scans_c, fn_c, lw)
            ctx = _ffn(ctx, mods_c, lw, g_fin, False)
    return x
```

```python
import functools
import math

import numpy as np
import jax
import jax.numpy as jnp
from jax import lax
from jax.experimental import pallas as pl
from jax.experimental.pallas import tpu as pltpu

F32 = jnp.float32
BF16 = jnp.bfloat16

D_MODEL = 1024
BR_W = 256
HEAD_DIM = 64
HEADS = 4
N_PARTS = 12
P_IN = N_PARTS * BR_W
GRID_W = 64
GLA_LR = 16
GLA_NORMALIZER = 16.0
RWKV_LR = 64
RWKV_G_LR = 160
D_FF = 2816
ROPE_BASE = 10000.0
EPS = 1e-6

SCAN_CHUNK = {"gla": 64, "ret": 128, "rwkv": 64}
GLA_SUB = 16
HALO = 8
NEG_BIG = -1e30
LOG2E = 1.4426950408889634

(F_GQ, F_GK, F_GV, F_GG, F_RQ, F_RK, F_RV, F_RG, F_WR, F_WV, F_WKK, F_WG, F_BON,
 F_LAF, F_LAB, F_LWF, F_LWB, F_KDF, F_KDB, F_ASF, F_ASB) = range(21)
N_FEAT = 21

LR_G, LR_GS, LR_H, LR_W, LR_A, LR_WS, LR_AS = 0, 256, 512, 640, 768, 896, 1024
LR_COLS = 1280

TOKEN_TILE = 512
MOD_ROWS = 16
MOD_COL_BLOCK = 1536
VMEM_LIMIT = 56 * 1024 * 1024


def _cparams(sem):
    return pltpu.CompilerParams(dimension_semantics=sem, vmem_limit_bytes=VMEM_LIMIT)


def _dot(a, b):
    return jnp.dot(a.astype(BF16), b.astype(BF16), preferred_element_type=F32)


def _dot_nt(a, b):
    return lax.dot_general(a.astype(BF16), b.astype(BF16), (((1,), (1,)), ((), ())),
                           preferred_element_type=F32)


def _dot_tn(a, b):
    return lax.dot_general(a.astype(BF16), b.astype(BF16), (((0,), (0,)), ((), ())),
                           preferred_element_type=F32)


def _split3(x):
    hi = x.astype(BF16)
    r1 = x - hi.astype(F32)
    mid = r1.astype(BF16)
    lo = (r1 - mid.astype(F32)).astype(BF16)
    return hi, mid, lo


def _dot_exact_lhs(a_bf16, x):
    hi, mid, lo = _split3(x)
    d = lambda t: jnp.dot(a_bf16, t, preferred_element_type=F32)
    return d(hi) + d(mid) + d(lo)


def _sigmoid(x):
    return 1.0 / (1.0 + jnp.exp(-x))


def _silu(x):
    return x * _sigmoid(x)


def _softplus(x):
    return jnp.maximum(x, 0.0) + jnp.log(1.0 + jnp.exp(-jnp.abs(x)))


def _iota(shape, dim):
    return lax.broadcasted_iota(jnp.int32, shape, dim)


def _head_ones():
    r = _iota((BR_W, BR_W), 0) // HEAD_DIM
    c = _iota((BR_W, BR_W), 1) // HEAD_DIM
    return jnp.where(r == c, 1.0, 0.0).astype(BF16)


def _stack_heads(x):
    c, w = x.shape
    xs = jnp.concatenate([x] * HEADS, axis=0)
    keep = (_iota((HEADS * c, w), 0) // c) == (_iota((HEADS * c, w), 1) // (w // HEADS))
    return jnp.where(keep, xs, 0.0)


def _tri(c, rev, strict=False):
    r = _iota((c, c), 0)
    s = _iota((c, c), 1)
    if rev:
        return (r < s) if strict else (r <= s)
    return (r > s) if strict else (r >= s)


def _halo_keep(n_ext, tb, i, nt):
    row = _iota((n_ext, 1), 0)
    first = jnp.where(i > 0, 1.0, 0.0)
    last = jnp.where(i < nt - 1, 1.0, 0.0)
    return jnp.where(row < HALO, first, jnp.where(row >= tb + HALO, last, 1.0))


def _rms_rows(x):
    return x * lax.rsqrt(jnp.mean(x * x, axis=-1, keepdims=True) + EPS)


def _mod_kernel(c_ref, w_ref, b_ref, o_ref):
    s = _silu(c_ref[...])
    o_ref[0] = _dot(s, w_ref[0]) + b_ref[0]


def _modulation(cmat, w_ada, b_ada):
    L = w_ada.shape[0]
    nblk = MOD_COL_BLOCK
    return pl.pallas_call(
        _mod_kernel,
        grid=(L, 6 * D_MODEL // nblk),
        in_specs=[
            pl.BlockSpec((MOD_ROWS, D_MODEL), lambda l, j: (0, 0)),
            pl.BlockSpec((1, D_MODEL, nblk), lambda l, j: (l, 0, j)),
            pl.BlockSpec((1, 1, nblk), lambda l, j: (l, 0, j)),
        ],
        out_specs=pl.BlockSpec((1, MOD_ROWS, nblk), lambda l, j: (l, 0, j)),
        out_shape=jax.ShapeDtypeStruct((L, MOD_ROWS, 6 * D_MODEL), F32),
        compiler_params=_cparams(("arbitrary", "arbitrary")),
        name="mod",
    )(cmat, w_ada, b_ada.reshape(L, 1, 6 * D_MODEL))


def _pre_kernel(use_rope, tb, nt,
                x_ref, xp_ref, xn_ref, mod_ref, g1_ref, win_ref, wlr_ref,
                w2_ref, vec_ref, conv_ref, dft_ref, cos_ref, sin_ref,
                f_ref, hb_ref, fc_ref, fs_ref, lr_ref, ps_ref):
    i = pl.program_id(1)
    n_ext = tb + 2 * HALO
    x_ext = jnp.concatenate([xp_ref[0], x_ref[0], xn_ref[0]], axis=0)
    mod = mod_ref[0]
    shift = mod[:, 0:D_MODEL]
    scale = mod[:, D_MODEL:2 * D_MODEL]
    h_ext = (_rms_rows(x_ext) * g1_ref[...]) * (1.0 + scale) + shift
    h_ext = h_ext * _halo_keep(n_ext, tb, i, nt)
    hb_ext = h_ext.astype(BF16)
    hb = h_ext[HALO:HALO + tb].astype(BF16)
    hb_ref[0] = hb
    lr_ref[...] = jnp.dot(hb_ext, wlr_ref[...], preferred_element_type=F32)
    ps_ref[...] = jnp.dot(hb_ext, win_ref[:, 8 * BR_W:11 * BR_W], preferred_element_type=F32)

    def mid(lo, hi):
        return lr_ref[HALO:HALO + tb, lo:hi]

    def shifted(lo, hi):
        return (0.5 * (lr_ref[HALO - 1:HALO - 1 + tb, lo:hi] + lr_ref[HALO + 1:HALO + 1 + tb, lo:hi])
                - lr_ref[HALO:HALO + tb, lo:hi])

    vec = vec_ref[...]
    ba_f, ba_b, w0_f, w0_b = vec[0:1], vec[1:2], vec[2:3], vec[3:4]
    a0_f, a0_b, kkw, kaw, rkw = vec[4:5], vec[5:6], vec[6:7], vec[7:8], vec[8:9]

    def put(j, val):
        f_ref[0, :, j * BR_W:(j + 1) * BR_W] = val

    zh = mid(LR_H, LR_H + 128)
    tw = jnp.tanh(mid(LR_W, LR_W + 128) + shifted(LR_WS, LR_WS + 128))
    ta = mid(LR_A, LR_A + 128) + shifted(LR_AS, LR_AS + 128)
    sg = _sigmoid(mid(LR_G, LR_G + 256) + shifted(LR_GS, LR_GS + 256))

    p_gla = jnp.dot(hb, win_ref[:, 0:4 * BR_W], preferred_element_type=F32)
    put(F_GQ, p_gla[:, 0:BR_W] * HEAD_DIM ** -0.5)
    put(F_GK, p_gla[:, BR_W:2 * BR_W])
    put(F_GV, p_gla[:, 2 * BR_W:3 * BR_W])
    put(F_GG, p_gla[:, 3 * BR_W:4 * BR_W])

    z_la = [_dot(zh, w2_ref[dr, 0:128, :]) for dr in range(2)]
    z_w = [_dot(tw, w2_ref[2 + dr, 0:128, :]) for dr in range(2)]
    z_a = [_dot(ta, w2_ref[4 + dr, 0:128, :]) for dr in range(2)]
    put(F_WG, _dot(sg, w2_ref[6]))

    for dr, (ba, dst) in enumerate(((ba_f, F_LAF), (ba_b, F_LAB))):
        put(dst, -_softplus(-(z_la[dr] + ba)) * (1.0 / GLA_NORMALIZER))

    p_ret = jnp.dot(hb, win_ref[:, 4 * BR_W:8 * BR_W], preferred_element_type=F32)

    conv = conv_ref[...]

    def dwconv(part):
        lo, hi = part * BR_W, (part + 1) * BR_W
        cw = conv[:, lo:hi]
        return (ps_ref[HALO - 1:HALO - 1 + tb, lo:hi] * cw[0:1]
                + ps_ref[HALO:HALO + tb, lo:hi] * cw[1:2]
                + ps_ref[HALO + 1:HALO + 1 + tb, lo:hi] * cw[2:3])

    r_c = dwconv(0)
    k_c = dwconv(1)
    v_c = dwconv(2)
    ones_h = _head_ones()
    kk = k_c * kkw
    kk = kk * lax.rsqrt(_dot(kk * kk, ones_h) + EPS)
    put(F_WR, r_c)
    put(F_WV, v_c)
    put(F_WKK, kk)

    p_fnet = jnp.dot(hb, win_ref[:, 11 * BR_W:12 * BR_W], preferred_element_type=F32)

    bonus = jnp.zeros((tb, BR_W), F32)
    for dr, (w0, a0, d_lw, d_kd, d_as) in enumerate(
            ((w0_f, a0_f, F_LWF, F_KDF, F_ASF), (w0_b, a0_b, F_LWB, F_KDB, F_ASB))):
        w_raw = -_softplus(-(w0 + z_w[dr])) - 0.5
        put(d_lw, -jnp.exp(w_raw))
        a_sig = _sigmoid(a0 + z_a[dr])
        kd = k_c * (1.0 + (a_sig - 1.0) * kaw)
        put(d_kd, kd)
        put(d_as, a_sig)
        bonus = bonus + _dot(r_c * kd * rkw, ones_h) * v_c
    put(F_BON, bonus)

    rq = p_ret[:, 0:BR_W]
    rk = p_ret[:, BR_W:2 * BR_W] * HEAD_DIM ** -0.5
    if use_rope:
        cosf = cos_ref[...]
        sins = sin_ref[...]
        low = (_iota((tb, BR_W), 1) % HEAD_DIM) < (HEAD_DIM // 2)

        def rope(t):
            partner = jnp.where(low, pltpu.roll(t, BR_W - HEAD_DIM // 2, 1),
                                pltpu.roll(t, HEAD_DIM // 2, 1))
            return t * cosf + partner * sins

        rq = rope(rq)
        rk = rope(rk)
    put(F_RQ, rq)
    put(F_RK, rk)
    put(F_RV, p_ret[:, 2 * BR_W:3 * BR_W])
    put(F_RG, p_ret[:, 3 * BR_W:4 * BR_W])

    fb = p_fnet.astype(BF16)
    fc_ref[0] = jnp.dot(fb, dft_ref[0], preferred_element_type=F32)
    fs_ref[0] = jnp.dot(fb, dft_ref[1], preferred_element_type=F32)


def _pre(x, mods, lw, use_rope, rope_tabs, dft_c):
    B, T, D = x.shape
    tb = min(TOKEN_TILE, T)
    nt = T // tb
    hb8 = tb // HALO
    n_h = T // HALO
    const2 = lambda b, i: (0, 0)
    const3 = lambda b, i: (0, 0, 0)
    kern = functools.partial(_pre_kernel, use_rope, tb, nt)
    return pl.pallas_call(
        kern,
        grid=(B, nt),
        in_specs=[
            pl.BlockSpec((1, tb, D), lambda b, i: (b, i, 0)),
            pl.BlockSpec((1, HALO, D), lambda b, i: (b, jnp.maximum(i * hb8 - 1, 0), 0)),
            pl.BlockSpec((1, HALO, D), lambda b, i: (b, jnp.minimum((i + 1) * hb8, n_h - 1), 0)),
            pl.BlockSpec((1, 1, 6 * D), lambda b, i: (b, 0, 0)),
            pl.BlockSpec((1, D), const2),
            pl.BlockSpec((D, P_IN), const2, pipeline_mode=pl.Buffered(1)),
            pl.BlockSpec((D, LR_COLS), const2, pipeline_mode=pl.Buffered(1)),
            pl.BlockSpec((7, 256, BR_W), const3, pipeline_mode=pl.Buffered(1)),
            pl.BlockSpec((16, BR_W), const2),
            pl.BlockSpec((3, 3 * BR_W), const2),
            pl.BlockSpec((2, BR_W, BR_W), const3),
            pl.BlockSpec((tb, BR_W), lambda b, i: (i, 0)),
            pl.BlockSpec((tb, BR_W), lambda b, i: (i, 0)),
        ],
        out_specs=[
            pl.BlockSpec((1, tb, N_FEAT * BR_W), lambda b, i: (b, i, 0)),
            pl.BlockSpec((1, tb, D), lambda b, i: (b, i, 0)),
            pl.BlockSpec((1, tb, BR_W), lambda b, i: (b, i, 0)),
            pl.BlockSpec((1, tb, BR_W), lambda b, i: (b, i, 0)),
        ],
        out_shape=[
            jax.ShapeDtypeStruct((B, T, N_FEAT * BR_W), F32),
            jax.ShapeDtypeStruct((B, T, D), BF16),
            jax.ShapeDtypeStruct((B, T, BR_W), F32),
            jax.ShapeDtypeStruct((B, T, BR_W), F32),
        ],
        scratch_shapes=[
            pltpu.VMEM((tb + 2 * HALO, LR_COLS), F32),
            pltpu.VMEM((tb + 2 * HALO, 3 * BR_W), F32),
        ],
        compiler_params=_cparams(("parallel", "arbitrary")),
        name="pre",
    )(x, x, x, mods, lw["g1"], lw["w_in"], lw["w_lr"],
      lw["w2"], lw["vec"], lw["rwkv_conv"], dft_c, rope_tabs[0], rope_tabs[1])


def _cumsum_chunk(x, rev):
    c = x.shape[0]
    tri = jnp.where(_tri(c, rev), 1.0, 0.0).astype(BF16)
    return _dot_exact_lhs(tri, x)


def _gla_chunks(chains):
    c = chains[0][0].shape[0]
    nblk = c // GLA_SUB
    ones_h = _head_ones()
    bd = (_iota((BR_W, BR_W), 0) // HEAD_DIM) == (_iota((BR_W, BR_W), 1) // HEAD_DIM)
    cums = [_cumsum_chunk(ch[3], ch[5]) for ch in chains]

    same = ((_iota((HEADS * GLA_SUB, BR_W), 0) // GLA_SUB)
            == (_iota((HEADS * GLA_SUB, BR_W), 1) // HEAD_DIM))
    offd, inter, st_new = [], [], []
    for (q, k, v, la, st, rev), cum in zip(chains, cums):
        cum_end = cum[0:1] if rev else cum[c - 1:c]
        parts = []
        for blk in range(nblk):
            base = blk * GLA_SUB
            if rev:
                k_lo, k_hi, ref = base + GLA_SUB, c, base + GLA_SUB
            else:
                k_lo, k_hi, ref = 0, base, base - 1
            if k_hi <= k_lo:
                parts.append(None)
                continue
            qt = q[base:base + GLA_SUB] * jnp.exp(cum[base:base + GLA_SUB] - cum[ref:ref + 1])
            kt = k[k_lo:k_hi] * jnp.exp(cum[ref:ref + 1] - cum[k_lo:k_hi])
            sc = _dot_nt(_stack_heads(qt), kt)
            ov = jnp.where(same, _dot(sc, v[k_lo:k_hi]), 0.0)
            parts.append(sum(ov[h * GLA_SUB:(h + 1) * GLA_SUB] for h in range(HEADS)))
        offd.append(parts)
        inter.append(_dot_nt(q * jnp.exp(cum), st))
        st_new.append(st * jnp.exp(cum_end)
                      + jnp.where(bd, _dot_tn(v, k * jnp.exp(cum_end - cum)), 0.0))

    sub = _iota((8, BR_W), 0)
    bias = {rev: [jnp.where((sub <= jj) if rev else (sub >= jj), 0.0, NEG_BIG) for jj in range(8)]
            for rev in {ch[5] for ch in chains}}
    reds, spans = [], []
    for (q, k, v, la, st, rev), cum in zip(chains, cums):
        cum2 = cum * LOG2E
        pieces, sp = [], []
        q_fac, k_fac = {}, {}
        for j in range(c):
            base = (j // GLA_SUB) * GLA_SUB
            own = (j // 8) * 8
            arg = cum2[own:own + 8] - cum2[j:j + 1] + bias[rev][j - own]
            pieces.append(q[own:own + 8] * (k[j:j + 1] * jnp.exp2(arg)))
            sp.append((j, own))
            for g0 in (range(base, own, 8) if rev else range(own + 8, base + GLA_SUB, 8)):
                ref = g0 + 8 if rev else g0 - 1
                if g0 not in q_fac:
                    q_fac[g0] = q[g0:g0 + 8] * jnp.exp2(cum2[g0:g0 + 8] - cum2[ref:ref + 1])
                if (own, g0) not in k_fac:
                    k_fac[own, g0] = k[own:own + 8] * jnp.exp2(cum2[ref:ref + 1] - cum2[own:own + 8])
                pieces.append(q_fac[g0] * k_fac[own, g0][j - own:j - own + 1])
                sp.append((j, g0))
        reds.append(jnp.dot(jnp.concatenate(pieces, axis=0).astype(BF16), ones_h,
                            preferred_element_type=F32))
        spans.append(sp)

    outs = []
    for ci, (q, k, v, la, st, rev) in enumerate(chains):
        groups = [None] * (c // 8)
        for idx, (j, g0) in enumerate(spans[ci]):
            piece = reds[ci][idx * 8:idx * 8 + 8] * v[j:j + 1]
            gi = g0 // 8
            groups[gi] = piece if groups[gi] is None else groups[gi] + piece
        blocks = []
        for blk in range(nblk):
            o_blk = jnp.concatenate(groups[blk * GLA_SUB // 8:(blk + 1) * GLA_SUB // 8], axis=0)
            if offd[ci][blk] is not None:
                o_blk = o_blk + offd[ci][blk]
            blocks.append(o_blk)
        outs.append((jnp.concatenate(blocks, axis=0) + inter[ci], st_new[ci]))
    return outs


def _ret_chunks(chains):
    bd = (_iota((BR_W, BR_W), 0) // HEAD_DIM) == (_iota((BR_W, BR_W), 1) // HEAD_DIM)
    sc = [_dot_nt(q, _stack_heads(k)) * dmat for q, k, v, s, dmat, _, _, _ in chains]
    inter = [_dot(q * qdec, s) for q, k, v, s, _, qdec, _, _ in chains]
    upd = [_dot_tn(k * kdec, v) for q, k, v, s, _, _, kdec, _ in chains]
    outs = []
    for ch, sc_i, inter_i, upd_i in zip(chains, sc, inter, upd):
        q, k, v, s, _, _, _, cdec = ch
        o = _dot(sc_i, _stack_heads(v)) + inter_i
        outs.append((o, s * cdec + jnp.where(bd, upd_i, 0.0)))
    return outs


def _tri_wide(c, rev, strict):
    t = _iota((c, HEADS * c), 0)
    s = _iota((c, HEADS * c), 1) % c
    if rev:
        return (s > t) if strict else (s >= t)
    return (s < t) if strict else (s <= t)


def _rwkv_chunks(chains):
    c = chains[0][0].shape[0]
    n = HEADS * c
    bf = lambda t: t.astype(BF16)
    stack = lambda t: bf(_stack_heads(t))
    masks = {rev: (_tri_wide(c, rev, True), _tri_wide(c, rev, False))
             for rev in {ch[7] for ch in chains}}
    eye = jnp.where(_iota((c, n), 1) % c == _iota((c, n), 0), 1.0, 0.0)
    bd = (_iota((BR_W, BR_W), 0) // HEAD_DIM) == (_iota((BR_W, BR_W), 1) // HEAD_DIM)
    cums = [_cumsum_chunk(ch[5], ch[7]) for ch in chains]
    st = []
    for (r, v, kk, kd, asig, lw, s, rev), cum in zip(chains, cums):
        cum_end = cum[0:1] if rev else cum[c - 1:c]
        einv = jnp.exp(-cum)
        ehat = jnp.exp(cum_end - cum)
        bvec = kk * asig
        st.append(dict(
            a=bf(-kk * jnp.exp(cum - lw)), r=bf(r * jnp.exp(cum)),
            b_s=stack(bvec * einv), k_s=stack(kd * einv), v=bf(v), v_s=stack(v),
            bh=bf(bvec * ehat), kh=bf(kd * ehat), g_end=jnp.exp(cum_end), s=s, sb=bf(s), rev=rev))
    for d in st:
        d["m"] = _dot_nt(jnp.concatenate([d["a"], d["r"]], axis=0),
                         jnp.concatenate([d["b_s"], d["k_s"]], axis=0))
    for d in st:
        strict, incl = masks[d["rev"]]
        m = d.pop("m")
        d["p"] = jnp.where(strict, m[0:c, 0:n], 0.0)
        d["t"] = eye + d["p"]
        d["ps"] = stack(d["p"])
        d["m_ak"] = bf(jnp.where(strict, m[0:c, n:2 * n], 0.0))
        d["n_rb"] = bf(jnp.where(incl, m[c:2 * c, 0:n], 0.0))
        d["n_rk"] = bf(jnp.where(incl, m[c:2 * c, n:2 * n], 0.0))
    for d in st:
        wy = (_dot_nt(jnp.concatenate([d["a"], d["r"]], axis=0), d["sb"])
              + _dot(jnp.concatenate([d["m_ak"], d["n_rk"]], axis=0), d["v_s"]))
        d["w"], d["y0"] = wy[0:c], wy[c:2 * c]
    for d in st:
        d["p"] = _dot(d["p"], d["ps"])
        d["ps"] = stack(d["p"])
    for _ in range(int(math.log2(c)) - 2):
        for d in st:
            tp = _dot(jnp.concatenate([d["t"], d["p"]], axis=0), d["ps"])
            d["t"] = d["t"] + tp[0:c]
            d["p"] = tp[c:2 * c]
            d["ps"] = stack(d["p"])
    for d in st:
        d["t"] = d["t"] + _dot(d["t"], d["ps"])
    for d in st:
        d["u"] = _dot(d["t"], stack(d["w"]))
    outs = []
    for d in st:
        y = d["y0"] + _dot(d["n_rb"], stack(d["u"]))
        upd = _dot_tn(jnp.concatenate([bf(d["u"]), d["v"]], axis=0),
                      jnp.concatenate([d["bh"], d["kh"]], axis=0))
        outs.append((y, d["s"] * d["g_end"] + jnp.where(bd, upd, 0.0)))
    return outs


def _scan_kernel(kind, n_in, n_tab, nc, bb, *refs):
    fwd = refs[0:n_in]
    bwd = refs[n_in:2 * n_in]
    tabs = refs[2 * n_in:2 * n_in + n_tab]
    s0_ref = refs[2 * n_in + n_tab]
    of_ref, ob_ref, sf_ref, st_ref = refs[2 * n_in + n_tab + 1:]
    i = pl.program_id(1)

    @pl.when(i == 0)
    def _():
        st_ref[...] = s0_ref[...]

    chains = [(dr, b) for b in range(bb) for dr in range(2)]
    ins_of = lambda dr, b: [t[b] for t in (fwd, bwd)[dr]]
    if kind == "rwkv":
        res = _rwkv_chunks([(*ins_of(dr, b), st_ref[dr, b], dr == 1) for dr, b in chains])
    elif kind == "gla":
        res = _gla_chunks([(*ins_of(dr, b), st_ref[dr, b], dr == 1) for dr, b in chains])
    else:
        res = _ret_chunks([(*ins_of(dr, b), st_ref[dr, b], *(t[dr] for t in tabs))
                           for dr, b in chains])
    for (dr, b), (o, s_new) in zip(chains, res):
        (of_ref, ob_ref)[dr][b] = o
        st_ref[dr, b] = s_new

    @pl.when(i == nc - 1)
    def _():
        sf_ref[...] = st_ref[...]


_SCAN_ROWS = {"gla": 8, "ret": 8, "rwkv": 8}


def _scan(kind, feats, cols_f, cols_b, s0, tabs=()):
    B, T, _ = feats.shape
    c = min(SCAN_CHUNK[kind], T)
    nc = T // c
    bb = math.gcd(_SCAN_ROWS[kind], B)
    n_in = len(cols_f)
    in_specs = []
    for j in cols_f:
        in_specs.append(pl.BlockSpec((bb, c, BR_W), lambda b, i, j=j: (b, i, j)))
    for j in cols_b:
        in_specs.append(pl.BlockSpec((bb, c, BR_W), lambda b, i, j=j: (b, nc - 1 - i, j)))
    for t in tabs:
        in_specs.append(pl.BlockSpec(t.shape, lambda b, i, nd=t.ndim: (0,) * nd))
    in_specs.append(pl.BlockSpec((2, bb, BR_W, BR_W), lambda b, i: (0, b, 0, 0)))
    kern = functools.partial(_scan_kernel, kind, n_in, len(tabs), nc, bb)
    return pl.pallas_call(
        kern,
        grid=(B // bb, nc),
        in_specs=in_specs,
        out_specs=[
            pl.BlockSpec((bb, c, BR_W), lambda b, i: (b, i, 0)),
            pl.BlockSpec((bb, c, BR_W), lambda b, i: (b, nc - 1 - i, 0)),
            pl.BlockSpec((2, bb, BR_W, BR_W), lambda b, i: (0, b, 0, 0)),
        ],
        out_shape=[
            jax.ShapeDtypeStruct((B, T, BR_W), F32),
            jax.ShapeDtypeStruct((B, T, BR_W), F32),
            jax.ShapeDtypeStruct((2, B, BR_W, BR_W), F32),
        ],
        scratch_shapes=[pltpu.VMEM((2, bb, BR_W, BR_W), F32)],
        compiler_params=_cparams(("parallel", "arbitrary")),
        name=kind,
    )(*([feats] * (2 * n_in)), *tabs, s0)


def _ret_tables(c):
    pos = np.arange(c, dtype=np.float64)
    lane_head = np.arange(BR_W) // HEAD_DIM
    dmats, qd, kd, cd = [], [], [], []
    for dr in range(2):
        expo = -5.0 - np.arange(HEADS, dtype=np.float64)
        if dr == 1:
            expo = expo[::-1]
        log_g = np.log1p(-np.exp2(expo))
        if dr == 0:
            rel = pos[:, None] - pos[None, :]
            qpow = pos + 1.0
            kpow = c - 1.0 - pos
        else:
            rel = pos[None, :] - pos[:, None]
            qpow = c - pos
            kpow = pos
        tri = rel >= 0
        dm = np.where(tri[None], np.exp(np.where(tri, rel, 0.0)[None] * log_g[:, None, None]), 0.0)
        dmats.append(np.concatenate([dm[h] for h in range(HEADS)], axis=1))
        qd.append(np.exp(qpow[:, None] * log_g[lane_head][None, :]))
        kd.append(np.exp(kpow[:, None] * log_g[lane_head][None, :]))
        cd.append(np.exp(c * log_g[lane_head])[None, :])
    f = lambda xs: jnp.asarray(np.stack(xs), dtype=F32)
    return f(dmats), f(qd), f(kd), f(cd)


def _fnet_dense_kernel(t_len, cs_ref, g1_ref, g2_ref, o_ref):
    o_ref[0] = (jnp.dot(cs_ref[:, 0:t_len], g1_ref[0].astype(BF16), preferred_element_type=F32)
                + jnp.dot(cs_ref[:, t_len:2 * t_len], g2_ref[0].astype(BF16),
                          preferred_element_type=F32))


def _fnet_dense(g1, g2, cs):
    B, T, _ = g1.shape
    tb = min(TOKEN_TILE, T)
    return pl.pallas_call(
        functools.partial(_fnet_dense_kernel, T),
        grid=(T // tb, B),
        in_specs=[
            pl.BlockSpec((tb, 2 * T), lambda i, b: (i, 0)),
            pl.BlockSpec((1, T, BR_W), lambda i, b: (b, 0, 0)),
            pl.BlockSpec((1, T, BR_W), lambda i, b: (b, 0, 0)),
        ],
        out_specs=pl.BlockSpec((1, tb, BR_W), lambda i, b: (b, i, 0)),
        out_shape=jax.ShapeDtypeStruct((B, T, BR_W), F32),
        compiler_params=_cparams(("arbitrary", "arbitrary")),
        name="fnet",
    )(cs, g1, g2)


def _bf16_const(table):
    return jnp.asarray(table, dtype=F32).astype(BF16)


FFT_R = 8


def _fft_kernel(n, k1_ref, k2_ref, tc_ref, ts_ref, g1_ref, g2_ref, o_ref, zr_ref, zi_ref):
    m = n * FFT_R
    for j in range(n // FFT_R):
        sl = slice(j * FFT_R, (j + 1) * FFT_R)
        x = jnp.concatenate([g1_ref[0, :, sl, :].reshape(m, BR_W),
                             g2_ref[0, :, sl, :].reshape(m, BR_W)], axis=1).astype(BF16)
        y = jnp.dot(k1_ref[...], x, preferred_element_type=F32)
        yr = y[0:m, 0:BR_W] - y[m:2 * m, BR_W:2 * BR_W]
        yi = -y[0:m, BR_W:2 * BR_W] - y[m:2 * m, 0:BR_W]
        tc = tc_ref[:, sl, :].reshape(m, BR_W)
        ts = ts_ref[:, sl, :].reshape(m, BR_W)
        zr_ref[:, sl, :] = (tc * yr + ts * yi).reshape(n, FFT_R, BR_W)
        zi_ref[:, sl, :] = (tc * yi - ts * yr).reshape(n, FFT_R, BR_W)
    for j in range(n // FFT_R):
        sl = slice(j * FFT_R, (j + 1) * FFT_R)
        z = jnp.concatenate([zr_ref[sl].reshape(m, BR_W), zi_ref[sl].reshape(m, BR_W)],
                            axis=0).astype(BF16)
        o = jnp.dot(k2_ref[...], z, preferred_element_type=F32)
        o_ref[0, :, sl, :] = o.reshape(n, FFT_R, BR_W)


def _fnet_fft(g1, g2):
    B, T, _ = g1.shape
    n = int(round(math.sqrt(T)))
    assert n * n == T and n % FFT_R == 0
    m = n * FFT_R
    idx = np.arange(n)
    a_n = 2.0 * np.pi * ((idx[:, None] * idx[None, :]) % n) / n
    eye = np.eye(FFT_R)
    k1 = np.concatenate([np.einsum("ap,ij->aipj", f, eye).reshape(m, m)
                         for f in (np.cos(a_n), np.sin(a_n))], axis=0)
    k2 = np.concatenate([np.einsum("pr,ij->pijr", f, eye).reshape(m, m)
                         for f in (np.cos(a_n), np.sin(a_n))], axis=1)
    a_t = 2.0 * np.pi * (idx[:, None] * idx[None, :]) / T
    tc = jnp.broadcast_to(jnp.asarray(np.cos(a_t), dtype=F32)[:, :, None], (n, n, BR_W))
    ts = jnp.broadcast_to(jnp.asarray(np.sin(a_t), dtype=F32)[:, :, None], (n, n, BR_W))
    const2 = lambda b: (0, 0)
    const3 = lambda b: (0, 0, 0)
    one = pl.Buffered(1)
    o = pl.pallas_call(
        functools.partial(_fft_kernel, n),
        grid=(B,),
        in_specs=[
            pl.BlockSpec((2 * m, m), const2, pipeline_mode=one),
            pl.BlockSpec((m, 2 * m), const2, pipeline_mode=one),
            pl.BlockSpec((n, n, BR_W), const3, pipeline_mode=one),
            pl.BlockSpec((n, n, BR_W), const3, pipeline_mode=one),
            pl.BlockSpec((1, n, n, BR_W), lambda b: (b, 0, 0, 0)),
            pl.BlockSpec((1, n, n, BR_W), lambda b: (b, 0, 0, 0)),
        ],
        out_specs=pl.BlockSpec((1, n, n, BR_W), lambda b: (b, 0, 0, 0)),
        out_shape=jax.ShapeDtypeStruct((B, n, n, BR_W), F32),
        scratch_shapes=[pltpu.VMEM((n, n, BR_W), F32), pltpu.VMEM((n, n, BR_W), F32)],
        compiler_params=_cparams(("arbitrary",)),
        name="fft",
    )(_bf16_const(k1), _bf16_const(k2), tc, ts,
      g1.reshape(B, n, n, BR_W), g2.reshape(B, n, n, BR_W))
    return o.reshape(B, T, BR_W)


def _time_dft_table(T):
    t = np.arange(T)
    ang = ((t[:, None] * t[None, :]) % T) * (2.0 * np.pi / T)
    return _bf16_const(np.concatenate([np.cos(ang), -np.sin(ang)], axis=1))


def _channel_dft_tables(T):
    ch = np.arange(HEAD_DIM)
    a64 = 2.0 * np.pi * ((ch[:, None] * ch[None, :]) % HEAD_DIM) / HEAD_DIM
    scale = (T * HEAD_DIM) ** -0.5
    eye = np.eye(HEADS)
    cbd = np.kron(eye, np.cos(a64)) * scale
    sbd = np.kron(eye, np.sin(a64)) * scale
    return _bf16_const(np.stack([cbd, sbd]))


def _head_norm(o, ones_h, center):
    if center:
        o = o - _dot(o, ones_h) * (1.0 / HEAD_DIM)
    var = _dot(o * o, ones_h) * (1.0 / HEAD_DIM)
    return o * lax.rsqrt(var + EPS)


def _merge_kernel(x_ref, hb_ref, mod_ref, gg_ref, rg_ref, wg_ref, bon_ref,
                  ogf_ref, ogb_ref, orf_ref, orb_ref, oyf_ref, oyb_ref, fn_ref,
                  gn_ref, wgate_ref, bgate_ref, wbr_ref, wout_ref, o_ref):
    ones_h = _head_ones()
    gn = gn_ref[...]
    gla = _head_norm(ogf_ref[0] + ogb_ref[0], ones_h, False) * gn[0:1] * _silu(gg_ref[0])
    ret = _head_norm(orf_ref[0] + orb_ref[0], ones_h, True) * gn[1:2] * _silu(rg_ref[0])
    rwkv = (_head_norm(oyf_ref[0] + oyb_ref[0], ones_h, True) * gn[2:3] + bon_ref[0]) * wg_ref[0]
    outs = (gla, ret, rwkv, fn_ref[0])
    hb = hb_ref[0]
    z = None
    for br in range(4):
        gate = _sigmoid(jnp.dot(hb, wgate_ref[br], preferred_element_type=F32) + bgate_ref[br])
        term = gate * _dot(outs[br], wbr_ref[br])
        z = term if z is None else z + term
    y = _dot(z, wout_ref[...])
    gate1 = mod_ref[0][:, 2 * D_MODEL:3 * D_MODEL]
    o_ref[0] = x_ref[0] + gate1 * y


def _merge(x, hb, mods, feats, scans, fnet_out, lw):
    B, T, D = x.shape
    tb = min(TOKEN_TILE, T)
    tok = lambda w: pl.BlockSpec((1, tb, w), lambda b, i: (b, i, 0))
    fcol = lambda j: pl.BlockSpec((1, tb, BR_W), lambda b, i, j=j: (b, i, j))
    const2 = lambda b, i: (0, 0)
    const3 = lambda b, i: (0, 0, 0)
    return pl.pallas_call(
        _merge_kernel,
        grid=(B, T // tb),
        in_specs=[
            tok(D), tok(D),
            pl.BlockSpec((1, 1, 6 * D), lambda b, i: (b, 0, 0)),
            fcol(F_GG), fcol(F_RG), fcol(F_WG), fcol(F_BON),
            tok(BR_W), tok(BR_W), tok(BR_W), tok(BR_W), tok(BR_W), tok(BR_W), tok(BR_W),
            pl.BlockSpec((3, BR_W), const2),
            pl.BlockSpec((4, D, D), const3, pipeline_mode=pl.Buffered(1)),
            pl.BlockSpec((4, 1, D), const3),
            pl.BlockSpec((4, BR_W, D), const3, pipeline_mode=pl.Buffered(1)),
            pl.BlockSpec((D, D), const2, pipeline_mode=pl.Buffered(1)),
        ],
        out_specs=tok(D),
        out_shape=jax.ShapeDtypeStruct((B, T, D), F32),
        compiler_params=_cparams(("parallel", "arbitrary")),
        name="merge",
    )(x, hb, mods, feats, feats, feats, feats, *scans, fnet_out,
      lw["gn"], lw["w_gate"], lw["b_gate"], lw["w_br"], lw["w_out"])


FFN_COL_CHUNKS = (1024, 1024, 768)


def _ffn_kernel(final, tb, nt,
                x_ref, xp_ref, xn_ref, mod_ref, g2_ref, up_ref, cw_ref, cb_ref,
                down_ref, gfin_ref, o_ref, act_ref):
    i = pl.program_id(1)
    n_ext = tb + 2 * HALO
    x_ext = jnp.concatenate([xp_ref[0], x_ref[0], xn_ref[0]], axis=0)
    mod = mod_ref[0]
    shift = mod[:, 3 * D_MODEL:4 * D_MODEL]
    scale = mod[:, 4 * D_MODEL:5 * D_MODEL]
    h2 = (_rms_rows(x_ext) * g2_ref[...]) * (1.0 + scale) + shift
    h2 = h2 * _halo_keep(n_ext, tb, i, nt)
    h2_ext = h2.astype(BF16)
    h2_mid = h2[HALO:HALO + tb].astype(BF16)
    cw = cw_ref[...]
    cb = cb_ref[...]
    lo = 0
    for width in FFN_COL_CHUNKS:
        a_ext = jnp.dot(h2_ext, up_ref[:, lo:lo + width], preferred_element_type=F32)
        u = jnp.dot(h2_mid, up_ref[:, D_FF + lo:D_FF + lo + width], preferred_element_type=F32)
        a = (a_ext[HALO - 1:HALO - 1 + tb] * cw[0:1, lo:lo + width]
             + a_ext[HALO:HALO + tb] * cw[1:2, lo:lo + width]
             + a_ext[HALO + 1:HALO + 1 + tb] * cw[2:3, lo:lo + width] + cb[:, lo:lo + width])
        act_ref[:, lo:lo + width] = (_silu(a) * u).astype(BF16)
        lo += width
    y = jnp.dot(act_ref[...], down_ref[...], preferred_element_type=F32)
    gate2 = mod[:, 5 * D_MODEL:6 * D_MODEL]
    res = x_ref[0] + gate2 * y
    if final:
        res = _rms_rows(res) * gfin_ref[...]
    o_ref[0] = res


def _ffn(x, mods, lw, g_final, final):
    B, T, D = x.shape
    tb = min(TOKEN_TILE, T)
    nt = T // tb
    hb8 = tb // HALO
    n_h = T // HALO
    const2 = lambda b, i: (0, 0)
    resident = lambda shape: pl.BlockSpec(shape, const2, pipeline_mode=pl.Buffered(1))
    kern = functools.partial(_ffn_kernel, final, tb, nt)
    return pl.pallas_call(
        kern,
        grid=(B, nt),
        in_specs=[
            pl.BlockSpec((1, tb, D), lambda b, i: (b, i, 0)),
            pl.BlockSpec((1, HALO, D), lambda b, i: (b, jnp.maximum(i * hb8 - 1, 0), 0)),
            pl.BlockSpec((1, HALO, D), lambda b, i: (b, jnp.minimum((i + 1) * hb8, n_h - 1), 0)),
            pl.BlockSpec((1, 1, 6 * D), lambda b, i: (b, 0, 0)),
            pl.BlockSpec((1, D), const2),
            resident((D, 2 * D_FF)),
            pl.BlockSpec((3, D_FF), const2),
            pl.BlockSpec((1, D_FF), const2),
            resident((D_FF, D)),
            pl.BlockSpec((1, D), const2),
        ],
        out_specs=pl.BlockSpec((1, tb, D), lambda b, i: (b, i, 0)),
        out_shape=jax.ShapeDtypeStruct((B, T, D), F32),
        scratch_shapes=[pltpu.VMEM((tb, D_FF), BF16)],
        compiler_params=_cparams(("parallel", "arbitrary")),
        name="ffn",
    )(x, x, x, mods, lw["g2"], lw["ffn_up"], lw["ffn_conv"], lw["ffn_conv_b"],
      lw["ffn_down"], g_final)


def _rope_tables(T):
    rows = T // GRID_W
    row = jnp.repeat(jnp.arange(rows, dtype=F32), GRID_W)
    colp = jnp.tile(jnp.arange(GRID_W, dtype=F32), rows)
    n_freq = HEAD_DIM // 4
    inv = ROPE_BASE ** (-jnp.arange(n_freq, dtype=F32) / n_freq)
    ang = jnp.concatenate([row[:, None] * inv, colp[:, None] * inv], axis=-1)
    cos, sin = jnp.cos(ang), jnp.sin(ang)
    cos_h = jnp.concatenate([cos, cos], axis=-1)
    sin_h = jnp.concatenate([-sin, sin], axis=-1)
    return jnp.tile(cos_h, (1, HEADS)), jnp.tile(sin_h, (1, HEADS))


def _pad_rows(w, lo, n):
    return jnp.zeros((n, w.shape[1]), w.dtype).at[lo:lo + w.shape[0]].set(w)


def _layer_weights(l, p):
    D = D_MODEL
    bf = lambda t: t.astype(BF16)
    zcols = lambda w, n: jnp.concatenate([w, jnp.zeros((w.shape[0], n - w.shape[1]), w.dtype)], axis=1)
    w_h = zcols(jnp.concatenate([p["gla_wa1"][l, 0], p["gla_wa1"][l, 1]], axis=1), 128)
    w_xw = jnp.concatenate([p["rwkv_w1"][l, 0], p["rwkv_w1"][l, 1]], axis=1)
    w_xa = jnp.concatenate([p["rwkv_a1"][l, 0], p["rwkv_a1"][l, 1]], axis=1)
    w_xg = zcols(p["rwkv_g1"][l], 256)
    mu = p["rwkv_mu"][l]
    w_lr = jnp.concatenate([w_xg, mu[2][:, None] * w_xg, w_h, w_xw, w_xa,
                            mu[0][:, None] * w_xw, mu[1][:, None] * w_xa,
                            jnp.zeros((D, LR_COLS - LR_AS - 128), F32)], axis=1)
    w2 = jnp.stack([
        _pad_rows(p["gla_wa2"][l, 0], 0, 256), _pad_rows(p["gla_wa2"][l, 1], GLA_LR, 256),
        _pad_rows(p["rwkv_w2"][l, 0], 0, 256), _pad_rows(p["rwkv_w2"][l, 1], RWKV_LR, 256),
        _pad_rows(p["rwkv_a2"][l, 0], 0, 256), _pad_rows(p["rwkv_a2"][l, 1], RWKV_LR, 256),
        _pad_rows(p["rwkv_g2"][l], 0, 256),
    ])
    vec = jnp.stack([p["gla_ba"][l, 0], p["gla_ba"][l, 1], p["rwkv_w0"][l, 0], p["rwkv_w0"][l, 1],
                     p["rwkv_a0"][l, 0], p["rwkv_a0"][l, 1], p["rwkv_kk"][l], p["rwkv_ka"][l],
                     p["rwkv_rk"][l]])
    vec = jnp.concatenate([vec, jnp.zeros((16 - vec.shape[0], BR_W), F32)], axis=0)
    return {
        "g1": p["g_norm1"][l].reshape(1, D), "g2": p["g_norm2"][l].reshape(1, D),
        "w_in": bf(p["w_in"][l]), "w_lr": bf(w_lr), "w2": bf(w2), "vec": vec,
        "rwkv_conv": p["rwkv_conv"][l],
        "gn": jnp.stack([p["gla_gn"][l], p["ret_gn"][l], p["rwkv_gn"][l]]),
        "w_gate": bf(p["w_gate"][l]), "b_gate": p["b_gate"][l].reshape(4, 1, D),
        "w_br": bf(p["w_br"][l]), "w_out": bf(p["w_out"][l]),
        "ffn_up": bf(p["ffn_up"][l]), "ffn_conv": p["ffn_conv"][l],
        "ffn_conv_b": p["ffn_conv_b"][l].reshape(1, D_FF), "ffn_down": bf(p["ffn_down"][l]),
    }


_GLA_F = (F_GQ, F_GK, F_GV, F_LAF)
_GLA_B = (F_GQ, F_GK, F_GV, F_LAB)
_RET = (F_RQ, F_RK, F_RV)
_RWKV_F = (F_WR, F_WV, F_WKK, F_KDF, F_ASF, F_LWF)
_RWKV_B = (F_WR, F_WV, F_WKK, F_KDB, F_ASB, F_LWB)


def _mixers(feats_c, feats_l, ret_tabs_c, ret_tabs_l, need_ctx):
    B = feats_l.shape[0]
    zero = jnp.zeros((2, B, BR_W, BR_W), F32)
    out_c, out_l = [], []
    for kind, cf, cb, tabs in (("gla", _GLA_F, _GLA_B, None), ("ret", _RET, _RET, True),
                               ("rwkv", _RWKV_F, _RWKV_B, None)):
        tc = ret_tabs_c if tabs else ()
        tl = ret_tabs_l if tabs else ()
        of_c, ob_c, s_c = _scan(kind, feats_c, cf, cb, zero, tc)
        of_l, ob_l, _ = _scan(kind, feats_l, cf, cb, s_c, tl)
        out_c += [of_c, ob_c]
        out_l += [of_l, ob_l]
    return out_l, (out_c if need_ctx else None)


def kernel(x, c, ctx, c_ctx, w_ada, b_ada, g_norm1, g_norm2, w_in, gla_wa1, gla_wa2, gla_ba, gla_gn, ret_gn, rwkv_conv, rwkv_mu, rwkv_w0, rwkv_w1, rwkv_w2, rwkv_a0, rwkv_a1, rwkv_a2, rwkv_g1, rwkv_g2, rwkv_kk, rwkv_ka, rwkv_rk, rwkv_gn, w_gate, b_gate, w_br, w_out, ffn_up, ffn_conv, ffn_conv_b, ffn_down, g_final):
    p = dict(g_norm1=g_norm1, g_norm2=g_norm2, w_in=w_in, gla_wa1=gla_wa1, gla_wa2=gla_wa2,
             gla_ba=gla_ba, gla_gn=gla_gn, ret_gn=ret_gn, rwkv_conv=rwkv_conv, rwkv_mu=rwkv_mu,
             rwkv_w0=rwkv_w0, rwkv_w1=rwkv_w1, rwkv_w2=rwkv_w2, rwkv_a0=rwkv_a0,
             rwkv_a1=rwkv_a1, rwkv_a2=rwkv_a2, rwkv_g1=rwkv_g1, rwkv_g2=rwkv_g2,
             rwkv_kk=rwkv_kk, rwkv_ka=rwkv_ka, rwkv_rk=rwkv_rk, rwkv_gn=rwkv_gn,
             w_gate=w_gate, b_gate=b_gate, w_br=w_br, w_out=w_out, ffn_up=ffn_up,
             ffn_conv=ffn_conv, ffn_conv_b=ffn_conv_b, ffn_down=ffn_down)
    B, T, D = x.shape
    Tc = ctx.shape[1]
    depth = w_ada.shape[0]

    cmat = jnp.concatenate([c, c_ctx[None, :], jnp.zeros((MOD_ROWS - B - 1, D), F32)], axis=0)
    mod_all = _modulation(cmat, w_ada, b_ada)

    rope_l = _rope_tables(T)
    rope_c = (jnp.zeros((Tc, BR_W), F32), jnp.zeros((Tc, BR_W), F32))
    dft_l = _channel_dft_tables(T)
    dft_c = _channel_dft_tables(Tc)
    cs_c = _time_dft_table(Tc)
    ret_tabs_l = _ret_tables(min(SCAN_CHUNK["ret"], T))
    ret_tabs_c = _ret_tables(min(SCAN_CHUNK["ret"], Tc))
    g_fin = g_final.reshape(1, D)

    for l in range(depth):
        last = l == depth - 1
        lw = _layer_weights(l, p)
        mods_l = mod_all[l, 0:B].reshape(B, 1, 6 * D)
        mods_c = jnp.broadcast_to(mod_all[l, B:B + 1].reshape(1, 1, 6 * D), (B, 1, 6 * D))

        feats_l, hb_l, g1_l, g2_l = _pre(x, mods_l, lw, True, rope_l, dft_l)
        feats_c, hb_c, g1_c, g2_c = _pre(ctx, mods_c, lw, False, rope_c, dft_c)
        scans_l, scans_c = _mixers(feats_c, feats_l, ret_tabs_c, ret_tabs_l, not last)

        fn_l = _fnet_fft(g1_l, g2_l)
        x = _merge(x, hb_l, mods_l, feats_l, scans_l, fn_l, lw)
        x = _ffn(x, mods_l, lw, g_fin, last)
        if not last:
            fn_c = _fnet_dense(g1_c, g2_c, cs_c)
            ctx = _merge(ctx, hb_c, mods_c, feats_c, scans_c, fn_c, lw)
            ctx = _ffn(ctx, mods_c, lw, g_fin, False)
    return x
```

```python
import functools
import math

import numpy as np
import jax
import jax.numpy as jnp
from jax import lax
from jax.experimental import pallas as pl
from jax.experimental.pallas import tpu as pltpu

F32 = jnp.float32
BF16 = jnp.bfloat16

D_MODEL = 1024
BR_W = 256
HEAD_DIM = 64
HEADS = 4
N_PARTS = 12
P_IN = N_PARTS * BR_W
GRID_W = 64
GLA_LR = 16
GLA_NORMALIZER = 16.0
RWKV_LR = 64
RWKV_G_LR = 160
D_FF = 2816
ROPE_BASE = 10000.0
EPS = 1e-6

SCAN_CHUNK = {"gla": 64, "ret": 128, "rwkv": 64}
GLA_SUB = 16
HALO = 8
NEG_BIG = -1e30
LOG2E = 1.4426950408889634

(F_GQ, F_GK, F_GV, F_GG, F_RQ, F_RK, F_RV, F_RG, F_WR, F_WV, F_WKK, F_WG, F_BON,
 F_LAF, F_LAB, F_LWF, F_LWB, F_KDF, F_KDB, F_ASF, F_ASB) = range(21)
N_FEAT = 21

LR_G, LR_GS, LR_H, LR_W, LR_A, LR_WS, LR_AS = 0, 256, 512, 640, 768, 896, 1024
LR_COLS = 1280

TOKEN_TILE = 512
MOD_ROWS = 16
MOD_COL_BLOCK = 1536
VMEM_LIMIT = 56 * 1024 * 1024


def _cparams(sem):
    return pltpu.CompilerParams(dimension_semantics=sem, vmem_limit_bytes=VMEM_LIMIT)


def _dot(a, b):
    return jnp.dot(a.astype(BF16), b.astype(BF16), preferred_element_type=F32)


def _dot_nt(a, b):
    return lax.dot_general(a.astype(BF16), b.astype(BF16), (((1,), (1,)), ((), ())),
                           preferred_element_type=F32)


def _dot_tn(a, b):
    return lax.dot_general(a.astype(BF16), b.astype(BF16), (((0,), (0,)), ((), ())),
                           preferred_element_type=F32)


def _split3(x):
    hi = x.astype(BF16)
    r1 = x - hi.astype(F32)
    mid = r1.astype(BF16)
    lo = (r1 - mid.astype(F32)).astype(BF16)
    return hi, mid, lo


def _dot_exact_lhs(a_bf16, x):
    hi, mid, lo = _split3(x)
    d = lambda t: jnp.dot(a_bf16, t, preferred_element_type=F32)
    return d(hi) + d(mid) + d(lo)


def _sigmoid(x):
    return 1.0 / (1.0 + jnp.exp(-x))


def _silu(x):
    return x * _sigmoid(x)


def _softplus(x):
    return jnp.maximum(x, 0.0) + jnp.log(1.0 + jnp.exp(-jnp.abs(x)))


def _iota(shape, dim):
    return lax.broadcasted_iota(jnp.int32, shape, dim)


def _head_ones():
    r = _iota((BR_W, BR_W), 0) // HEAD_DIM
    c = _iota((BR_W, BR_W), 1) // HEAD_DIM
    return jnp.where(r == c, 1.0, 0.0).astype(BF16)


def _stack_heads(x):
    c, w = x.shape
    xs = jnp.concatenate([x] * HEADS, axis=0)
    keep = (_iota((HEADS * c, w), 0) // c) == (_iota((HEADS * c, w), 1) // (w // HEADS))
    return jnp.where(keep, xs, 0.0)


def _tri(c, rev, strict=False):
    r = _iota((c, c), 0)
    s = _iota((c, c), 1)
    if rev:
        return (r < s) if strict else (r <= s)
    return (r > s) if strict else (r >= s)


def _halo_keep(n_ext, tb, i, nt):
    row = _iota((n_ext, 1), 0)
    first = jnp.where(i > 0, 1.0, 0.0)
    last = jnp.where(i < nt - 1, 1.0, 0.0)
    return jnp.where(row < HALO, first, jnp.where(row >= tb + HALO, last, 1.0))


def _rms_rows(x):
    return x * lax.rsqrt(jnp.mean(x * x, axis=-1, keepdims=True) + EPS)


def _mod_kernel(c_ref, w_ref, b_ref, o_ref):
    s = _silu(c_ref[...])
    o_ref[0] = _dot(s, w_ref[0]) + b_ref[0]


def _modulation(cmat, w_ada, b_ada):
    L = w_ada.shape[0]
    nblk = MOD_COL_BLOCK
    return pl.pallas_call(
        _mod_kernel,
        grid=(L, 6 * D_MODEL // nblk),
        in_specs=[
            pl.BlockSpec((MOD_ROWS, D_MODEL), lambda l, j: (0, 0)),
            pl.BlockSpec((1, D_MODEL, nblk), lambda l, j: (l, 0, j)),
            pl.BlockSpec((1, 1, nblk), lambda l, j: (l, 0, j)),
        ],
        out_specs=pl.BlockSpec((1, MOD_ROWS, nblk), lambda l, j: (l, 0, j)),
        out_shape=jax.ShapeDtypeStruct((L, MOD_ROWS, 6 * D_MODEL), F32),
        compiler_params=_cparams(("arbitrary", "arbitrary")),
        name="mod",
    )(cmat, w_ada, b_ada.reshape(L, 1, 6 * D_MODEL))


def _pre_kernel(use_rope, tb, nt,
                x_ref, xp_ref, xn_ref, mod_ref, g1_ref, win_ref, wlr_ref,
                w2_ref, vec_ref, conv_ref, dft_ref, cos_ref, sin_ref,
                f_ref, hb_ref, fc_ref, fs_ref, lr_ref, ps_ref):
    i = pl.program_id(1)
    n_ext = tb + 2 * HALO
    x_ext = jnp.concatenate([xp_ref[0], x_ref[0], xn_ref[0]], axis=0)
    mod = mod_ref[0]
    shift = mod[:, 0:D_MODEL]
    scale = mod[:, D_MODEL:2 * D_MODEL]
    h_ext = (_rms_rows(x_ext) * g1_ref[...]) * (1.0 + scale) + shift
    h_ext = h_ext * _halo_keep(n_ext, tb, i, nt)
    hb_ext = h_ext.astype(BF16)
    hb = h_ext[HALO:HALO + tb].astype(BF16)
    hb_ref[0] = hb
    lr_ref[...] = jnp.dot(hb_ext, wlr_ref[...], preferred_element_type=F32)
    ps_ref[...] = jnp.dot(hb_ext, win_ref[:, 8 * BR_W:11 * BR_W], preferred_element_type=F32)

    def mid(lo, hi):
        return lr_ref[HALO:HALO + tb, lo:hi]

    def shifted(lo, hi):
        return (0.5 * (lr_ref[HALO - 1:HALO - 1 + tb, lo:hi] + lr_ref[HALO + 1:HALO + 1 + tb, lo:hi])
                - lr_ref[HALO:HALO + tb, lo:hi])

    vec = vec_ref[...]
    ba_f, ba_b, w0_f, w0_b = vec[0:1], vec[1:2], vec[2:3], vec[3:4]
    a0_f, a0_b, kkw, kaw, rkw = vec[4:5], vec[5:6], vec[6:7], vec[7:8], vec[8:9]

    def put(j, val):
        f_ref[0, :, j * BR_W:(j + 1) * BR_W] = val

    zh = mid(LR_H, LR_H + 128)
    tw = jnp.tanh(mid(LR_W, LR_W + 128) + shifted(LR_WS, LR_WS + 128))
    ta = mid(LR_A, LR_A + 128) + shifted(LR_AS, LR_AS + 128)
    sg = _sigmoid(mid(LR_G, LR_G + 256) + shifted(LR_GS, LR_GS + 256))

    p_gla = jnp.dot(hb, win_ref[:, 0:4 * BR_W], preferred_element_type=F32)
    put(F_GQ, p_gla[:, 0:BR_W] * HEAD_DIM ** -0.5)
    put(F_GK, p_gla[:, BR_W:2 * BR_W])
    put(F_GV, p_gla[:, 2 * BR_W:3 * BR_W])
    put(F_GG, p_gla[:, 3 * BR_W:4 * BR_W])

    z_la = [_dot(zh, w2_ref[dr, 0:128, :]) for dr in range(2)]
    z_w = [_dot(tw, w2_ref[2 + dr, 0:128, :]) for dr in range(2)]
    z_a = [_dot(ta, w2_ref[4 + dr, 0:128, :]) for dr in range(2)]
    put(F_WG, _dot(sg, w2_ref[6]))

    for dr, (ba, dst) in enumerate(((ba_f, F_LAF), (ba_b, F_LAB))):
        put(dst, -_softplus(-(z_la[dr] + ba)) * (1.0 / GLA_NORMALIZER))

    p_ret = jnp.dot(hb, win_ref[:, 4 * BR_W:8 * BR_W], preferred_element_type=F32)

    conv = conv_ref[...]

    def dwconv(part):
        lo, hi = part * BR_W, (part + 1) * BR_W
        cw = conv[:, lo:hi]
        return (ps_ref[HALO - 1:HALO - 1 + tb, lo:hi] * cw[0:1]
                + ps_ref[HALO:HALO + tb, lo:hi] * cw[1:2]
                + ps_ref[HALO + 1:HALO + 1 + tb, lo:hi] * cw[2:3])

    r_c = dwconv(0)
    k_c = dwconv(1)
    v_c = dwconv(2)
    ones_h = _head_ones()
    kk = k_c * kkw
    kk = kk * lax.rsqrt(_dot(kk * kk, ones_h) + EPS)
    put(F_WR, r_c)
    put(F_WV, v_c)
    put(F_WKK, kk)

    p_fnet = jnp.dot(hb, win_ref[:, 11 * BR_W:12 * BR_W], preferred_element_type=F32)

    bonus = jnp.zeros((tb, BR_W), F32)
    for dr, (w0, a0, d_lw, d_kd, d_as) in enumerate(
            ((w0_f, a0_f, F_LWF, F_KDF, F_ASF), (w0_b, a0_b, F_LWB, F_KDB, F_ASB))):
        put(d_lw, -math.exp(-0.5) * _sigmoid(w0 + z_w[dr]))
        a_sig = _sigmoid(a0 + z_a[dr])
        kd = k_c * (1.0 + (a_sig - 1.0) * kaw)
        put(d_kd, kd)
        put(d_as, a_sig)
        bonus = bonus + _dot(r_c * kd * rkw, ones_h) * v_c
    put(F_BON, bonus)

    rq = p_ret[:, 0:BR_W]
    rk = p_ret[:, BR_W:2 * BR_W] * HEAD_DIM ** -0.5
    if use_rope:
        cosf = cos_ref[...]
        sins = sin_ref[...]
        low = (_iota((tb, BR_W), 1) % HEAD_DIM) < (HEAD_DIM // 2)

        def rope(t):
            partner = jnp.where(low, pltpu.roll(t, BR_W - HEAD_DIM // 2, 1),
                                pltpu.roll(t, HEAD_DIM // 2, 1))
            return t * cosf + partner * sins

        rq = rope(rq)
        rk = rope(rk)
    put(F_RQ, rq)
    put(F_RK, rk)
    put(F_RV, p_ret[:, 2 * BR_W:3 * BR_W])
    put(F_RG, p_ret[:, 3 * BR_W:4 * BR_W])

    fb = p_fnet.astype(BF16)
    fc_ref[0] = jnp.dot(fb, dft_ref[0], preferred_element_type=F32)
    fs_ref[0] = jnp.dot(fb, dft_ref[1], preferred_element_type=F32)


def _pre(x, mods, lw, use_rope, rope_tabs, dft_c):
    B, T, D = x.shape
    tb = min(TOKEN_TILE, T)
    nt = T // tb
    hb8 = tb // HALO
    n_h = T // HALO
    const2 = lambda b, i: (0, 0)
    const3 = lambda b, i: (0, 0, 0)
    kern = functools.partial(_pre_kernel, use_rope, tb, nt)
    return pl.pallas_call(
        kern,
        grid=(B, nt),
        in_specs=[
            pl.BlockSpec((1, tb, D), lambda b, i: (b, i, 0)),
            pl.BlockSpec((1, HALO, D), lambda b, i: (b, jnp.maximum(i * hb8 - 1, 0), 0)),
            pl.BlockSpec((1, HALO, D), lambda b, i: (b, jnp.minimum((i + 1) * hb8, n_h - 1), 0)),
            pl.BlockSpec((1, 1, 6 * D), lambda b, i: (b, 0, 0)),
            pl.BlockSpec((1, D), const2),
            pl.BlockSpec((D, P_IN), const2, pipeline_mode=pl.Buffered(1)),
            pl.BlockSpec((D, LR_COLS), const2, pipeline_mode=pl.Buffered(1)),
            pl.BlockSpec((7, 256, BR_W), const3, pipeline_mode=pl.Buffered(1)),
            pl.BlockSpec((16, BR_W), const2),
            pl.BlockSpec((3, 3 * BR_W), const2),
            pl.BlockSpec((2, BR_W, BR_W), const3),
            pl.BlockSpec((tb, BR_W), lambda b, i: (i, 0)),
            pl.BlockSpec((tb, BR_W), lambda b, i: (i, 0)),
        ],
        out_specs=[
            pl.BlockSpec((1, tb, N_FEAT * BR_W), lambda b, i: (b, i, 0)),
            pl.BlockSpec((1, tb, D), lambda b, i: (b, i, 0)),
            pl.BlockSpec((1, tb, BR_W), lambda b, i: (b, i, 0)),
            pl.BlockSpec((1, tb, BR_W), lambda b, i: (b, i, 0)),
        ],
        out_shape=[
            jax.ShapeDtypeStruct((B, T, N_FEAT * BR_W), F32),
            jax.ShapeDtypeStruct((B, T, D), BF16),
            jax.ShapeDtypeStruct((B, T, BR_W), F32),
            jax.ShapeDtypeStruct((B, T, BR_W), F32),
        ],
        scratch_shapes=[
            pltpu.VMEM((tb + 2 * HALO, LR_COLS), F32),
            pltpu.VMEM((tb + 2 * HALO, 3 * BR_W), F32),
        ],
        compiler_params=_cparams(("parallel", "arbitrary")),
        name="pre",
    )(x, x, x, mods, lw["g1"], lw["w_in"], lw["w_lr"],
      lw["w2"], lw["vec"], lw["rwkv_conv"], dft_c, rope_tabs[0], rope_tabs[1])


def _cumsum_chunk(x, rev):
    c = x.shape[0]
    tri = jnp.where(_tri(c, rev), 1.0, 0.0).astype(BF16)
    return _dot_exact_lhs(tri, x)


def _gla_chunks(chains):
    c = chains[0][0].shape[0]
    nblk = c // GLA_SUB
    ones_h = _head_ones()
    bd = (_iota((BR_W, BR_W), 0) // HEAD_DIM) == (_iota((BR_W, BR_W), 1) // HEAD_DIM)
    cums = [_cumsum_chunk(ch[3], ch[5]) for ch in chains]

    same = ((_iota((HEADS * GLA_SUB, BR_W), 0) // GLA_SUB)
            == (_iota((HEADS * GLA_SUB, BR_W), 1) // HEAD_DIM))
    offd, inter, st_new = [], [], []
    for (q, k, v, la, st, rev), cum in zip(chains, cums):
        cum_end = cum[0:1] if rev else cum[c - 1:c]
        parts = []
        for blk in range(nblk):
            base = blk * GLA_SUB
            if rev:
                k_lo, k_hi, ref = base + GLA_SUB, c, base + GLA_SUB
            else:
                k_lo, k_hi, ref = 0, base, base - 1
            if k_hi <= k_lo:
                parts.append(None)
                continue
            qt = q[base:base + GLA_SUB] * jnp.exp(cum[base:base + GLA_SUB] - cum[ref:ref + 1])
            kt = k[k_lo:k_hi] * jnp.exp(cum[ref:ref + 1] - cum[k_lo:k_hi])
            sc = _dot_nt(_stack_heads(qt), kt)
            ov = jnp.where(same, _dot(sc, v[k_lo:k_hi]), 0.0)
            parts.append(sum(ov[h * GLA_SUB:(h + 1) * GLA_SUB] for h in range(HEADS)))
        offd.append(parts)
        inter.append(_dot_nt(q * jnp.exp(cum), st))
        st_new.append(st * jnp.exp(cum_end)
                      + jnp.where(bd, _dot_tn(v, k * jnp.exp(cum_end - cum)), 0.0))

    sub = _iota((8, BR_W), 0)
    bias = {rev: [jnp.where((sub <= jj) if rev else (sub >= jj), 0.0, NEG_BIG) for jj in range(8)]
            for rev in {ch[5] for ch in chains}}
    reds, spans = [], []
    for (q, k, v, la, st, rev), cum in zip(chains, cums):
        cum2 = cum * LOG2E
        pieces, sp = [], []
        q_fac, k_fac = {}, {}
        for j in range(c):
            base = (j // GLA_SUB) * GLA_SUB
            own = (j // 8) * 8
            arg = cum2[own:own + 8] - cum2[j:j + 1] + bias[rev][j - own]
            pieces.append(q[own:own + 8] * (k[j:j + 1] * jnp.exp2(arg)))
            sp.append((j, own))
            for g0 in (range(base, own, 8) if rev else range(own + 8, base + GLA_SUB, 8)):
                ref = g0 + 8 if rev else g0 - 1
                if g0 not in q_fac:
                    q_fac[g0] = q[g0:g0 + 8] * jnp.exp2(cum2[g0:g0 + 8] - cum2[ref:ref + 1])
                if (own, g0) not in k_fac:
                    k_fac[own, g0] = k[own:own + 8] * jnp.exp2(cum2[ref:ref + 1] - cum2[own:own + 8])
                pieces.append(q_fac[g0] * k_fac[own, g0][j - own:j - own + 1])
                sp.append((j, g0))
        reds.append(jnp.dot(jnp.concatenate(pieces, axis=0).astype(BF16), ones_h,
                            preferred_element_type=F32))
        spans.append(sp)

    outs = []
    for ci, (q, k, v, la, st, rev) in enumerate(chains):
        groups = [None] * (c // 8)
        for idx, (j, g0) in enumerate(spans[ci]):
            piece = reds[ci][idx * 8:idx * 8 + 8] * v[j:j + 1]
            gi = g0 // 8
            groups[gi] = piece if groups[gi] is None else groups[gi] + piece
        blocks = []
        for blk in range(nblk):
            o_blk = jnp.concatenate(groups[blk * GLA_SUB // 8:(blk + 1) * GLA_SUB // 8], axis=0)
            if offd[ci][blk] is not None:
                o_blk = o_blk + offd[ci][blk]
            blocks.append(o_blk)
        outs.append((jnp.concatenate(blocks, axis=0) + inter[ci], st_new[ci]))
    return outs


def _ret_chunks(chains):
    bd = (_iota((BR_W, BR_W), 0) // HEAD_DIM) == (_iota((BR_W, BR_W), 1) // HEAD_DIM)
    sc = [_dot_nt(q, _stack_heads(k)) * dmat for q, k, v, s, dmat, _, _, _ in chains]
    inter = [_dot(q * qdec, s) for q, k, v, s, _, qdec, _, _ in chains]
    upd = [_dot_tn(k * kdec, v) for q, k, v, s, _, _, kdec, _ in chains]
    outs = []
    for ch, sc_i, inter_i, upd_i in zip(chains, sc, inter, upd):
        q, k, v, s, _, _, _, cdec = ch
        o = _dot(sc_i, _stack_heads(v)) + inter_i
        outs.append((o, s * cdec + jnp.where(bd, upd_i, 0.0)))
    return outs


def _tri_wide(c, rev, strict):
    t = _iota((c, HEADS * c), 0)
    s = _iota((c, HEADS * c), 1) % c
    if rev:
        return (s > t) if strict else (s >= t)
    return (s < t) if strict else (s <= t)


def _rwkv_chunks(chains):
    c = chains[0][0].shape[0]
    n = HEADS * c
    bf = lambda t: t.astype(BF16)
    stack = lambda t: bf(_stack_heads(t))
    masks = {rev: (_tri_wide(c, rev, True), _tri_wide(c, rev, False))
             for rev in {ch[7] for ch in chains}}
    eye = jnp.where(_iota((c, n), 1) % c == _iota((c, n), 0), 1.0, 0.0)
    bd = (_iota((BR_W, BR_W), 0) // HEAD_DIM) == (_iota((BR_W, BR_W), 1) // HEAD_DIM)
    cums = [_cumsum_chunk(ch[5], ch[7]) for ch in chains]
    st = []
    for (r, v, kk, kd, asig, lw, s, rev), cum in zip(chains, cums):
        cum_end = cum[0:1] if rev else cum[c - 1:c]
        einv = jnp.exp(-cum)
        ehat = jnp.exp(cum_end - cum)
        bvec = kk * asig
        st.append(dict(
            a=bf(-kk * jnp.exp(cum - lw)), r=bf(r * jnp.exp(cum)),
            b_s=stack(bvec * einv), k_s=stack(kd * einv), v=bf(v), v_s=stack(v),
            bh=bf(bvec * ehat), kh=bf(kd * ehat), g_end=jnp.exp(cum_end), s=s, sb=bf(s), rev=rev))
    for d in st:
        d["m"] = _dot_nt(jnp.concatenate([d["a"], d["r"]], axis=0),
                         jnp.concatenate([d["b_s"], d["k_s"]], axis=0))
    for d in st:
        strict, incl = masks[d["rev"]]
        m = d.pop("m")
        d["p"] = jnp.where(strict, m[0:c, 0:n], 0.0)
        d["t"] = eye + d["p"]
        d["ps"] = stack(d["p"])
        d["m_ak"] = bf(jnp.where(strict, m[0:c, n:2 * n], 0.0))
        d["n_rb"] = bf(jnp.where(incl, m[c:2 * c, 0:n], 0.0))
        d["n_rk"] = bf(jnp.where(incl, m[c:2 * c, n:2 * n], 0.0))
    for d in st:
        wy = (_dot_nt(jnp.concatenate([d["a"], d["r"]], axis=0), d["sb"])
              + _dot(jnp.concatenate([d["m_ak"], d["n_rk"]], axis=0), d["v_s"]))
        d["w"], d["y0"] = wy[0:c], wy[c:2 * c]
    for d in st:
        d["p"] = _dot(d["p"], d["ps"])
        d["ps"] = stack(d["p"])
    for _ in range(int(math.log2(c)) - 2):
        for d in st:
            tp = _dot(jnp.concatenate([d["t"], d["p"]], axis=0), d["ps"])
            d["t"] = d["t"] + tp[0:c]
            d["p"] = tp[c:2 * c]
            d["ps"] = stack(d["p"])
    for d in st:
        d["t"] = d["t"] + _dot(d["t"], d["ps"])
    for d in st:
        d["u"] = _dot(d["t"], stack(d["w"]))
    outs = []
    for d in st:
        y = d["y0"] + _dot(d["n_rb"], stack(d["u"]))
        upd = _dot_tn(jnp.concatenate([bf(d["u"]), d["v"]], axis=0),
                      jnp.concatenate([d["bh"], d["kh"]], axis=0))
        outs.append((y, d["s"] * d["g_end"] + jnp.where(bd, upd, 0.0)))
    return outs


def _scan_kernel(kind, n_in, n_tab, nc, bb, *refs):
    fwd = refs[0:n_in]
    bwd = refs[n_in:2 * n_in]
    tabs = refs[2 * n_in:2 * n_in + n_tab]
    s0_ref = refs[2 * n_in + n_tab]
    of_ref, ob_ref, sf_ref, st_ref = refs[2 * n_in + n_tab + 1:]
    i = pl.program_id(1)

    @pl.when(i == 0)
    def _():
        st_ref[...] = s0_ref[...]

    chains = [(dr, b) for b in range(bb) for dr in range(2)]
    ins_of = lambda dr, b: [t[b] for t in (fwd, bwd)[dr]]
    if kind == "rwkv":
        res = _rwkv_chunks([(*ins_of(dr, b), st_ref[dr, b], dr == 1) for dr, b in chains])
    elif kind == "gla":
        res = _gla_chunks([(*ins_of(dr, b), st_ref[dr, b], dr == 1) for dr, b in chains])
    else:
        res = _ret_chunks([(*ins_of(dr, b), st_ref[dr, b], *(t[dr] for t in tabs))
                           for dr, b in chains])
    for (dr, b), (o, s_new) in zip(chains, res):
        (of_ref, ob_ref)[dr][b] = o
        st_ref[dr, b] = s_new

    @pl.when(i == nc - 1)
    def _():
        sf_ref[...] = st_ref[...]


_SCAN_ROWS = {"gla": 8, "ret": 8, "rwkv": 8}


def _scan(kind, feats, cols_f, cols_b, s0, tabs=()):
    B, T, _ = feats.shape
    c = min(SCAN_CHUNK[kind], T)
    nc = T // c
    bb = math.gcd(_SCAN_ROWS[kind], B)
    n_in = len(cols_f)
    in_specs = []
    for j in cols_f:
        in_specs.append(pl.BlockSpec((bb, c, BR_W), lambda b, i, j=j: (b, i, j)))
    for j in cols_b:
        in_specs.append(pl.BlockSpec((bb, c, BR_W), lambda b, i, j=j: (b, nc - 1 - i, j)))
    for t in tabs:
        in_specs.append(pl.BlockSpec(t.shape, lambda b, i, nd=t.ndim: (0,) * nd))
    in_specs.append(pl.BlockSpec((2, bb, BR_W, BR_W), lambda b, i: (0, b, 0, 0)))
    kern = functools.partial(_scan_kernel, kind, n_in, len(tabs), nc, bb)
    return pl.pallas_call(
        kern,
        grid=(B // bb, nc),
        in_specs=in_specs,
        out_specs=[
            pl.BlockSpec((bb, c, BR_W), lambda b, i: (b, i, 0)),
            pl.BlockSpec((bb, c, BR_W), lambda b, i: (b, nc - 1 - i, 0)),
            pl.BlockSpec((2, bb, BR_W, BR_W), lambda b, i: (0, b, 0, 0)),
        ],
        out_shape=[
            jax.ShapeDtypeStruct((B, T, BR_W), F32),
            jax.ShapeDtypeStruct((B, T, BR_W), F32),
            jax.ShapeDtypeStruct((2, B, BR_W, BR_W), F32),
        ],
        scratch_shapes=[pltpu.VMEM((2, bb, BR_W, BR_W), F32)],
        compiler_params=_cparams(("parallel", "arbitrary")),
        name=kind,
    )(*([feats] * (2 * n_in)), *tabs, s0)


def _ret_tables(c):
    pos = np.arange(c, dtype=np.float64)
    lane_head = np.arange(BR_W) // HEAD_DIM
    dmats, qd, kd, cd = [], [], [], []
    for dr in range(2):
        expo = -5.0 - np.arange(HEADS, dtype=np.float64)
        if dr == 1:
            expo = expo[::-1]
        log_g = np.log1p(-np.exp2(expo))
        if dr == 0:
            rel = pos[:, None] - pos[None, :]
            qpow = pos + 1.0
            kpow = c - 1.0 - pos
        else:
            rel = pos[None, :] - pos[:, None]
            qpow = c - pos
            kpow = pos
        tri = rel >= 0
        dm = np.where(tri[None], np.exp(np.where(tri, rel, 0.0)[None] * log_g[:, None, None]), 0.0)
        dmats.append(np.concatenate([dm[h] for h in range(HEADS)], axis=1))
        qd.append(np.exp(qpow[:, None] * log_g[lane_head][None, :]))
        kd.append(np.exp(kpow[:, None] * log_g[lane_head][None, :]))
        cd.append(np.exp(c * log_g[lane_head])[None, :])
    f = lambda xs: jnp.asarray(np.stack(xs), dtype=F32)
    return f(dmats), f(qd), f(kd), f(cd)


def _fnet_dense_kernel(t_len, cs_ref, g1_ref, g2_ref, o_ref):
    o_ref[0] = (jnp.dot(cs_ref[:, 0:t_len], g1_ref[0].astype(BF16), preferred_element_type=F32)
                + jnp.dot(cs_ref[:, t_len:2 * t_len], g2_ref[0].astype(BF16),
                          preferred_element_type=F32))


def _fnet_dense(g1, g2, cs):
    B, T, _ = g1.shape
    tb = min(TOKEN_TILE, T)
    return pl.pallas_call(
        functools.partial(_fnet_dense_kernel, T),
        grid=(T // tb, B),
        in_specs=[
            pl.BlockSpec((tb, 2 * T), lambda i, b: (i, 0)),
            pl.BlockSpec((1, T, BR_W), lambda i, b: (b, 0, 0)),
            pl.BlockSpec((1, T, BR_W), lambda i, b: (b, 0, 0)),
        ],
        out_specs=pl.BlockSpec((1, tb, BR_W), lambda i, b: (b, i, 0)),
        out_shape=jax.ShapeDtypeStruct((B, T, BR_W), F32),
        compiler_params=_cparams(("arbitrary", "arbitrary")),
        name="fnet",
    )(cs, g1, g2)


def _bf16_const(table):
    return jnp.asarray(table, dtype=F32).astype(BF16)


FFT_R = 8


def _fft_kernel(n, k1_ref, k2_ref, tc_ref, ts_ref, g1_ref, g2_ref, o_ref, zr_ref, zi_ref):
    m = n * FFT_R
    for j in range(n // FFT_R):
        sl = slice(j * FFT_R, (j + 1) * FFT_R)
        x = jnp.concatenate([g1_ref[0, :, sl, :].reshape(m, BR_W),
                             g2_ref[0, :, sl, :].reshape(m, BR_W)], axis=1).astype(BF16)
        y = jnp.dot(k1_ref[...], x, preferred_element_type=F32)
        yr = y[0:m, 0:BR_W] - y[m:2 * m, BR_W:2 * BR_W]
        yi = -y[0:m, BR_W:2 * BR_W] - y[m:2 * m, 0:BR_W]
        tc = tc_ref[:, sl, :].reshape(m, BR_W)
        ts = ts_ref[:, sl, :].reshape(m, BR_W)
        zr_ref[:, sl, :] = (tc * yr + ts * yi).reshape(n, FFT_R, BR_W)
        zi_ref[:, sl, :] = (tc * yi - ts * yr).reshape(n, FFT_R, BR_W)
    for j in range(n // FFT_R):
        sl = slice(j * FFT_R, (j + 1) * FFT_R)
        z = jnp.concatenate([zr_ref[sl].reshape(m, BR_W), zi_ref[sl].reshape(m, BR_W)],
                            axis=0).astype(BF16)
        o = jnp.dot(k2_ref[...], z, preferred_element_type=F32)
        o_ref[0, :, sl, :] = o.reshape(n, FFT_R, BR_W)


def _fnet_fft(g1, g2):
    B, T, _ = g1.shape
    n = int(round(math.sqrt(T)))
    assert n * n == T and n % FFT_R == 0
    m = n * FFT_R
    idx = np.arange(n)
    a_n = 2.0 * np.pi * ((idx[:, None] * idx[None, :]) % n) / n
    eye = np.eye(FFT_R)
    k1 = np.concatenate([np.einsum("ap,ij->aipj", f, eye).reshape(m, m)
                         for f in (np.cos(a_n), np.sin(a_n))], axis=0)
    k2 = np.concatenate([np.einsum("pr,ij->pijr", f, eye).reshape(m, m)
                         for f in (np.cos(a_n), np.sin(a_n))], axis=1)
    a_t = 2.0 * np.pi * (idx[:, None] * idx[None, :]) / T
    tc = jnp.broadcast_to(jnp.asarray(np.cos(a_t), dtype=F32)[:, :, None], (n, n, BR_W))
    ts = jnp.broadcast_to(jnp.asarray(np.sin(a_t), dtype=F32)[:, :, None], (n, n, BR_W))
    const2 = lambda b: (0, 0)
    const3 = lambda b: (0, 0, 0)
    one = pl.Buffered(1)
    o = pl.pallas_call(
        functools.partial(_fft_kernel, n),
        grid=(B,),
        in_specs=[
            pl.BlockSpec((2 * m, m), const2, pipeline_mode=one),
            pl.BlockSpec((m, 2 * m), const2, pipeline_mode=one),
            pl.BlockSpec((n, n, BR_W), const3, pipeline_mode=one),
            pl.BlockSpec((n, n, BR_W), const3, pipeline_mode=one),
            pl.BlockSpec((1, n, n, BR_W), lambda b: (b, 0, 0, 0)),
            pl.BlockSpec((1, n, n, BR_W), lambda b: (b, 0, 0, 0)),
        ],
        out_specs=pl.BlockSpec((1, n, n, BR_W), lambda b: (b, 0, 0, 0)),
        out_shape=jax.ShapeDtypeStruct((B, n, n, BR_W), F32),
        scratch_shapes=[pltpu.VMEM((n, n, BR_W), F32), pltpu.VMEM((n, n, BR_W), F32)],
        compiler_params=_cparams(("arbitrary",)),
        name="fft",
    )(_bf16_const(k1), _bf16_const(k2), tc, ts,
      g1.reshape(B, n, n, BR_W), g2.reshape(B, n, n, BR_W))
    return o.reshape(B, T, BR_W)


def _time_dft_table(T):
    t = np.arange(T)
    ang = ((t[:, None] * t[None, :]) % T) * (2.0 * np.pi / T)
    return _bf16_const(np.concatenate([np.cos(ang), -np.sin(ang)], axis=1))


def _channel_dft_tables(T):
    ch = np.arange(HEAD_DIM)
    a64 = 2.0 * np.pi * ((ch[:, None] * ch[None, :]) % HEAD_DIM) / HEAD_DIM
    scale = (T * HEAD_DIM) ** -0.5
    eye = np.eye(HEADS)
    cbd = np.kron(eye, np.cos(a64)) * scale
    sbd = np.kron(eye, np.sin(a64)) * scale
    return _bf16_const(np.stack([cbd, sbd]))


def _head_norm(o, ones_h, center):
    if center:
        o = o - _dot(o, ones_h) * (1.0 / HEAD_DIM)
    var = _dot(o * o, ones_h) * (1.0 / HEAD_DIM)
    return o * lax.rsqrt(var + EPS)


def _merge_kernel(x_ref, hb_ref, mod_ref, gg_ref, rg_ref, wg_ref, bon_ref,
                  ogf_ref, ogb_ref, orf_ref, orb_ref, oyf_ref, oyb_ref, fn_ref,
                  gn_ref, wgate_ref, bgate_ref, wbr_ref, wout_ref, o_ref):
    ones_h = _head_ones()
    gn = gn_ref[...]
    gla = _head_norm(ogf_ref[0] + ogb_ref[0], ones_h, False) * gn[0:1] * _silu(gg_ref[0])
    ret = _head_norm(orf_ref[0] + orb_ref[0], ones_h, True) * gn[1:2] * _silu(rg_ref[0])
    rwkv = (_head_norm(oyf_ref[0] + oyb_ref[0], ones_h, True) * gn[2:3] + bon_ref[0]) * wg_ref[0]
    outs = (gla, ret, rwkv, fn_ref[0])
    hb = hb_ref[0]
    z = None
    for br in range(4):
        gate = _sigmoid(jnp.dot(hb, wgate_ref[br], preferred_element_type=F32) + bgate_ref[br])
        term = gate * _dot(outs[br], wbr_ref[br])
        z = term if z is None else z + term
    y = _dot(z, wout_ref[...])
    gate1 = mod_ref[0][:, 2 * D_MODEL:3 * D_MODEL]
    o_ref[0] = x_ref[0] + gate1 * y


def _merge(x, hb, mods, feats, scans, fnet_out, lw):
    B, T, D = x.shape
    tb = min(TOKEN_TILE, T)
    tok = lambda w: pl.BlockSpec((1, tb, w), lambda b, i: (b, i, 0))
    fcol = lambda j: pl.BlockSpec((1, tb, BR_W), lambda b, i, j=j: (b, i, j))
    const2 = lambda b, i: (0, 0)
    const3 = lambda b, i: (0, 0, 0)
    return pl.pallas_call(
        _merge_kernel,
        grid=(B, T // tb),
        in_specs=[
            tok(D), tok(D),
            pl.BlockSpec((1, 1, 6 * D), lambda b, i: (b, 0, 0)),
            fcol(F_GG), fcol(F_RG), fcol(F_WG), fcol(F_BON),
            tok(BR_W), tok(BR_W), tok(BR_W), tok(BR_W), tok(BR_W), tok(BR_W), tok(BR_W),
            pl.BlockSpec((3, BR_W), const2),
            pl.BlockSpec((4, D, D), const3, pipeline_mode=pl.Buffered(1)),
            pl.BlockSpec((4, 1, D), const3),
            pl.BlockSpec((4, BR_W, D), const3, pipeline_mode=pl.Buffered(1)),
            pl.BlockSpec((D, D), const2, pipeline_mode=pl.Buffered(1)),
        ],
        out_specs=tok(D),
        out_shape=jax.ShapeDtypeStruct((B, T, D), F32),
        compiler_params=_cparams(("parallel", "arbitrary")),
        name="merge",
    )(x, hb, mods, feats, feats, feats, feats, *scans, fnet_out,
      lw["gn"], lw["w_gate"], lw["b_gate"], lw["w_br"], lw["w_out"])


FFN_COL_CHUNKS = (1024, 1024, 768)


def _ffn_kernel(final, tb, nt,
                x_ref, xp_ref, xn_ref, mod_ref, g2_ref, up_ref, cw_ref, cb_ref,
                down_ref, gfin_ref, o_ref, act_ref):
    i = pl.program_id(1)
    n_ext = tb + 2 * HALO
    x_ext = jnp.concatenate([xp_ref[0], x_ref[0], xn_ref[0]], axis=0)
    mod = mod_ref[0]
    shift = mod[:, 3 * D_MODEL:4 * D_MODEL]
    scale = mod[:, 4 * D_MODEL:5 * D_MODEL]
    h2 = (_rms_rows(x_ext) * g2_ref[...]) * (1.0 + scale) + shift
    h2 = h2 * _halo_keep(n_ext, tb, i, nt)
    h2_ext = h2.astype(BF16)
    h2_mid = h2[HALO:HALO + tb].astype(BF16)
    cw = cw_ref[...]
    cb = cb_ref[...]
    lo = 0
    for width in FFN_COL_CHUNKS:
        a_ext = jnp.dot(h2_ext, up_ref[:, lo:lo + width], preferred_element_type=F32)
        u = jnp.dot(h2_mid, up_ref[:, D_FF + lo:D_FF + lo + width], preferred_element_type=F32)
        a = (a_ext[HALO - 1:HALO - 1 + tb] * cw[0:1, lo:lo + width]
             + a_ext[HALO:HALO + tb] * cw[1:2, lo:lo + width]
             + a_ext[HALO + 1:HALO + 1 + tb] * cw[2:3, lo:lo + width] + cb[:, lo:lo + width])
        act_ref[:, lo:lo + width] = (_silu(a) * u).astype(BF16)
        lo += width
    y = jnp.dot(act_ref[...], down_ref[...], preferred_element_type=F32)
    gate2 = mod[:, 5 * D_MODEL:6 * D_MODEL]
    res = x_ref[0] + gate2 * y
    if final:
        res = _rms_rows(res) * gfin_ref[...]
    o_ref[0] = res


def _ffn(x, mods, lw, g_final, final):
    B, T, D = x.shape
    tb = min(TOKEN_TILE, T)
    nt = T // tb
    hb8 = tb // HALO
    n_h = T // HALO
    const2 = lambda b, i: (0, 0)
    resident = lambda shape: pl.BlockSpec(shape, const2, pipeline_mode=pl.Buffered(1))
    kern = functools.partial(_ffn_kernel, final, tb, nt)
    return pl.pallas_call(
        kern,
        grid=(B, nt),
        in_specs=[
            pl.BlockSpec((1, tb, D), lambda b, i: (b, i, 0)),
            pl.BlockSpec((1, HALO, D), lambda b, i: (b, jnp.maximum(i * hb8 - 1, 0), 0)),
            pl.BlockSpec((1, HALO, D), lambda b, i: (b, jnp.minimum((i + 1) * hb8, n_h - 1), 0)),
            pl.BlockSpec((1, 1, 6 * D), lambda b, i: (b, 0, 0)),
            pl.BlockSpec((1, D), const2),
            resident((D, 2 * D_FF)),
            pl.BlockSpec((3, D_FF), const2),
            pl.BlockSpec((1, D_FF), const2),
            resident((D_FF, D)),
            pl.BlockSpec((1, D), const2),
        ],
        out_specs=pl.BlockSpec((1, tb, D), lambda b, i: (b, i, 0)),
        out_shape=jax.ShapeDtypeStruct((B, T, D), F32),
        scratch_shapes=[pltpu.VMEM((tb, D_FF), BF16)],
        compiler_params=_cparams(("parallel", "arbitrary")),
        name="ffn",
    )(x, x, x, mods, lw["g2"], lw["ffn_up"], lw["ffn_conv"], lw["ffn_conv_b"],
      lw["ffn_down"], g_final)


def _rope_tables(T):
    rows = T // GRID_W
    row = jnp.repeat(jnp.arange(rows, dtype=F32), GRID_W)
    colp = jnp.tile(jnp.arange(GRID_W, dtype=F32), rows)
    n_freq = HEAD_DIM // 4
    inv = ROPE_BASE ** (-jnp.arange(n_freq, dtype=F32) / n_freq)
    ang = jnp.concatenate([row[:, None] * inv, colp[:, None] * inv], axis=-1)
    cos, sin = jnp.cos(ang), jnp.sin(ang)
    cos_h = jnp.concatenate([cos, cos], axis=-1)
    sin_h = jnp.concatenate([-sin, sin], axis=-1)
    return jnp.tile(cos_h, (1, HEADS)), jnp.tile(sin_h, (1, HEADS))


def _pad_rows(w, lo, n):
    return jnp.zeros((n, w.shape[1]), w.dtype).at[lo:lo + w.shape[0]].set(w)


def _layer_weights(l, p):
    D = D_MODEL
    bf = lambda t: t.astype(BF16)
    zcols = lambda w, n: jnp.concatenate([w, jnp.zeros((w.shape[0], n - w.shape[1]), w.dtype)], axis=1)
    w_h = zcols(jnp.concatenate([p["gla_wa1"][l, 0], p["gla_wa1"][l, 1]], axis=1), 128)
    w_xw = jnp.concatenate([p["rwkv_w1"][l, 0], p["rwkv_w1"][l, 1]], axis=1)
    w_xa = jnp.concatenate([p["rwkv_a1"][l, 0], p["rwkv_a1"][l, 1]], axis=1)
    w_xg = zcols(p["rwkv_g1"][l], 256)
    mu = p["rwkv_mu"][l]
    w_lr = jnp.concatenate([w_xg, mu[2][:, None] * w_xg, w_h, w_xw, w_xa,
                            mu[0][:, None] * w_xw, mu[1][:, None] * w_xa,
                            jnp.zeros((D, LR_COLS - LR_AS - 128), F32)], axis=1)
    w2 = jnp.stack([
        _pad_rows(p["gla_wa2"][l, 0], 0, 256), _pad_rows(p["gla_wa2"][l, 1], GLA_LR, 256),
        _pad_rows(p["rwkv_w2"][l, 0], 0, 256), _pad_rows(p["rwkv_w2"][l, 1], RWKV_LR, 256),
        _pad_rows(p["rwkv_a2"][l, 0], 0, 256), _pad_rows(p["rwkv_a2"][l, 1], RWKV_LR, 256),
        _pad_rows(p["rwkv_g2"][l], 0, 256),
    ])
    vec = jnp.stack([p["gla_ba"][l, 0], p["gla_ba"][l, 1], p["rwkv_w0"][l, 0], p["rwkv_w0"][l, 1],
                     p["rwkv_a0"][l, 0], p["rwkv_a0"][l, 1], p["rwkv_kk"][l], p["rwkv_ka"][l],
                     p["rwkv_rk"][l]])
    vec = jnp.concatenate([vec, jnp.zeros((16 - vec.shape[0], BR_W), F32)], axis=0)
    return {
        "g1": p["g_norm1"][l].reshape(1, D), "g2": p["g_norm2"][l].reshape(1, D),
        "w_in": bf(p["w_in"][l]), "w_lr": bf(w_lr), "w2": bf(w2), "vec": vec,
        "rwkv_conv": p["rwkv_conv"][l],
        "gn": jnp.stack([p["gla_gn"][l], p["ret_gn"][l], p["rwkv_gn"][l]]),
        "w_gate": bf(p["w_gate"][l]), "b_gate": p["b_gate"][l].reshape(4, 1, D),
        "w_br": bf(p["w_br"][l]), "w_out": bf(p["w_out"][l]),
        "ffn_up": bf(p["ffn_up"][l]), "ffn_conv": p["ffn_conv"][l],
        "ffn_conv_b": p["ffn_conv_b"][l].reshape(1, D_FF), "ffn_down": bf(p["ffn_down"][l]),
    }


_GLA_F = (F_GQ, F_GK, F_GV, F_LAF)
_GLA_B = (F_GQ, F_GK, F_GV, F_LAB)
_RET = (F_RQ, F_RK, F_RV)
_RWKV_F = (F_WR, F_WV, F_WKK, F_KDF, F_ASF, F_LWF)
_RWKV_B = (F_WR, F_WV, F_WKK, F_KDB, F_ASB, F_LWB)


def _mixers(feats_c, feats_l, ret_tabs_c, ret_tabs_l, need_ctx):
    B = feats_l.shape[0]
    zero = jnp.zeros((2, B, BR_W, BR_W), F32)
    out_c, out_l = [], []
    for kind, cf, cb, tabs in (("gla", _GLA_F, _GLA_B, None), ("ret", _RET, _RET, True),
                               ("rwkv", _RWKV_F, _RWKV_B, None)):
        tc = ret_tabs_c if tabs else ()
        tl = ret_tabs_l if tabs else ()
        of_c, ob_c, s_c = _scan(kind, feats_c, cf, cb, zero, tc)
        of_l, ob_l, _ = _scan(kind, feats_l, cf, cb, s_c, tl)
        out_c += [of_c, ob_c]
        out_l += [of_l, ob_l]
    return out_l, (out_c if need_ctx else None)


def kernel(x, c, ctx, c_ctx, w_ada, b_ada, g_norm1, g_norm2, w_in, gla_wa1, gla_wa2, gla_ba, gla_gn, ret_gn, rwkv_conv, rwkv_mu, rwkv_w0, rwkv_w1, rwkv_w2, rwkv_a0, rwkv_a1, rwkv_a2, rwkv_g1, rwkv_g2, rwkv_kk, rwkv_ka, rwkv_rk, rwkv_gn, w_gate, b_gate, w_br, w_out, ffn_up, ffn_conv, ffn_conv_b, ffn_down, g_final):
    p = dict(g_norm1=g_norm1, g_norm2=g_norm2, w_in=w_in, gla_wa1=gla_wa1, gla_wa2=gla_wa2,
             gla_ba=gla_ba, gla_gn=gla_gn, ret_gn=ret_gn, rwkv_conv=rwkv_conv, rwkv_mu=rwkv_mu,
             rwkv_w0=rwkv_w0, rwkv_w1=rwkv_w1, rwkv_w2=rwkv_w2, rwkv_a0=rwkv_a0,
             rwkv_a1=rwkv_a1, rwkv_a2=rwkv_a2, rwkv_g1=rwkv_g1, rwkv_g2=rwkv_g2,
             rwkv_kk=rwkv_kk, rwkv_ka=rwkv_ka, rwkv_rk=rwkv_rk, rwkv_gn=rwkv_gn,
             w_gate=w_gate, b_gate=b_gate, w_br=w_br, w_out=w_out, ffn_up=ffn_up,
             ffn_conv=ffn_conv, ffn_conv_b=ffn_conv_b, ffn_down=ffn_down)
    B, T, D = x.shape
    Tc = ctx.shape[1]
    depth = w_ada.shape[0]

    cmat = jnp.concatenate([c, c_ctx[None, :], jnp.zeros((MOD_ROWS - B - 1, D), F32)], axis=0)
    mod_all = _modulation(cmat, w_ada, b_ada)

    rope_l = _rope_tables(T)
    rope_c = (jnp.zeros((Tc, BR_W), F32), jnp.zeros((Tc, BR_W), F32))
    dft_l = _channel_dft_tables(T)
    dft_c = _channel_dft_tables(Tc)
    cs_c = _time_dft_table(Tc)
    ret_tabs_l = _ret_tables(min(SCAN_CHUNK["ret"], T))
    ret_tabs_c = _ret_tables(min(SCAN_CHUNK["ret"], Tc))
    g_fin = g_final.reshape(1, D)

    for l in range(depth):
        last = l == depth - 1
        lw = _layer_weights(l, p)
        mods_l = mod_all[l, 0:B].reshape(B, 1, 6 * D)
        mods_c = jnp.broadcast_to(mod_all[l, B:B + 1].reshape(1, 1, 6 * D), (B, 1, 6 * D))

        feats_l, hb_l, g1_l, g2_l = _pre(x, mods_l, lw, True, rope_l, dft_l)
        feats_c, hb_c, g1_c, g2_c = _pre(ctx, mods_c, lw, False, rope_c, dft_c)
        scans_l, scans_c = _mixers(feats_c, feats_l, ret_tabs_c, ret_tabs_l, not last)

        fn_l = _fnet_fft(g1_l, g2_l)
        x = _merge(x, hb_l, mods_l, feats_l, scans_l, fn_l, lw)
        x = _ffn(x, mods_l, lw, g_fin, last)
        if not last:
            fn_c = _fnet_dense(g1_c, g2_c, cs_c)
            ctx = _merge(ctx, hb_c, mods_c, feats_c, scans_c, fn_c, lw)
            ctx = _ffn(ctx, mods_c, lw, g_fin, False)
    return x
```

```python
import functools
import math

import numpy as np
import jax
import jax.numpy as jnp
from jax import lax
from jax.experimental import pallas as pl
from jax.experimental.pallas import tpu as pltpu

F32 = jnp.float32
BF16 = jnp.bfloat16

D_MODEL = 1024
BR_W = 256
HEAD_DIM = 64
HEADS = 4
N_PARTS = 12
P_IN = N_PARTS * BR_W
GRID_W = 64
GLA_LR = 16
GLA_NORMALIZER = 16.0
RWKV_LR = 64
RWKV_G_LR = 160
D_FF = 2816
ROPE_BASE = 10000.0
EPS = 1e-6

SCAN_CHUNK = {"gla": 64, "ret": 128, "rwkv": 64}
GLA_SUB = 16
HALO = 8
NEG_BIG = -1e30
LOG2E = 1.4426950408889634

(F_GQ, F_GK, F_GV, F_GG, F_RQ, F_RK, F_RV, F_RG, F_WR, F_WV, F_WKK, F_WG, F_BON,
 F_LAF, F_LAB, F_LWF, F_LWB, F_KDF, F_KDB, F_ASF, F_ASB) = range(21)
N_FEAT = 21

LR_G, LR_GS, LR_H, LR_W, LR_A, LR_WS, LR_AS = 0, 256, 512, 640, 768, 896, 1024
LR_COLS = 1280

TOKEN_TILE = 512
MOD_ROWS = 16
MOD_COL_BLOCK = 1536
VMEM_LIMIT = 56 * 1024 * 1024


def _cparams(sem):
    return pltpu.CompilerParams(dimension_semantics=sem, vmem_limit_bytes=VMEM_LIMIT)


def _dot(a, b):
    return jnp.dot(a.astype(BF16), b.astype(BF16), preferred_element_type=F32)


def _dot_nt(a, b):
    return lax.dot_general(a.astype(BF16), b.astype(BF16), (((1,), (1,)), ((), ())),
                           preferred_element_type=F32)


def _dot_tn(a, b):
    return lax.dot_general(a.astype(BF16), b.astype(BF16), (((0,), (0,)), ((), ())),
                           preferred_element_type=F32)


def _split3(x):
    hi = x.astype(BF16)
    r1 = x - hi.astype(F32)
    mid = r1.astype(BF16)
    lo = (r1 - mid.astype(F32)).astype(BF16)
    return hi, mid, lo


def _dot_exact_lhs(a_bf16, x):
    hi, mid, lo = _split3(x)
    d = lambda t: jnp.dot(a_bf16, t, preferred_element_type=F32)
    return d(hi) + d(mid) + d(lo)


def _sigmoid(x):
    return 1.0 / (1.0 + jnp.exp(-x))


def _silu(x):
    return x * _sigmoid(x)


def _softplus(x):
    return jnp.maximum(x, 0.0) + jnp.log(1.0 + jnp.exp(-jnp.abs(x)))


def _iota(shape, dim):
    return lax.broadcasted_iota(jnp.int32, shape, dim)


def _head_ones():
    r = _iota((BR_W, BR_W), 0) // HEAD_DIM
    c = _iota((BR_W, BR_W), 1) // HEAD_DIM
    return jnp.where(r == c, 1.0, 0.0).astype(BF16)


def _stack_heads(x):
    c, w = x.shape
    xs = jnp.concatenate([x] * HEADS, axis=0)
    keep = (_iota((HEADS * c, w), 0) // c) == (_iota((HEADS * c, w), 1) // (w // HEADS))
    return jnp.where(keep, xs, 0.0)


def _tri(c, rev, strict=False):
    r = _iota((c, c), 0)
    s = _iota((c, c), 1)
    if rev:
        return (r < s) if strict else (r <= s)
    return (r > s) if strict else (r >= s)


def _halo_keep(n_ext, tb, i, nt):
    row = _iota((n_ext, 1), 0)
    first = jnp.where(i > 0, 1.0, 0.0)
    last = jnp.where(i < nt - 1, 1.0, 0.0)
    return jnp.where(row < HALO, first, jnp.where(row >= tb + HALO, last, 1.0))


def _rms_rows(x):
    return x * lax.rsqrt(jnp.mean(x * x, axis=-1, keepdims=True) + EPS)


def _mod_kernel(c_ref, w_ref, b_ref, o_ref):
    s = _silu(c_ref[...])
    o_ref[0] = _dot(s, w_ref[0]) + b_ref[0]


def _modulation(cmat, w_ada, b_ada):
    L = w_ada.shape[0]
    nblk = MOD_COL_BLOCK
    return pl.pallas_call(
        _mod_kernel,
        grid=(L, 6 * D_MODEL // nblk),
        in_specs=[
            pl.BlockSpec((MOD_ROWS, D_MODEL), lambda l, j: (0, 0)),
            pl.BlockSpec((1, D_MODEL, nblk), lambda l, j: (l, 0, j)),
            pl.BlockSpec((1, 1, nblk), lambda l, j: (l, 0, j)),
        ],
        out_specs=pl.BlockSpec((1, MOD_ROWS, nblk), lambda l, j: (l, 0, j)),
        out_shape=jax.ShapeDtypeStruct((L, MOD_ROWS, 6 * D_MODEL), F32),
        compiler_params=_cparams(("arbitrary", "arbitrary")),
        name="mod",
    )(cmat, w_ada, b_ada.reshape(L, 1, 6 * D_MODEL))


def _pre_kernel(use_rope, tb, nt,
                x_ref, xp_ref, xn_ref, mod_ref, g1_ref, win_ref, wlr_ref,
                w2_ref, vec_ref, conv_ref, dft_ref, cos_ref, sin_ref,
                f_ref, hb_ref, fc_ref, fs_ref, lr_ref, ps_ref):
    i = pl.program_id(1)
    n_ext = tb + 2 * HALO
    x_ext = jnp.concatenate([xp_ref[0], x_ref[0], xn_ref[0]], axis=0)
    mod = mod_ref[0]
    shift = mod[:, 0:D_MODEL]
    scale = mod[:, D_MODEL:2 * D_MODEL]
    h_ext = (_rms_rows(x_ext) * g1_ref[...]) * (1.0 + scale) + shift
    h_ext = h_ext * _halo_keep(n_ext, tb, i, nt)
    hb_ext = h_ext.astype(BF16)
    hb = h_ext[HALO:HALO + tb].astype(BF16)
    hb_ref[0] = hb
    lr_ref[...] = jnp.dot(hb_ext, wlr_ref[...], preferred_element_type=F32)
    ps_ref[...] = jnp.dot(hb_ext, win_ref[:, 8 * BR_W:11 * BR_W], preferred_element_type=F32)

    def mid(lo, hi):
        return lr_ref[HALO:HALO + tb, lo:hi]

    def shifted(lo, hi):
        return (0.5 * (lr_ref[HALO - 1:HALO - 1 + tb, lo:hi] + lr_ref[HALO + 1:HALO + 1 + tb, lo:hi])
                - lr_ref[HALO:HALO + tb, lo:hi])

    vec = vec_ref[...]
    ba_f, ba_b, w0_f, w0_b = vec[0:1], vec[1:2], vec[2:3], vec[3:4]
    a0_f, a0_b, kkw, kaw, rkw = vec[4:5], vec[5:6], vec[6:7], vec[7:8], vec[8:9]

    def put(j, val):
        f_ref[0, :, j * BR_W:(j + 1) * BR_W] = val

    zh = mid(LR_H, LR_H + 128)
    tw = jnp.tanh(mid(LR_W, LR_W + 128) + shifted(LR_WS, LR_WS + 128))
    ta = mid(LR_A, LR_A + 128) + shifted(LR_AS, LR_AS + 128)
    sg = _sigmoid(mid(LR_G, LR_G + 256) + shifted(LR_GS, LR_GS + 256))

    p_gla = jnp.dot(hb, win_ref[:, 0:4 * BR_W], preferred_element_type=F32)
    put(F_GQ, p_gla[:, 0:BR_W] * HEAD_DIM ** -0.5)
    put(F_GK, p_gla[:, BR_W:2 * BR_W])
    put(F_GV, p_gla[:, 2 * BR_W:3 * BR_W])
    put(F_GG, p_gla[:, 3 * BR_W:4 * BR_W])

    z_la = [_dot(zh, w2_ref[dr, 0:128, :]) for dr in range(2)]
    z_w = [_dot(tw, w2_ref[2 + dr, 0:128, :]) for dr in range(2)]
    z_a = [_dot(ta, w2_ref[4 + dr, 0:128, :]) for dr in range(2)]
    put(F_WG, _dot(sg, w2_ref[6]))

    for dr, (ba, dst) in enumerate(((ba_f, F_LAF), (ba_b, F_LAB))):
        put(dst, -_softplus(-(z_la[dr] + ba)) * (1.0 / GLA_NORMALIZER))

    p_ret = jnp.dot(hb, win_ref[:, 4 * BR_W:8 * BR_W], preferred_element_type=F32)

    conv = conv_ref[...]

    def dwconv(part):
        lo, hi = part * BR_W, (part + 1) * BR_W
        cw = conv[:, lo:hi]
        return (ps_ref[HALO - 1:HALO - 1 + tb, lo:hi] * cw[0:1]
                + ps_ref[HALO:HALO + tb, lo:hi] * cw[1:2]
                + ps_ref[HALO + 1:HALO + 1 + tb, lo:hi] * cw[2:3])

    r_c = dwconv(0)
    k_c = dwconv(1)
    v_c = dwconv(2)
    ones_h = _head_ones()
    kk = k_c * kkw
    kk = kk * lax.rsqrt(_dot(kk * kk, ones_h) + EPS)
    put(F_WR, r_c)
    put(F_WV, v_c)
    put(F_WKK, kk)

    p_fnet = jnp.dot(hb, win_ref[:, 11 * BR_W:12 * BR_W], preferred_element_type=F32)

    bonus = jnp.zeros((tb, BR_W), F32)
    for dr, (w0, a0, d_lw, d_kd, d_as) in enumerate(
            ((w0_f, a0_f, F_LWF, F_KDF, F_ASF), (w0_b, a0_b, F_LWB, F_KDB, F_ASB))):
        put(d_lw, -math.exp(-0.5) * _sigmoid(w0 + z_w[dr]))
        a_sig = _sigmoid(a0 + z_a[dr])
        kd = k_c * (1.0 + (a_sig - 1.0) * kaw)
        put(d_kd, kd)
        put(d_as, a_sig)
        bonus = bonus + _dot(r_c * kd * rkw, ones_h) * v_c
    put(F_BON, bonus)

    rq = p_ret[:, 0:BR_W]
    rk = p_ret[:, BR_W:2 * BR_W] * HEAD_DIM ** -0.5
    if use_rope:
        cosf = cos_ref[...]
        sins = sin_ref[...]
        low = (_iota((tb, BR_W), 1) % HEAD_DIM) < (HEAD_DIM // 2)

        def rope(t):
            partner = jnp.where(low, pltpu.roll(t, BR_W - HEAD_DIM // 2, 1),
                                pltpu.roll(t, HEAD_DIM // 2, 1))
            return t * cosf + partner * sins

        rq = rope(rq)
        rk = rope(rk)
    put(F_RQ, rq)
    put(F_RK, rk)
    put(F_RV, p_ret[:, 2 * BR_W:3 * BR_W])
    put(F_RG, p_ret[:, 3 * BR_W:4 * BR_W])

    fb = p_fnet.astype(BF16)
    fc_ref[0] = jnp.dot(fb, dft_ref[0], preferred_element_type=F32)
    fs_ref[0] = jnp.dot(fb, dft_ref[1], preferred_element_type=F32)


def _pre(x, mods, lw, use_rope, rope_tabs, dft_c):
    B, T, D = x.shape
    tb = min(TOKEN_TILE, T)
    nt = T // tb
    hb8 = tb // HALO
    n_h = T // HALO
    const2 = lambda b, i: (0, 0)
    const3 = lambda b, i: (0, 0, 0)
    kern = functools.partial(_pre_kernel, use_rope, tb, nt)
    return pl.pallas_call(
        kern,
        grid=(B, nt),
        in_specs=[
            pl.BlockSpec((1, tb, D), lambda b, i: (b, i, 0)),
            pl.BlockSpec((1, HALO, D), lambda b, i: (b, jnp.maximum(i * hb8 - 1, 0), 0)),
            pl.BlockSpec((1, HALO, D), lambda b, i: (b, jnp.minimum((i + 1) * hb8, n_h - 1), 0)),
            pl.BlockSpec((1, 1, 6 * D), lambda b, i: (b, 0, 0)),
            pl.BlockSpec((1, D), const2),
            pl.BlockSpec((D, P_IN), const2, pipeline_mode=pl.Buffered(1)),
            pl.BlockSpec((D, LR_COLS), const2, pipeline_mode=pl.Buffered(1)),
            pl.BlockSpec((7, 256, BR_W), const3, pipeline_mode=pl.Buffered(1)),
            pl.BlockSpec((16, BR_W), const2),
            pl.BlockSpec((3, 3 * BR_W), const2),
            pl.BlockSpec((2, BR_W, BR_W), const3),
            pl.BlockSpec((tb, BR_W), lambda b, i: (i, 0)),
            pl.BlockSpec((tb, BR_W), lambda b, i: (i, 0)),
        ],
        out_specs=[
            pl.BlockSpec((1, tb, N_FEAT * BR_W), lambda b, i: (b, i, 0)),
            pl.BlockSpec((1, tb, D), lambda b, i: (b, i, 0)),
            pl.BlockSpec((1, tb, BR_W), lambda b, i: (b, i, 0)),
            pl.BlockSpec((1, tb, BR_W), lambda b, i: (b, i, 0)),
        ],
        out_shape=[
            jax.ShapeDtypeStruct((B, T, N_FEAT * BR_W), F32),
            jax.ShapeDtypeStruct((B, T, D), BF16),
            jax.ShapeDtypeStruct((B, T, BR_W), F32),
            jax.ShapeDtypeStruct((B, T, BR_W), F32),
        ],
        scratch_shapes=[
            pltpu.VMEM((tb + 2 * HALO, LR_COLS), F32),
            pltpu.VMEM((tb + 2 * HALO, 3 * BR_W), F32),
        ],
        compiler_params=_cparams(("parallel", "arbitrary")),
        name="pre",
    )(x, x, x, mods, lw["g1"], lw["w_in"], lw["w_lr"],
      lw["w2"], lw["vec"], lw["rwkv_conv"], dft_c, rope_tabs[0], rope_tabs[1])


def _cumsum_chunk(x, rev):
    c = x.shape[0]
    tri = jnp.where(_tri(c, rev), 1.0, 0.0).astype(BF16)
    return _dot_exact_lhs(tri, x)


def _cumsum_chunk_vpu(x, rev):
    c = x.shape[0]
    row = _iota(x.shape, 0)
    k = 1
    while k < c:
        if rev:
            x = x + jnp.where(row < c - k, pltpu.roll(x, c - k, 0), 0.0)
        else:
            x = x + jnp.where(row >= k, pltpu.roll(x, k, 0), 0.0)
        k *= 2
    return x


def _gla_chunks(chains):
    c = chains[0][0].shape[0]
    nblk = c // GLA_SUB
    ones_h = _head_ones()
    bd = (_iota((BR_W, BR_W), 0) // HEAD_DIM) == (_iota((BR_W, BR_W), 1) // HEAD_DIM)
    cums = [_cumsum_chunk(ch[3], ch[5]) for ch in chains]

    same = ((_iota((HEADS * GLA_SUB, BR_W), 0) // GLA_SUB)
            == (_iota((HEADS * GLA_SUB, BR_W), 1) // HEAD_DIM))
    offd, inter, st_new = [], [], []
    for (q, k, v, la, st, rev), cum in zip(chains, cums):
        cum_end = cum[0:1] if rev else cum[c - 1:c]
        parts = []
        for blk in range(nblk):
            base = blk * GLA_SUB
            if rev:
                k_lo, k_hi, ref = base + GLA_SUB, c, base + GLA_SUB
            else:
                k_lo, k_hi, ref = 0, base, base - 1
            if k_hi <= k_lo:
                parts.append(None)
                continue
            qt = q[base:base + GLA_SUB] * jnp.exp(cum[base:base + GLA_SUB] - cum[ref:ref + 1])
            kt = k[k_lo:k_hi] * jnp.exp(cum[ref:ref + 1] - cum[k_lo:k_hi])
            sc = _dot_nt(_stack_heads(qt), kt)
            ov = jnp.where(same, _dot(sc, v[k_lo:k_hi]), 0.0)
            parts.append(sum(ov[h * GLA_SUB:(h + 1) * GLA_SUB] for h in range(HEADS)))
        offd.append(parts)
        inter.append(_dot_nt(q * jnp.exp(cum), st))
        st_new.append(st * jnp.exp(cum_end)
                      + jnp.where(bd, _dot_tn(v, k * jnp.exp(cum_end - cum)), 0.0))

    sub = _iota((8, BR_W), 0)
    bias = {rev: [jnp.where((sub <= jj) if rev else (sub >= jj), 0.0, NEG_BIG) for jj in range(8)]
            for rev in {ch[5] for ch in chains}}
    reds, spans = [], []
    for (q, k, v, la, st, rev), cum in zip(chains, cums):
        cum2 = cum * LOG2E
        pieces, sp = [], []
        q_fac, k_fac = {}, {}
        for j in range(c):
            base = (j // GLA_SUB) * GLA_SUB
            own = (j // 8) * 8
            arg = cum2[own:own + 8] - cum2[j:j + 1] + bias[rev][j - own]
            pieces.append(q[own:own + 8] * (k[j:j + 1] * jnp.exp2(arg)))
            sp.append((j, own))
            for g0 in (range(base, own, 8) if rev else range(own + 8, base + GLA_SUB, 8)):
                ref = g0 + 8 if rev else g0 - 1
                if g0 not in q_fac:
                    q_fac[g0] = q[g0:g0 + 8] * jnp.exp2(cum2[g0:g0 + 8] - cum2[ref:ref + 1])
                if (own, g0) not in k_fac:
                    k_fac[own, g0] = k[own:own + 8] * jnp.exp2(cum2[ref:ref + 1] - cum2[own:own + 8])
                pieces.append(q_fac[g0] * k_fac[own, g0][j - own:j - own + 1])
                sp.append((j, g0))
        reds.append(jnp.dot(jnp.concatenate(pieces, axis=0).astype(BF16), ones_h,
                            preferred_element_type=F32))
        spans.append(sp)

    outs = []
    for ci, (q, k, v, la, st, rev) in enumerate(chains):
        groups = [None] * (c // 8)
        for idx, (j, g0) in enumerate(spans[ci]):
            piece = reds[ci][idx * 8:idx * 8 + 8] * v[j:j + 1]
            gi = g0 // 8
            groups[gi] = piece if groups[gi] is None else groups[gi] + piece
        blocks = []
        for blk in range(nblk):
            o_blk = jnp.concatenate(groups[blk * GLA_SUB // 8:(blk + 1) * GLA_SUB // 8], axis=0)
            if offd[ci][blk] is not None:
                o_blk = o_blk + offd[ci][blk]
            blocks.append(o_blk)
        outs.append((jnp.concatenate(blocks, axis=0) + inter[ci], st_new[ci]))
    return outs


def _ret_chunks(chains):
    bd = (_iota((BR_W, BR_W), 0) // HEAD_DIM) == (_iota((BR_W, BR_W), 1) // HEAD_DIM)
    sc = [_dot_nt(q, _stack_heads(k)) * dmat for q, k, v, s, dmat, _, _, _ in chains]
    inter = [_dot(q * qdec, s) for q, k, v, s, _, qdec, _, _ in chains]
    upd = [_dot_tn(k * kdec, v) for q, k, v, s, _, _, kdec, _ in chains]
    outs = []
    for ch, sc_i, inter_i, upd_i in zip(chains, sc, inter, upd):
        q, k, v, s, _, _, _, cdec = ch
        o = _dot(sc_i, _stack_heads(v)) + inter_i
        outs.append((o, s * cdec + jnp.where(bd, upd_i, 0.0)))
    return outs


def _tri_wide(c, rev, strict):
    t = _iota((c, HEADS * c), 0)
    s = _iota((c, HEADS * c), 1) % c
    if rev:
        return (s > t) if strict else (s >= t)
    return (s < t) if strict else (s <= t)


def _rwkv_chunks(chains):
    c = chains[0][0].shape[0]
    n = HEADS * c
    bf = lambda t: t.astype(BF16)
    stack = lambda t: bf(_stack_heads(t))
    masks = {rev: (_tri_wide(c, rev, True), _tri_wide(c, rev, False))
             for rev in {ch[7] for ch in chains}}
    eye = jnp.where(_iota((c, n), 1) % c == _iota((c, n), 0), 1.0, 0.0)
    bd = (_iota((BR_W, BR_W), 0) // HEAD_DIM) == (_iota((BR_W, BR_W), 1) // HEAD_DIM)
    cums = [_cumsum_chunk_vpu(ch[5], ch[7]) for ch in chains]
    st = []
    for (r, v, kk, kd, asig, lw, s, rev), cum in zip(chains, cums):
        cum_end = cum[0:1] if rev else cum[c - 1:c]
        einv = jnp.exp(-cum)
        ehat = jnp.exp(cum_end - cum)
        bvec = kk * asig
        st.append(dict(
            a=bf(-kk * jnp.exp(cum - lw)), r=bf(r * jnp.exp(cum)),
            b_s=stack(bvec * einv), k_s=stack(kd * einv), v=bf(v), v_s=stack(v),
            bh=bf(bvec * ehat), kh=bf(kd * ehat), g_end=jnp.exp(cum_end), s=s, sb=bf(s), rev=rev))
    for d in st:
        d["m"] = _dot_nt(jnp.concatenate([d["a"], d["r"]], axis=0),
                         jnp.concatenate([d["b_s"], d["k_s"]], axis=0))
    for d in st:
        strict, incl = masks[d["rev"]]
        m = d.pop("m")
        d["p"] = jnp.where(strict, m[0:c, 0:n], 0.0)
        d["t"] = eye + d["p"]
        d["ps"] = stack(d["p"])
        d["m_ak"] = bf(jnp.where(strict, m[0:c, n:2 * n], 0.0))
        d["n_rb"] = bf(jnp.where(incl, m[c:2 * c, 0:n], 0.0))
        d["n_rk"] = bf(jnp.where(incl, m[c:2 * c, n:2 * n], 0.0))
    for d in st:
        wy = (_dot_nt(jnp.concatenate([d["a"], d["r"]], axis=0), d["sb"])
              + _dot(jnp.concatenate([d["m_ak"], d["n_rk"]], axis=0), d["v_s"]))
        d["w"], d["y0"] = wy[0:c], wy[c:2 * c]
    for d in st:
        d["p"] = _dot(d["p"], d["ps"])
        d["ps"] = stack(d["p"])
    for _ in range(int(math.log2(c)) - 2):
        for d in st:
            tp = _dot(jnp.concatenate([d["t"], d["p"]], axis=0), d["ps"])
            d["t"] = d["t"] + tp[0:c]
            d["p"] = tp[c:2 * c]
            d["ps"] = stack(d["p"])
    for d in st:
        d["t"] = d["t"] + _dot(d["t"], d["ps"])
    for d in st:
        d["u"] = _dot(d["t"], stack(d["w"]))
    outs = []
    for d in st:
        y = d["y0"] + _dot(d["n_rb"], stack(d["u"]))
        upd = _dot_tn(jnp.concatenate([bf(d["u"]), d["v"]], axis=0),
                      jnp.concatenate([d["bh"], d["kh"]], axis=0))
        outs.append((y, d["s"] * d["g_end"] + jnp.where(bd, upd, 0.0)))
    return outs


def _scan_kernel(kind, n_in, n_tab, nc, bb, *refs):
    fwd = refs[0:n_in]
    bwd = refs[n_in:2 * n_in]
    tabs = refs[2 * n_in:2 * n_in + n_tab]
    s0_ref = refs[2 * n_in + n_tab]
    of_ref, ob_ref, sf_ref, st_ref = refs[2 * n_in + n_tab + 1:]
    i = pl.program_id(1)

    @pl.when(i == 0)
    def _():
        st_ref[...] = s0_ref[...]

    chains = [(dr, b) for b in range(bb) for dr in range(2)]
    ins_of = lambda dr, b: [t[b] for t in (fwd, bwd)[dr]]
    if kind == "rwkv":
        res = _rwkv_chunks([(*ins_of(dr, b), st_ref[dr, b], dr == 1) for dr, b in chains])
    elif kind == "gla":
        res = _gla_chunks([(*ins_of(dr, b), st_ref[dr, b], dr == 1) for dr, b in chains])
    else:
        res = _ret_chunks([(*ins_of(dr, b), st_ref[dr, b], *(t[dr] for t in tabs))
                           for dr, b in chains])
    for (dr, b), (o, s_new) in zip(chains, res):
        (of_ref, ob_ref)[dr][b] = o
        st_ref[dr, b] = s_new

    @pl.when(i == nc - 1)
    def _():
        sf_ref[...] = st_ref[...]


_SCAN_ROWS = {"gla": 8, "ret": 8, "rwkv": 8}


def _scan(kind, feats, cols_f, cols_b, s0, tabs=()):
    B, T, _ = feats.shape
    c = min(SCAN_CHUNK[kind], T)
    nc = T // c
    bb = math.gcd(_SCAN_ROWS[kind], B)
    n_in = len(cols_f)
    in_specs = []
    for j in cols_f:
        in_specs.append(pl.BlockSpec((bb, c, BR_W), lambda b, i, j=j: (b, i, j)))
    for j in cols_b:
        in_specs.append(pl.BlockSpec((bb, c, BR_W), lambda b, i, j=j: (b, nc - 1 - i, j)))
    for t in tabs:
        in_specs.append(pl.BlockSpec(t.shape, lambda b, i, nd=t.ndim: (0,) * nd))
    in_specs.append(pl.BlockSpec((2, bb, BR_W, BR_W), lambda b, i: (0, b, 0, 0)))
    kern = functools.partial(_scan_kernel, kind, n_in, len(tabs), nc, bb)
    return pl.pallas_call(
        kern,
        grid=(B // bb, nc),
        in_specs=in_specs,
        out_specs=[
            pl.BlockSpec((bb, c, BR_W), lambda b, i: (b, i, 0)),
            pl.BlockSpec((bb, c, BR_W), lambda b, i: (b, nc - 1 - i, 0)),
            pl.BlockSpec((2, bb, BR_W, BR_W), lambda b, i: (0, b, 0, 0)),
        ],
        out_shape=[
            jax.ShapeDtypeStruct((B, T, BR_W), F32),
            jax.ShapeDtypeStruct((B, T, BR_W), F32),
            jax.ShapeDtypeStruct((2, B, BR_W, BR_W), F32),
        ],
        scratch_shapes=[pltpu.VMEM((2, bb, BR_W, BR_W), F32)],
        compiler_params=_cparams(("parallel", "arbitrary")),
        name=kind,
    )(*([feats] * (2 * n_in)), *tabs, s0)


def _ret_tables(c):
    pos = np.arange(c, dtype=np.float64)
    lane_head = np.arange(BR_W) // HEAD_DIM
    dmats, qd, kd, cd = [], [], [], []
    for dr in range(2):
        expo = -5.0 - np.arange(HEADS, dtype=np.float64)
        if dr == 1:
            expo = expo[::-1]
        log_g = np.log1p(-np.exp2(expo))
        if dr == 0:
            rel = pos[:, None] - pos[None, :]
            qpow = pos + 1.0
            kpow = c - 1.0 - pos
        else:
            rel = pos[None, :] - pos[:, None]
            qpow = c - pos
            kpow = pos
        tri = rel >= 0
        dm = np.where(tri[None], np.exp(np.where(tri, rel, 0.0)[None] * log_g[:, None, None]), 0.0)
        dmats.append(np.concatenate([dm[h] for h in range(HEADS)], axis=1))
        qd.append(np.exp(qpow[:, None] * log_g[lane_head][None, :]))
        kd.append(np.exp(kpow[:, None] * log_g[lane_head][None, :]))
        cd.append(np.exp(c * log_g[lane_head])[None, :])
    f = lambda xs: jnp.asarray(np.stack(xs), dtype=F32)
    return f(dmats), f(qd), f(kd), f(cd)


def _fnet_dense_kernel(t_len, cs_ref, g1_ref, g2_ref, o_ref):
    o_ref[0] = (jnp.dot(cs_ref[:, 0:t_len], g1_ref[0].astype(BF16), preferred_element_type=F32)
                + jnp.dot(cs_ref[:, t_len:2 * t_len], g2_ref[0].astype(BF16),
                          preferred_element_type=F32))


def _fnet_dense(g1, g2, cs):
    B, T, _ = g1.shape
    tb = min(TOKEN_TILE, T)
    return pl.pallas_call(
        functools.partial(_fnet_dense_kernel, T),
        grid=(T // tb, B),
        in_specs=[
            pl.BlockSpec((tb, 2 * T), lambda i, b: (i, 0)),
            pl.BlockSpec((1, T, BR_W), lambda i, b: (b, 0, 0)),
            pl.BlockSpec((1, T, BR_W), lambda i, b: (b, 0, 0)),
        ],
        out_specs=pl.BlockSpec((1, tb, BR_W), lambda i, b: (b, i, 0)),
        out_shape=jax.ShapeDtypeStruct((B, T, BR_W), F32),
        compiler_params=_cparams(("arbitrary", "arbitrary")),
        name="fnet",
    )(cs, g1, g2)


def _bf16_const(table):
    return jnp.asarray(table, dtype=F32).astype(BF16)


FFT_R = 8


def _fft_kernel(n, k1_ref, k2_ref, tc_ref, ts_ref, g1_ref, g2_ref, o_ref, zr_ref, zi_ref):
    m = n * FFT_R
    for j in range(n // FFT_R):
        sl = slice(j * FFT_R, (j + 1) * FFT_R)
        x = jnp.concatenate([g1_ref[0, :, sl, :].reshape(m, BR_W),
                             g2_ref[0, :, sl, :].reshape(m, BR_W)], axis=1).astype(BF16)
        y = jnp.dot(k1_ref[...], x, preferred_element_type=F32)
        yr = y[0:m, 0:BR_W] - y[m:2 * m, BR_W:2 * BR_W]
        yi = -y[0:m, BR_W:2 * BR_W] - y[m:2 * m, 0:BR_W]
        tc = tc_ref[:, sl, :].reshape(m, BR_W)
        ts = ts_ref[:, sl, :].reshape(m, BR_W)
        zr_ref[:, sl, :] = (tc * yr + ts * yi).reshape(n, FFT_R, BR_W)
        zi_ref[:, sl, :] = (tc * yi - ts * yr).reshape(n, FFT_R, BR_W)
    for j in range(n // FFT_R):
        sl = slice(j * FFT_R, (j + 1) * FFT_R)
        z = jnp.concatenate([zr_ref[sl].reshape(m, BR_W), zi_ref[sl].reshape(m, BR_W)],
                            axis=0).astype(BF16)
        o = jnp.dot(k2_ref[...], z, preferred_element_type=F32)
        o_ref[0, :, sl, :] = o.reshape(n, FFT_R, BR_W)


def _fnet_fft(g1, g2):
    B, T, _ = g1.shape
    n = int(round(math.sqrt(T)))
    assert n * n == T and n % FFT_R == 0
    m = n * FFT_R
    idx = np.arange(n)
    a_n = 2.0 * np.pi * ((idx[:, None] * idx[None, :]) % n) / n
    eye = np.eye(FFT_R)
    k1 = np.concatenate([np.einsum("ap,ij->aipj", f, eye).reshape(m, m)
                         for f in (np.cos(a_n), np.sin(a_n))], axis=0)
    k2 = np.concatenate([np.einsum("pr,ij->pijr", f, eye).reshape(m, m)
                         for f in (np.cos(a_n), np.sin(a_n))], axis=1)
    a_t = 2.0 * np.pi * (idx[:, None] * idx[None, :]) / T
    tc = jnp.broadcast_to(jnp.asarray(np.cos(a_t), dtype=F32)[:, :, None], (n, n, BR_W))
    ts = jnp.broadcast_to(jnp.asarray(np.sin(a_t), dtype=F32)[:, :, None], (n, n, BR_W))
    const2 = lambda b: (0, 0)
    const3 = lambda b: (0, 0, 0)
    one = pl.Buffered(1)
    o = pl.pallas_call(
        functools.partial(_fft_kernel, n),
        grid=(B,),
        in_specs=[
            pl.BlockSpec((2 * m, m), const2, pipeline_mode=one),
            pl.BlockSpec((m, 2 * m), const2, pipeline_mode=one),
            pl.BlockSpec((n, n, BR_W), const3, pipeline_mode=one),
            pl.BlockSpec((n, n, BR_W), const3, pipeline_mode=one),
            pl.BlockSpec((1, n, n, BR_W), lambda b: (b, 0, 0, 0)),
            pl.BlockSpec((1, n, n, BR_W), lambda b: (b, 0, 0, 0)),
        ],
        out_specs=pl.BlockSpec((1, n, n, BR_W), lambda b: (b, 0, 0, 0)),
        out_shape=jax.ShapeDtypeStruct((B, n, n, BR_W), F32),
        scratch_shapes=[pltpu.VMEM((n, n, BR_W), F32), pltpu.VMEM((n, n, BR_W), F32)],
        compiler_params=_cparams(("arbitrary",)),
        name="fft",
    )(_bf16_const(k1), _bf16_const(k2), tc, ts,
      g1.reshape(B, n, n, BR_W), g2.reshape(B, n, n, BR_W))
    return o.reshape(B, T, BR_W)


def _time_dft_table(T):
    t = np.arange(T)
    ang = ((t[:, None] * t[None, :]) % T) * (2.0 * np.pi / T)
    return _bf16_const(np.concatenate([np.cos(ang), -np.sin(ang)], axis=1))


def _channel_dft_tables(T):
    ch = np.arange(HEAD_DIM)
    a64 = 2.0 * np.pi * ((ch[:, None] * ch[None, :]) % HEAD_DIM) / HEAD_DIM
    scale = (T * HEAD_DIM) ** -0.5
    eye = np.eye(HEADS)
    cbd = np.kron(eye, np.cos(a64)) * scale
    sbd = np.kron(eye, np.sin(a64)) * scale
    return _bf16_const(np.stack([cbd, sbd]))


def _head_norm(o, ones_h, center):
    if center:
        o = o - _dot(o, ones_h) * (1.0 / HEAD_DIM)
    var = _dot(o * o, ones_h) * (1.0 / HEAD_DIM)
    return o * lax.rsqrt(var + EPS)


def _merge_kernel(x_ref, hb_ref, mod_ref, gg_ref, rg_ref, wg_ref, bon_ref,
                  ogf_ref, ogb_ref, orf_ref, orb_ref, oyf_ref, oyb_ref, fn_ref,
                  gn_ref, wgate_ref, bgate_ref, wbr_ref, wout_ref, o_ref):
    ones_h = _head_ones()
    gn = gn_ref[...]
    gla = _head_norm(ogf_ref[0] + ogb_ref[0], ones_h, False) * gn[0:1] * _silu(gg_ref[0])
    ret = _head_norm(orf_ref[0] + orb_ref[0], ones_h, True) * gn[1:2] * _silu(rg_ref[0])
    rwkv = (_head_norm(oyf_ref[0] + oyb_ref[0], ones_h, True) * gn[2:3] + bon_ref[0]) * wg_ref[0]
    outs = (gla, ret, rwkv, fn_ref[0])
    hb = hb_ref[0]
    z = None
    for br in range(4):
        gate = _sigmoid(jnp.dot(hb, wgate_ref[br], preferred_element_type=F32) + bgate_ref[br])
        term = gate * _dot(outs[br], wbr_ref[br])
        z = term if z is None else z + term
    y = _dot(z, wout_ref[...])
    gate1 = mod_ref[0][:, 2 * D_MODEL:3 * D_MODEL]
    o_ref[0] = x_ref[0] + gate1 * y


def _merge(x, hb, mods, feats, scans, fnet_out, lw):
    B, T, D = x.shape
    tb = min(TOKEN_TILE, T)
    tok = lambda w: pl.BlockSpec((1, tb, w), lambda b, i: (b, i, 0))
    fcol = lambda j: pl.BlockSpec((1, tb, BR_W), lambda b, i, j=j: (b, i, j))
    const2 = lambda b, i: (0, 0)
    const3 = lambda b, i: (0, 0, 0)
    return pl.pallas_call(
        _merge_kernel,
        grid=(B, T // tb),
        in_specs=[
            tok(D), tok(D),
            pl.BlockSpec((1, 1, 6 * D), lambda b, i: (b, 0, 0)),
            fcol(F_GG), fcol(F_RG), fcol(F_WG), fcol(F_BON),
            tok(BR_W), tok(BR_W), tok(BR_W), tok(BR_W), tok(BR_W), tok(BR_W), tok(BR_W),
            pl.BlockSpec((3, BR_W), const2),
            pl.BlockSpec((4, D, D), const3, pipeline_mode=pl.Buffered(1)),
            pl.BlockSpec((4, 1, D), const3),
            pl.BlockSpec((4, BR_W, D), const3, pipeline_mode=pl.Buffered(1)),
            pl.BlockSpec((D, D), const2, pipeline_mode=pl.Buffered(1)),
        ],
        out_specs=tok(D),
        out_shape=jax.ShapeDtypeStruct((B, T, D), F32),
        compiler_params=_cparams(("parallel", "arbitrary")),
        name="merge",
    )(x, hb, mods, feats, feats, feats, feats, *scans, fnet_out,
      lw["gn"], lw["w_gate"], lw["b_gate"], lw["w_br"], lw["w_out"])


FFN_COL_CHUNKS = (1024, 1024, 768)


def _ffn_kernel(final, tb, nt,
                x_ref, xp_ref, xn_ref, mod_ref, g2_ref, up_ref, cw_ref, cb_ref,
                down_ref, gfin_ref, o_ref, act_ref):
    i = pl.program_id(1)
    n_ext = tb + 2 * HALO
    x_ext = jnp.concatenate([xp_ref[0], x_ref[0], xn_ref[0]], axis=0)
    mod = mod_ref[0]
    shift = mod[:, 3 * D_MODEL:4 * D_MODEL]
    scale = mod[:, 4 * D_MODEL:5 * D_MODEL]
    h2 = (_rms_rows(x_ext) * g2_ref[...]) * (1.0 + scale) + shift
    h2 = h2 * _halo_keep(n_ext, tb, i, nt)
    h2_ext = h2.astype(BF16)
    h2_mid = h2[HALO:HALO + tb].astype(BF16)
    cw = cw_ref[...]
    cb = cb_ref[...]
    lo = 0
    for width in FFN_COL_CHUNKS:
        a_ext = jnp.dot(h2_ext, up_ref[:, lo:lo + width], preferred_element_type=F32)
        u = jnp.dot(h2_mid, up_ref[:, D_FF + lo:D_FF + lo + width], preferred_element_type=F32)
        a = (a_ext[HALO - 1:HALO - 1 + tb] * cw[0:1, lo:lo + width]
             + a_ext[HALO:HALO + tb] * cw[1:2, lo:lo + width]
             + a_ext[HALO + 1:HALO + 1 + tb] * cw[2:3, lo:lo + width] + cb[:, lo:lo + width])
        act_ref[:, lo:lo + width] = (_silu(a) * u).astype(BF16)
        lo += width
    y = jnp.dot(act_ref[...], down_ref[...], preferred_element_type=F32)
    gate2 = mod[:, 5 * D_MODEL:6 * D_MODEL]
    res = x_ref[0] + gate2 * y
    if final:
        res = _rms_rows(res) * gfin_ref[...]
    o_ref[0] = res


def _ffn(x, mods, lw, g_final, final):
    B, T, D = x.shape
    tb = min(TOKEN_TILE, T)
    nt = T // tb
    hb8 = tb // HALO
    n_h = T // HALO
    const2 = lambda b, i: (0, 0)
    resident = lambda shape: pl.BlockSpec(shape, const2, pipeline_mode=pl.Buffered(1))
    kern = functools.partial(_ffn_kernel, final, tb, nt)
    return pl.pallas_call(
        kern,
        grid=(B, nt),
        in_specs=[
            pl.BlockSpec((1, tb, D), lambda b, i: (b, i, 0)),
            pl.BlockSpec((1, HALO, D), lambda b, i: (b, jnp.maximum(i * hb8 - 1, 0), 0)),
            pl.BlockSpec((1, HALO, D), lambda b, i: (b, jnp.minimum((i + 1) * hb8, n_h - 1), 0)),
            pl.BlockSpec((1, 1, 6 * D), lambda b, i: (b, 0, 0)),
            pl.BlockSpec((1, D), const2),
            resident((D, 2 * D_FF)),
            pl.BlockSpec((3, D_FF), const2),
            pl.BlockSpec((1, D_FF), const2),
            resident((D_FF, D)),
            pl.BlockSpec((1, D), const2),
        ],
        out_specs=pl.BlockSpec((1, tb, D), lambda b, i: (b, i, 0)),
        out_shape=jax.ShapeDtypeStruct((B, T, D), F32),
        scratch_shapes=[pltpu.VMEM((tb, D_FF), BF16)],
        compiler_params=_cparams(("parallel", "arbitrary")),
        name="ffn",
    )(x, x, x, mods, lw["g2"], lw["ffn_up"], lw["ffn_conv"], lw["ffn_conv_b"],
      lw["ffn_down"], g_final)


def _rope_tables(T):
    rows = T // GRID_W
    row = jnp.repeat(jnp.arange(rows, dtype=F32), GRID_W)
    colp = jnp.tile(jnp.arange(GRID_W, dtype=F32), rows)
    n_freq = HEAD_DIM // 4
    inv = ROPE_BASE ** (-jnp.arange(n_freq, dtype=F32) / n_freq)
    ang = jnp.concatenate([row[:, None] * inv, colp[:, None] * inv], axis=-1)
    cos, sin = jnp.cos(ang), jnp.sin(ang)
    cos_h = jnp.concatenate([cos, cos], axis=-1)
    sin_h = jnp.concatenate([-sin, sin], axis=-1)
    return jnp.tile(cos_h, (1, HEADS)), jnp.tile(sin_h, (1, HEADS))


def _pad_rows(w, lo, n):
    return jnp.zeros((n, w.shape[1]), w.dtype).at[lo:lo + w.shape[0]].set(w)


def _layer_weights(l, p):
    D = D_MODEL
    bf = lambda t: t.astype(BF16)
    zcols = lambda w, n: jnp.concatenate([w, jnp.zeros((w.shape[0], n - w.shape[1]), w.dtype)], axis=1)
    w_h = zcols(jnp.concatenate([p["gla_wa1"][l, 0], p["gla_wa1"][l, 1]], axis=1), 128)
    w_xw = jnp.concatenate([p["rwkv_w1"][l, 0], p["rwkv_w1"][l, 1]], axis=1)
    w_xa = jnp.concatenate([p["rwkv_a1"][l, 0], p["rwkv_a1"][l, 1]], axis=1)
    w_xg = zcols(p["rwkv_g1"][l], 256)
    mu = p["rwkv_mu"][l]
    w_lr = jnp.concatenate([w_xg, mu[2][:, None] * w_xg, w_h, w_xw, w_xa,
                            mu[0][:, None] * w_xw, mu[1][:, None] * w_xa,
                            jnp.zeros((D, LR_COLS - LR_AS - 128), F32)], axis=1)
    w2 = jnp.stack([
        _pad_rows(p["gla_wa2"][l, 0], 0, 256), _pad_rows(p["gla_wa2"][l, 1], GLA_LR, 256),
        _pad_rows(p["rwkv_w2"][l, 0], 0, 256), _pad_rows(p["rwkv_w2"][l, 1], RWKV_LR, 256),
        _pad_rows(p["rwkv_a2"][l, 0], 0, 256), _pad_rows(p["rwkv_a2"][l, 1], RWKV_LR, 256),
        _pad_rows(p["rwkv_g2"][l], 0, 256),
    ])
    vec = jnp.stack([p["gla_ba"][l, 0], p["gla_ba"][l, 1], p["rwkv_w0"][l, 0], p["rwkv_w0"][l, 1],
                     p["rwkv_a0"][l, 0], p["rwkv_a0"][l, 1], p["rwkv_kk"][l], p["rwkv_ka"][l],
                     p["rwkv_rk"][l]])
    vec = jnp.concatenate([vec, jnp.zeros((16 - vec.shape[0], BR_W), F32)], axis=0)
    return {
        "g1": p["g_norm1"][l].reshape(1, D), "g2": p["g_norm2"][l].reshape(1, D),
        "w_in": bf(p["w_in"][l]), "w_lr": bf(w_lr), "w2": bf(w2), "vec": vec,
        "rwkv_conv": p["rwkv_conv"][l],
        "gn": jnp.stack([p["gla_gn"][l], p["ret_gn"][l], p["rwkv_gn"][l]]),
        "w_gate": bf(p["w_gate"][l]), "b_gate": p["b_gate"][l].reshape(4, 1, D),
        "w_br": bf(p["w_br"][l]), "w_out": bf(p["w_out"][l]),
        "ffn_up": bf(p["ffn_up"][l]), "ffn_conv": p["ffn_conv"][l],
        "ffn_conv_b": p["ffn_conv_b"][l].reshape(1, D_FF), "ffn_down": bf(p["ffn_down"][l]),
    }


_GLA_F = (F_GQ, F_GK, F_GV, F_LAF)
_GLA_B = (F_GQ, F_GK, F_GV, F_LAB)
_RET = (F_RQ, F_RK, F_RV)
_RWKV_F = (F_WR, F_WV, F_WKK, F_KDF, F_ASF, F_LWF)
_RWKV_B = (F_WR, F_WV, F_WKK, F_KDB, F_ASB, F_LWB)


def _mixers(feats_c, feats_l, ret_tabs_c, ret_tabs_l, need_ctx):
    B = feats_l.shape[0]
    zero = jnp.zeros((2, B, BR_W, BR_W), F32)
    out_c, out_l = [], []
    for kind, cf, cb, tabs in (("gla", _GLA_F, _GLA_B, None), ("ret", _RET, _RET, True),
                               ("rwkv", _RWKV_F, _RWKV_B, None)):
        tc = ret_tabs_c if tabs else ()
        tl = ret_tabs_l if tabs else ()
        of_c, ob_c, s_c = _scan(kind, feats_c, cf, cb, zero, tc)
        of_l, ob_l, _ = _scan(kind, feats_l, cf, cb, s_c, tl)
        out_c += [of_c, ob_c]
        out_l += [of_l, ob_l]
    return out_l, (out_c if need_ctx else None)


def kernel(x, c, ctx, c_ctx, w_ada, b_ada, g_norm1, g_norm2, w_in, gla_wa1, gla_wa2, gla_ba, gla_gn, ret_gn, rwkv_conv, rwkv_mu, rwkv_w0, rwkv_w1, rwkv_w2, rwkv_a0, rwkv_a1, rwkv_a2, rwkv_g1, rwkv_g2, rwkv_kk, rwkv_ka, rwkv_rk, rwkv_gn, w_gate, b_gate, w_br, w_out, ffn_up, ffn_conv, ffn_conv_b, ffn_down, g_final):
    p = dict(g_norm1=g_norm1, g_norm2=g_norm2, w_in=w_in, gla_wa1=gla_wa1, gla_wa2=gla_wa2,
             gla_ba=gla_ba, gla_gn=gla_gn, ret_gn=ret_gn, rwkv_conv=rwkv_conv, rwkv_mu=rwkv_mu,
             rwkv_w0=rwkv_w0, rwkv_w1=rwkv_w1, rwkv_w2=rwkv_w2, rwkv_a0=rwkv_a0,
             rwkv_a1=rwkv_a1, rwkv_a2=rwkv_a2, rwkv_g1=rwkv_g1, rwkv_g2=rwkv_g2,
             rwkv_kk=rwkv_kk, rwkv_ka=rwkv_ka, rwkv_rk=rwkv_rk, rwkv_gn=rwkv_gn,
             w_gate=w_gate, b_gate=b_gate, w_br=w_br, w_out=w_out, ffn_up=ffn_up,
             ffn_conv=ffn_conv, ffn_conv_b=ffn_conv_b, ffn_down=ffn_down)
    B, T, D = x.shape
    Tc = ctx.shape[1]
    depth = w_ada.shape[0]

    cmat = jnp.concatenate([c, c_ctx[None, :], jnp.zeros((MOD_ROWS - B - 1, D), F32)], axis=0)
    mod_all = _modulation(cmat, w_ada, b_ada)

    rope_l = _rope_tables(T)
    rope_c = (jnp.zeros((Tc, BR_W), F32), jnp.zeros((Tc, BR_W), F32))
    dft_l = _channel_dft_tables(T)
    dft_c = _channel_dft_tables(Tc)
    cs_c = _time_dft_table(Tc)
    ret_tabs_l = _ret_tables(min(SCAN_CHUNK["ret"], T))
    ret_tabs_c = _ret_tables(min(SCAN_CHUNK["ret"], Tc))
    g_fin = g_final.reshape(1, D)

    for l in range(depth):
        last = l == depth - 1
        lw = _layer_weights(l, p)
        mods_l = mod_all[l, 0:B].reshape(B, 1, 6 * D)
        mods_c = jnp.broadcast_to(mod_all[l, B:B + 1].reshape(1, 1, 6 * D), (B, 1, 6 * D))

        feats_l, hb_l, g1_l, g2_l = _pre(x, mods_l, lw, True, rope_l, dft_l)
        feats_c, hb_c, g1_c, g2_c = _pre(ctx, mods_c, lw, False, rope_c, dft_c)
        scans_l, scans_c = _mixers(feats_c, feats_l, ret_tabs_c, ret_tabs_l, not last)

        fn_l = _fnet_fft(g1_l, g2_l)
        x = _merge(x, hb_l, mods_l, feats_l, scans_l, fn_l, lw)
        x = _ffn(x, mods_l, lw, g_fin, last)
        if not last:
            fn_c = _fnet_dense(g1_c, g2_c, cs_c)
            ctx = _merge(ctx, hb_c, mods_c, feats_c, scans_c, fn_c, lw)
            ctx = _ffn(ctx, mods_c, lw, g_fin, False)
    return x
```
